```python
import jax, jax.numpy as jnp
from jax import lax
import numpy as np

D_MODEL = 1024
BATCH = 2
SEQ = 8192
DEPTH = 4

N_A_LAYERS = DEPTH // 2
N_B_LAYERS = DEPTH - N_A_LAYERS

GDN_HEADS = 8
GDN_DK = 128
GDN_DV = 128
D_QK = GDN_HEADS * GDN_DK
D_VA = GDN_HEADS * GDN_DV
GDN_CONV_CH = 2 * D_QK + D_VA
GDN_IN = 2 * D_QK + 2 * D_VA + 2 * GDN_HEADS
CONV_WIDTH = 4
CHUNK = 64
MLA_HEADS = 8
Q_LORA = 512
KV_LORA = 256
D_NOPE = 128
D_ROPE = 64
D_V = 128
ROPE_THETA = 10000.0
Q_BLOCK = 128
N_EXPERTS = 64
TOP_K = 8
N_GROUPS = 8
TOPK_GROUPS = 4
D_EXPERT = 256
D_SHARED = 256
ROUTED_SCALE = 2.5
EXPERT_BLOCK = 128
DN_ALPHA = (2 * DEPTH) ** 0.25
DN_BETA = (8 * DEPTH) ** -0.25
EPS = 1e-6
LN_EPS = 1e-5
F32 = jnp.float32

kernel_name = 'yoco_gdn_mla_moe_deepnorm'


def layer_norm(x, g, b):
    xf = x.astype(F32)
    mu = jnp.mean(xf, -1, keepdims=True)
    var = jnp.mean(jnp.square(xf - mu), -1, keepdims=True)
    return ((xf - mu) * lax.rsqrt(var + LN_EPS) * g + b).astype(x.dtype)


def rms_norm(x, w):
    xf = x.astype(F32)
    return (xf * lax.rsqrt(jnp.mean(xf * xf, -1, keepdims=True) + EPS) * w).astype(x.dtype)


def l2_norm(x):
    return x * lax.rsqrt(jnp.sum(x * x, -1, keepdims=True) + EPS)


def rope_tables(L):
    inv = 1.0 / (ROPE_THETA ** (jnp.arange(0, D_ROPE, 2, dtype=F32) / D_ROPE))
    ang = jnp.arange(L, dtype=F32)[:, None] * inv[None, :]
    return jnp.cos(ang), jnp.sin(ang)


def apply_rope(x, cos, sin):
    x1, x2 = jnp.split(x.astype(F32), 2, axis=-1)
    return jnp.concatenate([x1 * cos - x2 * sin, x2 * cos + x1 * sin], -1).astype(x.dtype)


def causal_dwconv_silu(x, w):
    C = x.shape[-1]
    y = lax.conv_general_dilated(x, w[:, None, :].astype(x.dtype), window_strides=(1,),
                                 padding=[(CONV_WIDTH - 1, 0)],
                                 dimension_numbers=('NWC', 'WIO', 'NWC'),
                                 feature_group_count=C)
    return jax.nn.silu(y)


def chunk_gated_delta_rule(q, k, v, g, beta):
    B, L, H, DK = q.shape
    DV = v.shape[-1]
    N = L // CHUNK

    def chunks(t):
        t = t.reshape((B, N, CHUNK, H) + t.shape[3:])
        return jnp.moveaxis(t, (1, 3), (0, 2))

    q, k, v, g, beta = (chunks(t) for t in (q, k, v, g, beta))
    g = jnp.cumsum(g, axis=-1)
    pos = jnp.arange(CHUNK)
    incl = pos[:, None] >= pos[None, :]
    strict = pos[:, None] > pos[None, :]
    decay = jnp.exp(jnp.where(incl, g[..., :, None] - g[..., None, :], -jnp.inf))
    kb = k * beta[..., None]
    lower = jnp.where(strict, jnp.einsum('nbhid,nbhjd->nbhij', kb, k) * decay, 0.0)
    rhs = jnp.concatenate([v * beta[..., None], kb * jnp.exp(g)[..., None]], -1)
    sol = lax.linalg.triangular_solve(jnp.eye(CHUNK, dtype=F32) + lower, rhs,
                                      left_side=True, lower=True, unit_diagonal=True)
    u, w = sol[..., :DV], sol[..., DV:]
    attn = jnp.einsum('nbhid,nbhjd->nbhij', q, k) * decay

    def step(S, inp):
        qc, kc, uc, wc, gc, ac = inp
        v_new = uc - jnp.einsum('bhck,bhkv->bhcv', wc, S)
        o = (jnp.einsum('bhck,bhkv->bhcv', qc * jnp.exp(gc)[..., None], S)
             + jnp.einsum('bhij,bhjv->bhiv', ac, v_new))
        g_last = gc[..., -1:]
        S = (S * jnp.exp(g_last)[..., None]
             + jnp.einsum('bhck,bhcv->bhkv', kc * jnp.exp(g_last - gc)[..., None], v_new))
        return S, o

    S0 = jnp.zeros((B, H, DK, DV), F32)
    _, o = lax.scan(step, S0, (q, k, u, w, g, attn))
    return jnp.moveaxis(o, (0, 2), (1, 3)).reshape(B, L, H, DV)


def gated_deltanet(x, w_in, conv_w, A_log, dt_bias, onorm_w, w_out):
    B, L, _ = x.shape
    H = GDN_HEADS
    proj = x @ w_in
    qkv, z, a, b = jnp.split(proj, [GDN_CONV_CH, GDN_CONV_CH + D_VA, GDN_CONV_CH + D_VA + H], axis=-1)
    qkv = causal_dwconv_silu(qkv, conv_w)
    q, k, v = jnp.split(qkv, [D_QK, 2 * D_QK], axis=-1)
    q = l2_norm(q.reshape(B, L, H, GDN_DK).astype(F32)) * (GDN_DK ** -0.5)
    k = l2_norm(k.reshape(B, L, H, GDN_DK).astype(F32))
    v = v.reshape(B, L, H, GDN_DV).astype(F32)
    beta = jax.nn.sigmoid(b.astype(F32))
    g = -jnp.exp(A_log.astype(F32)) * jax.nn.softplus(a.astype(F32) + dt_bias.astype(F32))
    o = chunk_gated_delta_rule(q, k, v, g, beta)
    o = rms_norm(o, onorm_w) * jax.nn.silu(z.reshape(B, L, H, GDN_DV).astype(F32))
    return o.reshape(B, L, D_VA).astype(x.dtype) @ w_out


def shared_mla_kv(h, w_dkv, kv_norm_w, w_ukv, cos, sin):
    B, L, _ = h.shape
    ckv = h @ w_dkv
    c = rms_norm(ckv[..., :KV_LORA], kv_norm_w)
    k_rope = apply_rope(ckv[..., KV_LORA:], cos, sin)
    kv = (c @ w_ukv).reshape(B, L, MLA_HEADS, D_NOPE + D_V)
    return kv[..., :D_NOPE], k_rope, kv[..., D_NOPE:]


def mla_attention(x, w_dq, qnorm_w, w_uq, w_o, k_nope, k_rope, v, cos, sin):
    B, L, _ = x.shape
    q = (rms_norm(x @ w_dq, qnorm_w) @ w_uq).reshape(B, L, MLA_HEADS, D_NOPE + D_ROPE)
    q_nope = q[..., :D_NOPE]
    q_rope = apply_rope(q[..., D_NOPE:], cos[:, None], sin[:, None])
    nb = L // Q_BLOCK
    scale = (D_NOPE + D_ROPE) ** -0.5
    kpos = jnp.arange(L)

    def blocks(t):
        return jnp.moveaxis(t.reshape((B, nb, Q_BLOCK) + t.shape[2:]), 1, 0)

    def attend(args):
        qn, qr, i = args
        s = (jnp.einsum('bqhd,bkhd->bhqk', qn, k_nope)
             + jnp.einsum('bqhd,bkd->bhqk', qr, k_rope)).astype(F32) * scale
        qpos = i * Q_BLOCK + jnp.arange(Q_BLOCK)
        s = jnp.where(kpos[None, :] <= qpos[:, None], s, -jnp.inf)
        p = jax.nn.softmax(s, axis=-1).astype(v.dtype)
        return jnp.einsum('bhqk,bkhd->bqhd', p, v)

    o = lax.map(attend, (blocks(q_nope), blocks(q_rope), jnp.arange(nb)))
    return jnp.moveaxis(o, 0, 1).reshape(B, L, MLA_HEADS * D_V) @ w_o


def swiglu(x, wg, wu, wd):
    return (jax.nn.silu(x @ wg) * (x @ wu)) @ wd


def route(xf, w_router, bias):
    T = xf.shape[0]
    scores = jax.nn.sigmoid((xf @ w_router).astype(F32))
    biased = scores + bias.astype(F32)
    grouped = biased.reshape(T, N_GROUPS, N_EXPERTS // N_GROUPS)
    group_score = lax.top_k(grouped, 2)[0].sum(-1)
    _, top_groups = lax.top_k(group_score, TOPK_GROUPS)
    group_mask = jax.nn.one_hot(top_groups, N_GROUPS, dtype=F32).sum(1) > 0
    expert_mask = jnp.repeat(group_mask, N_EXPERTS // N_GROUPS, axis=-1)
    _, idx = lax.top_k(jnp.where(expert_mask, biased, -jnp.inf), TOP_K)
    w = jnp.take_along_axis(scores, idx, axis=-1)
    w = w / (w.sum(-1, keepdims=True) + 1e-20) * ROUTED_SCALE
    return idx, w


def routed_experts(xf, idx, w, w_gate, w_up, w_down):
    T, D = xf.shape
    A = T * TOP_K
    n_blocks = -(-A // EXPERT_BLOCK) + N_EXPERTS
    flat_e = idx.reshape(A)
    order = jnp.argsort(flat_e)
    e_sorted = flat_e[order]
    counts = jnp.bincount(flat_e, length=N_EXPERTS)
    padded = (counts + EXPERT_BLOCK - 1) // EXPERT_BLOCK * EXPERT_BLOCK
    pad_end = jnp.cumsum(padded)
    pad_start = pad_end - padded
    start = jnp.cumsum(counts) - counts
    dest = pad_start[e_sorted] + jnp.arange(A) - start[e_sorted]
    slot_tok = jnp.full((n_blocks * EXPERT_BLOCK,), T, jnp.int32).at[dest].set((order // TOP_K).astype(jnp.int32))
    slot_w = jnp.zeros((n_blocks * EXPERT_BLOCK,), F32).at[dest].set(w.reshape(A)[order])
    block_e = jnp.minimum(jnp.searchsorted(pad_end, jnp.arange(n_blocks) * EXPERT_BLOCK, side='right'),
                          N_EXPERTS - 1)
    x_pad = jnp.concatenate([xf, jnp.zeros((1, D), xf.dtype)], axis=0)

    def step(acc, inp):
        tok, wt, e = inp
        y = swiglu(x_pad[tok], w_gate[e], w_up[e], w_down[e]).astype(F32) * wt[:, None]
        return acc.at[tok].add(y), None

    acc, _ = lax.scan(step, jnp.zeros((T + 1, D), F32),
                      (slot_tok.reshape(n_blocks, EXPERT_BLOCK), slot_w.reshape(n_blocks, EXPERT_BLOCK), block_e))
    return acc[:T]


def moe_ffn(x, w_router, router_bias, w_gate, w_up, w_down, ws_gate, ws_up, ws_down):
    B, L, D = x.shape
    xf = x.reshape(B * L, D)
    idx, w = route(xf, w_router, router_bias)
    y = routed_experts(xf, idx, w, w_gate, w_up, w_down) + swiglu(xf, ws_gate, ws_up, ws_down).astype(F32)
    return y.astype(x.dtype).reshape(B, L, D)


def setup_inputs(seed: int = 0) -> dict:
    key = jax.random.key(seed)
    keys = iter(jax.random.split(key, 40))
    D = D_MODEL
    NA, NB = N_A_LAYERS, N_B_LAYERS

    def nrm(shape, scale):
        return jax.random.normal(next(keys), shape, F32) * scale

    def gain(shape):
        return 1.0 + nrm(shape, 0.02)

    a_col_scale = jnp.concatenate([jnp.ones((2 * D_QK,), F32), jnp.full((D_VA,), DN_BETA, F32),
                                   jnp.ones((D_VA + 2 * GDN_HEADS,), F32)])
    dt = jnp.exp(jax.random.uniform(next(keys), (NA, GDN_HEADS), F32,
                                    minval=float(np.log(1e-3)), maxval=float(np.log(1e-1))))
    ukv_col_scale = jnp.tile(jnp.concatenate([jnp.ones((D_NOPE,), F32), jnp.full((D_V,), DN_BETA, F32)]), MLA_HEADS)
    return {
        'x': jax.random.normal(next(keys), (BATCH, SEQ, D), F32),
        'a_w_in': nrm((NA, D, GDN_IN), D ** -0.5) * a_col_scale,
        'a_conv_w': nrm((NA, CONV_WIDTH, GDN_CONV_CH), CONV_WIDTH ** -0.5),
        'a_A_log': jnp.log(jax.random.uniform(next(keys), (NA, GDN_HEADS), F32, minval=1.0, maxval=16.0)),
        'a_dt_bias': dt + jnp.log(-jnp.expm1(-dt)),
        'a_onorm_w': gain((NA, GDN_DV)),
        'a_w_out': nrm((NA, D_VA, D), D_VA ** -0.5 * DN_BETA),
        'b_w_dq': nrm((NB, D, Q_LORA), D ** -0.5),
        'b_qnorm_w': gain((NB, Q_LORA)),
        'b_w_uq': nrm((NB, Q_LORA, MLA_HEADS * (D_NOPE + D_ROPE)), Q_LORA ** -0.5),
        'b_w_o': nrm((NB, MLA_HEADS * D_V, D), (MLA_HEADS * D_V) ** -0.5 * DN_BETA),
        'kv_w_dkv': nrm((D, KV_LORA + D_ROPE), D ** -0.5),
        'kv_norm_w': gain((KV_LORA,)),
        'kv_w_ukv': nrm((KV_LORA, MLA_HEADS * (D_NOPE + D_V)), KV_LORA ** -0.5) * ukv_col_scale,
        'ln1_g': gain((DEPTH, D)),
        'ln1_b': nrm((DEPTH, D), 0.02),
        'ln2_g': gain((DEPTH, D)),
        'ln2_b': nrm((DEPTH, D), 0.02),
        'moe_w_router': nrm((DEPTH, D, N_EXPERTS), D ** -0.5),
        'moe_router_bias': nrm((DEPTH, N_EXPERTS), 0.01),
        'moe_w_gate': nrm((DEPTH, N_EXPERTS, D, D_EXPERT), D ** -0.5),
        'moe_w_up': nrm((DEPTH, N_EXPERTS, D, D_EXPERT), D ** -0.5),
        'moe_w_down': nrm((DEPTH, N_EXPERTS, D_EXPERT, D), D_EXPERT ** -0.5 * DN_BETA),
        'moe_ws_gate': nrm((DEPTH, D, D_SHARED), D ** -0.5),
        'moe_ws_up': nrm((DEPTH, D, D_SHARED), D ** -0.5),
        'moe_ws_down': nrm((DEPTH, D_SHARED, D), D_SHARED ** -0.5 * DN_BETA),
    }


def reference(x, a_w_in, a_conv_w, a_A_log, a_dt_bias, a_onorm_w, a_w_out,
              b_w_dq, b_qnorm_w, b_w_uq, b_w_o, kv_w_dkv, kv_norm_w, kv_w_ukv,
              ln1_g, ln1_b, ln2_g, ln2_b, moe_w_router, moe_router_bias,
              moe_w_gate, moe_w_up, moe_w_down, moe_ws_gate, moe_ws_up, moe_ws_down):
    L = x.shape[1]
    cos, sin = rope_tables(L)
    h = x
    k_nope = k_rope = v = None
    for layer in range(DEPTH):
        if layer < N_A_LAYERS:
            i = layer
            mix = gated_deltanet(h, a_w_in[i], a_conv_w[i], a_A_log[i], a_dt_bias[i], a_onorm_w[i], a_w_out[i])
        else:
            i = layer - N_A_LAYERS
            mix = mla_attention(h, b_w_dq[i], b_qnorm_w[i], b_w_uq[i], b_w_o[i], k_nope, k_rope, v, cos, sin)
        h = layer_norm(DN_ALPHA * h + mix, ln1_g[layer], ln1_b[layer])
        ffn = moe_ffn(h, moe_w_router[layer], moe_router_bias[layer], moe_w_gate[layer], moe_w_up[layer],
                      moe_w_down[layer], moe_ws_gate[layer], moe_ws_up[layer], moe_ws_down[layer])
        h = layer_norm(DN_ALPHA * h + ffn, ln2_g[layer], ln2_b[layer])
        if layer == N_A_LAYERS - 1:
            k_nope, k_rope, v = shared_mla_kv(h, kv_w_dkv, kv_norm_w, kv_w_ukv, cos, sin)
    return h
```

```python
import functools

import jax
import jax.numpy as jnp
import numpy as np
from jax import lax
from jax.experimental import pallas as pl
from jax.experimental.pallas import tpu as pltpu

F32 = jnp.float32
BF16 = jnp.bfloat16
HIGHEST = lax.Precision.HIGHEST

D_MODEL = 1024
DEPTH = 4
N_A_LAYERS = DEPTH // 2
GDN_HEADS = 8
GDN_DK = 128
GDN_DV = 128
D_QK = GDN_HEADS * GDN_DK
D_VA = GDN_HEADS * GDN_DV
GDN_CONV_CH = 2 * D_QK + D_VA
CONV_WIDTH = 4
CHUNK = 64
MLA_HEADS = 8
Q_LORA = 512
KV_LORA = 256
D_NOPE = 128
D_ROPE = 64
D_V = 128
ROPE_THETA = 10000.0
N_EXPERTS = 64
TOP_K = 8
N_GROUPS = 8
GROUP_SIZE = N_EXPERTS // N_GROUPS
TOPK_GROUPS = 4
D_EXPERT = 256
D_SHARED = 256
ROUTED_SCALE = 2.5
DN_ALPHA = (2 * DEPTH) ** 0.25
EPS = 1e-6
LN_EPS = 1e-5

LANES = 128
SUBLANES = 8
VMEM_LIMIT_BYTES = 56 * 1024 * 1024

EXPERT_ROWS = 128
ROW_CHUNKS = D_MODEL // LANES
STAGE_PITCH = EXPERT_ROWS + 1
SCATTER_BATCH = 16


def _cparams(*sem):
    return pltpu.CompilerParams(dimension_semantics=sem, vmem_limit_bytes=VMEM_LIMIT_BYTES)


def _silu(x):
    return x * jax.nn.sigmoid(x)


def _layer_norm(x, g, b):
    mu = jnp.mean(x, -1, keepdims=True)
    xc = x - mu
    var = jnp.mean(xc * xc, -1, keepdims=True)
    return xc * lax.rsqrt(var + LN_EPS) * g + b


def _rms_norm(x, w):
    return x * lax.rsqrt(jnp.mean(x * x, -1, keepdims=True) + EPS) * w


def _bdot(a, b):
    return jnp.dot(a.astype(BF16), b.astype(BF16), preferred_element_type=F32)


def _bdot_nt(a, b):
    return lax.dot_general(a.astype(BF16), b.astype(BF16), (((1,), (1,)), ((), ())),
                           preferred_element_type=F32)


def _hdot(a, b):
    return jnp.dot(a, b, precision=HIGHEST, preferred_element_type=F32)


def _matmul_kernel(x_ref, w_ref, o_ref):
    o_ref[...] = _bdot(x_ref[...], w_ref[...]).astype(o_ref.dtype)


def _matmul(x, w, out_dtype, tm, tn):
    m, k = x.shape
    n = w.shape[1]
    tm = min(tm, m)
    tn = min(tn, n)
    return pl.pallas_call(
        _matmul_kernel,
        grid=(m // tm, n // tn),
        in_specs=[pl.BlockSpec((tm, k), lambda i, j: (i, 0)),
                  pl.BlockSpec((k, tn), lambda i, j: (0, j))],
        out_specs=pl.BlockSpec((tm, tn), lambda i, j: (i, j)),
        out_shape=jax.ShapeDtypeStruct((m, n), out_dtype),
        compiler_params=_cparams("parallel", "parallel"),
        name="matmul",
    )(x, w)


def _matmul_ln_kernel(x_ref, w_ref, r_ref, g_ref, b_ref, o_ref):
    mix = _bdot(x_ref[...], w_ref[...])
    o_ref[...] = _layer_norm(DN_ALPHA * r_ref[...] + mix, g_ref[...], b_ref[...])


def _matmul_ln(x, w, res, g, b, tm=512):
    m, k = x.shape
    n = w.shape[1]
    tm = min(tm, m)
    return pl.pallas_call(
        _matmul_ln_kernel,
        grid=(m // tm,),
        in_specs=[pl.BlockSpec((tm, k), lambda i: (i, 0)),
                  pl.BlockSpec((k, n), lambda i: (0, 0)),
                  pl.BlockSpec((tm, n), lambda i: (i, 0)),
                  pl.BlockSpec((1, n), lambda i: (0, 0)),
                  pl.BlockSpec((1, n), lambda i: (0, 0))],
        out_specs=pl.BlockSpec((tm, n), lambda i: (i, 0)),
        out_shape=jax.ShapeDtypeStruct((m, n), F32),
        compiler_params=_cparams("parallel"),
        name="matmul_ln",
    )(x, w, res, g.reshape(1, n), b.reshape(1, n))


def _softplus(x):
    return jnp.maximum(x, 0.0) + jnp.log1p(jnp.exp(-jnp.abs(x)))


def _gdn_gates_kernel(x_ref, wab_ref, wabt_ref, alog_ref, dtb_ref, alogt_ref, dtbt_ref,
                      gc_ref, beta_ref, gct_ref):
    x = x_ref[...]
    tl = x.shape[0]
    h = GDN_HEADS
    ab = _hdot(x, wab_ref[...])
    abt = lax.dot_general(wabt_ref[...], x, (((1,), (1,)), ((), ())),
                          precision=HIGHEST, preferred_element_type=F32)
    g = -jnp.exp(alog_ref[...]) * _softplus(ab[:, :h] + dtb_ref[...])
    gt = -jnp.exp(alogt_ref[...]) * _softplus(abt[:h, :] + dtbt_ref[...])
    beta_ref[...] = jax.nn.sigmoid(ab[:, h:2 * h])
    row = lax.broadcasted_iota(jnp.int32, (tl, tl), 0)
    col = lax.broadcasted_iota(jnp.int32, (tl, tl), 1)
    shift = CHUNK.bit_length() - 1
    same = (row >> shift) == (col >> shift)
    lower = jnp.where(same & (col <= row), 1.0, 0.0).astype(F32)
    upper = jnp.where(same & (row <= col), 1.0, 0.0).astype(F32)
    gc_ref[...] = _hdot(lower, g)
    gct_ref[...] = _hdot(gt, upper)


def _gdn_gates(x, w_ab, a_log, dt_bias, tl=512):
    t = x.shape[0]
    tl = min(tl, t)
    h = GDN_HEADS
    wab = jnp.zeros((D_MODEL, LANES), F32).at[:, :2 * h].set(w_ab)
    wabt = w_ab.T
    return pl.pallas_call(
        _gdn_gates_kernel,
        grid=(t // tl,),
        in_specs=[pl.BlockSpec((tl, D_MODEL), lambda i: (i, 0)),
                  pl.BlockSpec((D_MODEL, LANES), lambda i: (0, 0)),
                  pl.BlockSpec((2 * h, D_MODEL), lambda i: (0, 0)),
                  pl.BlockSpec((1, h), lambda i: (0, 0)),
                  pl.BlockSpec((1, h), lambda i: (0, 0)),
                  pl.BlockSpec((h, 1), lambda i: (0, 0)),
                  pl.BlockSpec((h, 1), lambda i: (0, 0))],
        out_specs=[pl.BlockSpec((tl, h), lambda i: (i, 0)),
                   pl.BlockSpec((tl, h), lambda i: (i, 0)),
                   pl.BlockSpec((h, tl), lambda i: (0, i))],
        out_shape=[jax.ShapeDtypeStruct((t, h), F32),
                   jax.ShapeDtypeStruct((t, h), F32),
                   jax.ShapeDtypeStruct((h, t), F32)],
        compiler_params=_cparams("parallel"),
        name="gdn_gates",
    )(x, wab, wabt, a_log.reshape(1, h), dt_bias.reshape(1, h),
      a_log.reshape(h, 1), dt_bias.reshape(h, 1))


def _gdn_conv_kernel(cur_ref, prev_ref, w_ref, q_ref, k_ref, v_ref):
    i = pl.program_id(1)
    x = cur_ref[0]
    tl = x.shape[0]
    prev = jnp.where(i > 0, prev_ref[0], 0.0)
    xx = jnp.concatenate([prev, x], axis=0)
    w = w_ref[...]
    y = None
    for j in range(CONV_WIDTH):
        off = SUBLANES - (CONV_WIDTH - 1) + j
        term = xx[off:off + tl, :] * w[j:j + 1, :]
        y = term if y is None else y + term
    y = _silu(y)
    for h in range(GDN_HEADS):
        sl = slice(h * GDN_DK, (h + 1) * GDN_DK)
        qh = y[:, sl]
        q_ref[0, :, sl] = qh * lax.rsqrt(jnp.sum(qh * qh, -1, keepdims=True) + EPS) * (GDN_DK ** -0.5)
        kh = y[:, D_QK + h * GDN_DK:D_QK + (h + 1) * GDN_DK]
        k_ref[0, :, sl] = kh * lax.rsqrt(jnp.sum(kh * kh, -1, keepdims=True) + EPS)
    v_ref[0] = y[:, 2 * D_QK:]


def _gdn_conv(proj, conv_w, tl=256):
    b, l, _ = proj.shape
    tl = min(tl, l)
    c3 = GDN_CONV_CH
    per = tl // SUBLANES
    out = jax.ShapeDtypeStruct((b, l, D_QK), F32)
    return pl.pallas_call(
        _gdn_conv_kernel,
        grid=(b, l // tl),
        in_specs=[pl.BlockSpec((1, tl, c3), lambda bi, i: (bi, i, 0)),
                  pl.BlockSpec((1, SUBLANES, c3), lambda bi, i: (bi, jnp.maximum(i * per - 1, 0), 0)),
                  pl.BlockSpec((CONV_WIDTH, c3), lambda bi, i: (0, 0))],
        out_specs=[pl.BlockSpec((1, tl, D_QK), lambda bi, i: (bi, i, 0))] * 3,
        out_shape=[out, out, out],
        compiler_params=_cparams("parallel", "parallel"),
        name="gdn_conv",
    )(proj, proj, conv_w)


def _unit_lower_inverse(a):
    c = a.shape[0]
    eye = jnp.where(lax.broadcasted_iota(jnp.int32, (c, c), 0)
                    == lax.broadcasted_iota(jnp.int32, (c, c), 1), 1.0, 0.0).astype(F32)
    m = -a
    p = eye + m
    span = 2
    while span < c:
        m = _hdot(m, m)
        p = p + _hdot(p, m)
        span *= 2
    return p


def _gdn_chunk_kernel(q_ref, k_ref, v_ref, z_ref, gc_ref, beta_ref, gct_ref, onw_ref, o_ref, s_ref):
    n = pl.program_id(1)

    @pl.when(n == 0)
    def _():
        s_ref[...] = jnp.zeros_like(s_ref)

    c = CHUNK
    row = lax.broadcasted_iota(jnp.int32, (c, c), 0)
    col = lax.broadcasted_iota(jnp.int32, (c, c), 1)
    incl = row >= col
    strict = row > col
    gc = gc_ref[0]
    beta = beta_ref[0]
    gct = gct_ref[0, 0]
    onw = onw_ref[...]
    for h in range(GDN_HEADS):
        sl = slice(h * GDN_DK, (h + 1) * GDN_DK)
        q = q_ref[0, :, sl]
        k = k_ref[0, :, sl]
        v = v_ref[0, :, sl]
        gcol = gc[:, h:h + 1]
        grow = gct[h:h + 1, :]
        bcol = beta[:, h:h + 1]
        diff = gcol - grow
        decay = jnp.where(incl, jnp.exp(jnp.where(incl, diff, 0.0)), 0.0)
        kb = k * bcol
        a = jnp.where(strict, _bdot_nt(kb, k) * decay, 0.0)
        tinv = _unit_lower_inverse(a)
        eg = jnp.exp(gcol)
        u = _hdot(tinv, v * bcol)
        w = _hdot(tinv, kb * eg)
        attn = _bdot_nt(q, k) * decay
        s = s_ref[h]
        v_new = u - _bdot(w, s)
        o = _bdot(q * eg, s) + _bdot(attn, v_new)
        g_last = gcol[c - 1:c, :]
        s_ref[h] = s * jnp.exp(g_last) + _bdot((k * jnp.exp(g_last - gcol)).T, v_new)
        o = _rms_norm(o, onw) * _silu(z_ref[0, :, sl])
        o_ref[0, :, sl] = o.astype(o_ref.dtype)


def _gdn_chunk(q, k, v, proj, gc, beta, gct, onorm_w):
    b, l, _ = q.shape
    n = l // CHUNK
    h = GDN_HEADS
    z_block = GDN_CONV_CH // D_VA
    blk = pl.BlockSpec((1, CHUNK, D_QK), lambda bi, i: (bi, i, 0))
    return pl.pallas_call(
        _gdn_chunk_kernel,
        grid=(b, n),
        in_specs=[blk, blk, blk,
                  pl.BlockSpec((1, CHUNK, D_VA), lambda bi, i: (bi, i, z_block)),
                  pl.BlockSpec((1, CHUNK, h), lambda bi, i: (bi, i, 0)),
                  pl.BlockSpec((1, CHUNK, h), lambda bi, i: (bi, i, 0)),
                  pl.BlockSpec((1, 1, h, CHUNK), lambda bi, i: (bi, i, 0, 0)),
                  pl.BlockSpec((1, GDN_DV), lambda bi, i: (0, 0))],
        out_specs=pl.BlockSpec((1, CHUNK, D_VA), lambda bi, i: (bi, i, 0)),
        out_shape=jax.ShapeDtypeStruct((b, l, D_VA), BF16),
        scratch_shapes=[pltpu.VMEM((h, GDN_DK, GDN_DV), F32)],
        compiler_params=_cparams("parallel", "arbitrary"),
        name="gdn_chunk",
    )(q, k, v, proj, gc.reshape(b, l, h), beta.reshape(b, l, h),
      gct.reshape(h, b, n, CHUNK).transpose(1, 2, 0, 3), onorm_w.reshape(1, GDN_DV))


def _gated_deltanet_ln(hid, w_in, conv_w, a_log, dt_bias, onorm_w, w_out, ln_g, ln_b):
    b, l, d = hid.shape
    x = hid.reshape(b * l, d)
    n_main = GDN_CONV_CH + D_VA
    proj = _matmul(x, w_in[:, :n_main].astype(BF16), F32, 512, 1024).reshape(b, l, n_main)
    gc, beta, gct = _gdn_gates(x, w_in[:, n_main:], a_log, dt_bias)
    q, k, v = _gdn_conv(proj, conv_w)
    o = _gdn_chunk(q, k, v, proj, gc, beta, gct, onorm_w)
    out = _matmul_ln(o.reshape(b * l, D_VA), w_out.astype(BF16), x, ln_g, ln_b)
    return out.reshape(b, l, d)


def _rope_lane_tables(l):
    inv = 1.0 / (ROPE_THETA ** (jnp.arange(0, D_ROPE, 2, dtype=F32) / D_ROPE))
    ang = jnp.arange(l, dtype=F32)[:, None] * inv[None, :]
    cos, sin = jnp.cos(ang), jnp.sin(ang)
    zero = jnp.zeros((l, LANES - D_ROPE), F32)
    return (jnp.concatenate([cos, cos, zero], -1),
            jnp.concatenate([-sin, sin, zero], -1))


def _rope_weight_groups(w_rope):
    kdim = w_rope.shape[0]
    half = D_ROPE // 2
    zero = jnp.zeros((kdim, LANES - D_ROPE), w_rope.dtype)
    x1, x2 = w_rope[:, :half], w_rope[:, half:]
    return jnp.concatenate([x1, x2, zero, x2, x1, zero], axis=-1)


def _mla_kv_kernel(x_ref, wd_ref, nw_ref, wu_ref, cos_ref, sin_ref, k_ref, v_ref):
    x = x_ref[0]
    ckv = _bdot(x, wd_ref[...])
    c = _rms_norm(ckv[:, :KV_LORA], nw_ref[...])
    k_rope = (ckv[:, KV_LORA:KV_LORA + LANES] * cos_ref[...]
              + ckv[:, KV_LORA + LANES:] * sin_ref[...]).astype(k_ref.dtype)
    kv = _bdot(c, wu_ref[...])
    per = D_NOPE + D_V
    for h in range(MLA_HEADS):
        k_ref[0, h, :, :D_NOPE] = kv[:, h * per:h * per + D_NOPE].astype(k_ref.dtype)
        k_ref[0, h, :, D_NOPE:] = k_rope
        v_ref[0, h] = kv[:, h * per + D_NOPE:(h + 1) * per].astype(v_ref.dtype)


def _mla_kv(hid, w_dkv, kv_norm_w, w_ukv, cos_t, sin_t, tl=512):
    b, l, d = hid.shape
    tl = min(tl, l)
    wd = jnp.concatenate([w_dkv[:, :KV_LORA], _rope_weight_groups(w_dkv[:, KV_LORA:])], -1).astype(BF16)
    nd = wd.shape[1]
    hh = MLA_HEADS
    return pl.pallas_call(
        _mla_kv_kernel,
        grid=(b, l // tl),
        in_specs=[pl.BlockSpec((1, tl, d), lambda bi, i: (bi, i, 0)),
                  pl.BlockSpec((d, nd), lambda bi, i: (0, 0)),
                  pl.BlockSpec((1, KV_LORA), lambda bi, i: (0, 0)),
                  pl.BlockSpec((KV_LORA, hh * (D_NOPE + D_V)), lambda bi, i: (0, 0)),
                  pl.BlockSpec((tl, LANES), lambda bi, i: (i, 0)),
                  pl.BlockSpec((tl, LANES), lambda bi, i: (i, 0))],
        out_specs=[pl.BlockSpec((1, hh, tl, D_NOPE + LANES), lambda bi, i: (bi, 0, i, 0)),
                   pl.BlockSpec((1, hh, tl, D_V), lambda bi, i: (bi, 0, i, 0))],
        out_shape=[jax.ShapeDtypeStruct((b, hh, l, D_NOPE + LANES), BF16),
                   jax.ShapeDtypeStruct((b, hh, l, D_V), BF16)],
        compiler_params=_cparams("parallel", "parallel"),
        name="mla_kv",
    )(hid, wd, kv_norm_w.reshape(1, KV_LORA), w_ukv.astype(BF16), cos_t, sin_t)


def _mla_cq_kernel(x_ref, w_ref, nw_ref, o_ref):
    o_ref[...] = _rms_norm(_bdot(x_ref[...], w_ref[...]), nw_ref[...]).astype(o_ref.dtype)


def _mla_cq(x, w_dq, qnorm_w, tm=512):
    m, k = x.shape
    tm = min(tm, m)
    return pl.pallas_call(
        _mla_cq_kernel,
        grid=(m // tm,),
        in_specs=[pl.BlockSpec((tm, k), lambda i: (i, 0)),
                  pl.BlockSpec((k, Q_LORA), lambda i: (0, 0)),
                  pl.BlockSpec((1, Q_LORA), lambda i: (0, 0))],
        out_specs=pl.BlockSpec((tm, Q_LORA), lambda i: (i, 0)),
        out_shape=jax.ShapeDtypeStruct((m, Q_LORA), BF16),
        compiler_params=_cparams("parallel"),
        name="mla_cq",
    )(x, w_dq.astype(BF16), qnorm_w.reshape(1, Q_LORA))


def _mla_q_kernel(c_ref, w_ref, cos_ref, sin_ref, q_ref):
    c = c_ref[0]
    scale = (D_NOPE + D_ROPE) ** -0.5
    per = D_NOPE + 2 * LANES
    for h in range(MLA_HEADS):
        qh = _bdot(c, w_ref[:, h * per:(h + 1) * per])
        rope = qh[:, D_NOPE:D_NOPE + LANES] * cos_ref[...] + qh[:, D_NOPE + LANES:] * sin_ref[...]
        q_ref[0, h, :, :D_NOPE] = (qh[:, :D_NOPE] * scale).astype(q_ref.dtype)
        q_ref[0, h, :, D_NOPE:] = (rope * scale).astype(q_ref.dtype)


def _mla_q(cq, w_uq, cos_t, sin_t, tl=512):
    b, l, _ = cq.shape
    tl = min(tl, l)
    hh = MLA_HEADS
    per_in = D_NOPE + D_ROPE
    groups = []
    for h in range(hh):
        wh = w_uq[:, h * per_in:(h + 1) * per_in]
        groups += [wh[:, :D_NOPE], _rope_weight_groups(wh[:, D_NOPE:])]
    w = jnp.concatenate(groups, -1).astype(BF16)
    return pl.pallas_call(
        _mla_q_kernel,
        grid=(b, l // tl),
        in_specs=[pl.BlockSpec((1, tl, Q_LORA), lambda bi, i: (bi, i, 0)),
                  pl.BlockSpec(w.shape, lambda bi, i: (0, 0)),
                  pl.BlockSpec((tl, LANES), lambda bi, i: (i, 0)),
                  pl.BlockSpec((tl, LANES), lambda bi, i: (i, 0))],
        out_specs=pl.BlockSpec((1, hh, tl, D_NOPE + LANES), lambda bi, i: (bi, 0, i, 0)),
        out_shape=jax.ShapeDtypeStruct((b, hh, l, D_NOPE + LANES), BF16),
        compiler_params=_cparams("parallel", "parallel"),
        name="mla_q",
    )(cq, w, cos_t, sin_t)


def _mla_attn_kernel(q_ref, k_ref, v_ref, o_ref, *, tk):
    qi = pl.program_id(2)
    q = q_ref[0, 0]
    tq = q.shape[0]
    per = tq // tk

    def scores(j):
        start = pl.multiple_of(j * tk, tk)
        kb = k_ref[0, 0, pl.ds(start, tk), :]
        vb = v_ref[0, 0, pl.ds(start, tk), :]
        return _bdot_nt(q, kb), vb

    def update(carry, s, vb):
        m, l, acc = carry
        m_new = jnp.maximum(m, jnp.max(s, -1, keepdims=True))
        alpha = jnp.exp(m - m_new)
        p = jnp.exp(s - m_new)
        l = alpha * l + jnp.sum(p, -1, keepdims=True)
        acc = alpha * acc + _bdot(p, vb)
        return m_new, l, acc

    def full_block(j, carry):
        s, vb = scores(j)
        return update(carry, s, vb)

    carry = (jnp.full((tq, 1), -jnp.inf, F32), jnp.zeros((tq, 1), F32), jnp.zeros((tq, D_V), F32))
    carry = lax.fori_loop(0, qi * per, full_block, carry)
    qpos = lax.broadcasted_iota(jnp.int32, (tq, tk), 0)
    kpos = lax.broadcasted_iota(jnp.int32, (tq, tk), 1)
    for d in range(per):
        s, vb = scores(qi * per + d)
        s = jnp.where(kpos + d * tk <= qpos, s, -jnp.inf)
        carry = update(carry, s, vb)
    _, l, acc = carry
    o_ref[0] = (acc / l).astype(o_ref.dtype)


def _mla_attn(q, k, v, tq=512, tk=512):
    b, hh, l, dq = q.shape
    tq = min(tq, l)
    tk = min(tk, tq)
    return pl.pallas_call(
        functools.partial(_mla_attn_kernel, tk=tk),
        grid=(b, hh, l // tq),
        in_specs=[pl.BlockSpec((1, 1, tq, dq), lambda bi, h, i: (bi, h, i, 0)),
                  pl.BlockSpec((1, 1, l, dq), lambda bi, h, i: (bi, h, 0, 0)),
                  pl.BlockSpec((1, 1, l, D_V), lambda bi, h, i: (bi, h, 0, 0))],
        out_specs=pl.BlockSpec((1, tq, D_V), lambda bi, h, i: (bi, i, h)),
        out_shape=jax.ShapeDtypeStruct((b, l, hh * D_V), BF16),
        compiler_params=_cparams("parallel", "parallel", "parallel"),
        name="mla_attn",
    )(q, k, v)


def _mla_ln(hid, w_dq, qnorm_w, w_uq, w_o, k, v, cos_t, sin_t, ln_g, ln_b):
    b, l, d = hid.shape
    x = hid.reshape(b * l, d)
    cq = _mla_cq(x, w_dq, qnorm_w).reshape(b, l, Q_LORA)
    q = _mla_q(cq, w_uq, cos_t, sin_t)
    o = _mla_attn(q, k, v)
    out = _matmul_ln(o.reshape(b * l, MLA_HEADS * D_V), w_o.astype(BF16), x, ln_g, ln_b)
    return out.reshape(b, l, d)


def _first_argmax(x, ids, n):
    m = jnp.max(x, axis=0, keepdims=True)
    first = jnp.min(jnp.where(x == m, ids, n), axis=0, keepdims=True)
    return m, first


def _router_kernel(x_ref, wt_ref, bias_ref, idx_ref, w_ref):
    x = x_ref[...]
    t = x.shape[0]
    logits = lax.dot_general(wt_ref[...], x, (((1,), (1,)), ((), ())),
                             precision=HIGHEST, preferred_element_type=F32)
    scores = jax.nn.sigmoid(logits)
    biased = scores + bias_ref[...]
    neg = -jnp.inf
    sub = lax.broadcasted_iota(jnp.int32, (GROUP_SIZE, t), 0).astype(F32)
    gscores = []
    for g in range(N_GROUPS):
        xg = biased[g * GROUP_SIZE:(g + 1) * GROUP_SIZE, :]
        m1, i1 = _first_argmax(xg, sub, float(GROUP_SIZE))
        m2 = jnp.max(jnp.where(sub == i1, neg, xg), axis=0, keepdims=True)
        gscores.append(m1 + m2)
    gs = jnp.concatenate(gscores, axis=0)
    gid = lax.broadcasted_iota(jnp.int32, (N_GROUPS, t), 0).astype(F32)
    gsel = jnp.zeros((N_GROUPS, t), F32)
    for _ in range(TOPK_GROUPS):
        _, gi = _first_argmax(gs, gid, float(N_GROUPS))
        hit = gid == gi
        gsel = jnp.where(hit, 1.0, gsel)
        gs = jnp.where(hit, neg, gs)
    eid = lax.broadcasted_iota(jnp.int32, (N_EXPERTS, t), 0).astype(F32)
    allowed = jnp.concatenate(
        [jnp.broadcast_to(gsel[g:g + 1, :], (GROUP_SIZE, t)) for g in range(N_GROUPS)], axis=0)
    cand = jnp.where(allowed > 0.0, biased, neg)
    idxs, ws = [], []
    for _ in range(TOP_K):
        _, ei = _first_argmax(cand, eid, float(N_EXPERTS))
        hit = eid == ei
        idxs.append(ei)
        ws.append(jnp.sum(jnp.where(hit, scores, 0.0), axis=0, keepdims=True))
        cand = jnp.where(hit, neg, cand)
    w = jnp.concatenate(ws, axis=0)
    w = w / (jnp.sum(w, axis=0, keepdims=True) + 1e-20) * ROUTED_SCALE
    idx_ref[...] = jnp.concatenate(idxs, axis=0).astype(jnp.int32)
    w_ref[...] = w


def _router(x, w_router, bias, tr=512):
    t, d = x.shape
    tr = min(tr, t)
    return pl.pallas_call(
        _router_kernel,
        grid=(t // tr,),
        in_specs=[pl.BlockSpec((tr, d), lambda i: (i, 0)),
                  pl.BlockSpec((N_EXPERTS, d), lambda i: (0, 0)),
                  pl.BlockSpec((N_EXPERTS, 1), lambda i: (0, 0))],
        out_specs=[pl.BlockSpec((TOP_K, tr), lambda i: (0, i)),
                   pl.BlockSpec((TOP_K, tr), lambda i: (0, i))],
        out_shape=[jax.ShapeDtypeStruct((TOP_K, t), jnp.int32),
                   jax.ShapeDtypeStruct((TOP_K, t), F32)],
        compiler_params=_cparams("parallel"),
        name="moe_router",
    )(x, w_router.T, bias.reshape(N_EXPERTS, 1))


def _dispatch_plan(idx, w, tile_tokens):
    t = idx.shape[0]
    n_tiles = t // tile_tokens
    nb = tile_tokens * TOP_K // EXPERT_ROWS + N_EXPERTS
    hit = (idx[:, :, None] == jnp.arange(N_EXPERTS, dtype=jnp.int32)).any(1).astype(jnp.int32)
    hit = hit.reshape(n_tiles, tile_tokens, N_EXPERTS)
    csum = jnp.cumsum(hit, axis=1)
    rank = (csum - hit).reshape(t, N_EXPERTS)
    counts = csum[:, -1, :]
    padded = (counts + EXPERT_ROWS - 1) // EXPERT_ROWS * EXPERT_ROWS
    pad_end = jnp.cumsum(padded, axis=1)
    pad_start = pad_end - padded
    tile_id = jnp.arange(t, dtype=jnp.int32) // tile_tokens
    pos = (jnp.take_along_axis(pad_start[tile_id], idx, axis=1)
           + jnp.take_along_axis(rank, idx, axis=1))
    flat = (tile_id[:, None] * (nb * EXPERT_ROWS) + pos).reshape(-1)
    local = jnp.broadcast_to((jnp.arange(t, dtype=jnp.int32) % tile_tokens)[:, None], idx.shape).reshape(-1)
    n_slots = n_tiles * nb * EXPERT_ROWS
    slot_tok = jnp.full((n_slots,), tile_tokens, jnp.int32).at[flat].set(local, unique_indices=True)
    slot_w = jnp.zeros((n_slots,), F32).at[flat].set(w.reshape(-1), unique_indices=True)
    nblk = (pad_end[:, -1] // EXPERT_ROWS).astype(jnp.int32)
    starts = jnp.arange(nb, dtype=jnp.int32) * EXPERT_ROWS
    blk_e = jax.vmap(lambda pe: jnp.searchsorted(pe, starts, side='right'))(pad_end).astype(jnp.int32)
    blk_e = jnp.minimum(blk_e, N_EXPERTS - 1)
    last = jnp.take_along_axis(blk_e, jnp.maximum(nblk - 1, 0)[:, None], axis=1)
    blk_e = jnp.where(starts[None, :] // EXPERT_ROWS < nblk[:, None], blk_e, last)
    return (slot_tok.reshape(n_tiles * nb, 1, EXPERT_ROWS), slot_w.reshape(n_tiles * nb, 1, EXPERT_ROWS),
            blk_e.reshape(-1), nblk, nb)


def _experts_kernel(blk_e_ref, nblk_ref, tok_ref, sw_ref, x_ref, wg_ref, wu_ref, wd_ref,
                    acc_ref, xs_ref, ys_ref, *, tile_tokens):
    i = pl.program_id(0)
    j = pl.program_id(1)
    rows = EXPERT_ROWS
    pitch = STAGE_PITCH

    @pl.when(j == 0)
    def _():
        acc_ref[...] = jnp.zeros_like(acc_ref)

    @pl.when(j < nblk_ref[i])
    def _():
        for r in range(rows):
            t = jnp.minimum(tok_ref[0, 0, r], tile_tokens - 1)
            slab = x_ref[pl.ds(pl.multiple_of(t * SUBLANES, SUBLANES), SUBLANES), :]
            xs_ref[pl.ds(r, ROW_CHUNKS, stride=pitch), :] = slab
        x = jnp.concatenate([xs_ref[pl.ds(c * pitch, rows), :] for c in range(ROW_CHUNKS)], axis=1)
        hid = _silu(_bdot(x, wg_ref[0])) * _bdot(x, wu_ref[0])
        y = _bdot(hid, wd_ref[0])
        for c in range(ROW_CHUNKS):
            ys_ref[pl.ds(c * pitch, rows), :] = y[:, c * LANES:(c + 1) * LANES]
        for r0 in range(0, rows, SCATTER_BATCH):
            new = []
            for r in range(r0, r0 + SCATTER_BATCH):
                at = pl.ds(pl.multiple_of(tok_ref[0, 0, r] * SUBLANES, SUBLANES), SUBLANES)
                new.append((at, acc_ref[0, at, :]
                            + sw_ref[0, 0, r] * ys_ref[pl.ds(r, ROW_CHUNKS, stride=pitch), :]))
            for at, val in new:
                acc_ref[0, at, :] = val


def _routed_experts(x, idx, w, w_gate, w_up, w_down, tile_tokens=2048):
    t, d = x.shape
    tile_tokens = min(tile_tokens, t)
    n_tiles = t // tile_tokens
    slot_tok, slot_w, blk_e, nblk, nb = _dispatch_plan(idx, w, tile_tokens)
    x_slabs = x.reshape(t * ROW_CHUNKS, LANES)
    acc_rows = (tile_tokens + 1) * SUBLANES

    def blk(i, j, nblk_ref):
        return i * nb + jnp.minimum(j, jnp.maximum(nblk_ref[i] - 1, 0))

    grid_spec = pltpu.PrefetchScalarGridSpec(
        num_scalar_prefetch=2,
        grid=(n_tiles, nb),
        in_specs=[
            pl.BlockSpec((1, 1, EXPERT_ROWS), lambda i, j, be, nk: (blk(i, j, nk), 0, 0),
                         memory_space=pltpu.SMEM),
            pl.BlockSpec((1, 1, EXPERT_ROWS), lambda i, j, be, nk: (blk(i, j, nk), 0, 0),
                         memory_space=pltpu.SMEM),
            pl.BlockSpec((tile_tokens * ROW_CHUNKS, LANES), lambda i, j, be, nk: (i, 0)),
            pl.BlockSpec((1, d, D_EXPERT), lambda i, j, be, nk: (be[i * nb + j], 0, 0)),
            pl.BlockSpec((1, d, D_EXPERT), lambda i, j, be, nk: (be[i * nb + j], 0, 0)),
            pl.BlockSpec((1, D_EXPERT, d), lambda i, j, be, nk: (be[i * nb + j], 0, 0)),
        ],
        out_specs=pl.BlockSpec((1, acc_rows, LANES), lambda i, j, be, nk: (i, 0, 0)),
        scratch_shapes=[pltpu.VMEM((ROW_CHUNKS * STAGE_PITCH, LANES), F32),
                        pltpu.VMEM((ROW_CHUNKS * STAGE_PITCH, LANES), F32)],
    )
    acc = pl.pallas_call(
        functools.partial(_experts_kernel, tile_tokens=tile_tokens),
        grid_spec=grid_spec,
        out_shape=jax.ShapeDtypeStruct((n_tiles, acc_rows, LANES), F32),
        compiler_params=_cparams("parallel", "arbitrary"),
        name="moe_experts",
    )(blk_e, nblk, slot_tok, slot_w, x_slabs, w_gate, w_up, w_down)
    return acc[:, :tile_tokens * ROW_CHUNKS, :].reshape(t, d)


def _moe_out_kernel(x_ref, r_ref, wg_ref, wu_ref, wd_ref, g_ref, b_ref, o_ref):
    x = x_ref[...]
    hid = _silu(_bdot(x, wg_ref[...])) * _bdot(x, wu_ref[...])
    y = r_ref[...] + _bdot(hid, wd_ref[...])
    o_ref[...] = _layer_norm(DN_ALPHA * x + y, g_ref[...], b_ref[...])


def _moe_out(x, routed, ws_gate, ws_up, ws_down, g, b, tm=512):
    t, d = x.shape
    tm = min(tm, t)
    return pl.pallas_call(
        _moe_out_kernel,
        grid=(t // tm,),
        in_specs=[pl.BlockSpec((tm, d), lambda i: (i, 0)),
                  pl.BlockSpec((tm, d), lambda i: (i, 0)),
                  pl.BlockSpec((d, D_SHARED), lambda i: (0, 0)),
                  pl.BlockSpec((d, D_SHARED), lambda i: (0, 0)),
                  pl.BlockSpec((D_SHARED, d), lambda i: (0, 0)),
                  pl.BlockSpec((1, d), lambda i: (0, 0)),
                  pl.BlockSpec((1, d), lambda i: (0, 0))],
        out_specs=pl.BlockSpec((tm, d), lambda i: (i, 0)),
        out_shape=jax.ShapeDtypeStruct((t, d), F32),
        compiler_params=_cparams("parallel"),
        name="moe_out",
    )(x, routed, ws_gate.astype(BF16), ws_up.astype(BF16), ws_down.astype(BF16),
      g.reshape(1, d), b.reshape(1, d))


def _moe_ln(hid, w_router, router_bias, w_gate, w_up, w_down, ws_gate, ws_up, ws_down, ln_g, ln_b):
    b, l, d = hid.shape
    x = hid.reshape(b * l, d)
    idx_t, w_t = _router(x, w_router, router_bias)
    routed = _routed_experts(x, idx_t.T, w_t.T, w_gate.astype(BF16), w_up.astype(BF16), w_down.astype(BF16))
    return _moe_out(x, routed, ws_gate, ws_up, ws_down, ln_g, ln_b).reshape(b, l, d)


def kernel(x, a_w_in, a_conv_w, a_A_log, a_dt_bias, a_onorm_w, a_w_out, b_w_dq, b_qnorm_w, b_w_uq, b_w_o,
           kv_w_dkv, kv_norm_w, kv_w_ukv, ln1_g, ln1_b, ln2_g, ln2_b, moe_w_router, moe_router_bias,
           moe_w_gate, moe_w_up, moe_w_down, moe_ws_gate, moe_ws_up, moe_ws_down):
    l = x.shape[1]
    cos_t, sin_t = _rope_lane_tables(l)
    h = x
    k = v = None
    for layer in range(DEPTH):
        if layer < N_A_LAYERS:
            i = layer
            h = _gated_deltanet_ln(h, a_w_in[i], a_conv_w[i], a_A_log[i], a_dt_bias[i], a_onorm_w[i],
                                   a_w_out[i], ln1_g[layer], ln1_b[layer])
        else:
            i = layer - N_A_LAYERS
            h = _mla_ln(h, b_w_dq[i], b_qnorm_w[i], b_w_uq[i], b_w_o[i], k, v, cos_t, sin_t,
                        ln1_g[layer], ln1_b[layer])
        h = _moe_ln(h, moe_w_router[layer], moe_router_bias[layer], moe_w_gate[layer], moe_w_up[layer],
                    moe_w_down[layer], moe_ws_gate[layer], moe_ws_up[layer], moe_ws_down[layer],
                    ln2_g[layer], ln2_b[layer])
        if layer == N_A_LAYERS - 1:
            k, v = _mla_kv(h, kv_w_dkv, kv_norm_w, kv_w_ukv, cos_t, sin_t)
    return h
```

```python
import functools

import jax
import jax.numpy as jnp
import numpy as np
from jax import lax
from jax.experimental import pallas as pl
from jax.experimental.pallas import tpu as pltpu

F32 = jnp.float32
BF16 = jnp.bfloat16
HIGHEST = lax.Precision.HIGHEST

D_MODEL = 1024
DEPTH = 4
N_A_LAYERS = DEPTH // 2
GDN_HEADS = 8
GDN_DK = 128
GDN_DV = 128
D_QK = GDN_HEADS * GDN_DK
D_VA = GDN_HEADS * GDN_DV
GDN_CONV_CH = 2 * D_QK + D_VA
CONV_WIDTH = 4
CHUNK = 64
MLA_HEADS = 8
Q_LORA = 512
KV_LORA = 256
D_NOPE = 128
D_ROPE = 64
D_V = 128
ROPE_THETA = 10000.0
N_EXPERTS = 64
TOP_K = 8
N_GROUPS = 8
GROUP_SIZE = N_EXPERTS // N_GROUPS
TOPK_GROUPS = 4
D_EXPERT = 256
D_SHARED = 256
ROUTED_SCALE = 2.5
DN_ALPHA = (2 * DEPTH) ** 0.25
EPS = 1e-6
LN_EPS = 1e-5

LANES = 128
SUBLANES = 8
VMEM_LIMIT_BYTES = 56 * 1024 * 1024

EXPERT_ROWS = 128
ROW_CHUNKS = D_MODEL // LANES
STAGE_PITCH = EXPERT_ROWS + 1
SCATTER_BATCH = 16


def _cparams(*sem):
    return pltpu.CompilerParams(dimension_semantics=sem, vmem_limit_bytes=VMEM_LIMIT_BYTES)


def _silu(x):
    return x * jax.nn.sigmoid(x)


def _layer_norm(x, g, b):
    mu = jnp.mean(x, -1, keepdims=True)
    xc = x - mu
    var = jnp.mean(xc * xc, -1, keepdims=True)
    return xc * lax.rsqrt(var + LN_EPS) * g + b


def _rms_norm(x, w):
    return x * lax.rsqrt(jnp.mean(x * x, -1, keepdims=True) + EPS) * w


def _bdot(a, b):
    return jnp.dot(a.astype(BF16), b.astype(BF16), preferred_element_type=F32)


def _bdot_nt(a, b):
    return lax.dot_general(a.astype(BF16), b.astype(BF16), (((1,), (1,)), ((), ())),
                           preferred_element_type=F32)


def _hdot(a, b):
    return jnp.dot(a, b, precision=HIGHEST, preferred_element_type=F32)


def _matmul_kernel(x_ref, w_ref, o_ref):
    o_ref[...] = _bdot(x_ref[...], w_ref[...]).astype(o_ref.dtype)


def _matmul(x, w, out_dtype, tm, tn):
    m, k = x.shape
    n = w.shape[1]
    tm = min(tm, m)
    tn = min(tn, n)
    return pl.pallas_call(
        _matmul_kernel,
        grid=(m // tm, n // tn),
        in_specs=[pl.BlockSpec((tm, k), lambda i, j: (i, 0)),
                  pl.BlockSpec((k, tn), lambda i, j: (0, j))],
        out_specs=pl.BlockSpec((tm, tn), lambda i, j: (i, j)),
        out_shape=jax.ShapeDtypeStruct((m, n), out_dtype),
        compiler_params=_cparams("parallel", "parallel"),
        name="matmul",
    )(x, w)


def _matmul_ln_kernel(x_ref, w_ref, r_ref, g_ref, b_ref, o_ref):
    mix = _bdot(x_ref[...], w_ref[...])
    o_ref[...] = _layer_norm(DN_ALPHA * r_ref[...] + mix, g_ref[...], b_ref[...])


def _matmul_ln(x, w, res, g, b, tm=512):
    m, k = x.shape
    n = w.shape[1]
    tm = min(tm, m)
    return pl.pallas_call(
        _matmul_ln_kernel,
        grid=(m // tm,),
        in_specs=[pl.BlockSpec((tm, k), lambda i: (i, 0)),
                  pl.BlockSpec((k, n), lambda i: (0, 0)),
                  pl.BlockSpec((tm, n), lambda i: (i, 0)),
                  pl.BlockSpec((1, n), lambda i: (0, 0)),
                  pl.BlockSpec((1, n), lambda i: (0, 0))],
        out_specs=pl.BlockSpec((tm, n), lambda i: (i, 0)),
        out_shape=jax.ShapeDtypeStruct((m, n), F32),
        compiler_params=_cparams("parallel"),
        name="matmul_ln",
    )(x, w, res, g.reshape(1, n), b.reshape(1, n))


def _softplus(x):
    return jnp.maximum(x, 0.0) + jnp.log1p(jnp.exp(-jnp.abs(x)))


def _gdn_gates_kernel(x_ref, wab_ref, wabt_ref, alog_ref, dtb_ref, alogt_ref, dtbt_ref,
                      gc_ref, beta_ref, gct_ref):
    x = x_ref[...]
    tl = x.shape[0]
    h = GDN_HEADS
    ab = _hdot(x, wab_ref[...])
    abt = lax.dot_general(wabt_ref[...], x, (((1,), (1,)), ((), ())),
                          precision=HIGHEST, preferred_element_type=F32)
    g = -jnp.exp(alog_ref[...]) * _softplus(ab[:, :h] + dtb_ref[...])
    gt = -jnp.exp(alogt_ref[...]) * _softplus(abt[:h, :] + dtbt_ref[...])
    beta_ref[...] = jax.nn.sigmoid(ab[:, h:2 * h])
    row = lax.broadcasted_iota(jnp.int32, (tl, tl), 0)
    col = lax.broadcasted_iota(jnp.int32, (tl, tl), 1)
    shift = CHUNK.bit_length() - 1
    same = (row >> shift) == (col >> shift)
    lower = jnp.where(same & (col <= row), 1.0, 0.0).astype(F32)
    upper = jnp.where(same & (row <= col), 1.0, 0.0).astype(F32)
    gc_ref[...] = _hdot(lower, g)
    gct_ref[...] = _hdot(gt, upper)


def _gdn_gates(x, w_ab, a_log, dt_bias, tl=512):
    t = x.shape[0]
    tl = min(tl, t)
    h = GDN_HEADS
    wab = jnp.zeros((D_MODEL, LANES), F32).at[:, :2 * h].set(w_ab)
    wabt = w_ab.T
    return pl.pallas_call(
        _gdn_gates_kernel,
        grid=(t // tl,),
        in_specs=[pl.BlockSpec((tl, D_MODEL), lambda i: (i, 0)),
                  pl.BlockSpec((D_MODEL, LANES), lambda i: (0, 0)),
                  pl.BlockSpec((2 * h, D_MODEL), lambda i: (0, 0)),
                  pl.BlockSpec((1, h), lambda i: (0, 0)),
                  pl.BlockSpec((1, h), lambda i: (0, 0)),
                  pl.BlockSpec((h, 1), lambda i: (0, 0)),
                  pl.BlockSpec((h, 1), lambda i: (0, 0))],
        out_specs=[pl.BlockSpec((tl, h), lambda i: (i, 0)),
                   pl.BlockSpec((tl, h), lambda i: (i, 0)),
                   pl.BlockSpec((h, tl), lambda i: (0, i))],
        out_shape=[jax.ShapeDtypeStruct((t, h), F32),
                   jax.ShapeDtypeStruct((t, h), F32),
                   jax.ShapeDtypeStruct((h, t), F32)],
        compiler_params=_cparams("parallel"),
        name="gdn_gates",
    )(x, wab, wabt, a_log.reshape(1, h), dt_bias.reshape(1, h),
      a_log.reshape(h, 1), dt_bias.reshape(h, 1))


def _gdn_conv_kernel(cur_ref, prev_ref, w_ref, q_ref, k_ref, v_ref):
    i = pl.program_id(1)
    x = cur_ref[0]
    tl = x.shape[0]
    prev = jnp.where(i > 0, prev_ref[0], 0.0)
    xx = jnp.concatenate([prev, x], axis=0)
    w = w_ref[...]
    y = None
    for j in range(CONV_WIDTH):
        off = SUBLANES - (CONV_WIDTH - 1) + j
        term = xx[off:off + tl, :] * w[j:j + 1, :]
        y = term if y is None else y + term
    y = _silu(y)
    for h in range(GDN_HEADS):
        sl = slice(h * GDN_DK, (h + 1) * GDN_DK)
        qh = y[:, sl]
        q_ref[0, :, sl] = qh * lax.rsqrt(jnp.sum(qh * qh, -1, keepdims=True) + EPS) * (GDN_DK ** -0.5)
        kh = y[:, D_QK + h * GDN_DK:D_QK + (h + 1) * GDN_DK]
        k_ref[0, :, sl] = kh * lax.rsqrt(jnp.sum(kh * kh, -1, keepdims=True) + EPS)
    v_ref[0] = y[:, 2 * D_QK:]


def _gdn_conv(proj, conv_w, tl=256):
    b, l, _ = proj.shape
    tl = min(tl, l)
    c3 = GDN_CONV_CH
    per = tl // SUBLANES
    out = jax.ShapeDtypeStruct((b, l, D_QK), F32)
    return pl.pallas_call(
        _gdn_conv_kernel,
        grid=(b, l // tl),
        in_specs=[pl.BlockSpec((1, tl, c3), lambda bi, i: (bi, i, 0)),
                  pl.BlockSpec((1, SUBLANES, c3), lambda bi, i: (bi, jnp.maximum(i * per - 1, 0), 0)),
                  pl.BlockSpec((CONV_WIDTH, c3), lambda bi, i: (0, 0))],
        out_specs=[pl.BlockSpec((1, tl, D_QK), lambda bi, i: (bi, i, 0))] * 3,
        out_shape=[out, out, out],
        compiler_params=_cparams("parallel", "parallel"),
        name="gdn_conv",
    )(proj, proj, conv_w)


def _split_bf16(x):
    hi = x.astype(BF16)
    lo = (x - hi.astype(F32)).astype(BF16)
    return hi, lo


def _dot_split(a, b):
    (ah, al), (bh, bl) = a, b
    dot = functools.partial(jnp.dot, preferred_element_type=F32)
    return dot(ah, bh) + (dot(ah, bl) + dot(al, bh))


def _gdn_prep_kernel(q_ref, k_ref, v_ref, gc_ref, beta_ref, gct_ref, wq_ref, u_ref, attn_ref, kgt_ref,
                     *, chunks):
    c = CHUNK
    row = lax.broadcasted_iota(jnp.int32, (c, c), 0)
    col = lax.broadcasted_iota(jnp.int32, (c, c), 1)
    incl = row >= col
    strict = row > col
    eye = jnp.where(row == col, 1.0, 0.0).astype(F32)
    items = [(ci, h) for ci in range(chunks) for h in range(GDN_HEADS)]

    st = {}
    for ci, h in items:
        rows = slice(ci * c, (ci + 1) * c)
        sl = slice(h * GDN_DK, (h + 1) * GDN_DK)
        q = q_ref[0, rows, sl]
        k = k_ref[0, rows, sl]
        v = v_ref[0, rows, sl]
        gcol = gc_ref[0, rows, h:h + 1]
        grow = gct_ref[0, ci, h:h + 1, :]
        bcol = beta_ref[0, rows, h:h + 1]
        decay = jnp.where(incl, jnp.exp(jnp.where(incl, gcol - grow, 0.0)), 0.0)
        kb = k * bcol
        eg = jnp.exp(gcol)
        g_last = gcol[c - 1:c, :]
        sc = _bdot_nt(jnp.concatenate([kb, q], axis=0), k)
        a = jnp.where(strict, sc[:c] * decay, 0.0)
        attn_ref[0, ci, h] = (sc[c:] * decay).astype(attn_ref.dtype)
        kgt_ref[0, ci, h] = (k * jnp.exp(g_last - gcol)).T.astype(kgt_ref.dtype)
        wq_ref[0, ci, h, c:, :] = (q * eg).astype(wq_ref.dtype)
        st[ci, h] = dict(m=-a, rhs=jnp.concatenate([v * bcol, kb * eg], axis=1))

    for it in items:
        st[it]["p"] = eye + st[it]["m"]
    span = 2
    while span < c:
        for it in items:
            ms = _split_bf16(st[it]["m"])
            st[it]["m"] = _dot_split(ms, ms)
        for it in items:
            st[it]["p"] = st[it]["p"] + _dot_split(_split_bf16(st[it]["p"]), _split_bf16(st[it]["m"]))
        span *= 2

    for ci, h in items:
        sol = _dot_split(_split_bf16(st[ci, h]["p"]), _split_bf16(st[ci, h]["rhs"]))
        u_ref[0, ci * c:(ci + 1) * c, h * GDN_DV:(h + 1) * GDN_DV] = sol[:, :GDN_DV]
        wq_ref[0, ci, h, :c, :] = sol[:, GDN_DV:].astype(wq_ref.dtype)


def _gdn_prep(q, k, v, gc, beta, gct, chunks=2):
    b, l, _ = q.shape
    n = l // CHUNK
    chunks = min(chunks, n)
    h = GDN_HEADS
    rows = chunks * CHUNK
    blk = pl.BlockSpec((1, rows, D_QK), lambda bi, i: (bi, i, 0))
    gate = pl.BlockSpec((1, rows, h), lambda bi, i: (bi, i, 0))
    return pl.pallas_call(
        functools.partial(_gdn_prep_kernel, chunks=chunks),
        grid=(b, n // chunks),
        in_specs=[blk, blk, blk, gate, gate,
                  pl.BlockSpec((1, chunks, h, CHUNK), lambda bi, i: (bi, i, 0, 0))],
        out_specs=[pl.BlockSpec((1, chunks, h, 2 * CHUNK, GDN_DK), lambda bi, i: (bi, i, 0, 0, 0)),
                   pl.BlockSpec((1, rows, D_VA), lambda bi, i: (bi, i, 0)),
                   pl.BlockSpec((1, chunks, h, CHUNK, CHUNK), lambda bi, i: (bi, i, 0, 0, 0)),
                   pl.BlockSpec((1, chunks, h, GDN_DK, CHUNK), lambda bi, i: (bi, i, 0, 0, 0))],
        out_shape=[jax.ShapeDtypeStruct((b, n, h, 2 * CHUNK, GDN_DK), BF16),
                   jax.ShapeDtypeStruct((b, l, D_VA), F32),
                   jax.ShapeDtypeStruct((b, n, h, CHUNK, CHUNK), BF16),
                   jax.ShapeDtypeStruct((b, n, h, GDN_DK, CHUNK), BF16)],
        compiler_params=_cparams("parallel", "parallel"),
        name="gdn_prep",
    )(q, k, v, gc, beta, gct)


def _gdn_state_kernel(wq_ref, u_ref, attn_ref, kgt_ref, gc_ref, z_ref, onw_ref, o_ref, s_ref, *, chunks):
    @pl.when(pl.program_id(1) == 0)
    def _():
        s_ref[...] = jnp.zeros_like(s_ref)

    c = CHUNK
    onw = onw_ref[...]
    dot = functools.partial(jnp.dot, preferred_element_type=F32)
    heads = range(GDN_HEADS)
    for ci in range(chunks):
        rows = slice(ci * c, (ci + 1) * c)
        s_old = [s_ref[h] for h in heads]
        r = [dot(wq_ref[0, ci, h], s_old[h].astype(BF16)) for h in heads]
        v_new = [(u_ref[0, rows, h * GDN_DV:(h + 1) * GDN_DV] - r[h][:c]).astype(BF16) for h in heads]
        for h in heads:
            decay_last = jnp.exp(gc_ref[0, (ci + 1) * c - 1:(ci + 1) * c, h:h + 1])
            s_ref[h] = s_old[h] * decay_last + dot(kgt_ref[0, ci, h], v_new[h])
        for h in heads:
            sl = slice(h * GDN_DV, (h + 1) * GDN_DV)
            o = r[h][c:] + dot(attn_ref[0, ci, h], v_new[h])
            o_ref[0, rows, sl] = (_rms_norm(o, onw) * _silu(z_ref[0, rows, sl])).astype(o_ref.dtype)


def _gdn_state(wq, u, attn, kgt, gc, proj, onorm_w, chunks=4):
    b, n, h = wq.shape[:3]
    l = n * CHUNK
    chunks = min(chunks, n)
    rows = chunks * CHUNK
    z_block = GDN_CONV_CH // D_VA
    return pl.pallas_call(
        functools.partial(_gdn_state_kernel, chunks=chunks),
        grid=(b, n // chunks),
        in_specs=[pl.BlockSpec((1, chunks, h, 2 * CHUNK, GDN_DK), lambda bi, i: (bi, i, 0, 0, 0)),
                  pl.BlockSpec((1, rows, D_VA), lambda bi, i: (bi, i, 0)),
                  pl.BlockSpec((1, chunks, h, CHUNK, CHUNK), lambda bi, i: (bi, i, 0, 0, 0)),
                  pl.BlockSpec((1, chunks, h, GDN_DK, CHUNK), lambda bi, i: (bi, i, 0, 0, 0)),
                  pl.BlockSpec((1, rows, h), lambda bi, i: (bi, i, 0)),
                  pl.BlockSpec((1, rows, D_VA), lambda bi, i: (bi, i, z_block)),
                  pl.BlockSpec((1, GDN_DV), lambda bi, i: (0, 0))],
        out_specs=pl.BlockSpec((1, rows, D_VA), lambda bi, i: (bi, i, 0)),
        out_shape=jax.ShapeDtypeStruct((b, l, D_VA), BF16),
        scratch_shapes=[pltpu.VMEM((h, GDN_DK, GDN_DV), F32)],
        compiler_params=_cparams("parallel", "arbitrary"),
        name="gdn_state",
    )(wq, u, attn, kgt, gc, proj, onorm_w.reshape(1, GDN_DV))


def _gated_deltanet_ln(hid, w_in, conv_w, a_log, dt_bias, onorm_w, w_out, ln_g, ln_b):
    b, l, d = hid.shape
    h = GDN_HEADS
    n = l // CHUNK
    x = hid.reshape(b * l, d)
    n_main = GDN_CONV_CH + D_VA
    proj = _matmul(x, w_in[:, :n_main].astype(BF16), F32, 512, 1024).reshape(b, l, n_main)
    gc, beta, gct = _gdn_gates(x, w_in[:, n_main:], a_log, dt_bias)
    gc = gc.reshape(b, l, h)
    gct = gct.reshape(h, b, n, CHUNK).transpose(1, 2, 0, 3)
    q, k, v = _gdn_conv(proj, conv_w)
    wq, u, attn, kgt = _gdn_prep(q, k, v, gc, beta.reshape(b, l, h), gct)
    o = _gdn_state(wq, u, attn, kgt, gc, proj, onorm_w)
    out = _matmul_ln(o.reshape(b * l, D_VA), w_out.astype(BF16), x, ln_g, ln_b)
    return out.reshape(b, l, d)


def _rope_lane_tables(l):
    inv = 1.0 / (ROPE_THETA ** (jnp.arange(0, D_ROPE, 2, dtype=F32) / D_ROPE))
    ang = jnp.arange(l, dtype=F32)[:, None] * inv[None, :]
    cos, sin = jnp.cos(ang), jnp.sin(ang)
    zero = jnp.zeros((l, LANES - D_ROPE), F32)
    return (jnp.concatenate([cos, cos, zero], -1),
            jnp.concatenate([-sin, sin, zero], -1))


def _rope_weight_groups(w_rope):
    kdim = w_rope.shape[0]
    half = D_ROPE // 2
    zero = jnp.zeros((kdim, LANES - D_ROPE), w_rope.dtype)
    x1, x2 = w_rope[:, :half], w_rope[:, half:]
    return jnp.concatenate([x1, x2, zero, x2, x1, zero], axis=-1)


def _mla_kv_kernel(x_ref, wd_ref, nw_ref, wu_ref, cos_ref, sin_ref, k_ref, v_ref):
    x = x_ref[0]
    ckv = _bdot(x, wd_ref[...])
    c = _rms_norm(ckv[:, :KV_LORA], nw_ref[...])
    k_rope = (ckv[:, KV_LORA:KV_LORA + LANES] * cos_ref[...]
              + ckv[:, KV_LORA + LANES:] * sin_ref[...]).astype(k_ref.dtype)
    kv = _bdot(c, wu_ref[...])
    per = D_NOPE + D_V
    for h in range(MLA_HEADS):
        k_ref[0, h, :, :D_NOPE] = kv[:, h * per:h * per + D_NOPE].astype(k_ref.dtype)
        k_ref[0, h, :, D_NOPE:] = k_rope
        v_ref[0, h] = kv[:, h * per + D_NOPE:(h + 1) * per].astype(v_ref.dtype)


def _mla_kv(hid, w_dkv, kv_norm_w, w_ukv, cos_t, sin_t, tl=512):
    b, l, d = hid.shape
    tl = min(tl, l)
    wd = jnp.concatenate([w_dkv[:, :KV_LORA], _rope_weight_groups(w_dkv[:, KV_LORA:])], -1).astype(BF16)
    nd = wd.shape[1]
    hh = MLA_HEADS
    return pl.pallas_call(
        _mla_kv_kernel,
        grid=(b, l // tl),
        in_specs=[pl.BlockSpec((1, tl, d), lambda bi, i: (bi, i, 0)),
                  pl.BlockSpec((d, nd), lambda bi, i: (0, 0)),
                  pl.BlockSpec((1, KV_LORA), lambda bi, i: (0, 0)),
                  pl.BlockSpec((KV_LORA, hh * (D_NOPE + D_V)), lambda bi, i: (0, 0)),
                  pl.BlockSpec((tl, LANES), lambda bi, i: (i, 0)),
                  pl.BlockSpec((tl, LANES), lambda bi, i: (i, 0))],
        out_specs=[pl.BlockSpec((1, hh, tl, D_NOPE + LANES), lambda bi, i: (bi, 0, i, 0)),
                   pl.BlockSpec((1, hh, tl, D_V), lambda bi, i: (bi, 0, i, 0))],
        out_shape=[jax.ShapeDtypeStruct((b, hh, l, D_NOPE + LANES), BF16),
                   jax.ShapeDtypeStruct((b, hh, l, D_V), BF16)],
        compiler_params=_cparams("parallel", "parallel"),
        name="mla_kv",
    )(hid, wd, kv_norm_w.reshape(1, KV_LORA), w_ukv.astype(BF16), cos_t, sin_t)


def _mla_cq_kernel(x_ref, w_ref, nw_ref, o_ref):
    o_ref[...] = _rms_norm(_bdot(x_ref[...], w_ref[...]), nw_ref[...]).astype(o_ref.dtype)


def _mla_cq(x, w_dq, qnorm_w, tm=512):
    m, k = x.shape
    tm = min(tm, m)
    return pl.pallas_call(
        _mla_cq_kernel,
        grid=(m // tm,),
        in_specs=[pl.BlockSpec((tm, k), lambda i: (i, 0)),
                  pl.BlockSpec((k, Q_LORA), lambda i: (0, 0)),
                  pl.BlockSpec((1, Q_LORA), lambda i: (0, 0))],
        out_specs=pl.BlockSpec((tm, Q_LORA), lambda i: (i, 0)),
        out_shape=jax.ShapeDtypeStruct((m, Q_LORA), BF16),
        compiler_params=_cparams("parallel"),
        name="mla_cq",
    )(x, w_dq.astype(BF16), qnorm_w.reshape(1, Q_LORA))


def _mla_q_kernel(c_ref, w_ref, cos_ref, sin_ref, q_ref):
    c = c_ref[0]
    scale = (D_NOPE + D_ROPE) ** -0.5
    per = D_NOPE + 2 * LANES
    for h in range(MLA_HEADS):
        qh = _bdot(c, w_ref[:, h * per:(h + 1) * per])
        rope = qh[:, D_NOPE:D_NOPE + LANES] * cos_ref[...] + qh[:, D_NOPE + LANES:] * sin_ref[...]
        q_ref[0, h, :, :D_NOPE] = (qh[:, :D_NOPE] * scale).astype(q_ref.dtype)
        q_ref[0, h, :, D_NOPE:] = (rope * scale).astype(q_ref.dtype)


def _mla_q(cq, w_uq, cos_t, sin_t, tl=512):
    b, l, _ = cq.shape
    tl = min(tl, l)
    hh = MLA_HEADS
    per_in = D_NOPE + D_ROPE
    groups = []
    for h in range(hh):
        wh = w_uq[:, h * per_in:(h + 1) * per_in]
        groups += [wh[:, :D_NOPE], _rope_weight_groups(wh[:, D_NOPE:])]
    w = jnp.concatenate(groups, -1).astype(BF16)
    return pl.pallas_call(
        _mla_q_kernel,
        grid=(b, l // tl),
        in_specs=[pl.BlockSpec((1, tl, Q_LORA), lambda bi, i: (bi, i, 0)),
                  pl.BlockSpec(w.shape, lambda bi, i: (0, 0)),
                  pl.BlockSpec((tl, LANES), lambda bi, i: (i, 0)),
                  pl.BlockSpec((tl, LANES), lambda bi, i: (i, 0))],
        out_specs=pl.BlockSpec((1, hh, tl, D_NOPE + LANES), lambda bi, i: (bi, 0, i, 0)),
        out_shape=jax.ShapeDtypeStruct((b, hh, l, D_NOPE + LANES), BF16),
        compiler_params=_cparams("parallel", "parallel"),
        name="mla_q",
    )(cq, w, cos_t, sin_t)


def _mla_attn_kernel(q_ref, k_ref, v_ref, o_ref, *, tk):
    qi = pl.program_id(2)
    q = q_ref[0, 0]
    tq = q.shape[0]
    per = tq // tk

    def scores(j):
        start = pl.multiple_of(j * tk, tk)
        kb = k_ref[0, 0, pl.ds(start, tk), :]
        vb = v_ref[0, 0, pl.ds(start, tk), :]
        return _bdot_nt(q, kb), vb

    def update(carry, s, vb):
        m, l, acc = carry
        m_new = jnp.maximum(m, jnp.max(s, -1, keepdims=True))
        alpha = jnp.exp(m - m_new)
        p = jnp.exp(s - m_new)
        l = alpha * l + jnp.sum(p, -1, keepdims=True)
        acc = alpha * acc + _bdot(p, vb)
        return m_new, l, acc

    def full_block(j, carry):
        s, vb = scores(j)
        return update(carry, s, vb)

    carry = (jnp.full((tq, 1), -jnp.inf, F32), jnp.zeros((tq, 1), F32), jnp.zeros((tq, D_V), F32))
    carry = lax.fori_loop(0, qi * per, full_block, carry)
    qpos = lax.broadcasted_iota(jnp.int32, (tq, tk), 0)
    kpos = lax.broadcasted_iota(jnp.int32, (tq, tk), 1)
    for d in range(per):
        s, vb = scores(qi * per + d)
        s = jnp.where(kpos + d * tk <= qpos, s, -jnp.inf)
        carry = update(carry, s, vb)
    _, l, acc = carry
    o_ref[0] = (acc / l).astype(o_ref.dtype)


def _mla_attn(q, k, v, tq=512, tk=512):
    b, hh, l, dq = q.shape
    tq = min(tq, l)
    tk = min(tk, tq)
    return pl.pallas_call(
        functools.partial(_mla_attn_kernel, tk=tk),
        grid=(b, hh, l // tq),
        in_specs=[pl.BlockSpec((1, 1, tq, dq), lambda bi, h, i: (bi, h, i, 0)),
                  pl.BlockSpec((1, 1, l, dq), lambda bi, h, i: (bi, h, 0, 0)),
                  pl.BlockSpec((1, 1, l, D_V), lambda bi, h, i: (bi, h, 0, 0))],
        out_specs=pl.BlockSpec((1, tq, D_V), lambda bi, h, i: (bi, i, h)),
        out_shape=jax.ShapeDtypeStruct((b, l, hh * D_V), BF16),
        compiler_params=_cparams("parallel", "parallel", "parallel"),
        name="mla_attn",
    )(q, k, v)


def _mla_ln(hid, w_dq, qnorm_w, w_uq, w_o, k, v, cos_t, sin_t, ln_g, ln_b):
    b, l, d = hid.shape
    x = hid.reshape(b * l, d)
    cq = _mla_cq(x, w_dq, qnorm_w).reshape(b, l, Q_LORA)
    q = _mla_q(cq, w_uq, cos_t, sin_t)
    o = _mla_attn(q, k, v)
    out = _matmul_ln(o.reshape(b * l, MLA_HEADS * D_V), w_o.astype(BF16), x, ln_g, ln_b)
    return out.reshape(b, l, d)


def _first_argmax(x, ids, n):
    m = jnp.max(x, axis=0, keepdims=True)
    first = jnp.min(jnp.where(x == m, ids, n), axis=0, keepdims=True)
    return m, first


def _router_kernel(x_ref, wt_ref, bias_ref, idx_ref, w_ref):
    x = x_ref[...]
    t = x.shape[0]
    logits = lax.dot_general(wt_ref[...], x, (((1,), (1,)), ((), ())),
                             precision=HIGHEST, preferred_element_type=F32)
    scores = jax.nn.sigmoid(logits)
    biased = scores + bias_ref[...]
    neg = -jnp.inf
    sub = lax.broadcasted_iota(jnp.int32, (GROUP_SIZE, t), 0).astype(F32)
    gscores = []
    for g in range(N_GROUPS):
        xg = biased[g * GROUP_SIZE:(g + 1) * GROUP_SIZE, :]
        m1, i1 = _first_argmax(xg, sub, float(GROUP_SIZE))
        m2 = jnp.max(jnp.where(sub == i1, neg, xg), axis=0, keepdims=True)
        gscores.append(m1 + m2)
    gs = jnp.concatenate(gscores, axis=0)
    gid = lax.broadcasted_iota(jnp.int32, (N_GROUPS, t), 0).astype(F32)
    gsel = jnp.zeros((N_GROUPS, t), F32)
    for _ in range(TOPK_GROUPS):
        _, gi = _first_argmax(gs, gid, float(N_GROUPS))
        hit = gid == gi
        gsel = jnp.where(hit, 1.0, gsel)
        gs = jnp.where(hit, neg, gs)
    eid = lax.broadcasted_iota(jnp.int32, (N_EXPERTS, t), 0).astype(F32)
    allowed = jnp.concatenate(
        [jnp.broadcast_to(gsel[g:g + 1, :], (GROUP_SIZE, t)) for g in range(N_GROUPS)], axis=0)
    cand = jnp.where(allowed > 0.0, biased, neg)
    idxs, ws = [], []
    for _ in range(TOP_K):
        _, ei = _first_argmax(cand, eid, float(N_EXPERTS))
        hit = eid == ei
        idxs.append(ei)
        ws.append(jnp.sum(jnp.where(hit, scores, 0.0), axis=0, keepdims=True))
        cand = jnp.where(hit, neg, cand)
    w = jnp.concatenate(ws, axis=0)
    w = w / (jnp.sum(w, axis=0, keepdims=True) + 1e-20) * ROUTED_SCALE
    idx_ref[...] = jnp.concatenate(idxs, axis=0).astype(jnp.int32)
    w_ref[...] = w


def _router(x, w_router, bias, tr=512):
    t, d = x.shape
    tr = min(tr, t)
    return pl.pallas_call(
        _router_kernel,
        grid=(t // tr,),
        in_specs=[pl.BlockSpec((tr, d), lambda i: (i, 0)),
                  pl.BlockSpec((N_EXPERTS, d), lambda i: (0, 0)),
                  pl.BlockSpec((N_EXPERTS, 1), lambda i: (0, 0))],
        out_specs=[pl.BlockSpec((TOP_K, tr), lambda i: (0, i)),
                   pl.BlockSpec((TOP_K, tr), lambda i: (0, i))],
        out_shape=[jax.ShapeDtypeStruct((TOP_K, t), jnp.int32),
                   jax.ShapeDtypeStruct((TOP_K, t), F32)],
        compiler_params=_cparams("parallel"),
        name="moe_router",
    )(x, w_router.T, bias.reshape(N_EXPERTS, 1))


def _dispatch_plan(idx, w, tile_tokens):
    t = idx.shape[0]
    n_tiles = t // tile_tokens
    nb = tile_tokens * TOP_K // EXPERT_ROWS + N_EXPERTS
    hit = (idx[:, :, None] == jnp.arange(N_EXPERTS, dtype=jnp.int32)).any(1).astype(jnp.int32)
    hit = hit.reshape(n_tiles, tile_tokens, N_EXPERTS)
    csum = jnp.cumsum(hit, axis=1)
    rank = (csum - hit).reshape(t, N_EXPERTS)
    counts = csum[:, -1, :]
    padded = (counts + EXPERT_ROWS - 1) // EXPERT_ROWS * EXPERT_ROWS
    pad_end = jnp.cumsum(padded, axis=1)
    pad_start = pad_end - padded
    tile_id = jnp.arange(t, dtype=jnp.int32) // tile_tokens
    pos = (jnp.take_along_axis(pad_start[tile_id], idx, axis=1)
           + jnp.take_along_axis(rank, idx, axis=1))
    flat = (tile_id[:, None] * (nb * EXPERT_ROWS) + pos).reshape(-1)
    local = jnp.broadcast_to((jnp.arange(t, dtype=jnp.int32) % tile_tokens)[:, None], idx.shape).reshape(-1)
    n_slots = n_tiles * nb * EXPERT_ROWS
    slot_tok = jnp.full((n_slots,), tile_tokens, jnp.int32).at[flat].set(local, unique_indices=True)
    slot_w = jnp.zeros((n_slots,), F32).at[flat].set(w.reshape(-1), unique_indices=True)
    nblk = (pad_end[:, -1] // EXPERT_ROWS).astype(jnp.int32)
    starts = jnp.arange(nb, dtype=jnp.int32) * EXPERT_ROWS
    blk_e = jax.vmap(lambda pe: jnp.searchsorted(pe, starts, side='right'))(pad_end).astype(jnp.int32)
    blk_e = jnp.minimum(blk_e, N_EXPERTS - 1)
    last = jnp.take_along_axis(blk_e, jnp.maximum(nblk - 1, 0)[:, None], axis=1)
    blk_e = jnp.where(starts[None, :] // EXPERT_ROWS < nblk[:, None], blk_e, last)
    return (slot_tok.reshape(n_tiles * nb, 1, EXPERT_ROWS), slot_w.reshape(n_tiles * nb, 1, EXPERT_ROWS),
            blk_e.reshape(-1), nblk, nb)


def _experts_kernel(blk_e_ref, nblk_ref, tok_ref, sw_ref, x_ref, wg_ref, wu_ref, wd_ref,
                    acc_ref, xs_ref, ys_ref, *, tile_tokens):
    i = pl.program_id(0)
    j = pl.program_id(1)
    rows = EXPERT_ROWS
    pitch = STAGE_PITCH

    @pl.when(j == 0)
    def _():
        acc_ref[...] = jnp.zeros_like(acc_ref)

    @pl.when(j < nblk_ref[i])
    def _():
        for r in range(rows):
            t = jnp.minimum(tok_ref[0, 0, r], tile_tokens - 1)
            slab = x_ref[pl.ds(pl.multiple_of(t * SUBLANES, SUBLANES), SUBLANES), :]
            xs_ref[pl.ds(r, ROW_CHUNKS, stride=pitch), :] = slab
        x = jnp.concatenate([xs_ref[pl.ds(c * pitch, rows), :] for c in range(ROW_CHUNKS)], axis=1)
        hid = _silu(_bdot(x, wg_ref[0])) * _bdot(x, wu_ref[0])
        y = _bdot(hid, wd_ref[0])
        for c in range(ROW_CHUNKS):
            ys_ref[pl.ds(c * pitch, rows), :] = y[:, c * LANES:(c + 1) * LANES]
        for r0 in range(0, rows, SCATTER_BATCH):
            new = []
            for r in range(r0, r0 + SCATTER_BATCH):
                at = pl.ds(pl.multiple_of(tok_ref[0, 0, r] * SUBLANES, SUBLANES), SUBLANES)
                new.append((at, acc_ref[0, at, :]
                            + sw_ref[0, 0, r] * ys_ref[pl.ds(r, ROW_CHUNKS, stride=pitch), :]))
            for at, val in new:
                acc_ref[0, at, :] = val


def _routed_experts(x, idx, w, w_gate, w_up, w_down, tile_tokens=2048):
    t, d = x.shape
    tile_tokens = min(tile_tokens, t)
    n_tiles = t // tile_tokens
    slot_tok, slot_w, blk_e, nblk, nb = _dispatch_plan(idx, w, tile_tokens)
    x_slabs = x.reshape(t * ROW_CHUNKS, LANES)
    acc_rows = (tile_tokens + 1) * SUBLANES

    def blk(i, j, nblk_ref):
        return i * nb + jnp.minimum(j, jnp.maximum(nblk_ref[i] - 1, 0))

    grid_spec = pltpu.PrefetchScalarGridSpec(
        num_scalar_prefetch=2,
        grid=(n_tiles, nb),
        in_specs=[
            pl.BlockSpec((1, 1, EXPERT_ROWS), lambda i, j, be, nk: (blk(i, j, nk), 0, 0),
                         memory_space=pltpu.SMEM),
            pl.BlockSpec((1, 1, EXPERT_ROWS), lambda i, j, be, nk: (blk(i, j, nk), 0, 0),
                         memory_space=pltpu.SMEM),
            pl.BlockSpec((tile_tokens * ROW_CHUNKS, LANES), lambda i, j, be, nk: (i, 0)),
            pl.BlockSpec((1, d, D_EXPERT), lambda i, j, be, nk: (be[i * nb + j], 0, 0)),
            pl.BlockSpec((1, d, D_EXPERT), lambda i, j, be, nk: (be[i * nb + j], 0, 0)),
            pl.BlockSpec((1, D_EXPERT, d), lambda i, j, be, nk: (be[i * nb + j], 0, 0)),
        ],
        out_specs=pl.BlockSpec((1, acc_rows, LANES), lambda i, j, be, nk: (i, 0, 0)),
        scratch_shapes=[pltpu.VMEM((ROW_CHUNKS * STAGE_PITCH, LANES), F32),
                        pltpu.VMEM((ROW_CHUNKS * STAGE_PITCH, LANES), F32)],
    )
    acc = pl.pallas_call(
        functools.partial(_experts_kernel, tile_tokens=tile_tokens),
        grid_spec=grid_spec,
        out_shape=jax.ShapeDtypeStruct((n_tiles, acc_rows, LANES), F32),
        compiler_params=_cparams("parallel", "arbitrary"),
        name="moe_experts",
    )(blk_e, nblk, slot_tok, slot_w, x_slabs, w_gate, w_up, w_down)
    return acc[:, :tile_tokens * ROW_CHUNKS, :].reshape(t, d)


def _moe_out_kernel(x_ref, r_ref, wg_ref, wu_ref, wd_ref, g_ref, b_ref, o_ref):
    x = x_ref[...]
    hid = _silu(_bdot(x, wg_ref[...])) * _bdot(x, wu_ref[...])
    y = r_ref[...] + _bdot(hid, wd_ref[...])
    o_ref[...] = _layer_norm(DN_ALPHA * x + y, g_ref[...], b_ref[...])


def _moe_out(x, routed, ws_gate, ws_up, ws_down, g, b, tm=512):
    t, d = x.shape
    tm = min(tm, t)
    return pl.pallas_call(
        _moe_out_kernel,
        grid=(t // tm,),
        in_specs=[pl.BlockSpec((tm, d), lambda i: (i, 0)),
                  pl.BlockSpec((tm, d), lambda i: (i, 0)),
                  pl.BlockSpec((d, D_SHARED), lambda i: (0, 0)),
                  pl.BlockSpec((d, D_SHARED), lambda i: (0, 0)),
                  pl.BlockSpec((D_SHARED, d), lambda i: (0, 0)),
                  pl.BlockSpec((1, d), lambda i: (0, 0)),
                  pl.BlockSpec((1, d), lambda i: (0, 0))],
        out_specs=pl.BlockSpec((tm, d), lambda i: (i, 0)),
        out_shape=jax.ShapeDtypeStruct((t, d), F32),
        compiler_params=_cparams("parallel"),
        name="moe_out",
    )(x, routed, ws_gate.astype(BF16), ws_up.astype(BF16), ws_down.astype(BF16),
      g.reshape(1, d), b.reshape(1, d))


def _moe_ln(hid, w_router, router_bias, w_gate, w_up, w_down, ws_gate, ws_up, ws_down, ln_g, ln_b):
    b, l, d = hid.shape
    x = hid.reshape(b * l, d)
    idx_t, w_t = _router(x, w_router, router_bias)
    routed = _routed_experts(x, idx_t.T, w_t.T, w_gate.astype(BF16), w_up.astype(BF16), w_down.astype(BF16))
    return _moe_out(x, routed, ws_gate, ws_up, ws_down, ln_g, ln_b).reshape(b, l, d)


def kernel(x, a_w_in, a_conv_w, a_A_log, a_dt_bias, a_onorm_w, a_w_out, b_w_dq, b_qnorm_w, b_w_uq, b_w_o,
           kv_w_dkv, kv_norm_w, kv_w_ukv, ln1_g, ln1_b, ln2_g, ln2_b, moe_w_router, moe_router_bias,
           moe_w_gate, moe_w_up, moe_w_down, moe_ws_gate, moe_ws_up, moe_ws_down):
    l = x.shape[1]
    cos_t, sin_t = _rope_lane_tables(l)
    h = x
    k = v = None
    for layer in range(DEPTH):
        if layer < N_A_LAYERS:
            i = layer
            h = _gated_deltanet_ln(h, a_w_in[i], a_conv_w[i], a_A_log[i], a_dt_bias[i], a_onorm_w[i],
                                   a_w_out[i], ln1_g[layer], ln1_b[layer])
        else:
            i = layer - N_A_LAYERS
            h = _mla_ln(h, b_w_dq[i], b_qnorm_w[i], b_w_uq[i], b_w_o[i], k, v, cos_t, sin_t,
                        ln1_g[layer], ln1_b[layer])
        h = _moe_ln(h, moe_w_router[layer], moe_router_bias[layer], moe_w_gate[layer], moe_w_up[layer],
                    moe_w_down[layer], moe_ws_gate[layer], moe_ws_up[layer], moe_ws_down[layer],
                    ln2_g[layer], ln2_b[layer])
        if layer == N_A_LAYERS - 1:
            k, v = _mla_kv(h, kv_w_dkv, kv_norm_w, kv_w_ukv, cos_t, sin_t)
    return h
```

```python
import functools

import jax
import jax.numpy as jnp
import numpy as np
from jax import lax
from jax.experimental import pallas as pl
from jax.experimental.pallas import tpu as pltpu

F32 = jnp.float32
BF16 = jnp.bfloat16
HIGHEST = lax.Precision.HIGHEST

D_MODEL = 1024
DEPTH = 4
N_A_LAYERS = DEPTH // 2
GDN_HEADS = 8
GDN_DK = 128
GDN_DV = 128
D_QK = GDN_HEADS * GDN_DK
D_VA = GDN_HEADS * GDN_DV
GDN_CONV_CH = 2 * D_QK + D_VA
CONV_WIDTH = 4
CHUNK = 64
MLA_HEADS = 8
Q_LORA = 512
KV_LORA = 256
D_NOPE = 128
D_ROPE = 64
D_V = 128
ROPE_THETA = 10000.0
N_EXPERTS = 64
TOP_K = 8
N_GROUPS = 8
GROUP_SIZE = N_EXPERTS // N_GROUPS
TOPK_GROUPS = 4
D_EXPERT = 256
D_SHARED = 256
ROUTED_SCALE = 2.5
DN_ALPHA = (2 * DEPTH) ** 0.25
EPS = 1e-6
LN_EPS = 1e-5

LANES = 128
SUBLANES = 8
VMEM_LIMIT_BYTES = 56 * 1024 * 1024

EXPERT_ROWS = 128
ROW_CHUNKS = D_MODEL // LANES
STAGE_PITCH = EXPERT_ROWS + 1
SCATTER_BATCH = 16
BLOCKS_PER_STEP = 2
MOE_TILE_TOKENS = 2048
PLAN_PARTS = 4


def _cparams(*sem):
    return pltpu.CompilerParams(dimension_semantics=sem, vmem_limit_bytes=VMEM_LIMIT_BYTES)


def _silu(x):
    return x * jax.nn.sigmoid(x)


def _layer_norm(x, g, b):
    mu = jnp.mean(x, -1, keepdims=True)
    xc = x - mu
    var = jnp.mean(xc * xc, -1, keepdims=True)
    return xc * lax.rsqrt(var + LN_EPS) * g + b


def _rms_norm(x, w):
    return x * lax.rsqrt(jnp.mean(x * x, -1, keepdims=True) + EPS) * w


def _bdot(a, b):
    return jnp.dot(a.astype(BF16), b.astype(BF16), preferred_element_type=F32)


def _bdot_nt(a, b):
    return lax.dot_general(a.astype(BF16), b.astype(BF16), (((1,), (1,)), ((), ())),
                           preferred_element_type=F32)


def _hdot(a, b):
    return jnp.dot(a, b, precision=HIGHEST, preferred_element_type=F32)


def _matmul_kernel(x_ref, w_ref, o_ref):
    o_ref[...] = _bdot(x_ref[...], w_ref[...]).astype(o_ref.dtype)


def _matmul(x, w, out_dtype, tm, tn):
    m, k = x.shape
    n = w.shape[1]
    tm = min(tm, m)
    tn = min(tn, n)
    return pl.pallas_call(
        _matmul_kernel,
        grid=(m // tm, n // tn),
        in_specs=[pl.BlockSpec((tm, k), lambda i, j: (i, 0)),
                  pl.BlockSpec((k, tn), lambda i, j: (0, j))],
        out_specs=pl.BlockSpec((tm, tn), lambda i, j: (i, j)),
        out_shape=jax.ShapeDtypeStruct((m, n), out_dtype),
        compiler_params=_cparams("parallel", "parallel"),
        name="matmul",
    )(x, w)


def _matmul_ln_kernel(x_ref, w_ref, r_ref, g_ref, b_ref, o_ref):
    mix = _bdot(x_ref[...], w_ref[...])
    o_ref[...] = _layer_norm(DN_ALPHA * r_ref[...] + mix, g_ref[...], b_ref[...])


def _matmul_ln(x, w, res, g, b, tm=512):
    m, k = x.shape
    n = w.shape[1]
    tm = min(tm, m)
    return pl.pallas_call(
        _matmul_ln_kernel,
        grid=(m // tm,),
        in_specs=[pl.BlockSpec((tm, k), lambda i: (i, 0)),
                  pl.BlockSpec((k, n), lambda i: (0, 0)),
                  pl.BlockSpec((tm, n), lambda i: (i, 0)),
                  pl.BlockSpec((1, n), lambda i: (0, 0)),
                  pl.BlockSpec((1, n), lambda i: (0, 0))],
        out_specs=pl.BlockSpec((tm, n), lambda i: (i, 0)),
        out_shape=jax.ShapeDtypeStruct((m, n), F32),
        compiler_params=_cparams("parallel"),
        name="matmul_ln",
    )(x, w, res, g.reshape(1, n), b.reshape(1, n))


def _softplus(x):
    return jnp.maximum(x, 0.0) + jnp.log1p(jnp.exp(-jnp.abs(x)))


def _gdn_gates_kernel(x_ref, wab_ref, wabt_ref, alog_ref, dtb_ref, alogt_ref, dtbt_ref,
                      gc_ref, beta_ref, gct_ref):
    x = x_ref[...]
    tl = x.shape[0]
    h = GDN_HEADS
    ab = _hdot(x, wab_ref[...])
    abt = lax.dot_general(wabt_ref[...], x, (((1,), (1,)), ((), ())),
                          precision=HIGHEST, preferred_element_type=F32)
    g = -jnp.exp(alog_ref[...]) * _softplus(ab[:, :h] + dtb_ref[...])
    gt = -jnp.exp(alogt_ref[...]) * _softplus(abt[:h, :] + dtbt_ref[...])
    beta_ref[...] = jax.nn.sigmoid(ab[:, h:2 * h])
    row = lax.broadcasted_iota(jnp.int32, (tl, tl), 0)
    col = lax.broadcasted_iota(jnp.int32, (tl, tl), 1)
    shift = CHUNK.bit_length() - 1
    same = (row >> shift) == (col >> shift)
    lower = jnp.where(same & (col <= row), 1.0, 0.0).astype(F32)
    upper = jnp.where(same & (row <= col), 1.0, 0.0).astype(F32)
    gc_ref[...] = _hdot(lower, g)
    gct_ref[...] = _hdot(gt, upper)


def _gdn_gates(x, w_ab, a_log, dt_bias, tl=512):
    t = x.shape[0]
    tl = min(tl, t)
    h = GDN_HEADS
    wab = jnp.zeros((D_MODEL, LANES), F32).at[:, :2 * h].set(w_ab)
    wabt = w_ab.T
    return pl.pallas_call(
        _gdn_gates_kernel,
        grid=(t // tl,),
        in_specs=[pl.BlockSpec((tl, D_MODEL), lambda i: (i, 0)),
                  pl.BlockSpec((D_MODEL, LANES), lambda i: (0, 0)),
                  pl.BlockSpec((2 * h, D_MODEL), lambda i: (0, 0)),
                  pl.BlockSpec((1, h), lambda i: (0, 0)),
                  pl.BlockSpec((1, h), lambda i: (0, 0)),
                  pl.BlockSpec((h, 1), lambda i: (0, 0)),
                  pl.BlockSpec((h, 1), lambda i: (0, 0))],
        out_specs=[pl.BlockSpec((tl, h), lambda i: (i, 0)),
                   pl.BlockSpec((tl, h), lambda i: (i, 0)),
                   pl.BlockSpec((h, tl), lambda i: (0, i))],
        out_shape=[jax.ShapeDtypeStruct((t, h), F32),
                   jax.ShapeDtypeStruct((t, h), F32),
                   jax.ShapeDtypeStruct((h, t), F32)],
        compiler_params=_cparams("parallel"),
        name="gdn_gates",
    )(x, wab, wabt, a_log.reshape(1, h), dt_bias.reshape(1, h),
      a_log.reshape(h, 1), dt_bias.reshape(h, 1))


def _gdn_conv_kernel(cur_ref, prev_ref, w_ref, q_ref, k_ref, v_ref):
    i = pl.program_id(1)
    x = cur_ref[0]
    tl = x.shape[0]
    prev = jnp.where(i > 0, prev_ref[0], 0.0)
    xx = jnp.concatenate([prev, x], axis=0)
    w = w_ref[...]
    y = None
    for j in range(CONV_WIDTH):
        off = SUBLANES - (CONV_WIDTH - 1) + j
        term = xx[off:off + tl, :] * w[j:j + 1, :]
        y = term if y is None else y + term
    y = _silu(y)
    for h in range(GDN_HEADS):
        sl = slice(h * GDN_DK, (h + 1) * GDN_DK)
        qh = y[:, sl]
        q_ref[0, :, sl] = qh * lax.rsqrt(jnp.sum(qh * qh, -1, keepdims=True) + EPS) * (GDN_DK ** -0.5)
        kh = y[:, D_QK + h * GDN_DK:D_QK + (h + 1) * GDN_DK]
        k_ref[0, :, sl] = kh * lax.rsqrt(jnp.sum(kh * kh, -1, keepdims=True) + EPS)
    v_ref[0] = y[:, 2 * D_QK:]


def _gdn_conv(proj, conv_w, tl=256):
    b, l, _ = proj.shape
    tl = min(tl, l)
    c3 = GDN_CONV_CH
    per = tl // SUBLANES
    out = jax.ShapeDtypeStruct((b, l, D_QK), F32)
    return pl.pallas_call(
        _gdn_conv_kernel,
        grid=(b, l // tl),
        in_specs=[pl.BlockSpec((1, tl, c3), lambda bi, i: (bi, i, 0)),
                  pl.BlockSpec((1, SUBLANES, c3), lambda bi, i: (bi, jnp.maximum(i * per - 1, 0), 0)),
                  pl.BlockSpec((CONV_WIDTH, c3), lambda bi, i: (0, 0))],
        out_specs=[pl.BlockSpec((1, tl, D_QK), lambda bi, i: (bi, i, 0))] * 3,
        out_shape=[out, out, out],
        compiler_params=_cparams("parallel", "parallel"),
        name="gdn_conv",
    )(proj, proj, conv_w)


def _split_bf16(x):
    hi = x.astype(BF16)
    lo = (x - hi.astype(F32)).astype(BF16)
    return hi, lo


def _dot_split(a, b):
    (ah, al), (bh, bl) = a, b
    dot = functools.partial(jnp.dot, preferred_element_type=F32)
    return dot(ah, bh) + (dot(ah, bl) + dot(al, bh))


def _gdn_prep_kernel(q_ref, k_ref, v_ref, gc_ref, beta_ref, gct_ref, wq_ref, u_ref, attn_ref, kgt_ref,
                     *, chunks):
    c = CHUNK
    row = lax.broadcasted_iota(jnp.int32, (c, c), 0)
    col = lax.broadcasted_iota(jnp.int32, (c, c), 1)
    incl = row >= col
    strict = row > col
    eye = jnp.where(row == col, 1.0, 0.0).astype(F32)
    items = [(ci, h) for ci in range(chunks) for h in range(GDN_HEADS)]

    st = {}
    for ci, h in items:
        rows = slice(ci * c, (ci + 1) * c)
        sl = slice(h * GDN_DK, (h + 1) * GDN_DK)
        q = q_ref[0, rows, sl]
        k = k_ref[0, rows, sl]
        v = v_ref[0, rows, sl]
        gcol = gc_ref[0, rows, h:h + 1]
        grow = gct_ref[0, ci, h:h + 1, :]
        bcol = beta_ref[0, rows, h:h + 1]
        decay = jnp.where(incl, jnp.exp(jnp.where(incl, gcol - grow, 0.0)), 0.0)
        kb = k * bcol
        eg = jnp.exp(gcol)
        g_last = gcol[c - 1:c, :]
        sc = _bdot_nt(jnp.concatenate([kb, q], axis=0), k)
        a = jnp.where(strict, sc[:c] * decay, 0.0)
        attn_ref[0, ci, h] = (sc[c:] * decay).astype(attn_ref.dtype)
        kgt_ref[0, ci, h] = (k * jnp.exp(g_last - gcol)).T.astype(kgt_ref.dtype)
        wq_ref[0, ci, h, c:, :] = (q * eg).astype(wq_ref.dtype)
        st[ci, h] = dict(m=-a, rhs=jnp.concatenate([v * bcol, kb * eg], axis=1))

    for it in items:
        st[it]["p"] = eye + st[it]["m"]
    span = 2
    while span < c:
        for it in items:
            ms = _split_bf16(st[it]["m"])
            st[it]["m"] = _dot_split(ms, ms)
        for it in items:
            st[it]["p"] = st[it]["p"] + _dot_split(_split_bf16(st[it]["p"]), _split_bf16(st[it]["m"]))
        span *= 2

    for ci, h in items:
        sol = _dot_split(_split_bf16(st[ci, h]["p"]), _split_bf16(st[ci, h]["rhs"]))
        u_ref[0, ci * c:(ci + 1) * c, h * GDN_DV:(h + 1) * GDN_DV] = sol[:, :GDN_DV]
        wq_ref[0, ci, h, :c, :] = sol[:, GDN_DV:].astype(wq_ref.dtype)


def _gdn_prep(q, k, v, gc, beta, gct, chunks=2):
    b, l, _ = q.shape
    n = l // CHUNK
    chunks = min(chunks, n)
    h = GDN_HEADS
    rows = chunks * CHUNK
    blk = pl.BlockSpec((1, rows, D_QK), lambda bi, i: (bi, i, 0))
    gate = pl.BlockSpec((1, rows, h), lambda bi, i: (bi, i, 0))
    return pl.pallas_call(
        functools.partial(_gdn_prep_kernel, chunks=chunks),
        grid=(b, n // chunks),
        in_specs=[blk, blk, blk, gate, gate,
                  pl.BlockSpec((1, chunks, h, CHUNK), lambda bi, i: (bi, i, 0, 0))],
        out_specs=[pl.BlockSpec((1, chunks, h, 2 * CHUNK, GDN_DK), lambda bi, i: (bi, i, 0, 0, 0)),
                   pl.BlockSpec((1, rows, D_VA), lambda bi, i: (bi, i, 0)),
                   pl.BlockSpec((1, chunks, h, CHUNK, CHUNK), lambda bi, i: (bi, i, 0, 0, 0)),
                   pl.BlockSpec((1, chunks, h, GDN_DK, CHUNK), lambda bi, i: (bi, i, 0, 0, 0))],
        out_shape=[jax.ShapeDtypeStruct((b, n, h, 2 * CHUNK, GDN_DK), BF16),
                   jax.ShapeDtypeStruct((b, l, D_VA), F32),
                   jax.ShapeDtypeStruct((b, n, h, CHUNK, CHUNK), BF16),
                   jax.ShapeDtypeStruct((b, n, h, GDN_DK, CHUNK), BF16)],
        compiler_params=_cparams("parallel", "parallel"),
        name="gdn_prep",
    )(q, k, v, gc, beta, gct)


def _gdn_state_kernel(wq_ref, u_ref, attn_ref, kgt_ref, gc_ref, z_ref, onw_ref, o_ref, s_ref, *, chunks):
    @pl.when(pl.program_id(1) == 0)
    def _():
        s_ref[...] = jnp.zeros_like(s_ref)

    c = CHUNK
    onw = onw_ref[...]
    dot = functools.partial(jnp.dot, preferred_element_type=F32)
    heads = range(GDN_HEADS)
    for ci in range(chunks):
        rows = slice(ci * c, (ci + 1) * c)
        s_old = [s_ref[h] for h in heads]
        r = [dot(wq_ref[0, ci, h], s_old[h].astype(BF16)) for h in heads]
        v_new = [(u_ref[0, rows, h * GDN_DV:(h + 1) * GDN_DV] - r[h][:c]).astype(BF16) for h in heads]
        for h in heads:
            decay_last = jnp.exp(gc_ref[0, (ci + 1) * c - 1:(ci + 1) * c, h:h + 1])
            s_ref[h] = s_old[h] * decay_last + dot(kgt_ref[0, ci, h], v_new[h])
        for h in heads:
            sl = slice(h * GDN_DV, (h + 1) * GDN_DV)
            o = r[h][c:] + dot(attn_ref[0, ci, h], v_new[h])
            o_ref[0, rows, sl] = (_rms_norm(o, onw) * _silu(z_ref[0, rows, sl])).astype(o_ref.dtype)


def _gdn_state(wq, u, attn, kgt, gc, proj, onorm_w, chunks=4):
    b, n, h = wq.shape[:3]
    l = n * CHUNK
    chunks = min(chunks, n)
    rows = chunks * CHUNK
    z_block = GDN_CONV_CH // D_VA
    return pl.pallas_call(
        functools.partial(_gdn_state_kernel, chunks=chunks),
        grid=(b, n // chunks),
        in_specs=[pl.BlockSpec((1, chunks, h, 2 * CHUNK, GDN_DK), lambda bi, i: (bi, i, 0, 0, 0)),
                  pl.BlockSpec((1, rows, D_VA), lambda bi, i: (bi, i, 0)),
                  pl.BlockSpec((1, chunks, h, CHUNK, CHUNK), lambda bi, i: (bi, i, 0, 0, 0)),
                  pl.BlockSpec((1, chunks, h, GDN_DK, CHUNK), lambda bi, i: (bi, i, 0, 0, 0)),
                  pl.BlockSpec((1, rows, h), lambda bi, i: (bi, i, 0)),
                  pl.BlockSpec((1, rows, D_VA), lambda bi, i: (bi, i, z_block)),
                  pl.BlockSpec((1, GDN_DV), lambda bi, i: (0, 0))],
        out_specs=pl.BlockSpec((1, rows, D_VA), lambda bi, i: (bi, i, 0)),
        out_shape=jax.ShapeDtypeStruct((b, l, D_VA), BF16),
        scratch_shapes=[pltpu.VMEM((h, GDN_DK, GDN_DV), F32)],
        compiler_params=_cparams("parallel", "arbitrary"),
        name="gdn_state",
    )(wq, u, attn, kgt, gc, proj, onorm_w.reshape(1, GDN_DV))


def _gated_deltanet_ln(hid, w_in, conv_w, a_log, dt_bias, onorm_w, w_out, ln_g, ln_b):
    b, l, d = hid.shape
    h = GDN_HEADS
    n = l // CHUNK
    x = hid.reshape(b * l, d)
    n_main = GDN_CONV_CH + D_VA
    proj = _matmul(x, w_in[:, :n_main].astype(BF16), F32, 512, 1024).reshape(b, l, n_main)
    gc, beta, gct = _gdn_gates(x, w_in[:, n_main:], a_log, dt_bias)
    gc = gc.reshape(b, l, h)
    gct = gct.reshape(h, b, n, CHUNK).transpose(1, 2, 0, 3)
    q, k, v = _gdn_conv(proj, conv_w)
    wq, u, attn, kgt = _gdn_prep(q, k, v, gc, beta.reshape(b, l, h), gct)
    o = _gdn_state(wq, u, attn, kgt, gc, proj, onorm_w)
    out = _matmul_ln(o.reshape(b * l, D_VA), w_out.astype(BF16), x, ln_g, ln_b)
    return out.reshape(b, l, d)


def _rope_lane_tables(l):
    inv = 1.0 / (ROPE_THETA ** (jnp.arange(0, D_ROPE, 2, dtype=F32) / D_ROPE))
    ang = jnp.arange(l, dtype=F32)[:, None] * inv[None, :]
    cos, sin = jnp.cos(ang), jnp.sin(ang)
    zero = jnp.zeros((l, LANES - D_ROPE), F32)
    return (jnp.concatenate([cos, cos, zero], -1),
            jnp.concatenate([-sin, sin, zero], -1))


def _rope_weight_groups(w_rope):
    kdim = w_rope.shape[0]
    half = D_ROPE // 2
    zero = jnp.zeros((kdim, LANES - D_ROPE), w_rope.dtype)
    x1, x2 = w_rope[:, :half], w_rope[:, half:]
    return jnp.concatenate([x1, x2, zero, x2, x1, zero], axis=-1)


def _mla_kv_kernel(x_ref, wd_ref, nw_ref, wu_ref, cos_ref, sin_ref, k_ref, v_ref):
    x = x_ref[0]
    ckv = _bdot(x, wd_ref[...])
    c = _rms_norm(ckv[:, :KV_LORA], nw_ref[...])
    k_rope = (ckv[:, KV_LORA:KV_LORA + LANES] * cos_ref[...]
              + ckv[:, KV_LORA + LANES:] * sin_ref[...]).astype(k_ref.dtype)
    kv = _bdot(c, wu_ref[...])
    per = D_NOPE + D_V
    for h in range(MLA_HEADS):
        k_ref[0, h, :, :D_NOPE] = kv[:, h * per:h * per + D_NOPE].astype(k_ref.dtype)
        k_ref[0, h, :, D_NOPE:] = k_rope
        v_ref[0, h] = kv[:, h * per + D_NOPE:(h + 1) * per].astype(v_ref.dtype)


def _mla_kv(hid, w_dkv, kv_norm_w, w_ukv, cos_t, sin_t, tl=512):
    b, l, d = hid.shape
    tl = min(tl, l)
    wd = jnp.concatenate([w_dkv[:, :KV_LORA], _rope_weight_groups(w_dkv[:, KV_LORA:])], -1).astype(BF16)
    nd = wd.shape[1]
    hh = MLA_HEADS
    return pl.pallas_call(
        _mla_kv_kernel,
        grid=(b, l // tl),
        in_specs=[pl.BlockSpec((1, tl, d), lambda bi, i: (bi, i, 0)),
                  pl.BlockSpec((d, nd), lambda bi, i: (0, 0)),
                  pl.BlockSpec((1, KV_LORA), lambda bi, i: (0, 0)),
                  pl.BlockSpec((KV_LORA, hh * (D_NOPE + D_V)), lambda bi, i: (0, 0)),
                  pl.BlockSpec((tl, LANES), lambda bi, i: (i, 0)),
                  pl.BlockSpec((tl, LANES), lambda bi, i: (i, 0))],
        out_specs=[pl.BlockSpec((1, hh, tl, D_NOPE + LANES), lambda bi, i: (bi, 0, i, 0)),
                   pl.BlockSpec((1, hh, tl, D_V), lambda bi, i: (bi, 0, i, 0))],
        out_shape=[jax.ShapeDtypeStruct((b, hh, l, D_NOPE + LANES), BF16),
                   jax.ShapeDtypeStruct((b, hh, l, D_V), BF16)],
        compiler_params=_cparams("parallel", "parallel"),
        name="mla_kv",
    )(hid, wd, kv_norm_w.reshape(1, KV_LORA), w_ukv.astype(BF16), cos_t, sin_t)


def _mla_cq_kernel(x_ref, w_ref, nw_ref, o_ref):
    o_ref[...] = _rms_norm(_bdot(x_ref[...], w_ref[...]), nw_ref[...]).astype(o_ref.dtype)


def _mla_cq(x, w_dq, qnorm_w, tm=512):
    m, k = x.shape
    tm = min(tm, m)
    return pl.pallas_call(
        _mla_cq_kernel,
        grid=(m // tm,),
        in_specs=[pl.BlockSpec((tm, k), lambda i: (i, 0)),
                  pl.BlockSpec((k, Q_LORA), lambda i: (0, 0)),
                  pl.BlockSpec((1, Q_LORA), lambda i: (0, 0))],
        out_specs=pl.BlockSpec((tm, Q_LORA), lambda i: (i, 0)),
        out_shape=jax.ShapeDtypeStruct((m, Q_LORA), BF16),
        compiler_params=_cparams("parallel"),
        name="mla_cq",
    )(x, w_dq.astype(BF16), qnorm_w.reshape(1, Q_LORA))


def _mla_q_kernel(c_ref, w_ref, cos_ref, sin_ref, q_ref):
    c = c_ref[0]
    scale = (D_NOPE + D_ROPE) ** -0.5
    per = D_NOPE + 2 * LANES
    for h in range(MLA_HEADS):
        qh = _bdot(c, w_ref[:, h * per:(h + 1) * per])
        rope = qh[:, D_NOPE:D_NOPE + LANES] * cos_ref[...] + qh[:, D_NOPE + LANES:] * sin_ref[...]
        q_ref[0, h, :, :D_NOPE] = (qh[:, :D_NOPE] * scale).astype(q_ref.dtype)
        q_ref[0, h, :, D_NOPE:] = (rope * scale).astype(q_ref.dtype)


def _mla_q(cq, w_uq, cos_t, sin_t, tl=512):
    b, l, _ = cq.shape
    tl = min(tl, l)
    hh = MLA_HEADS
    per_in = D_NOPE + D_ROPE
    groups = []
    for h in range(hh):
        wh = w_uq[:, h * per_in:(h + 1) * per_in]
        groups += [wh[:, :D_NOPE], _rope_weight_groups(wh[:, D_NOPE:])]
    w = jnp.concatenate(groups, -1).astype(BF16)
    return pl.pallas_call(
        _mla_q_kernel,
        grid=(b, l // tl),
        in_specs=[pl.BlockSpec((1, tl, Q_LORA), lambda bi, i: (bi, i, 0)),
                  pl.BlockSpec(w.shape, lambda bi, i: (0, 0)),
                  pl.BlockSpec((tl, LANES), lambda bi, i: (i, 0)),
                  pl.BlockSpec((tl, LANES), lambda bi, i: (i, 0))],
        out_specs=pl.BlockSpec((1, hh, tl, D_NOPE + LANES), lambda bi, i: (bi, 0, i, 0)),
        out_shape=jax.ShapeDtypeStruct((b, hh, l, D_NOPE + LANES), BF16),
        compiler_params=_cparams("parallel", "parallel"),
        name="mla_q",
    )(cq, w, cos_t, sin_t)


def _mla_attn_kernel(q_ref, k_ref, v_ref, o_ref, *, tk):
    qi = pl.program_id(2)
    q = q_ref[0, 0]
    tq = q.shape[0]
    per = tq // tk

    def scores(j):
        start = pl.multiple_of(j * tk, tk)
        kb = k_ref[0, 0, pl.ds(start, tk), :]
        vb = v_ref[0, 0, pl.ds(start, tk), :]
        return _bdot_nt(q, kb), vb

    def update(carry, s, vb):
        m, l, acc = carry
        m_new = jnp.maximum(m, jnp.max(s, -1, keepdims=True))
        alpha = jnp.exp(m - m_new)
        p = jnp.exp(s - m_new)
        l = alpha * l + jnp.sum(p, -1, keepdims=True)
        acc = alpha * acc + _bdot(p, vb)
        return m_new, l, acc

    def full_block(j, carry):
        s, vb = scores(j)
        return update(carry, s, vb)

    carry = (jnp.full((tq, 1), -jnp.inf, F32), jnp.zeros((tq, 1), F32), jnp.zeros((tq, D_V), F32))
    carry = lax.fori_loop(0, qi * per, full_block, carry)
    qpos = lax.broadcasted_iota(jnp.int32, (tq, tk), 0)
    kpos = lax.broadcasted_iota(jnp.int32, (tq, tk), 1)
    for d in range(per):
        s, vb = scores(qi * per + d)
        s = jnp.where(kpos + d * tk <= qpos, s, -jnp.inf)
        carry = update(carry, s, vb)
    _, l, acc = carry
    o_ref[0] = (acc / l).astype(o_ref.dtype)


def _mla_attn(q, k, v, tq=512, tk=512):
    b, hh, l, dq = q.shape
    tq = min(tq, l)
    tk = min(tk, tq)
    return pl.pallas_call(
        functools.partial(_mla_attn_kernel, tk=tk),
        grid=(b, hh, l // tq),
        in_specs=[pl.BlockSpec((1, 1, tq, dq), lambda bi, h, i: (bi, h, i, 0)),
                  pl.BlockSpec((1, 1, l, dq), lambda bi, h, i: (bi, h, 0, 0)),
                  pl.BlockSpec((1, 1, l, D_V), lambda bi, h, i: (bi, h, 0, 0))],
        out_specs=pl.BlockSpec((1, tq, D_V), lambda bi, h, i: (bi, i, h)),
        out_shape=jax.ShapeDtypeStruct((b, l, hh * D_V), BF16),
        compiler_params=_cparams("parallel", "parallel", "parallel"),
        name="mla_attn",
    )(q, k, v)


def _mla_ln(hid, w_dq, qnorm_w, w_uq, w_o, k, v, cos_t, sin_t, ln_g, ln_b):
    b, l, d = hid.shape
    x = hid.reshape(b * l, d)
    cq = _mla_cq(x, w_dq, qnorm_w).reshape(b, l, Q_LORA)
    q = _mla_q(cq, w_uq, cos_t, sin_t)
    o = _mla_attn(q, k, v)
    out = _matmul_ln(o.reshape(b * l, MLA_HEADS * D_V), w_o.astype(BF16), x, ln_g, ln_b)
    return out.reshape(b, l, d)


def _first_argmax(x, ids, n):
    m = jnp.max(x, axis=0, keepdims=True)
    first = jnp.min(jnp.where(x == m, ids, n), axis=0, keepdims=True)
    return m, first


def _router_kernel(x_ref, wt_ref, bias_ref, idx_ref, w_ref, rank_ref, cnt_out_ref, cnt_ref, *, steps_per_tile):
    @pl.when(pl.program_id(0) % steps_per_tile == 0)
    def _():
        cnt_ref[...] = jnp.zeros_like(cnt_ref)

    x = x_ref[...]
    t = x.shape[0]
    logits = lax.dot_general(wt_ref[...], x, (((1,), (1,)), ((), ())),
                             precision=HIGHEST, preferred_element_type=F32)
    scores = jax.nn.sigmoid(logits)
    biased = scores + bias_ref[...]
    neg = -jnp.inf
    sub = lax.broadcasted_iota(jnp.int32, (GROUP_SIZE, t), 0).astype(F32)
    gscores = []
    for g in range(N_GROUPS):
        xg = biased[g * GROUP_SIZE:(g + 1) * GROUP_SIZE, :]
        m1, i1 = _first_argmax(xg, sub, float(GROUP_SIZE))
        m2 = jnp.max(jnp.where(sub == i1, neg, xg), axis=0, keepdims=True)
        gscores.append(m1 + m2)
    gs = jnp.concatenate(gscores, axis=0)
    gid = lax.broadcasted_iota(jnp.int32, (N_GROUPS, t), 0).astype(F32)
    gsel = jnp.zeros((N_GROUPS, t), F32)
    for _ in range(TOPK_GROUPS):
        _, gi = _first_argmax(gs, gid, float(N_GROUPS))
        hit = gid == gi
        gsel = jnp.where(hit, 1.0, gsel)
        gs = jnp.where(hit, neg, gs)
    eid = lax.broadcasted_iota(jnp.int32, (N_EXPERTS, t), 0).astype(F32)
    allowed = jnp.concatenate(
        [jnp.broadcast_to(gsel[g:g + 1, :], (GROUP_SIZE, t)) for g in range(N_GROUPS)], axis=0)
    cand = jnp.where(allowed > 0.0, biased, neg)
    idxs, ws, hits = [], [], []
    for _ in range(TOP_K):
        _, ei = _first_argmax(cand, eid, float(N_EXPERTS))
        hit = eid == ei
        idxs.append(ei)
        hits.append(hit)
        ws.append(jnp.sum(jnp.where(hit, scores, 0.0), axis=0, keepdims=True))
        cand = jnp.where(hit, neg, cand)
    w = jnp.concatenate(ws, axis=0)
    w = w / (jnp.sum(w, axis=0, keepdims=True) + 1e-20) * ROUTED_SCALE
    idx_ref[...] = jnp.concatenate(idxs, axis=0).astype(jnp.int32)
    w_ref[...] = w
    chosen = jnp.zeros((N_EXPERTS, t), F32)
    for hit in hits:
        chosen = jnp.where(hit, 1.0, chosen)
    before = (lax.broadcasted_iota(jnp.int32, (t, t), 0) < lax.broadcasted_iota(jnp.int32, (t, t), 1))
    prior = jnp.dot(chosen.astype(BF16), jnp.where(before, 1.0, 0.0).astype(BF16),
                    preferred_element_type=F32) + cnt_ref[...]
    rank_ref[...] = jnp.concatenate(
        [jnp.sum(jnp.where(hit, prior, 0.0), axis=0, keepdims=True) for hit in hits], axis=0).astype(jnp.int32)
    cnt_ref[...] += jnp.sum(chosen, axis=1, keepdims=True)
    cnt_out_ref[0] = cnt_ref[...]


def _router(x, w_router, bias, tile_tokens, tr=512):
    t, d = x.shape
    tr = min(tr, tile_tokens)
    steps_per_tile = tile_tokens // tr
    n_tiles = t // tile_tokens
    kt = pl.BlockSpec((TOP_K, tr), lambda i: (0, i))
    idx, w, rank, counts = pl.pallas_call(
        functools.partial(_router_kernel, steps_per_tile=steps_per_tile),
        grid=(t // tr,),
        in_specs=[pl.BlockSpec((tr, d), lambda i: (i, 0)),
                  pl.BlockSpec((N_EXPERTS, d), lambda i: (0, 0)),
                  pl.BlockSpec((N_EXPERTS, 1), lambda i: (0, 0))],
        out_specs=[kt, kt, kt,
                   pl.BlockSpec((1, N_EXPERTS, 1), lambda i: (i // steps_per_tile, 0, 0))],
        out_shape=[jax.ShapeDtypeStruct((TOP_K, t), jnp.int32),
                   jax.ShapeDtypeStruct((TOP_K, t), F32),
                   jax.ShapeDtypeStruct((TOP_K, t), jnp.int32),
                   jax.ShapeDtypeStruct((n_tiles, N_EXPERTS, 1), F32)],
        scratch_shapes=[pltpu.VMEM((N_EXPERTS, 1), F32)],
        compiler_params=_cparams("arbitrary"),
        name="moe_router",
    )(x, w_router.T, bias.reshape(N_EXPERTS, 1))
    return idx, w, rank, counts.reshape(n_tiles, N_EXPERTS).astype(jnp.int32)


def _blocks_per_tile(tile_tokens):
    nb = tile_tokens * TOP_K // EXPERT_ROWS + N_EXPERTS
    return -(-nb // BLOCKS_PER_STEP) * BLOCKS_PER_STEP


def _plan_kernel(pstart_ref, idx_ref, rank_ref, src_ref, *, tile_tokens, part):
    i = pl.program_id(0)
    q = pl.program_id(1)
    n_slots = src_ref.shape[-1]
    empty = tile_tokens * TOP_K

    @pl.when(q == 0)
    def _():
        def fill(s, carry):
            src_ref[0, 0, s] = empty
            return carry
        lax.fori_loop(0, n_slots, fill, 0, unroll=8)

    def place(a, carry):
        e = idx_ref[0, 0, a]
        src_ref[0, 0, pstart_ref[i * N_EXPERTS + e] + rank_ref[0, 0, a]] = q * part + a
        return carry
    lax.fori_loop(0, part, place, 0, unroll=8)


def _dispatch_plan(idx_t, rank_t, counts, tile_tokens):
    t = idx_t.shape[1]
    n_tiles = t // tile_tokens
    nb = _blocks_per_tile(tile_tokens)
    padded = (counts + EXPERT_ROWS - 1) // EXPERT_ROWS * EXPERT_ROWS
    pad_end = jnp.cumsum(padded, axis=1)
    pad_start = (pad_end - padded).astype(jnp.int32)
    nblk = (pad_end[:, -1] // EXPERT_ROWS).astype(jnp.int32)
    starts = jnp.arange(nb, dtype=jnp.int32) * EXPERT_ROWS
    blk_e = jnp.sum(starts[None, :, None] >= pad_end[:, None, :], axis=-1).astype(jnp.int32)
    blk_e = jnp.minimum(blk_e, N_EXPERTS - 1)
    last = jnp.take_along_axis(blk_e, jnp.maximum(nblk - 1, 0)[:, None], axis=1)
    blk_e = jnp.where(starts[None, :] // EXPERT_ROWS < nblk[:, None], blk_e, last)

    parts = PLAN_PARTS
    part = tile_tokens * TOP_K // parts
    flat = lambda a: a.T.reshape(n_tiles * parts, 1, part)
    src = pl.pallas_call(
        functools.partial(_plan_kernel, tile_tokens=tile_tokens, part=part),
        grid_spec=pltpu.PrefetchScalarGridSpec(
            num_scalar_prefetch=1,
            grid=(n_tiles, parts),
            in_specs=[pl.BlockSpec((1, 1, part), lambda i, q, ps: (i * parts + q, 0, 0),
                                   memory_space=pltpu.SMEM)] * 2,
            out_specs=pl.BlockSpec((1, 1, nb * EXPERT_ROWS), lambda i, q, ps: (i, 0, 0),
                                   memory_space=pltpu.SMEM),
        ),
        out_shape=jax.ShapeDtypeStruct((n_tiles, 1, nb * EXPERT_ROWS), jnp.int32),
        compiler_params=_cparams("parallel", "arbitrary"),
        name="moe_plan",
    )(pad_start.reshape(-1), flat(idx_t), flat(rank_t))
    return src, blk_e.reshape(-1), nblk, nb


def _experts_kernel(blk_e_ref, nblk_ref, src_ref, wt_ref, x_ref, *refs):
    n_sub = BLOCKS_PER_STEP
    w_refs = [refs[3 * s:3 * s + 3] for s in range(n_sub)]
    acc_ref = refs[3 * n_sub]
    xs_refs = refs[3 * n_sub + 1:3 * n_sub + 1 + n_sub]
    ys_refs = refs[3 * n_sub + 1 + n_sub:]
    i = pl.program_id(0)
    j = pl.program_id(1)
    rows = EXPERT_ROWS
    pitch = STAGE_PITCH
    shift = SUBLANES.bit_length() - 1

    @pl.when(j == 0)
    def _():
        acc_ref[...] = jnp.zeros_like(acc_ref)

    def token_rows(src):
        return pl.ds(pl.multiple_of((src >> shift) << shift, SUBLANES), SUBLANES)

    @pl.when(j * n_sub < nblk_ref[i])
    def _():
        for s in range(n_sub):
            for r in range(rows):
                slab = x_ref[0, token_rows(src_ref[0, 0, s * rows + r]), :]
                xs_refs[s][pl.ds(r, ROW_CHUNKS, stride=pitch), :] = slab
        ys = []
        for s in range(n_sub):
            wg_ref, wu_ref, wd_ref = w_refs[s]
            x = jnp.concatenate([xs_refs[s][pl.ds(c * pitch, rows), :] for c in range(ROW_CHUNKS)], axis=1)
            hid = _silu(_bdot(x, wg_ref[0])) * _bdot(x, wu_ref[0])
            ys.append(_bdot(hid, wd_ref[0]))
        for s in range(n_sub):
            for c in range(ROW_CHUNKS):
                ys_refs[s][pl.ds(c * pitch, rows), :] = ys[s][:, c * LANES:(c + 1) * LANES]
        for s in range(n_sub):
            for r0 in range(0, rows, SCATTER_BATCH):
                new = []
                for r in range(r0, r0 + SCATTER_BATCH):
                    src = src_ref[0, 0, s * rows + r]
                    at = token_rows(src)
                    new.append((at, acc_ref[0, at, :]
                                + wt_ref[0, 0, src] * ys_refs[s][pl.ds(r, ROW_CHUNKS, stride=pitch), :]))
                for at, val in new:
                    acc_ref[0, at, :] = val


def _routed_experts(x, idx_t, w_t, rank_t, counts, w_gate, w_up, w_down, tile_tokens):
    t, d = x.shape
    n_tiles = t // tile_tokens
    n_sub = BLOCKS_PER_STEP
    src, blk_e, nblk, nb = _dispatch_plan(idx_t, rank_t, counts, tile_tokens)
    steps = nb // n_sub
    src = src.reshape(n_tiles * steps, 1, n_sub * EXPERT_ROWS)
    slab_rows = (tile_tokens + 1) * ROW_CHUNKS
    x_slabs = jnp.pad(x.reshape(n_tiles, tile_tokens, d), ((0, 0), (0, 1), (0, 0))).reshape(n_tiles, slab_rows, LANES)
    w_tab = jnp.pad(w_t.T.reshape(n_tiles, 1, tile_tokens * TOP_K), ((0, 0), (0, 0), (0, LANES)))

    def src_blk(i, j, be, nk):
        return (i * steps + jnp.minimum(j, jnp.maximum(nk[i] - 1, 0) // n_sub), 0, 0)

    def weight_specs(s):
        pick = lambda i, j, be, nk: (be[i * nb + j * n_sub + s], 0, 0)
        return [pl.BlockSpec((1, d, D_EXPERT), pick), pl.BlockSpec((1, d, D_EXPERT), pick),
                pl.BlockSpec((1, D_EXPERT, d), pick)]

    stage = pltpu.VMEM((ROW_CHUNKS * STAGE_PITCH, LANES), F32)
    grid_spec = pltpu.PrefetchScalarGridSpec(
        num_scalar_prefetch=2,
        grid=(n_tiles, steps),
        in_specs=[
            pl.BlockSpec((1, 1, n_sub * EXPERT_ROWS), src_blk, memory_space=pltpu.SMEM),
            pl.BlockSpec((1, 1, w_tab.shape[-1]), lambda i, j, be, nk: (i, 0, 0), memory_space=pltpu.SMEM),
            pl.BlockSpec((1, slab_rows, LANES), lambda i, j, be, nk: (i, 0, 0)),
        ] + [spec for s in range(n_sub) for spec in weight_specs(s)],
        out_specs=pl.BlockSpec((1, slab_rows, LANES), lambda i, j, be, nk: (i, 0, 0)),
        scratch_shapes=[stage] * (2 * n_sub),
    )
    return pl.pallas_call(
        _experts_kernel,
        grid_spec=grid_spec,
        out_shape=jax.ShapeDtypeStruct((n_tiles, slab_rows, LANES), F32),
        compiler_params=_cparams("parallel", "arbitrary"),
        name="moe_experts",
    )(blk_e, nblk, src, w_tab, x_slabs, *([w_gate, w_up, w_down] * n_sub))


def _moe_out_kernel(x_ref, r_ref, wg_ref, wu_ref, wd_ref, g_ref, b_ref, o_ref):
    x = x_ref[...]
    tm = x.shape[0]
    hid = _silu(_bdot(x, wg_ref[...])) * _bdot(x, wu_ref[...])
    routed = jnp.concatenate([r_ref[0, pl.ds(c, tm, stride=ROW_CHUNKS), :] for c in range(ROW_CHUNKS)], axis=1)
    y = routed + _bdot(hid, wd_ref[...])
    o_ref[...] = _layer_norm(DN_ALPHA * x + y, g_ref[...], b_ref[...])


def _moe_out(x, routed, ws_gate, ws_up, ws_down, g, b, tile_tokens, tm=512):
    t, d = x.shape
    tm = min(tm, tile_tokens)
    per_tile = tile_tokens // tm
    return pl.pallas_call(
        _moe_out_kernel,
        grid=(t // tm,),
        in_specs=[pl.BlockSpec((tm, d), lambda i: (i, 0)),
                  pl.BlockSpec((1, tm * ROW_CHUNKS, LANES), lambda i: (i // per_tile, i % per_tile, 0)),
                  pl.BlockSpec((d, D_SHARED), lambda i: (0, 0)),
                  pl.BlockSpec((d, D_SHARED), lambda i: (0, 0)),
                  pl.BlockSpec((D_SHARED, d), lambda i: (0, 0)),
                  pl.BlockSpec((1, d), lambda i: (0, 0)),
                  pl.BlockSpec((1, d), lambda i: (0, 0))],
        out_specs=pl.BlockSpec((tm, d), lambda i: (i, 0)),
        out_shape=jax.ShapeDtypeStruct((t, d), F32),
        compiler_params=_cparams("parallel"),
        name="moe_out",
    )(x, routed, ws_gate.astype(BF16), ws_up.astype(BF16), ws_down.astype(BF16),
      g.reshape(1, d), b.reshape(1, d))


def _moe_ln(hid, w_router, router_bias, w_gate, w_up, w_down, ws_gate, ws_up, ws_down, ln_g, ln_b):
    b, l, d = hid.shape
    x = hid.reshape(b * l, d)
    tile_tokens = min(MOE_TILE_TOKENS, b * l)
    idx_t, w_t, rank_t, counts = _router(x, w_router, router_bias, tile_tokens)
    routed = _routed_experts(x, idx_t, w_t, rank_t, counts, w_gate.astype(BF16), w_up.astype(BF16),
                             w_down.astype(BF16), tile_tokens)
    return _moe_out(x, routed, ws_gate, ws_up, ws_down, ln_g, ln_b, tile_tokens).reshape(b, l, d)


def kernel(x, a_w_in, a_conv_w, a_A_log, a_dt_bias, a_onorm_w, a_w_out, b_w_dq, b_qnorm_w, b_w_uq, b_w_o,
           kv_w_dkv, kv_norm_w, kv_w_ukv, ln1_g, ln1_b, ln2_g, ln2_b, moe_w_router, moe_router_bias,
           moe_w_gate, moe_w_up, moe_w_down, moe_ws_gate, moe_ws_up, moe_ws_down):
    l = x.shape[1]
    cos_t, sin_t = _rope_lane_tables(l)
    h = x
    k = v = None
    for layer in range(DEPTH):
        if layer < N_A_LAYERS:
            i = layer
            h = _gated_deltanet_ln(h, a_w_in[i], a_conv_w[i], a_A_log[i], a_dt_bias[i], a_onorm_w[i],
                                   a_w_out[i], ln1_g[layer], ln1_b[layer])
        else:
            i = layer - N_A_LAYERS
            h = _mla_ln(h, b_w_dq[i], b_qnorm_w[i], b_w_uq[i], b_w_o[i], k, v, cos_t, sin_t,
                        ln1_g[layer], ln1_b[layer])
        h = _moe_ln(h, moe_w_router[layer], moe_router_bias[layer], moe_w_gate[layer], moe_w_up[layer],
                    moe_w_down[layer], moe_ws_gate[layer], moe_ws_up[layer], moe_ws_down[layer],
                    ln2_g[layer], ln2_b[layer])
        if layer == N_A_LAYERS - 1:
            k, v = _mla_kv(h, kv_w_dkv, kv_norm_w, kv_w_ukv, cos_t, sin_t)
    return h
```

```python
import functools

import jax
import jax.numpy as jnp
import numpy as np
from jax import lax
from jax.experimental import pallas as pl
from jax.experimental.pallas import tpu as pltpu

F32 = jnp.float32
BF16 = jnp.bfloat16
HIGHEST = lax.Precision.HIGHEST

D_MODEL = 1024
DEPTH = 4
N_A_LAYERS = DEPTH // 2
GDN_HEADS = 8
GDN_DK = 128
GDN_DV = 128
D_QK = GDN_HEADS * GDN_DK
D_VA = GDN_HEADS * GDN_DV
GDN_CONV_CH = 2 * D_QK + D_VA
CONV_WIDTH = 4
CHUNK = 64
MLA_HEADS = 8
Q_LORA = 512
KV_LORA = 256
D_NOPE = 128
D_ROPE = 64
D_V = 128
ROPE_THETA = 10000.0
N_EXPERTS = 64
TOP_K = 8
N_GROUPS = 8
GROUP_SIZE = N_EXPERTS // N_GROUPS
TOPK_GROUPS = 4
D_EXPERT = 256
D_SHARED = 256
ROUTED_SCALE = 2.5
DN_ALPHA = (2 * DEPTH) ** 0.25
EPS = 1e-6
LN_EPS = 1e-5
LOG2_E = 1.4426950408889634

LANES = 128
SUBLANES = 8
VMEM_LIMIT_BYTES = 56 * 1024 * 1024

EXPERT_ROWS = 128
ROW_CHUNKS = D_MODEL // LANES
STAGE_PITCH = EXPERT_ROWS + 1
SCATTER_BATCH = 16
BLOCKS_PER_STEP = 2
MOE_TILE_TOKENS = 2048
PLAN_PARTS = 4
assert TOP_K == SUBLANES


def _cparams(*sem):
    return pltpu.CompilerParams(dimension_semantics=sem, vmem_limit_bytes=VMEM_LIMIT_BYTES)


def _silu(x):
    return x * jax.nn.sigmoid(x)


def _layer_norm(x, g, b):
    mu = jnp.mean(x, -1, keepdims=True)
    xc = x - mu
    var = jnp.mean(xc * xc, -1, keepdims=True)
    return xc * lax.rsqrt(var + LN_EPS) * g + b


def _rms_norm(x, w):
    return x * lax.rsqrt(jnp.mean(x * x, -1, keepdims=True) + EPS) * w


def _bdot(a, b):
    return jnp.dot(a.astype(BF16), b.astype(BF16), preferred_element_type=F32)


def _bdot_nt(a, b):
    return lax.dot_general(a.astype(BF16), b.astype(BF16), (((1,), (1,)), ((), ())),
                           preferred_element_type=F32)


def _hdot(a, b):
    return jnp.dot(a, b, precision=HIGHEST, preferred_element_type=F32)


def _matmul_kernel(x_ref, w_ref, o_ref):
    o_ref[...] = _bdot(x_ref[...], w_ref[...]).astype(o_ref.dtype)


def _matmul(x, w, out_dtype, tm, tn):
    m, k = x.shape
    n = w.shape[1]
    tm = min(tm, m)
    tn = min(tn, n)
    return pl.pallas_call(
        _matmul_kernel,
        grid=(m // tm, n // tn),
        in_specs=[pl.BlockSpec((tm, k), lambda i, j: (i, 0)),
                  pl.BlockSpec((k, tn), lambda i, j: (0, j))],
        out_specs=pl.BlockSpec((tm, tn), lambda i, j: (i, j)),
        out_shape=jax.ShapeDtypeStruct((m, n), out_dtype),
        compiler_params=_cparams("parallel", "parallel"),
        name="matmul",
    )(x, w)


def _matmul_ln_kernel(x_ref, w_ref, r_ref, g_ref, b_ref, o_ref):
    mix = _bdot(x_ref[...], w_ref[...])
    o_ref[...] = _layer_norm(DN_ALPHA * r_ref[...] + mix, g_ref[...], b_ref[...])


def _matmul_ln(x, w, res, g, b, tm=512):
    m, k = x.shape
    n = w.shape[1]
    tm = min(tm, m)
    return pl.pallas_call(
        _matmul_ln_kernel,
        grid=(m // tm,),
        in_specs=[pl.BlockSpec((tm, k), lambda i: (i, 0)),
                  pl.BlockSpec((k, n), lambda i: (0, 0)),
                  pl.BlockSpec((tm, n), lambda i: (i, 0)),
                  pl.BlockSpec((1, n), lambda i: (0, 0)),
                  pl.BlockSpec((1, n), lambda i: (0, 0))],
        out_specs=pl.BlockSpec((tm, n), lambda i: (i, 0)),
        out_shape=jax.ShapeDtypeStruct((m, n), F32),
        compiler_params=_cparams("parallel"),
        name="matmul_ln",
    )(x, w, res, g.reshape(1, n), b.reshape(1, n))


def _softplus(x):
    return jnp.maximum(x, 0.0) + jnp.log1p(jnp.exp(-jnp.abs(x)))


def _gdn_gates_kernel(x_ref, wab_ref, wabt_ref, alog_ref, dtb_ref, alogt_ref, dtbt_ref,
                      gc_ref, beta_ref, gct_ref):
    x = x_ref[...]
    tl = x.shape[0]
    h = GDN_HEADS
    ab = _hdot(x, wab_ref[...])
    abt = lax.dot_general(wabt_ref[...], x, (((1,), (1,)), ((), ())),
                          precision=HIGHEST, preferred_element_type=F32)
    g = -jnp.exp(alog_ref[...]) * _softplus(ab[:, :h] + dtb_ref[...])
    gt = -jnp.exp(alogt_ref[...]) * _softplus(abt[:h, :] + dtbt_ref[...])
    beta_ref[...] = jax.nn.sigmoid(ab[:, h:2 * h])
    row = lax.broadcasted_iota(jnp.int32, (tl, tl), 0)
    col = lax.broadcasted_iota(jnp.int32, (tl, tl), 1)
    shift = CHUNK.bit_length() - 1
    same = (row >> shift) == (col >> shift)
    lower = jnp.where(same & (col <= row), 1.0, 0.0).astype(F32)
    upper = jnp.where(same & (row <= col), 1.0, 0.0).astype(F32)
    gc_ref[...] = _hdot(lower, g)
    gct_ref[...] = _hdot(gt, upper)


def _gdn_gates(x, w_ab, a_log, dt_bias, tl=512):
    t = x.shape[0]
    tl = min(tl, t)
    h = GDN_HEADS
    wab = jnp.zeros((D_MODEL, LANES), F32).at[:, :2 * h].set(w_ab)
    wabt = w_ab.T
    return pl.pallas_call(
        _gdn_gates_kernel,
        grid=(t // tl,),
        in_specs=[pl.BlockSpec((tl, D_MODEL), lambda i: (i, 0)),
                  pl.BlockSpec((D_MODEL, LANES), lambda i: (0, 0)),
                  pl.BlockSpec((2 * h, D_MODEL), lambda i: (0, 0)),
                  pl.BlockSpec((1, h), lambda i: (0, 0)),
                  pl.BlockSpec((1, h), lambda i: (0, 0)),
                  pl.BlockSpec((h, 1), lambda i: (0, 0)),
                  pl.BlockSpec((h, 1), lambda i: (0, 0))],
        out_specs=[pl.BlockSpec((tl, h), lambda i: (i, 0)),
                   pl.BlockSpec((tl, h), lambda i: (i, 0)),
                   pl.BlockSpec((h, tl), lambda i: (0, i))],
        out_shape=[jax.ShapeDtypeStruct((t, h), F32),
                   jax.ShapeDtypeStruct((t, h), F32),
                   jax.ShapeDtypeStruct((h, t), F32)],
        compiler_params=_cparams("parallel"),
        name="gdn_gates",
    )(x, wab, wabt, a_log.reshape(1, h), dt_bias.reshape(1, h),
      a_log.reshape(h, 1), dt_bias.reshape(h, 1))


def _gdn_conv_kernel(cur_ref, prev_ref, w_ref, q_ref, k_ref, v_ref):
    i = pl.program_id(1)
    x = cur_ref[0]
    tl = x.shape[0]
    prev = jnp.where(i > 0, prev_ref[0], 0.0)
    xx = jnp.concatenate([prev, x], axis=0)
    w = w_ref[...]
    y = None
    for j in range(CONV_WIDTH):
        off = SUBLANES - (CONV_WIDTH - 1) + j
        term = xx[off:off + tl, :] * w[j:j + 1, :]
        y = term if y is None else y + term
    y = _silu(y)
    for h in range(GDN_HEADS):
        sl = slice(h * GDN_DK, (h + 1) * GDN_DK)
        qh = y[:, sl]
        q_ref[0, :, sl] = qh * lax.rsqrt(jnp.sum(qh * qh, -1, keepdims=True) + EPS) * (GDN_DK ** -0.5)
        kh = y[:, D_QK + h * GDN_DK:D_QK + (h + 1) * GDN_DK]
        k_ref[0, :, sl] = kh * lax.rsqrt(jnp.sum(kh * kh, -1, keepdims=True) + EPS)
    v_ref[0] = y[:, 2 * D_QK:]


def _gdn_conv(proj, conv_w, tl=256):
    b, l, _ = proj.shape
    tl = min(tl, l)
    c3 = GDN_CONV_CH
    per = tl // SUBLANES
    out = jax.ShapeDtypeStruct((b, l, D_QK), F32)
    return pl.pallas_call(
        _gdn_conv_kernel,
        grid=(b, l // tl),
        in_specs=[pl.BlockSpec((1, tl, c3), lambda bi, i: (bi, i, 0)),
                  pl.BlockSpec((1, SUBLANES, c3), lambda bi, i: (bi, jnp.maximum(i * per - 1, 0), 0)),
                  pl.BlockSpec((CONV_WIDTH, c3), lambda bi, i: (0, 0))],
        out_specs=[pl.BlockSpec((1, tl, D_QK), lambda bi, i: (bi, i, 0))] * 3,
        out_shape=[out, out, out],
        compiler_params=_cparams("parallel", "parallel"),
        name="gdn_conv",
    )(proj, proj, conv_w)


def _split_bf16(x):
    hi = x.astype(BF16)
    lo = (x - hi.astype(F32)).astype(BF16)
    return hi, lo


def _dot_split(a, b):
    (ah, al), (bh, bl) = a, b
    dot = functools.partial(jnp.dot, preferred_element_type=F32)
    return dot(ah, bh) + (dot(ah, bl) + dot(al, bh))


def _gdn_prep_kernel(q_ref, k_ref, v_ref, gc_ref, beta_ref, gct_ref, wq_ref, u_ref, attn_ref, kgt_ref,
                     *, chunks):
    c = CHUNK
    row = lax.broadcasted_iota(jnp.int32, (c, c), 0)
    col = lax.broadcasted_iota(jnp.int32, (c, c), 1)
    incl = row >= col
    strict = row > col
    eye = jnp.where(row == col, 1.0, 0.0).astype(F32)
    items = [(ci, h) for ci in range(chunks) for h in range(GDN_HEADS)]

    st = {}
    for ci, h in items:
        rows = slice(ci * c, (ci + 1) * c)
        sl = slice(h * GDN_DK, (h + 1) * GDN_DK)
        q = q_ref[0, rows, sl]
        k = k_ref[0, rows, sl]
        v = v_ref[0, rows, sl]
        gcol = gc_ref[0, rows, h:h + 1]
        grow = gct_ref[0, ci, h:h + 1, :]
        bcol = beta_ref[0, rows, h:h + 1]
        decay = jnp.where(incl, jnp.exp(jnp.where(incl, gcol - grow, 0.0)), 0.0)
        kb = k * bcol
        eg = jnp.exp(gcol)
        g_last = gcol[c - 1:c, :]
        sc = _bdot_nt(jnp.concatenate([kb, q], axis=0), k)
        a = jnp.where(strict, sc[:c] * decay, 0.0)
        attn_ref[0, ci, h] = (sc[c:] * decay).astype(attn_ref.dtype)
        kgt_ref[0, ci, h] = (k * jnp.exp(g_last - gcol)).T.astype(kgt_ref.dtype)
        wq_ref[0, ci, h, c:, :] = (q * eg).astype(wq_ref.dtype)
        st[ci, h] = dict(m=-a, rhs=jnp.concatenate([v * bcol, kb * eg], axis=1))

    for it in items:
        st[it]["p"] = eye + st[it]["m"]
    span = 2
    while span < c:
        for it in items:
            ms = _split_bf16(st[it]["m"])
            st[it]["m"] = _dot_split(ms, ms)
        for it in items:
            st[it]["p"] = st[it]["p"] + _dot_split(_split_bf16(st[it]["p"]), _split_bf16(st[it]["m"]))
        span *= 2

    for ci, h in items:
        sol = _dot_split(_split_bf16(st[ci, h]["p"]), _split_bf16(st[ci, h]["rhs"]))
        u_ref[0, ci * c:(ci + 1) * c, h * GDN_DV:(h + 1) * GDN_DV] = sol[:, :GDN_DV]
        wq_ref[0, ci, h, :c, :] = sol[:, GDN_DV:].astype(wq_ref.dtype)


def _gdn_prep(q, k, v, gc, beta, gct, chunks=2):
    b, l, _ = q.shape
    n = l // CHUNK
    chunks = min(chunks, n)
    h = GDN_HEADS
    rows = chunks * CHUNK
    blk = pl.BlockSpec((1, rows, D_QK), lambda bi, i: (bi, i, 0))
    gate = pl.BlockSpec((1, rows, h), lambda bi, i: (bi, i, 0))
    return pl.pallas_call(
        functools.partial(_gdn_prep_kernel, chunks=chunks),
        grid=(b, n // chunks),
        in_specs=[blk, blk, blk, gate, gate,
                  pl.BlockSpec((1, chunks, h, CHUNK), lambda bi, i: (bi, i, 0, 0))],
        out_specs=[pl.BlockSpec((1, chunks, h, 2 * CHUNK, GDN_DK), lambda bi, i: (bi, i, 0, 0, 0)),
                   pl.BlockSpec((1, rows, D_VA), lambda bi, i: (bi, i, 0)),
                   pl.BlockSpec((1, chunks, h, CHUNK, CHUNK), lambda bi, i: (bi, i, 0, 0, 0)),
                   pl.BlockSpec((1, chunks, h, GDN_DK, CHUNK), lambda bi, i: (bi, i, 0, 0, 0))],
        out_shape=[jax.ShapeDtypeStruct((b, n, h, 2 * CHUNK, GDN_DK), BF16),
                   jax.ShapeDtypeStruct((b, l, D_VA), F32),
                   jax.ShapeDtypeStruct((b, n, h, CHUNK, CHUNK), BF16),
                   jax.ShapeDtypeStruct((b, n, h, GDN_DK, CHUNK), BF16)],
        compiler_params=_cparams("parallel", "parallel"),
        name="gdn_prep",
    )(q, k, v, gc, beta, gct)


def _gdn_state_kernel(wq_ref, u_ref, attn_ref, kgt_ref, gc_ref, z_ref, onw_ref, o_ref, s_ref, *, chunks):
    @pl.when(pl.program_id(1) == 0)
    def _():
        s_ref[...] = jnp.zeros_like(s_ref)

    c = CHUNK
    onw = onw_ref[...]
    dot = functools.partial(jnp.dot, preferred_element_type=F32)
    heads = range(GDN_HEADS)
    for ci in range(chunks):
        rows = slice(ci * c, (ci + 1) * c)
        s_old = [s_ref[h] for h in heads]
        r = [dot(wq_ref[0, ci, h], s_old[h].astype(BF16)) for h in heads]
        v_new = [(u_ref[0, rows, h * GDN_DV:(h + 1) * GDN_DV] - r[h][:c]).astype(BF16) for h in heads]
        for h in heads:
            decay_last = jnp.exp(gc_ref[0, (ci + 1) * c - 1:(ci + 1) * c, h:h + 1])
            s_ref[h] = s_old[h] * decay_last + dot(kgt_ref[0, ci, h], v_new[h])
        for h in heads:
            sl = slice(h * GDN_DV, (h + 1) * GDN_DV)
            o = r[h][c:] + dot(attn_ref[0, ci, h], v_new[h])
            o_ref[0, rows, sl] = (_rms_norm(o, onw) * _silu(z_ref[0, rows, sl])).astype(o_ref.dtype)


def _gdn_state(wq, u, attn, kgt, gc, proj, onorm_w, chunks=4):
    b, n, h = wq.shape[:3]
    l = n * CHUNK
    chunks = min(chunks, n)
    rows = chunks * CHUNK
    z_block = GDN_CONV_CH // D_VA
    return pl.pallas_call(
        functools.partial(_gdn_state_kernel, chunks=chunks),
        grid=(b, n // chunks),
        in_specs=[pl.BlockSpec((1, chunks, h, 2 * CHUNK, GDN_DK), lambda bi, i: (bi, i, 0, 0, 0)),
                  pl.BlockSpec((1, rows, D_VA), lambda bi, i: (bi, i, 0)),
                  pl.BlockSpec((1, chunks, h, CHUNK, CHUNK), lambda bi, i: (bi, i, 0, 0, 0)),
                  pl.BlockSpec((1, chunks, h, GDN_DK, CHUNK), lambda bi, i: (bi, i, 0, 0, 0)),
                  pl.BlockSpec((1, rows, h), lambda bi, i: (bi, i, 0)),
                  pl.BlockSpec((1, rows, D_VA), lambda bi, i: (bi, i, z_block)),
                  pl.BlockSpec((1, GDN_DV), lambda bi, i: (0, 0))],
        out_specs=pl.BlockSpec((1, rows, D_VA), lambda bi, i: (bi, i, 0)),
        out_shape=jax.ShapeDtypeStruct((b, l, D_VA), BF16),
        scratch_shapes=[pltpu.VMEM((h, GDN_DK, GDN_DV), F32)],
        compiler_params=_cparams("parallel", "arbitrary"),
        name="gdn_state",
    )(wq, u, attn, kgt, gc, proj, onorm_w.reshape(1, GDN_DV))


def _gated_deltanet_ln(hid, w_in, conv_w, a_log, dt_bias, onorm_w, w_out, ln_g, ln_b):
    b, l, d = hid.shape
    h = GDN_HEADS
    n = l // CHUNK
    x = hid.reshape(b * l, d)
    n_main = GDN_CONV_CH + D_VA
    proj = _matmul(x, w_in[:, :n_main].astype(BF16), F32, 512, 1024).reshape(b, l, n_main)
    gc, beta, gct = _gdn_gates(x, w_in[:, n_main:], a_log, dt_bias)
    gc = gc.reshape(b, l, h)
    gct = gct.reshape(h, b, n, CHUNK).transpose(1, 2, 0, 3)
    q, k, v = _gdn_conv(proj, conv_w)
    wq, u, attn, kgt = _gdn_prep(q, k, v, gc, beta.reshape(b, l, h), gct)
    o = _gdn_state(wq, u, attn, kgt, gc, proj, onorm_w)
    out = _matmul_ln(o.reshape(b * l, D_VA), w_out.astype(BF16), x, ln_g, ln_b)
    return out.reshape(b, l, d)


def _rope_lane_tables(l):
    inv = 1.0 / (ROPE_THETA ** (jnp.arange(0, D_ROPE, 2, dtype=F32) / D_ROPE))
    ang = jnp.arange(l, dtype=F32)[:, None] * inv[None, :]
    cos, sin = jnp.cos(ang), jnp.sin(ang)
    zero = jnp.zeros((l, LANES - D_ROPE), F32)
    return (jnp.concatenate([cos, cos, zero], -1),
            jnp.concatenate([-sin, sin, zero], -1))


def _rope_weight_groups(w_rope):
    kdim = w_rope.shape[0]
    half = D_ROPE // 2
    zero = jnp.zeros((kdim, LANES - D_ROPE), w_rope.dtype)
    x1, x2 = w_rope[:, :half], w_rope[:, half:]
    return jnp.concatenate([x1, x2, zero, x2, x1, zero], axis=-1)


def _mla_kv_kernel(x_ref, wd_ref, nw_ref, wu_ref, cos_ref, sin_ref, k_ref, v_ref):
    x = x_ref[0]
    ckv = _bdot(x, wd_ref[...])
    c = _rms_norm(ckv[:, :KV_LORA], nw_ref[...])
    k_rope = (ckv[:, KV_LORA:KV_LORA + LANES] * cos_ref[...]
              + ckv[:, KV_LORA + LANES:] * sin_ref[...]).astype(k_ref.dtype)
    kv = _bdot(c, wu_ref[...])
    per = D_NOPE + D_V
    for h in range(MLA_HEADS):
        k_ref[0, h, :, :D_NOPE] = kv[:, h * per:h * per + D_NOPE].astype(k_ref.dtype)
        k_ref[0, h, :, D_NOPE:] = k_rope
        v_ref[0, h] = kv[:, h * per + D_NOPE:(h + 1) * per].astype(v_ref.dtype)


def _mla_kv(hid, w_dkv, kv_norm_w, w_ukv, cos_t, sin_t, tl=512):
    b, l, d = hid.shape
    tl = min(tl, l)
    wd = jnp.concatenate([w_dkv[:, :KV_LORA], _rope_weight_groups(w_dkv[:, KV_LORA:])], -1).astype(BF16)
    nd = wd.shape[1]
    hh = MLA_HEADS
    return pl.pallas_call(
        _mla_kv_kernel,
        grid=(b, l // tl),
        in_specs=[pl.BlockSpec((1, tl, d), lambda bi, i: (bi, i, 0)),
                  pl.BlockSpec((d, nd), lambda bi, i: (0, 0)),
                  pl.BlockSpec((1, KV_LORA), lambda bi, i: (0, 0)),
                  pl.BlockSpec((KV_LORA, hh * (D_NOPE + D_V)), lambda bi, i: (0, 0)),
                  pl.BlockSpec((tl, LANES), lambda bi, i: (i, 0)),
                  pl.BlockSpec((tl, LANES), lambda bi, i: (i, 0))],
        out_specs=[pl.BlockSpec((1, hh, tl, D_NOPE + LANES), lambda bi, i: (bi, 0, i, 0)),
                   pl.BlockSpec((1, hh, tl, D_V), lambda bi, i: (bi, 0, i, 0))],
        out_shape=[jax.ShapeDtypeStruct((b, hh, l, D_NOPE + LANES), BF16),
                   jax.ShapeDtypeStruct((b, hh, l, D_V), BF16)],
        compiler_params=_cparams("parallel", "parallel"),
        name="mla_kv",
    )(hid, wd, kv_norm_w.reshape(1, KV_LORA), w_ukv.astype(BF16), cos_t, sin_t)


def _mla_cq_kernel(x_ref, w_ref, nw_ref, o_ref):
    o_ref[...] = _rms_norm(_bdot(x_ref[...], w_ref[...]), nw_ref[...]).astype(o_ref.dtype)


def _mla_cq(x, w_dq, qnorm_w, tm=512):
    m, k = x.shape
    tm = min(tm, m)
    return pl.pallas_call(
        _mla_cq_kernel,
        grid=(m // tm,),
        in_specs=[pl.BlockSpec((tm, k), lambda i: (i, 0)),
                  pl.BlockSpec((k, Q_LORA), lambda i: (0, 0)),
                  pl.BlockSpec((1, Q_LORA), lambda i: (0, 0))],
        out_specs=pl.BlockSpec((tm, Q_LORA), lambda i: (i, 0)),
        out_shape=jax.ShapeDtypeStruct((m, Q_LORA), BF16),
        compiler_params=_cparams("parallel"),
        name="mla_cq",
    )(x, w_dq.astype(BF16), qnorm_w.reshape(1, Q_LORA))


def _mla_q_kernel(c_ref, w_ref, cos_ref, sin_ref, q_ref):
    c = c_ref[0]
    scale = (D_NOPE + D_ROPE) ** -0.5 * LOG2_E
    per = D_NOPE + 2 * LANES
    for h in range(MLA_HEADS):
        qh = _bdot(c, w_ref[:, h * per:(h + 1) * per])
        rope = qh[:, D_NOPE:D_NOPE + LANES] * cos_ref[...] + qh[:, D_NOPE + LANES:] * sin_ref[...]
        q_ref[0, h, :, :D_NOPE] = (qh[:, :D_NOPE] * scale).astype(q_ref.dtype)
        q_ref[0, h, :, D_NOPE:] = (rope * scale).astype(q_ref.dtype)


def _mla_q(cq, w_uq, cos_t, sin_t, tl=512):
    b, l, _ = cq.shape
    tl = min(tl, l)
    hh = MLA_HEADS
    per_in = D_NOPE + D_ROPE
    groups = []
    for h in range(hh):
        wh = w_uq[:, h * per_in:(h + 1) * per_in]
        groups += [wh[:, :D_NOPE], _rope_weight_groups(wh[:, D_NOPE:])]
    w = jnp.concatenate(groups, -1).astype(BF16)
    return pl.pallas_call(
        _mla_q_kernel,
        grid=(b, l // tl),
        in_specs=[pl.BlockSpec((1, tl, Q_LORA), lambda bi, i: (bi, i, 0)),
                  pl.BlockSpec(w.shape, lambda bi, i: (0, 0)),
                  pl.BlockSpec((tl, LANES), lambda bi, i: (i, 0)),
                  pl.BlockSpec((tl, LANES), lambda bi, i: (i, 0))],
        out_specs=pl.BlockSpec((1, hh, tl, D_NOPE + LANES), lambda bi, i: (bi, 0, i, 0)),
        out_shape=jax.ShapeDtypeStruct((b, hh, l, D_NOPE + LANES), BF16),
        compiler_params=_cparams("parallel", "parallel"),
        name="mla_q",
    )(cq, w, cos_t, sin_t)


def _mla_attn_kernel(q_ref, k_ref, v_ref, o_ref, *, tk):
    qi = pl.program_id(2)
    q = q_ref[0, 0]
    tq = q.shape[0]
    per = tq // tk

    def scores(j):
        start = pl.multiple_of(j * tk, tk)
        kb = k_ref[0, 0, pl.ds(start, tk), :]
        vb = v_ref[0, 0, pl.ds(start, tk), :]
        return _bdot_nt(q, kb), vb

    def update(carry, s, vb):
        m, l, acc = carry
        m_new = jnp.maximum(m, jnp.max(s, -1, keepdims=True))
        alpha = jnp.exp2(m - m_new)
        p = jnp.exp2(s - m_new)
        l = alpha * l + jnp.sum(p, -1, keepdims=True)
        acc = alpha * acc + _bdot(p, vb)
        return m_new, l, acc

    def full_block(j, carry):
        s, vb = scores(j)
        return update(carry, s, vb)

    carry = (jnp.full((tq, 1), -jnp.inf, F32), jnp.zeros((tq, 1), F32), jnp.zeros((tq, D_V), F32))
    carry = lax.fori_loop(0, qi * per, full_block, carry)
    qpos = lax.broadcasted_iota(jnp.int32, (tq, tk), 0)
    kpos = lax.broadcasted_iota(jnp.int32, (tq, tk), 1)
    for d in range(per):
        s, vb = scores(qi * per + d)
        s = jnp.where(kpos + d * tk <= qpos, s, -jnp.inf)
        carry = update(carry, s, vb)
    _, l, acc = carry
    o_ref[0] = (acc / l).astype(o_ref.dtype)


def _mla_attn(q, k, v, tq=512, tk=512):
    b, hh, l, dq = q.shape
    tq = min(tq, l)
    tk = min(tk, tq)
    return pl.pallas_call(
        functools.partial(_mla_attn_kernel, tk=tk),
        grid=(b, hh, l // tq),
        in_specs=[pl.BlockSpec((1, 1, tq, dq), lambda bi, h, i: (bi, h, i, 0)),
                  pl.BlockSpec((1, 1, l, dq), lambda bi, h, i: (bi, h, 0, 0)),
                  pl.BlockSpec((1, 1, l, D_V), lambda bi, h, i: (bi, h, 0, 0))],
        out_specs=pl.BlockSpec((1, tq, D_V), lambda bi, h, i: (bi, i, h)),
        out_shape=jax.ShapeDtypeStruct((b, l, hh * D_V), BF16),
        compiler_params=_cparams("parallel", "parallel", "parallel"),
        name="mla_attn",
    )(q, k, v)


def _mla_ln(hid, w_dq, qnorm_w, w_uq, w_o, k, v, cos_t, sin_t, ln_g, ln_b):
    b, l, d = hid.shape
    x = hid.reshape(b * l, d)
    cq = _mla_cq(x, w_dq, qnorm_w).reshape(b, l, Q_LORA)
    q = _mla_q(cq, w_uq, cos_t, sin_t)
    o = _mla_attn(q, k, v)
    out = _matmul_ln(o.reshape(b * l, MLA_HEADS * D_V), w_o.astype(BF16), x, ln_g, ln_b)
    return out.reshape(b, l, d)


def _first_argmax(x, ids, n):
    m = jnp.max(x, axis=0, keepdims=True)
    first = jnp.min(jnp.where(x == m, ids, n), axis=0, keepdims=True)
    return m, first


def _router_kernel(x_ref, wt_ref, bias_ref, idx_ref, w_ref, rank_ref, cnt_out_ref, cnt_ref, *, steps_per_tile):
    @pl.when(pl.program_id(0) % steps_per_tile == 0)
    def _():
        cnt_ref[...] = jnp.zeros_like(cnt_ref)

    x = x_ref[...]
    t = x.shape[0]
    logits = lax.dot_general(wt_ref[...], x, (((1,), (1,)), ((), ())),
                             precision=HIGHEST, preferred_element_type=F32)
    scores = jax.nn.sigmoid(logits)
    biased = scores + bias_ref[...]
    neg = -jnp.inf
    sub = lax.broadcasted_iota(jnp.int32, (GROUP_SIZE, t), 0).astype(F32)
    gscores = []
    for g in range(N_GROUPS):
        xg = biased[g * GROUP_SIZE:(g + 1) * GROUP_SIZE, :]
        m1, i1 = _first_argmax(xg, sub, float(GROUP_SIZE))
        m2 = jnp.max(jnp.where(sub == i1, neg, xg), axis=0, keepdims=True)
        gscores.append(m1 + m2)
    gs = jnp.concatenate(gscores, axis=0)
    gid = lax.broadcasted_iota(jnp.int32, (N_GROUPS, t), 0).astype(F32)
    gsel = jnp.zeros((N_GROUPS, t), F32)
    for _ in range(TOPK_GROUPS):
        _, gi = _first_argmax(gs, gid, float(N_GROUPS))
        hit = gid == gi
        gsel = jnp.where(hit, 1.0, gsel)
        gs = jnp.where(hit, neg, gs)
    eid = lax.broadcasted_iota(jnp.int32, (N_EXPERTS, t), 0).astype(F32)
    allowed = jnp.concatenate(
        [jnp.broadcast_to(gsel[g:g + 1, :], (GROUP_SIZE, t)) for g in range(N_GROUPS)], axis=0)
    cand = jnp.where(allowed > 0.0, biased, neg)
    idxs, ws, hits = [], [], []
    for _ in range(TOP_K):
        _, ei = _first_argmax(cand, eid, float(N_EXPERTS))
        hit = eid == ei
        idxs.append(ei)
        hits.append(hit)
        ws.append(jnp.sum(jnp.where(hit, scores, 0.0), axis=0, keepdims=True))
        cand = jnp.where(hit, neg, cand)
    w = jnp.concatenate(ws, axis=0)
    w = w / (jnp.sum(w, axis=0, keepdims=True) + 1e-20) * ROUTED_SCALE
    idx_ref[...] = jnp.concatenate(idxs, axis=0).astype(jnp.int32)
    w_ref[...] = w
    chosen = jnp.zeros((N_EXPERTS, t), F32)
    for hit in hits:
        chosen = jnp.where(hit, 1.0, chosen)
    before = (lax.broadcasted_iota(jnp.int32, (t, t), 0) < lax.broadcasted_iota(jnp.int32, (t, t), 1))
    prior = jnp.dot(chosen.astype(BF16), jnp.where(before, 1.0, 0.0).astype(BF16),
                    preferred_element_type=F32) + cnt_ref[...]
    rank_ref[...] = jnp.concatenate(
        [jnp.sum(jnp.where(hit, prior, 0.0), axis=0, keepdims=True) for hit in hits], axis=0).astype(jnp.int32)
    cnt_ref[...] += jnp.sum(chosen, axis=1, keepdims=True)
    cnt_out_ref[0] = cnt_ref[...]


def _router(x, w_router, bias, tile_tokens, tr=512):
    t, d = x.shape
    tr = min(tr, tile_tokens)
    steps_per_tile = tile_tokens // tr
    n_tiles = t // tile_tokens
    kt = pl.BlockSpec((TOP_K, tr), lambda i: (0, i))
    idx, w, rank, counts = pl.pallas_call(
        functools.partial(_router_kernel, steps_per_tile=steps_per_tile),
        grid=(t // tr,),
        in_specs=[pl.BlockSpec((tr, d), lambda i: (i, 0)),
                  pl.BlockSpec((N_EXPERTS, d), lambda i: (0, 0)),
                  pl.BlockSpec((N_EXPERTS, 1), lambda i: (0, 0))],
        out_specs=[kt, kt, kt,
                   pl.BlockSpec((1, N_EXPERTS, 1), lambda i: (i // steps_per_tile, 0, 0))],
        out_shape=[jax.ShapeDtypeStruct((TOP_K, t), jnp.int32),
                   jax.ShapeDtypeStruct((TOP_K, t), F32),
                   jax.ShapeDtypeStruct((TOP_K, t), jnp.int32),
                   jax.ShapeDtypeStruct((n_tiles, N_EXPERTS, 1), F32)],
        scratch_shapes=[pltpu.VMEM((N_EXPERTS, 1), F32)],
        compiler_params=_cparams("arbitrary"),
        name="moe_router",
    )(x, w_router.T, bias.reshape(N_EXPERTS, 1))
    return idx, w, rank, counts.reshape(n_tiles, N_EXPERTS).astype(jnp.int32)


def _blocks_per_tile(tile_tokens):
    nb = tile_tokens * TOP_K // EXPERT_ROWS + N_EXPERTS
    return -(-nb // BLOCKS_PER_STEP) * BLOCKS_PER_STEP


def _plan_kernel(fill_lo_ref, fill_hi_ref, pos_ref, w_ref, row_ref, sw_ref, *, tile_tokens, part):
    i = pl.program_id(0)
    q = pl.program_id(1)
    spare = tile_tokens * SUBLANES
    unroll = 8
    shift = SUBLANES.bit_length() - 1

    @pl.when(q == 0)
    def _():
        def fill_range(g, carry):
            lo = fill_lo_ref[i * (N_EXPERTS + 1) + g]
            hi = fill_hi_ref[i * (N_EXPERTS + 1) + g]

            def fill(b, carry):
                for d in range(unroll):
                    s = jnp.maximum(hi - 1 - b * unroll - d, 0)
                    row_ref[0, 0, s] = spare
                    sw_ref[0, 0, s] = 0.0
                return carry
            return lax.fori_loop(0, (hi - lo + unroll - 1) // unroll, fill, carry)
        lax.fori_loop(0, N_EXPERTS + 1, fill_range, 0)

    def place(a, carry):
        p = pos_ref[0, 0, a]
        row_ref[0, 0, p] = ((q * part + a) >> shift) << shift
        sw_ref[0, 0, p] = w_ref[0, 0, a]
        return carry
    lax.fori_loop(0, part, place, 0, unroll=unroll)


def _dispatch_plan(idx_t, w_t, rank_t, counts, tile_tokens):
    t = idx_t.shape[1]
    n_tiles = t // tile_tokens
    nb = _blocks_per_tile(tile_tokens)
    padded = (counts + EXPERT_ROWS - 1) // EXPERT_ROWS * EXPERT_ROWS
    pad_end = jnp.cumsum(padded, axis=1)
    pad_start = (pad_end - padded).astype(jnp.int32)
    nblk = (pad_end[:, -1] // EXPERT_ROWS).astype(jnp.int32)
    starts = jnp.arange(nb, dtype=jnp.int32) * EXPERT_ROWS
    blk_e = jnp.sum(starts[None, :, None] >= pad_end[:, None, :], axis=-1).astype(jnp.int32)
    blk_e = jnp.minimum(blk_e, N_EXPERTS - 1)
    last = jnp.take_along_axis(blk_e, jnp.maximum(nblk - 1, 0)[:, None], axis=1)
    blk_e = jnp.where(starts[None, :] // EXPERT_ROWS < nblk[:, None], blk_e, last)

    start_of = jnp.repeat(pad_start.T, tile_tokens, axis=1)
    experts = jnp.arange(N_EXPERTS, dtype=jnp.int32)[None, :, None]
    pos_t = rank_t + jnp.sum(jnp.where(idx_t[:, None, :] == experts, start_of[None], 0), axis=1)
    step_rows = BLOCKS_PER_STEP * EXPERT_ROWS
    tail_lo = pad_end[:, -1:]
    fill_lo = jnp.concatenate([pad_start + counts, tail_lo], axis=1).astype(jnp.int32)
    fill_hi = jnp.concatenate([pad_end, (tail_lo + step_rows - 1) // step_rows * step_rows], axis=1).astype(jnp.int32)

    parts = PLAN_PARTS
    part = tile_tokens * TOP_K // parts
    flat = lambda a: a.T.reshape(n_tiles * parts, 1, part)
    part_spec = pl.BlockSpec((1, 1, part), lambda i, q, lo, hi: (i * parts + q, 0, 0), memory_space=pltpu.SMEM)
    table_spec = pl.BlockSpec((1, 1, nb * EXPERT_ROWS), lambda i, q, lo, hi: (i, 0, 0), memory_space=pltpu.SMEM)
    rows, slot_w = pl.pallas_call(
        functools.partial(_plan_kernel, tile_tokens=tile_tokens, part=part),
        grid_spec=pltpu.PrefetchScalarGridSpec(
            num_scalar_prefetch=2,
            grid=(n_tiles, parts),
            in_specs=[part_spec, part_spec],
            out_specs=[table_spec, table_spec],
        ),
        out_shape=[jax.ShapeDtypeStruct((n_tiles, 1, nb * EXPERT_ROWS), jnp.int32),
                   jax.ShapeDtypeStruct((n_tiles, 1, nb * EXPERT_ROWS), F32)],
        compiler_params=_cparams("parallel", "arbitrary"),
        name="moe_plan",
    )(fill_lo.reshape(-1), fill_hi.reshape(-1), flat(pos_t), flat(w_t))
    return rows, slot_w, blk_e.reshape(-1), nblk, nb


def _experts_kernel(blk_e_ref, nblk_ref, row_ref, sw_ref, x_ref, *refs):
    n_sub = BLOCKS_PER_STEP
    w_refs = [refs[3 * s:3 * s + 3] for s in range(n_sub)]
    acc_ref = refs[3 * n_sub]
    xs_refs = refs[3 * n_sub + 1:3 * n_sub + 1 + n_sub]
    ys_refs = refs[3 * n_sub + 1 + n_sub:]
    i = pl.program_id(0)
    j = pl.program_id(1)
    rows = EXPERT_ROWS
    pitch = STAGE_PITCH

    @pl.when(j == 0)
    def _():
        acc_ref[...] = jnp.zeros_like(acc_ref)

    def token_rows(slot):
        return pl.ds(pl.multiple_of(row_ref[0, 0, slot], SUBLANES), SUBLANES)

    @pl.when(j * n_sub < nblk_ref[i])
    def _():
        for s in range(n_sub):
            for r in range(rows):
                xs_refs[s][pl.ds(r, ROW_CHUNKS, stride=pitch), :] = x_ref[0, token_rows(s * rows + r), :]
        ys = []
        for s in range(n_sub):
            wg_ref, wu_ref, wd_ref = w_refs[s]
            x = jnp.concatenate([xs_refs[s][pl.ds(c * pitch, rows), :] for c in range(ROW_CHUNKS)], axis=1)
            hid = _silu(_bdot(x, wg_ref[0])) * _bdot(x, wu_ref[0])
            ys.append(_bdot(hid, wd_ref[0]) * sw_ref[0, s * rows:(s + 1) * rows, :])
        for s in range(n_sub):
            for c in range(ROW_CHUNKS):
                ys_refs[s][pl.ds(c * pitch, rows), :] = ys[s][:, c * LANES:(c + 1) * LANES]
        for s in range(n_sub):
            for r0 in range(0, rows, SCATTER_BATCH):
                new = [acc_ref[0, token_rows(s * rows + r), :] + ys_refs[s][pl.ds(r, ROW_CHUNKS, stride=pitch), :]
                       for r in range(r0, r0 + SCATTER_BATCH)]
                for r, val in zip(range(r0, r0 + SCATTER_BATCH), new):
                    acc_ref[0, token_rows(s * rows + r), :] = val


def _routed_experts(x, idx_t, w_t, rank_t, counts, w_gate, w_up, w_down, tile_tokens):
    t, d = x.shape
    n_tiles = t // tile_tokens
    n_sub = BLOCKS_PER_STEP
    rows, slot_w, blk_e, nblk, nb = _dispatch_plan(idx_t, w_t, rank_t, counts, tile_tokens)
    steps = nb // n_sub
    step_rows = n_sub * EXPERT_ROWS
    rows = rows.reshape(n_tiles * steps, 1, step_rows)
    slot_w = slot_w.reshape(n_tiles * steps, step_rows, 1)
    slab_rows = (tile_tokens + 1) * ROW_CHUNKS
    x_slabs = jnp.pad(x.reshape(n_tiles, tile_tokens, d), ((0, 0), (0, 1), (0, 0))).reshape(n_tiles, slab_rows, LANES)

    def src_blk(i, j, be, nk):
        return (i * steps + jnp.minimum(j, jnp.maximum(nk[i] - 1, 0) // n_sub), 0, 0)

    def weight_specs(s):
        pick = lambda i, j, be, nk: (be[i * nb + j * n_sub + s], 0, 0)
        return [pl.BlockSpec((1, d, D_EXPERT), pick), pl.BlockSpec((1, d, D_EXPERT), pick),
                pl.BlockSpec((1, D_EXPERT, d), pick)]

    stage = pltpu.VMEM((ROW_CHUNKS * STAGE_PITCH, LANES), F32)
    grid_spec = pltpu.PrefetchScalarGridSpec(
        num_scalar_prefetch=2,
        grid=(n_tiles, steps),
        in_specs=[
            pl.BlockSpec((1, 1, step_rows), src_blk, memory_space=pltpu.SMEM),
            pl.BlockSpec((1, step_rows, 1), src_blk),
            pl.BlockSpec((1, slab_rows, LANES), lambda i, j, be, nk: (i, 0, 0)),
        ] + [spec for s in range(n_sub) for spec in weight_specs(s)],
        out_specs=pl.BlockSpec((1, slab_rows, LANES), lambda i, j, be, nk: (i, 0, 0)),
        scratch_shapes=[stage] * (2 * n_sub),
    )
    return pl.pallas_call(
        _experts_kernel,
        grid_spec=grid_spec,
        out_shape=jax.ShapeDtypeStruct((n_tiles, slab_rows, LANES), F32),
        compiler_params=_cparams("parallel", "arbitrary"),
        name="moe_experts",
    )(blk_e, nblk, rows, slot_w, x_slabs, *([w_gate, w_up, w_down] * n_sub))


def _moe_out_kernel(x_ref, r_ref, wg_ref, wu_ref, wd_ref, g_ref, b_ref, o_ref):
    x = x_ref[...]
    tm = x.shape[0]
    hid = _silu(_bdot(x, wg_ref[...])) * _bdot(x, wu_ref[...])
    routed = jnp.concatenate([r_ref[0, pl.ds(c, tm, stride=ROW_CHUNKS), :] for c in range(ROW_CHUNKS)], axis=1)
    y = routed + _bdot(hid, wd_ref[...])
    o_ref[...] = _layer_norm(DN_ALPHA * x + y, g_ref[...], b_ref[...])


def _moe_out(x, routed, ws_gate, ws_up, ws_down, g, b, tile_tokens, tm=512):
    t, d = x.shape
    tm = min(tm, tile_tokens)
    per_tile = tile_tokens // tm
    return pl.pallas_call(
        _moe_out_kernel,
        grid=(t // tm,),
        in_specs=[pl.BlockSpec((tm, d), lambda i: (i, 0)),
                  pl.BlockSpec((1, tm * ROW_CHUNKS, LANES), lambda i: (i // per_tile, i % per_tile, 0)),
                  pl.BlockSpec((d, D_SHARED), lambda i: (0, 0)),
                  pl.BlockSpec((d, D_SHARED), lambda i: (0, 0)),
                  pl.BlockSpec((D_SHARED, d), lambda i: (0, 0)),
                  pl.BlockSpec((1, d), lambda i: (0, 0)),
                  pl.BlockSpec((1, d), lambda i: (0, 0))],
        out_specs=pl.BlockSpec((tm, d), lambda i: (i, 0)),
        out_shape=jax.ShapeDtypeStruct((t, d), F32),
        compiler_params=_cparams("parallel"),
        name="moe_out",
    )(x, routed, ws_gate.astype(BF16), ws_up.astype(BF16), ws_down.astype(BF16),
      g.reshape(1, d), b.reshape(1, d))


def _moe_ln(hid, w_router, router_bias, w_gate, w_up, w_down, ws_gate, ws_up, ws_down, ln_g, ln_b):
    b, l, d = hid.shape
    x = hid.reshape(b * l, d)
    tile_tokens = min(MOE_TILE_TOKENS, b * l)
    idx_t, w_t, rank_t, counts = _router(x, w_router, router_bias, tile_tokens)
    routed = _routed_experts(x, idx_t, w_t, rank_t, counts, w_gate.astype(BF16), w_up.astype(BF16),
                             w_down.astype(BF16), tile_tokens)
    return _moe_out(x, routed, ws_gate, ws_up, ws_down, ln_g, ln_b, tile_tokens).reshape(b, l, d)


def kernel(x, a_w_in, a_conv_w, a_A_log, a_dt_bias, a_onorm_w, a_w_out, b_w_dq, b_qnorm_w, b_w_uq, b_w_o,
           kv_w_dkv, kv_norm_w, kv_w_ukv, ln1_g, ln1_b, ln2_g, ln2_b, moe_w_router, moe_router_bias,
           moe_w_gate, moe_w_up, moe_w_down, moe_ws_gate, moe_ws_up, moe_ws_down):
    l = x.shape[1]
    cos_t, sin_t = _rope_lane_tables(l)
    h = x
    k = v = None
    for layer in range(DEPTH):
        if layer < N_A_LAYERS:
            i = layer
            h = _gated_deltanet_ln(h, a_w_in[i], a_conv_w[i], a_A_log[i], a_dt_bias[i], a_onorm_w[i],
                                   a_w_out[i], ln1_g[layer], ln1_b[layer])
        else:
            i = layer - N_A_LAYERS
            h = _mla_ln(h, b_w_dq[i], b_qnorm_w[i], b_w_uq[i], b_w_o[i], k, v, cos_t, sin_t,
                        ln1_g[layer], ln1_b[layer])
        h = _moe_ln(h, moe_w_router[layer], moe_router_bias[layer], moe_w_gate[layer], moe_w_up[layer],
                    moe_w_down[layer], moe_ws_gate[layer], moe_ws_up[layer], moe_ws_down[layer],
                    ln2_g[layer], ln2_b[layer])
        if layer == N_A_LAYERS - 1:
            k, v = _mla_kv(h, kv_w_dkv, kv_norm_w, kv_w_ukv, cos_t, sin_t)
    return h
```

```python
import functools

import jax
import jax.numpy as jnp
import numpy as np
from jax import lax
from jax.experimental import pallas as pl
from jax.experimental.pallas import tpu as pltpu

F32 = jnp.float32
BF16 = jnp.bfloat16
HIGHEST = lax.Precision.HIGHEST

D_MODEL = 1024
DEPTH = 4
N_A_LAYERS = DEPTH // 2
GDN_HEADS = 8
GDN_DK = 128
GDN_DV = 128
D_QK = GDN_HEADS * GDN_DK
D_VA = GDN_HEADS * GDN_DV
GDN_CONV_CH = 2 * D_QK + D_VA
CONV_WIDTH = 4
CHUNK = 64
MLA_HEADS = 8
Q_LORA = 512
KV_LORA = 256
D_NOPE = 128
D_ROPE = 64
D_V = 128
ROPE_THETA = 10000.0
N_EXPERTS = 64
TOP_K = 8
N_GROUPS = 8
GROUP_SIZE = N_EXPERTS // N_GROUPS
TOPK_GROUPS = 4
D_EXPERT = 256
D_SHARED = 256
ROUTED_SCALE = 2.5
DN_ALPHA = (2 * DEPTH) ** 0.25
EPS = 1e-6
LN_EPS = 1e-5
LOG2_E = 1.4426950408889634

LANES = 128
SUBLANES = 8
VMEM_LIMIT_BYTES = 56 * 1024 * 1024

EXPERT_ROWS = 128
ROW_CHUNKS = D_MODEL // LANES
STAGE_PITCH = EXPERT_ROWS + 1
SCATTER_BATCH = 16
BLOCKS_PER_STEP = 2
STEP_ROWS = BLOCKS_PER_STEP * EXPERT_ROWS
MOE_TILE_TOKENS = 4096
PLAN_PARTS = 4
assert TOP_K == SUBLANES


def _cparams(*sem):
    return pltpu.CompilerParams(dimension_semantics=sem, vmem_limit_bytes=VMEM_LIMIT_BYTES)


def _silu(x):
    return x * jax.nn.sigmoid(x)


def _layer_norm(x, g, b):
    mu = jnp.mean(x, -1, keepdims=True)
    xc = x - mu
    var = jnp.mean(xc * xc, -1, keepdims=True)
    return xc * lax.rsqrt(var + LN_EPS) * g + b


def _rms_norm(x, w):
    return x * lax.rsqrt(jnp.mean(x * x, -1, keepdims=True) + EPS) * w


def _bdot(a, b):
    return jnp.dot(a.astype(BF16), b.astype(BF16), preferred_element_type=F32)


def _bdot_nt(a, b):
    return lax.dot_general(a.astype(BF16), b.astype(BF16), (((1,), (1,)), ((), ())),
                           preferred_element_type=F32)


def _hdot(a, b):
    return jnp.dot(a, b, precision=HIGHEST, preferred_element_type=F32)


def _matmul_kernel(x_ref, w_ref, o_ref):
    o_ref[...] = _bdot(x_ref[...], w_ref[...]).astype(o_ref.dtype)


def _matmul(x, w, out_dtype, tm, tn):
    m, k = x.shape
    n = w.shape[1]
    tm = min(tm, m)
    tn = min(tn, n)
    return pl.pallas_call(
        _matmul_kernel,
        grid=(m // tm, n // tn),
        in_specs=[pl.BlockSpec((tm, k), lambda i, j: (i, 0)),
                  pl.BlockSpec((k, tn), lambda i, j: (0, j))],
        out_specs=pl.BlockSpec((tm, tn), lambda i, j: (i, j)),
        out_shape=jax.ShapeDtypeStruct((m, n), out_dtype),
        compiler_params=_cparams("parallel", "parallel"),
        name="matmul",
    )(x, w)


def _matmul_ln_kernel(x_ref, w_ref, r_ref, g_ref, b_ref, o_ref):
    mix = _bdot(x_ref[...], w_ref[...])
    o_ref[...] = _layer_norm(DN_ALPHA * r_ref[...] + mix, g_ref[...], b_ref[...])


def _matmul_ln(x, w, res, g, b, tm=512):
    m, k = x.shape
    n = w.shape[1]
    tm = min(tm, m)
    return pl.pallas_call(
        _matmul_ln_kernel,
        grid=(m // tm,),
        in_specs=[pl.BlockSpec((tm, k), lambda i: (i, 0)),
                  pl.BlockSpec((k, n), lambda i: (0, 0)),
                  pl.BlockSpec((tm, n), lambda i: (i, 0)),
                  pl.BlockSpec((1, n), lambda i: (0, 0)),
                  pl.BlockSpec((1, n), lambda i: (0, 0))],
        out_specs=pl.BlockSpec((tm, n), lambda i: (i, 0)),
        out_shape=jax.ShapeDtypeStruct((m, n), F32),
        compiler_params=_cparams("parallel"),
        name="matmul_ln",
    )(x, w, res, g.reshape(1, n), b.reshape(1, n))


def _softplus(x):
    return jnp.maximum(x, 0.0) + jnp.log1p(jnp.exp(-jnp.abs(x)))


def _gdn_gates_kernel(x_ref, wab_ref, wabt_ref, alog_ref, dtb_ref, alogt_ref, dtbt_ref,
                      gc_ref, beta_ref, gct_ref):
    x = x_ref[...]
    tl = x.shape[0]
    h = GDN_HEADS
    ab = _hdot(x, wab_ref[...])
    abt = lax.dot_general(wabt_ref[...], x, (((1,), (1,)), ((), ())),
                          precision=HIGHEST, preferred_element_type=F32)
    g = -jnp.exp(alog_ref[...]) * _softplus(ab[:, :h] + dtb_ref[...])
    gt = -jnp.exp(alogt_ref[...]) * _softplus(abt[:h, :] + dtbt_ref[...])
    beta_ref[...] = jax.nn.sigmoid(ab[:, h:2 * h])
    row = lax.broadcasted_iota(jnp.int32, (tl, tl), 0)
    col = lax.broadcasted_iota(jnp.int32, (tl, tl), 1)
    shift = CHUNK.bit_length() - 1
    same = (row >> shift) == (col >> shift)
    lower = jnp.where(same & (col <= row), 1.0, 0.0).astype(F32)
    upper = jnp.where(same & (row <= col), 1.0, 0.0).astype(F32)
    gc_ref[...] = _hdot(lower, g)
    gct_ref[...] = _hdot(gt, upper)


def _gdn_gates(x, w_ab, a_log, dt_bias, tl=512):
    t = x.shape[0]
    tl = min(tl, t)
    h = GDN_HEADS
    wab = jnp.zeros((D_MODEL, LANES), F32).at[:, :2 * h].set(w_ab)
    wabt = w_ab.T
    return pl.pallas_call(
        _gdn_gates_kernel,
        grid=(t // tl,),
        in_specs=[pl.BlockSpec((tl, D_MODEL), lambda i: (i, 0)),
                  pl.BlockSpec((D_MODEL, LANES), lambda i: (0, 0)),
                  pl.BlockSpec((2 * h, D_MODEL), lambda i: (0, 0)),
                  pl.BlockSpec((1, h), lambda i: (0, 0)),
                  pl.BlockSpec((1, h), lambda i: (0, 0)),
                  pl.BlockSpec((h, 1), lambda i: (0, 0)),
                  pl.BlockSpec((h, 1), lambda i: (0, 0))],
        out_specs=[pl.BlockSpec((tl, h), lambda i: (i, 0)),
                   pl.BlockSpec((tl, h), lambda i: (i, 0)),
                   pl.BlockSpec((h, tl), lambda i: (0, i))],
        out_shape=[jax.ShapeDtypeStruct((t, h), F32),
                   jax.ShapeDtypeStruct((t, h), F32),
                   jax.ShapeDtypeStruct((h, t), F32)],
        compiler_params=_cparams("parallel"),
        name="gdn_gates",
    )(x, wab, wabt, a_log.reshape(1, h), dt_bias.reshape(1, h),
      a_log.reshape(h, 1), dt_bias.reshape(h, 1))


def _gdn_conv_kernel(cur_ref, prev_ref, w_ref, q_ref, k_ref, v_ref):
    i = pl.program_id(1)
    x = cur_ref[0]
    tl = x.shape[0]
    prev = jnp.where(i > 0, prev_ref[0], 0.0)
    xx = jnp.concatenate([prev, x], axis=0)
    w = w_ref[...]
    y = None
    for j in range(CONV_WIDTH):
        off = SUBLANES - (CONV_WIDTH - 1) + j
        term = xx[off:off + tl, :] * w[j:j + 1, :]
        y = term if y is None else y + term
    y = _silu(y)
    for h in range(GDN_HEADS):
        sl = slice(h * GDN_DK, (h + 1) * GDN_DK)
        qh = y[:, sl]
        q_ref[0, :, sl] = qh * lax.rsqrt(jnp.sum(qh * qh, -1, keepdims=True) + EPS) * (GDN_DK ** -0.5)
        kh = y[:, D_QK + h * GDN_DK:D_QK + (h + 1) * GDN_DK]
        k_ref[0, :, sl] = kh * lax.rsqrt(jnp.sum(kh * kh, -1, keepdims=True) + EPS)
    v_ref[0] = y[:, 2 * D_QK:]


def _gdn_conv(proj, conv_w, tl=256):
    b, l, _ = proj.shape
    tl = min(tl, l)
    c3 = GDN_CONV_CH
    per = tl // SUBLANES
    out = jax.ShapeDtypeStruct((b, l, D_QK), F32)
    return pl.pallas_call(
        _gdn_conv_kernel,
        grid=(b, l // tl),
        in_specs=[pl.BlockSpec((1, tl, c3), lambda bi, i: (bi, i, 0)),
                  pl.BlockSpec((1, SUBLANES, c3), lambda bi, i: (bi, jnp.maximum(i * per - 1, 0), 0)),
                  pl.BlockSpec((CONV_WIDTH, c3), lambda bi, i: (0, 0))],
        out_specs=[pl.BlockSpec((1, tl, D_QK), lambda bi, i: (bi, i, 0))] * 3,
        out_shape=[out, out, out],
        compiler_params=_cparams("parallel", "parallel"),
        name="gdn_conv",
    )(proj, proj, conv_w)


def _split_bf16(x):
    hi = x.astype(BF16)
    lo = (x - hi.astype(F32)).astype(BF16)
    return hi, lo


def _dot_split(a, b):
    (ah, al), (bh, bl) = a, b
    dot = functools.partial(jnp.dot, preferred_element_type=F32)
    return dot(ah, bh) + (dot(ah, bl) + dot(al, bh))


def _gdn_prep_kernel(q_ref, k_ref, v_ref, gc_ref, beta_ref, gct_ref, wq_ref, u_ref, attn_ref, kgt_ref,
                     *, chunks):
    c = CHUNK
    row = lax.broadcasted_iota(jnp.int32, (c, c), 0)
    col = lax.broadcasted_iota(jnp.int32, (c, c), 1)
    incl = row >= col
    strict = row > col
    eye = jnp.where(row == col, 1.0, 0.0).astype(F32)
    items = [(ci, h) for ci in range(chunks) for h in range(GDN_HEADS)]

    st = {}
    for ci, h in items:
        rows = slice(ci * c, (ci + 1) * c)
        sl = slice(h * GDN_DK, (h + 1) * GDN_DK)
        q = q_ref[0, rows, sl]
        k = k_ref[0, rows, sl]
        v = v_ref[0, rows, sl]
        gcol = gc_ref[0, rows, h:h + 1]
        grow = gct_ref[0, ci, h:h + 1, :]
        bcol = beta_ref[0, rows, h:h + 1]
        decay = jnp.where(incl, jnp.exp(jnp.where(incl, gcol - grow, 0.0)), 0.0)
        kb = k * bcol
        eg = jnp.exp(gcol)
        g_last = gcol[c - 1:c, :]
        sc = _bdot_nt(jnp.concatenate([kb, q], axis=0), k)
        a = jnp.where(strict, sc[:c] * decay, 0.0)
        attn_ref[0, ci, h] = (sc[c:] * decay).astype(attn_ref.dtype)
        kgt_ref[0, ci, h] = (k * jnp.exp(g_last - gcol)).T.astype(kgt_ref.dtype)
        wq_ref[0, ci, h, c:, :] = (q * eg).astype(wq_ref.dtype)
        st[ci, h] = dict(m=-a, rhs=jnp.concatenate([v * bcol, kb * eg], axis=1))

    for it in items:
        st[it]["p"] = eye + st[it]["m"]
    span = 2
    while span < c:
        for it in items:
            ms = _split_bf16(st[it]["m"])
            st[it]["m"] = _dot_split(ms, ms)
        for it in items:
            st[it]["p"] = st[it]["p"] + _dot_split(_split_bf16(st[it]["p"]), _split_bf16(st[it]["m"]))
        span *= 2

    for ci, h in items:
        sol = _dot_split(_split_bf16(st[ci, h]["p"]), _split_bf16(st[ci, h]["rhs"]))
        u_ref[0, ci * c:(ci + 1) * c, h * GDN_DV:(h + 1) * GDN_DV] = sol[:, :GDN_DV]
        wq_ref[0, ci, h, :c, :] = sol[:, GDN_DV:].astype(wq_ref.dtype)


def _gdn_prep(q, k, v, gc, beta, gct, chunks=2):
    b, l, _ = q.shape
    n = l // CHUNK
    chunks = min(chunks, n)
    h = GDN_HEADS
    rows = chunks * CHUNK
    blk = pl.BlockSpec((1, rows, D_QK), lambda bi, i: (bi, i, 0))
    gate = pl.BlockSpec((1, rows, h), lambda bi, i: (bi, i, 0))
    return pl.pallas_call(
        functools.partial(_gdn_prep_kernel, chunks=chunks),
        grid=(b, n // chunks),
        in_specs=[blk, blk, blk, gate, gate,
                  pl.BlockSpec((1, chunks, h, CHUNK), lambda bi, i: (bi, i, 0, 0))],
        out_specs=[pl.BlockSpec((1, chunks, h, 2 * CHUNK, GDN_DK), lambda bi, i: (bi, i, 0, 0, 0)),
                   pl.BlockSpec((1, rows, D_VA), lambda bi, i: (bi, i, 0)),
                   pl.BlockSpec((1, chunks, h, CHUNK, CHUNK), lambda bi, i: (bi, i, 0, 0, 0)),
                   pl.BlockSpec((1, chunks, h, GDN_DK, CHUNK), lambda bi, i: (bi, i, 0, 0, 0))],
        out_shape=[jax.ShapeDtypeStruct((b, n, h, 2 * CHUNK, GDN_DK), BF16),
                   jax.ShapeDtypeStruct((b, l, D_VA), F32),
                   jax.ShapeDtypeStruct((b, n, h, CHUNK, CHUNK), BF16),
                   jax.ShapeDtypeStruct((b, n, h, GDN_DK, CHUNK), BF16)],
        compiler_params=_cparams("parallel", "parallel"),
        name="gdn_prep",
    )(q, k, v, gc, beta, gct)


def _gdn_state_kernel(wq_ref, u_ref, attn_ref, kgt_ref, gc_ref, z_ref, onw_ref, o_ref, s_ref, *, chunks):
    @pl.when(pl.program_id(1) == 0)
    def _():
        s_ref[...] = jnp.zeros_like(s_ref)

    c = CHUNK
    onw = onw_ref[...]
    dot = functools.partial(jnp.dot, preferred_element_type=F32)
    heads = range(GDN_HEADS)
    for ci in range(chunks):
        rows = slice(ci * c, (ci + 1) * c)
        s_old = [s_ref[h] for h in heads]
        r = [dot(wq_ref[0, ci, h], s_old[h].astype(BF16)) for h in heads]
        v_new = [(u_ref[0, rows, h * GDN_DV:(h + 1) * GDN_DV] - r[h][:c]).astype(BF16) for h in heads]
        for h in heads:
            decay_last = jnp.exp(gc_ref[0, (ci + 1) * c - 1:(ci + 1) * c, h:h + 1])
            s_ref[h] = s_old[h] * decay_last + dot(kgt_ref[0, ci, h], v_new[h])
        for h in heads:
            sl = slice(h * GDN_DV, (h + 1) * GDN_DV)
            o = r[h][c:] + dot(attn_ref[0, ci, h], v_new[h])
            o_ref[0, rows, sl] = (_rms_norm(o, onw) * _silu(z_ref[0, rows, sl])).astype(o_ref.dtype)


def _gdn_state(wq, u, attn, kgt, gc, proj, onorm_w, chunks=4):
    b, n, h = wq.shape[:3]
    l = n * CHUNK
    chunks = min(chunks, n)
    rows = chunks * CHUNK
    z_block = GDN_CONV_CH // D_VA
    return pl.pallas_call(
        functools.partial(_gdn_state_kernel, chunks=chunks),
        grid=(b, n // chunks),
        in_specs=[pl.BlockSpec((1, chunks, h, 2 * CHUNK, GDN_DK), lambda bi, i: (bi, i, 0, 0, 0)),
                  pl.BlockSpec((1, rows, D_VA), lambda bi, i: (bi, i, 0)),
                  pl.BlockSpec((1, chunks, h, CHUNK, CHUNK), lambda bi, i: (bi, i, 0, 0, 0)),
                  pl.BlockSpec((1, chunks, h, GDN_DK, CHUNK), lambda bi, i: (bi, i, 0, 0, 0)),
                  pl.BlockSpec((1, rows, h), lambda bi, i: (bi, i, 0)),
                  pl.BlockSpec((1, rows, D_VA), lambda bi, i: (bi, i, z_block)),
                  pl.BlockSpec((1, GDN_DV), lambda bi, i: (0, 0))],
        out_specs=pl.BlockSpec((1, rows, D_VA), lambda bi, i: (bi, i, 0)),
        out_shape=jax.ShapeDtypeStruct((b, l, D_VA), BF16),
        scratch_shapes=[pltpu.VMEM((h, GDN_DK, GDN_DV), F32)],
        compiler_params=_cparams("parallel", "arbitrary"),
        name="gdn_state",
    )(wq, u, attn, kgt, gc, proj, onorm_w.reshape(1, GDN_DV))


def _gated_deltanet_ln(hid, w_in, conv_w, a_log, dt_bias, onorm_w, w_out, ln_g, ln_b):
    b, l, d = hid.shape
    h = GDN_HEADS
    n = l // CHUNK
    x = hid.reshape(b * l, d)
    n_main = GDN_CONV_CH + D_VA
    proj = _matmul(x, w_in[:, :n_main].astype(BF16), F32, 512, 1024).reshape(b, l, n_main)
    gc, beta, gct = _gdn_gates(x, w_in[:, n_main:], a_log, dt_bias)
    gc = gc.reshape(b, l, h)
    gct = gct.reshape(h, b, n, CHUNK).transpose(1, 2, 0, 3)
    q, k, v = _gdn_conv(proj, conv_w)
    wq, u, attn, kgt = _gdn_prep(q, k, v, gc, beta.reshape(b, l, h), gct)
    o = _gdn_state(wq, u, attn, kgt, gc, proj, onorm_w)
    out = _matmul_ln(o.reshape(b * l, D_VA), w_out.astype(BF16), x, ln_g, ln_b)
    return out.reshape(b, l, d)


def _rope_lane_tables(l):
    inv = 1.0 / (ROPE_THETA ** (jnp.arange(0, D_ROPE, 2, dtype=F32) / D_ROPE))
    ang = jnp.arange(l, dtype=F32)[:, None] * inv[None, :]
    cos, sin = jnp.cos(ang), jnp.sin(ang)
    zero = jnp.zeros((l, LANES - D_ROPE), F32)
    return (jnp.concatenate([cos, cos, zero], -1),
            jnp.concatenate([-sin, sin, zero], -1))


def _rope_weight_groups(w_rope):
    kdim = w_rope.shape[0]
    half = D_ROPE // 2
    zero = jnp.zeros((kdim, LANES - D_ROPE), w_rope.dtype)
    x1, x2 = w_rope[:, :half], w_rope[:, half:]
    return jnp.concatenate([x1, x2, zero, x2, x1, zero], axis=-1)


def _mla_kv_kernel(x_ref, wd_ref, nw_ref, wu_ref, cos_ref, sin_ref, k_ref, v_ref):
    x = x_ref[0]
    ckv = _bdot(x, wd_ref[...])
    c = _rms_norm(ckv[:, :KV_LORA], nw_ref[...])
    k_rope = (ckv[:, KV_LORA:KV_LORA + LANES] * cos_ref[...]
              + ckv[:, KV_LORA + LANES:] * sin_ref[...]).astype(k_ref.dtype)
    kv = _bdot(c, wu_ref[...])
    per = D_NOPE + D_V
    for h in range(MLA_HEADS):
        k_ref[0, h, :, :D_NOPE] = kv[:, h * per:h * per + D_NOPE].astype(k_ref.dtype)
        k_ref[0, h, :, D_NOPE:] = k_rope
        v_ref[0, h] = kv[:, h * per + D_NOPE:(h + 1) * per].astype(v_ref.dtype)


def _mla_kv(hid, w_dkv, kv_norm_w, w_ukv, cos_t, sin_t, tl=512):
    b, l, d = hid.shape
    tl = min(tl, l)
    wd = jnp.concatenate([w_dkv[:, :KV_LORA], _rope_weight_groups(w_dkv[:, KV_LORA:])], -1).astype(BF16)
    nd = wd.shape[1]
    hh = MLA_HEADS
    return pl.pallas_call(
        _mla_kv_kernel,
        grid=(b, l // tl),
        in_specs=[pl.BlockSpec((1, tl, d), lambda bi, i: (bi, i, 0)),
                  pl.BlockSpec((d, nd), lambda bi, i: (0, 0)),
                  pl.BlockSpec((1, KV_LORA), lambda bi, i: (0, 0)),
                  pl.BlockSpec((KV_LORA, hh * (D_NOPE + D_V)), lambda bi, i: (0, 0)),
                  pl.BlockSpec((tl, LANES), lambda bi, i: (i, 0)),
                  pl.BlockSpec((tl, LANES), lambda bi, i: (i, 0))],
        out_specs=[pl.BlockSpec((1, hh, tl, D_NOPE + LANES), lambda bi, i: (bi, 0, i, 0)),
                   pl.BlockSpec((1, hh, tl, D_V), lambda bi, i: (bi, 0, i, 0))],
        out_shape=[jax.ShapeDtypeStruct((b, hh, l, D_NOPE + LANES), BF16),
                   jax.ShapeDtypeStruct((b, hh, l, D_V), BF16)],
        compiler_params=_cparams("parallel", "parallel"),
        name="mla_kv",
    )(hid, wd, kv_norm_w.reshape(1, KV_LORA), w_ukv.astype(BF16), cos_t, sin_t)


def _mla_cq_kernel(x_ref, w_ref, nw_ref, o_ref):
    o_ref[...] = _rms_norm(_bdot(x_ref[...], w_ref[...]), nw_ref[...]).astype(o_ref.dtype)


def _mla_cq(x, w_dq, qnorm_w, tm=512):
    m, k = x.shape
    tm = min(tm, m)
    return pl.pallas_call(
        _mla_cq_kernel,
        grid=(m // tm,),
        in_specs=[pl.BlockSpec((tm, k), lambda i: (i, 0)),
                  pl.BlockSpec((k, Q_LORA), lambda i: (0, 0)),
                  pl.BlockSpec((1, Q_LORA), lambda i: (0, 0))],
        out_specs=pl.BlockSpec((tm, Q_LORA), lambda i: (i, 0)),
        out_shape=jax.ShapeDtypeStruct((m, Q_LORA), BF16),
        compiler_params=_cparams("parallel"),
        name="mla_cq",
    )(x, w_dq.astype(BF16), qnorm_w.reshape(1, Q_LORA))


def _mla_q_kernel(c_ref, w_ref, cos_ref, sin_ref, q_ref):
    c = c_ref[0]
    scale = (D_NOPE + D_ROPE) ** -0.5 * LOG2_E
    per = D_NOPE + 2 * LANES
    for h in range(MLA_HEADS):
        qh = _bdot(c, w_ref[:, h * per:(h + 1) * per])
        rope = qh[:, D_NOPE:D_NOPE + LANES] * cos_ref[...] + qh[:, D_NOPE + LANES:] * sin_ref[...]
        q_ref[0, h, :, :D_NOPE] = (qh[:, :D_NOPE] * scale).astype(q_ref.dtype)
        q_ref[0, h, :, D_NOPE:] = (rope * scale).astype(q_ref.dtype)


def _mla_q(cq, w_uq, cos_t, sin_t, tl=512):
    b, l, _ = cq.shape
    tl = min(tl, l)
    hh = MLA_HEADS
    per_in = D_NOPE + D_ROPE
    groups = []
    for h in range(hh):
        wh = w_uq[:, h * per_in:(h + 1) * per_in]
        groups += [wh[:, :D_NOPE], _rope_weight_groups(wh[:, D_NOPE:])]
    w = jnp.concatenate(groups, -1).astype(BF16)
    return pl.pallas_call(
        _mla_q_kernel,
        grid=(b, l // tl),
        in_specs=[pl.BlockSpec((1, tl, Q_LORA), lambda bi, i: (bi, i, 0)),
                  pl.BlockSpec(w.shape, lambda bi, i: (0, 0)),
                  pl.BlockSpec((tl, LANES), lambda bi, i: (i, 0)),
                  pl.BlockSpec((tl, LANES), lambda bi, i: (i, 0))],
        out_specs=pl.BlockSpec((1, hh, tl, D_NOPE + LANES), lambda bi, i: (bi, 0, i, 0)),
        out_shape=jax.ShapeDtypeStruct((b, hh, l, D_NOPE + LANES), BF16),
        compiler_params=_cparams("parallel", "parallel"),
        name="mla_q",
    )(cq, w, cos_t, sin_t)


def _mla_attn_kernel(q_ref, k_ref, v_ref, o_ref, *, tk):
    qi = pl.program_id(2)
    q = q_ref[0, 0]
    tq = q.shape[0]
    per = tq // tk

    def scores(j):
        start = pl.multiple_of(j * tk, tk)
        kb = k_ref[0, 0, pl.ds(start, tk), :]
        vb = v_ref[0, 0, pl.ds(start, tk), :]
        return _bdot_nt(q, kb), vb

    def update(carry, s, vb):
        m, l, acc = carry
        m_new = jnp.maximum(m, jnp.max(s, -1, keepdims=True))
        alpha = jnp.exp2(m - m_new)
        p = jnp.exp2(s - m_new)
        l = alpha * l + jnp.sum(p, -1, keepdims=True)
        acc = alpha * acc + _bdot(p, vb)
        return m_new, l, acc

    def full_block(j, carry):
        s, vb = scores(j)
        return update(carry, s, vb)

    carry = (jnp.full((tq, 1), -jnp.inf, F32), jnp.zeros((tq, 1), F32), jnp.zeros((tq, D_V), F32))
    carry = lax.fori_loop(0, qi * per, full_block, carry)
    qpos = lax.broadcasted_iota(jnp.int32, (tq, tk), 0)
    kpos = lax.broadcasted_iota(jnp.int32, (tq, tk), 1)
    for d in range(per):
        s, vb = scores(qi * per + d)
        s = jnp.where(kpos + d * tk <= qpos, s, -jnp.inf)
        carry = update(carry, s, vb)
    _, l, acc = carry
    o_ref[0] = (acc / l).astype(o_ref.dtype)


def _mla_attn(q, k, v, tq=512, tk=512):
    b, hh, l, dq = q.shape
    tq = min(tq, l)
    tk = min(tk, tq)
    return pl.pallas_call(
        functools.partial(_mla_attn_kernel, tk=tk),
        grid=(b, hh, l // tq),
        in_specs=[pl.BlockSpec((1, 1, tq, dq), lambda bi, h, i: (bi, h, i, 0)),
                  pl.BlockSpec((1, 1, l, dq), lambda bi, h, i: (bi, h, 0, 0)),
                  pl.BlockSpec((1, 1, l, D_V), lambda bi, h, i: (bi, h, 0, 0))],
        out_specs=pl.BlockSpec((1, tq, D_V), lambda bi, h, i: (bi, i, h)),
        out_shape=jax.ShapeDtypeStruct((b, l, hh * D_V), BF16),
        compiler_params=_cparams("parallel", "parallel", "parallel"),
        name="mla_attn",
    )(q, k, v)


def _mla_ln(hid, w_dq, qnorm_w, w_uq, w_o, k, v, cos_t, sin_t, ln_g, ln_b):
    b, l, d = hid.shape
    x = hid.reshape(b * l, d)
    cq = _mla_cq(x, w_dq, qnorm_w).reshape(b, l, Q_LORA)
    q = _mla_q(cq, w_uq, cos_t, sin_t)
    o = _mla_attn(q, k, v)
    out = _matmul_ln(o.reshape(b * l, MLA_HEADS * D_V), w_o.astype(BF16), x, ln_g, ln_b)
    return out.reshape(b, l, d)


def _first_argmax(x, ids, n):
    m = jnp.max(x, axis=0, keepdims=True)
    first = jnp.min(jnp.where(x == m, ids, n), axis=0, keepdims=True)
    return m, first


def _router_kernel(x_ref, wt_ref, bias_ref, idx_ref, w_ref, rank_ref, cnt_out_ref, cnt_ref, *, steps_per_tile):
    @pl.when(pl.program_id(0) % steps_per_tile == 0)
    def _():
        cnt_ref[...] = jnp.zeros_like(cnt_ref)

    x = x_ref[...]
    t = x.shape[0]
    logits = lax.dot_general(wt_ref[...], x, (((1,), (1,)), ((), ())),
                             precision=HIGHEST, preferred_element_type=F32)
    scores = jax.nn.sigmoid(logits)
    biased = scores + bias_ref[...]
    neg = -jnp.inf
    sub = lax.broadcasted_iota(jnp.int32, (GROUP_SIZE, t), 0).astype(F32)
    gscores = []
    for g in range(N_GROUPS):
        xg = biased[g * GROUP_SIZE:(g + 1) * GROUP_SIZE, :]
        m1, i1 = _first_argmax(xg, sub, float(GROUP_SIZE))
        m2 = jnp.max(jnp.where(sub == i1, neg, xg), axis=0, keepdims=True)
        gscores.append(m1 + m2)
    gs = jnp.concatenate(gscores, axis=0)
    gid = lax.broadcasted_iota(jnp.int32, (N_GROUPS, t), 0).astype(F32)
    gsel = jnp.zeros((N_GROUPS, t), F32)
    for _ in range(TOPK_GROUPS):
        _, gi = _first_argmax(gs, gid, float(N_GROUPS))
        hit = gid == gi
        gsel = jnp.where(hit, 1.0, gsel)
        gs = jnp.where(hit, neg, gs)
    eid = lax.broadcasted_iota(jnp.int32, (N_EXPERTS, t), 0).astype(F32)
    allowed = jnp.concatenate(
        [jnp.broadcast_to(gsel[g:g + 1, :], (GROUP_SIZE, t)) for g in range(N_GROUPS)], axis=0)
    cand = jnp.where(allowed > 0.0, biased, neg)
    idxs, ws, hits = [], [], []
    for _ in range(TOP_K):
        _, ei = _first_argmax(cand, eid, float(N_EXPERTS))
        hit = eid == ei
        idxs.append(ei)
        hits.append(hit)
        ws.append(jnp.sum(jnp.where(hit, scores, 0.0), axis=0, keepdims=True))
        cand = jnp.where(hit, neg, cand)
    w = jnp.concatenate(ws, axis=0)
    w = w / (jnp.sum(w, axis=0, keepdims=True) + 1e-20) * ROUTED_SCALE
    idx_ref[...] = jnp.concatenate(idxs, axis=0).astype(jnp.int32)
    w_ref[...] = w
    chosen = jnp.zeros((N_EXPERTS, t), F32)
    for hit in hits:
        chosen = jnp.where(hit, 1.0, chosen)
    before = (lax.broadcasted_iota(jnp.int32, (t, t), 0) < lax.broadcasted_iota(jnp.int32, (t, t), 1))
    prior = jnp.dot(chosen.astype(BF16), jnp.where(before, 1.0, 0.0).astype(BF16),
                    preferred_element_type=F32) + cnt_ref[...]
    rank_ref[...] = jnp.concatenate(
        [jnp.sum(jnp.where(hit, prior, 0.0), axis=0, keepdims=True) for hit in hits], axis=0).astype(jnp.int32)
    cnt_ref[...] += jnp.sum(chosen, axis=1, keepdims=True)
    cnt_out_ref[0] = cnt_ref[...]


def _router(x, w_router, bias, tile_tokens, tr=512):
    t, d = x.shape
    tr = min(tr, tile_tokens)
    steps_per_tile = tile_tokens // tr
    n_tiles = t // tile_tokens
    kt = pl.BlockSpec((TOP_K, tr), lambda i: (0, i))
    idx, w, rank, counts = pl.pallas_call(
        functools.partial(_router_kernel, steps_per_tile=steps_per_tile),
        grid=(t // tr,),
        in_specs=[pl.BlockSpec((tr, d), lambda i: (i, 0)),
                  pl.BlockSpec((N_EXPERTS, d), lambda i: (0, 0)),
                  pl.BlockSpec((N_EXPERTS, 1), lambda i: (0, 0))],
        out_specs=[kt, kt, kt,
                   pl.BlockSpec((1, N_EXPERTS, 1), lambda i: (i // steps_per_tile, 0, 0))],
        out_shape=[jax.ShapeDtypeStruct((TOP_K, t), jnp.int32),
                   jax.ShapeDtypeStruct((TOP_K, t), F32),
                   jax.ShapeDtypeStruct((TOP_K, t), jnp.int32),
                   jax.ShapeDtypeStruct((n_tiles, N_EXPERTS, 1), F32)],
        scratch_shapes=[pltpu.VMEM((N_EXPERTS, 1), F32)],
        compiler_params=_cparams("arbitrary"),
        name="moe_router",
    )(x, w_router.T, bias.reshape(N_EXPERTS, 1))
    return idx, w, rank, counts.reshape(n_tiles, N_EXPERTS).astype(jnp.int32)


def _steps_per_tile(tile_tokens):
    return tile_tokens * TOP_K // STEP_ROWS + N_EXPERTS


def _plan_kernel(fill_lo_ref, fill_hi_ref, pos_ref, w_ref, row_ref, sw_ref, *, tile_tokens, part):
    i = pl.program_id(0)
    q = pl.program_id(1)
    spare = tile_tokens * SUBLANES
    group = 16
    shift = SUBLANES.bit_length() - 1

    @pl.when(q == 0)
    def _():
        def fill_range(g, carry):
            lo = fill_lo_ref[i * N_EXPERTS + g]
            hi = fill_hi_ref[i * N_EXPERTS + g]

            def fill(b, carry):
                for d in range(group):
                    s = jnp.maximum(hi - 1 - b * group - d, 0)
                    row_ref[0, 0, s] = spare
                    sw_ref[0, 0, s] = 0.0
                return carry
            return lax.fori_loop(0, (hi - lo + group - 1) // group, fill, carry)
        lax.fori_loop(0, N_EXPERTS, fill_range, 0)

    def place(b, carry):
        base = b * group
        slots = [pos_ref[0, 0, base + d] for d in range(group)]
        weights = [w_ref[0, 0, base + d] for d in range(group)]
        for d in range(group):
            row_ref[0, 0, slots[d]] = ((q * part + base + d) >> shift) << shift
            sw_ref[0, 0, slots[d]] = weights[d]
        return carry
    lax.fori_loop(0, part // group, place, 0)


def _dispatch_plan(idx_t, w_t, rank_t, counts, tile_tokens):
    t = idx_t.shape[1]
    n_tiles = t // tile_tokens
    steps = _steps_per_tile(tile_tokens)
    padded = (counts + STEP_ROWS - 1) // STEP_ROWS * STEP_ROWS
    pad_end = jnp.cumsum(padded, axis=1)
    pad_start = (pad_end - padded).astype(jnp.int32)
    n_used = (pad_end[:, -1] // STEP_ROWS).astype(jnp.int32)
    starts = jnp.arange(steps, dtype=jnp.int32) * STEP_ROWS
    step_e = jnp.sum(starts[None, :, None] >= pad_end[:, None, :], axis=-1).astype(jnp.int32)
    step_e = jnp.minimum(step_e, N_EXPERTS - 1)
    last = jnp.take_along_axis(step_e, jnp.maximum(n_used - 1, 0)[:, None], axis=1)
    step_e = jnp.where(starts[None, :] // STEP_ROWS < n_used[:, None], step_e, last)

    start_of = jnp.repeat(pad_start.T, tile_tokens, axis=1)
    experts = jnp.arange(N_EXPERTS, dtype=jnp.int32)[None, :, None]
    pos_t = rank_t + jnp.sum(jnp.where(idx_t[:, None, :] == experts, start_of[None], 0), axis=1)
    fill_lo = (pad_start + counts).astype(jnp.int32)
    fill_hi = pad_end.astype(jnp.int32)

    parts = PLAN_PARTS
    part = tile_tokens * TOP_K // parts
    n_slots = steps * STEP_ROWS
    flat = lambda a: a.T.reshape(n_tiles * parts, 1, part)
    part_spec = pl.BlockSpec((1, 1, part), lambda i, q, lo, hi: (i * parts + q, 0, 0), memory_space=pltpu.SMEM)
    table_spec = pl.BlockSpec((1, 1, n_slots), lambda i, q, lo, hi: (i, 0, 0), memory_space=pltpu.SMEM)
    rows, slot_w = pl.pallas_call(
        functools.partial(_plan_kernel, tile_tokens=tile_tokens, part=part),
        grid_spec=pltpu.PrefetchScalarGridSpec(
            num_scalar_prefetch=2,
            grid=(n_tiles, parts),
            in_specs=[part_spec, part_spec],
            out_specs=[table_spec, table_spec],
        ),
        out_shape=[jax.ShapeDtypeStruct((n_tiles, 1, n_slots), jnp.int32),
                   jax.ShapeDtypeStruct((n_tiles, 1, n_slots), F32)],
        compiler_params=_cparams("parallel", "arbitrary"),
        name="moe_plan",
    )(fill_lo.reshape(-1), fill_hi.reshape(-1), flat(pos_t), flat(w_t))
    return rows, slot_w, step_e.reshape(-1), n_used, steps


def _experts_kernel(step_e_ref, n_used_ref, row_ref, sw_ref, x_ref, wg_ref, wu_ref, wd_ref, acc_ref, *stage_refs):
    n_sub = BLOCKS_PER_STEP
    xs_refs = stage_refs[:n_sub]
    ys_refs = stage_refs[n_sub:]
    i = pl.program_id(0)
    j = pl.program_id(1)
    rows = EXPERT_ROWS
    pitch = STAGE_PITCH

    @pl.when(j == 0)
    def _():
        acc_ref[...] = jnp.zeros_like(acc_ref)

    def token_rows(slot):
        return pl.ds(pl.multiple_of(row_ref[0, 0, slot], SUBLANES), SUBLANES)

    @pl.when(j < n_used_ref[i])
    def _():
        for s in range(n_sub):
            for r in range(rows):
                xs_refs[s][pl.ds(r, ROW_CHUNKS, stride=pitch), :] = x_ref[0, token_rows(s * rows + r), :]
        eye = (lax.broadcasted_iota(jnp.int32, (rows, rows), 0)
               == lax.broadcasted_iota(jnp.int32, (rows, rows), 1))
        ys = []
        for s in range(n_sub):
            x = jnp.concatenate([xs_refs[s][pl.ds(c * pitch, rows), :] for c in range(ROW_CHUNKS)], axis=1)
            hid = _silu(_bdot(x, wg_ref[0])) * _bdot(x, wu_ref[0])
            w_row = sw_ref[0, :, s * rows:(s + 1) * rows]
            w_col = jnp.sum(jnp.where(eye, w_row, 0.0), axis=1, keepdims=True)
            ys.append(_bdot(hid, wd_ref[0]) * w_col)
        for s in range(n_sub):
            for c in range(ROW_CHUNKS):
                ys_refs[s][pl.ds(c * pitch, rows), :] = ys[s][:, c * LANES:(c + 1) * LANES]
        for s in range(n_sub):
            for r0 in range(0, rows, SCATTER_BATCH):
                new = [acc_ref[0, token_rows(s * rows + r), :] + ys_refs[s][pl.ds(r, ROW_CHUNKS, stride=pitch), :]
                       for r in range(r0, r0 + SCATTER_BATCH)]
                for r, val in zip(range(r0, r0 + SCATTER_BATCH), new):
                    acc_ref[0, token_rows(s * rows + r), :] = val


def _routed_experts(x, idx_t, w_t, rank_t, counts, w_gate, w_up, w_down, tile_tokens):
    t, d = x.shape
    n_tiles = t // tile_tokens
    rows, slot_w, step_e, n_used, steps = _dispatch_plan(idx_t, w_t, rank_t, counts, tile_tokens)
    rows = rows.reshape(n_tiles * steps, 1, STEP_ROWS)
    slot_w = slot_w.reshape(n_tiles * steps, 1, STEP_ROWS)
    slab_rows = (tile_tokens + 1) * ROW_CHUNKS
    x_slabs = jnp.pad(x.reshape(n_tiles, tile_tokens, d), ((0, 0), (0, 1), (0, 0))).reshape(n_tiles, slab_rows, LANES)

    def used_step(i, j, se, nu):
        return (i * steps + jnp.minimum(j, jnp.maximum(nu[i] - 1, 0)), 0, 0)

    pick = lambda i, j, se, nu: (se[i * steps + j], 0, 0)
    tile = lambda i, j, se, nu: (i, 0, 0)
    resident = pl.Buffered(1)
    stage = pltpu.VMEM((ROW_CHUNKS * STAGE_PITCH, LANES), F32)
    grid_spec = pltpu.PrefetchScalarGridSpec(
        num_scalar_prefetch=2,
        grid=(n_tiles, steps),
        in_specs=[
            pl.BlockSpec((1, 1, STEP_ROWS), used_step, memory_space=pltpu.SMEM),
            pl.BlockSpec((1, 1, STEP_ROWS), used_step),
            pl.BlockSpec((1, slab_rows, LANES), tile, pipeline_mode=resident),
            pl.BlockSpec((1, d, D_EXPERT), pick),
            pl.BlockSpec((1, d, D_EXPERT), pick),
            pl.BlockSpec((1, D_EXPERT, d), pick),
        ],
        out_specs=pl.BlockSpec((1, slab_rows, LANES), tile, pipeline_mode=resident),
        scratch_shapes=[stage] * (2 * BLOCKS_PER_STEP),
    )
    return pl.pallas_call(
        _experts_kernel,
        grid_spec=grid_spec,
        out_shape=jax.ShapeDtypeStruct((n_tiles, slab_rows, LANES), F32),
        compiler_params=_cparams("parallel", "arbitrary"),
        name="moe_experts",
    )(step_e, n_used, rows, slot_w, x_slabs, w_gate, w_up, w_down)


def _moe_out_kernel(x_ref, r_ref, wg_ref, wu_ref, wd_ref, g_ref, b_ref, o_ref):
    x = x_ref[...]
    tm = x.shape[0]
    hid = _silu(_bdot(x, wg_ref[...])) * _bdot(x, wu_ref[...])
    routed = jnp.concatenate([r_ref[0, pl.ds(c, tm, stride=ROW_CHUNKS), :] for c in range(ROW_CHUNKS)], axis=1)
    y = routed + _bdot(hid, wd_ref[...])
    o_ref[...] = _layer_norm(DN_ALPHA * x + y, g_ref[...], b_ref[...])


def _moe_out(x, routed, ws_gate, ws_up, ws_down, g, b, tile_tokens, tm=512):
    t, d = x.shape
    tm = min(tm, tile_tokens)
    per_tile = tile_tokens // tm
    return pl.pallas_call(
        _moe_out_kernel,
        grid=(t // tm,),
        in_specs=[pl.BlockSpec((tm, d), lambda i: (i, 0)),
                  pl.BlockSpec((1, tm * ROW_CHUNKS, LANES), lambda i: (i // per_tile, i % per_tile, 0)),
                  pl.BlockSpec((d, D_SHARED), lambda i: (0, 0)),
                  pl.BlockSpec((d, D_SHARED), lambda i: (0, 0)),
                  pl.BlockSpec((D_SHARED, d), lambda i: (0, 0)),
                  pl.BlockSpec((1, d), lambda i: (0, 0)),
                  pl.BlockSpec((1, d), lambda i: (0, 0))],
        out_specs=pl.BlockSpec((tm, d), lambda i: (i, 0)),
        out_shape=jax.ShapeDtypeStruct((t, d), F32),
        compiler_params=_cparams("parallel"),
        name="moe_out",
    )(x, routed, ws_gate.astype(BF16), ws_up.astype(BF16), ws_down.astype(BF16),
      g.reshape(1, d), b.reshape(1, d))


def _moe_ln(hid, w_router, router_bias, w_gate, w_up, w_down, ws_gate, ws_up, ws_down, ln_g, ln_b):
    b, l, d = hid.shape
    x = hid.reshape(b * l, d)
    tile_tokens = min(MOE_TILE_TOKENS, b * l)
    idx_t, w_t, rank_t, counts = _router(x, w_router, router_bias, tile_tokens)
    routed = _routed_experts(x, idx_t, w_t, rank_t, counts, w_gate.astype(BF16), w_up.astype(BF16),
                             w_down.astype(BF16), tile_tokens)
    return _moe_out(x, routed, ws_gate, ws_up, ws_down, ln_g, ln_b, tile_tokens).reshape(b, l, d)


def kernel(x, a_w_in, a_conv_w, a_A_log, a_dt_bias, a_onorm_w, a_w_out, b_w_dq, b_qnorm_w, b_w_uq, b_w_o,
           kv_w_dkv, kv_norm_w, kv_w_ukv, ln1_g, ln1_b, ln2_g, ln2_b, moe_w_router, moe_router_bias,
           moe_w_gate, moe_w_up, moe_w_down, moe_ws_gate, moe_ws_up, moe_ws_down):
    l = x.shape[1]
    cos_t, sin_t = _rope_lane_tables(l)
    h = x
    k = v = None
    for layer in range(DEPTH):
        if layer < N_A_LAYERS:
            i = layer
            h = _gated_deltanet_ln(h, a_w_in[i], a_conv_w[i], a_A_log[i], a_dt_bias[i], a_onorm_w[i],
                                   a_w_out[i], ln1_g[layer], ln1_b[layer])
        else:
            i = layer - N_A_LAYERS
            h = _mla_ln(h, b_w_dq[i], b_qnorm_w[i], b_w_uq[i], b_w_o[i], k, v, cos_t, sin_t,
                        ln1_g[layer], ln1_b[layer])
        h = _moe_ln(h, moe_w_router[layer], moe_router_bias[layer], moe_w_gate[layer], moe_w_up[layer],
                    moe_w_down[layer], moe_ws_gate[layer], moe_ws_up[layer], moe_ws_down[layer],
                    ln2_g[layer], ln2_b[layer])
        if layer == N_A_LAYERS - 1:
            k, v = _mla_kv(h, kv_w_dkv, kv_norm_w, kv_w_ukv, cos_t, sin_t)
    return h
```

```python
import functools

import jax
import jax.numpy as jnp
import numpy as np
from jax import lax
from jax.experimental import pallas as pl
from jax.experimental.pallas import tpu as pltpu

F32 = jnp.float32
BF16 = jnp.bfloat16
HIGHEST = lax.Precision.HIGHEST

D_MODEL = 1024
DEPTH = 4
N_A_LAYERS = DEPTH // 2
GDN_HEADS = 8
GDN_DK = 128
GDN_DV = 128
D_QK = GDN_HEADS * GDN_DK
D_VA = GDN_HEADS * GDN_DV
GDN_CONV_CH = 2 * D_QK + D_VA
CONV_WIDTH = 4
CHUNK = 64
MLA_HEADS = 8
Q_LORA = 512
KV_LORA = 256
D_NOPE = 128
D_ROPE = 64
D_V = 128
ROPE_THETA = 10000.0
N_EXPERTS = 64
TOP_K = 8
N_GROUPS = 8
GROUP_SIZE = N_EXPERTS // N_GROUPS
TOPK_GROUPS = 4
D_EXPERT = 256
D_SHARED = 256
ROUTED_SCALE = 2.5
DN_ALPHA = (2 * DEPTH) ** 0.25
EPS = 1e-6
LN_EPS = 1e-5
LOG2_E = 1.4426950408889634

LANES = 128
SUBLANES = 8
VMEM_LIMIT_BYTES = 56 * 1024 * 1024

EXPERT_ROWS = 128
ROW_CHUNKS = D_MODEL // LANES
STAGE_PITCH = EXPERT_ROWS + 1
SCATTER_BATCH = 16
BLOCKS_PER_STEP = 2
STEP_ROWS = BLOCKS_PER_STEP * EXPERT_ROWS
MOE_TILE_TOKENS = 4096
PLAN_PARTS = 4
ATTN_ROW_GROUPS = 4
assert TOP_K == SUBLANES


def _cparams(*sem):
    return pltpu.CompilerParams(dimension_semantics=sem, vmem_limit_bytes=VMEM_LIMIT_BYTES)


def _silu(x):
    return x * jax.nn.sigmoid(x)


def _layer_norm(x, g, b):
    mu = jnp.mean(x, -1, keepdims=True)
    xc = x - mu
    var = jnp.mean(xc * xc, -1, keepdims=True)
    return xc * lax.rsqrt(var + LN_EPS) * g + b


def _rms_norm(x, w):
    return x * lax.rsqrt(jnp.mean(x * x, -1, keepdims=True) + EPS) * w


def _bdot(a, b):
    return jnp.dot(a.astype(BF16), b.astype(BF16), preferred_element_type=F32)


def _bdot_nt(a, b):
    return lax.dot_general(a.astype(BF16), b.astype(BF16), (((1,), (1,)), ((), ())),
                           preferred_element_type=F32)


def _hdot(a, b):
    return jnp.dot(a, b, precision=HIGHEST, preferred_element_type=F32)


def _matmul_kernel(x_ref, w_ref, o_ref):
    o_ref[...] = _bdot(x_ref[...], w_ref[...]).astype(o_ref.dtype)


def _matmul(x, w, out_dtype, tm, tn):
    m, k = x.shape
    n = w.shape[1]
    tm = min(tm, m)
    tn = min(tn, n)
    return pl.pallas_call(
        _matmul_kernel,
        grid=(m // tm, n // tn),
        in_specs=[pl.BlockSpec((tm, k), lambda i, j: (i, 0)),
                  pl.BlockSpec((k, tn), lambda i, j: (0, j))],
        out_specs=pl.BlockSpec((tm, tn), lambda i, j: (i, j)),
        out_shape=jax.ShapeDtypeStruct((m, n), out_dtype),
        compiler_params=_cparams("parallel", "parallel"),
        name="matmul",
    )(x, w)


def _matmul_ln_kernel(x_ref, w_ref, r_ref, g_ref, b_ref, o_ref):
    mix = _bdot(x_ref[...], w_ref[...])
    o_ref[...] = _layer_norm(DN_ALPHA * r_ref[...] + mix, g_ref[...], b_ref[...])


def _matmul_ln(x, w, res, g, b, tm=512):
    m, k = x.shape
    n = w.shape[1]
    tm = min(tm, m)
    return pl.pallas_call(
        _matmul_ln_kernel,
        grid=(m // tm,),
        in_specs=[pl.BlockSpec((tm, k), lambda i: (i, 0)),
                  pl.BlockSpec((k, n), lambda i: (0, 0)),
                  pl.BlockSpec((tm, n), lambda i: (i, 0)),
                  pl.BlockSpec((1, n), lambda i: (0, 0)),
                  pl.BlockSpec((1, n), lambda i: (0, 0))],
        out_specs=pl.BlockSpec((tm, n), lambda i: (i, 0)),
        out_shape=jax.ShapeDtypeStruct((m, n), F32),
        compiler_params=_cparams("parallel"),
        name="matmul_ln",
    )(x, w, res, g.reshape(1, n), b.reshape(1, n))


def _softplus(x):
    return jnp.maximum(x, 0.0) + jnp.log1p(jnp.exp(-jnp.abs(x)))


def _gdn_gates_kernel(x_ref, wab_ref, wabt_ref, alog_ref, dtb_ref, alogt_ref, dtbt_ref,
                      gc_ref, beta_ref, gct_ref):
    x = x_ref[...]
    tl = x.shape[0]
    h = GDN_HEADS
    ab = _hdot(x, wab_ref[...])
    abt = lax.dot_general(wabt_ref[...], x, (((1,), (1,)), ((), ())),
                          precision=HIGHEST, preferred_element_type=F32)
    g = -jnp.exp(alog_ref[...]) * _softplus(ab[:, :h] + dtb_ref[...])
    gt = -jnp.exp(alogt_ref[...]) * _softplus(abt[:h, :] + dtbt_ref[...])
    beta_ref[...] = jax.nn.sigmoid(ab[:, h:2 * h])
    row = lax.broadcasted_iota(jnp.int32, (tl, tl), 0)
    col = lax.broadcasted_iota(jnp.int32, (tl, tl), 1)
    shift = CHUNK.bit_length() - 1
    same = (row >> shift) == (col >> shift)
    lower = jnp.where(same & (col <= row), 1.0, 0.0).astype(F32)
    upper = jnp.where(same & (row <= col), 1.0, 0.0).astype(F32)
    gc_ref[...] = _hdot(lower, g)
    gct_ref[...] = _hdot(gt, upper)


def _gdn_gates(x, w_ab, a_log, dt_bias, tl=512):
    t = x.shape[0]
    tl = min(tl, t)
    h = GDN_HEADS
    wab = jnp.zeros((D_MODEL, LANES), F32).at[:, :2 * h].set(w_ab)
    wabt = w_ab.T
    return pl.pallas_call(
        _gdn_gates_kernel,
        grid=(t // tl,),
        in_specs=[pl.BlockSpec((tl, D_MODEL), lambda i: (i, 0)),
                  pl.BlockSpec((D_MODEL, LANES), lambda i: (0, 0)),
                  pl.BlockSpec((2 * h, D_MODEL), lambda i: (0, 0)),
                  pl.BlockSpec((1, h), lambda i: (0, 0)),
                  pl.BlockSpec((1, h), lambda i: (0, 0)),
                  pl.BlockSpec((h, 1), lambda i: (0, 0)),
                  pl.BlockSpec((h, 1), lambda i: (0, 0))],
        out_specs=[pl.BlockSpec((tl, h), lambda i: (i, 0)),
                   pl.BlockSpec((tl, h), lambda i: (i, 0)),
                   pl.BlockSpec((h, tl), lambda i: (0, i))],
        out_shape=[jax.ShapeDtypeStruct((t, h), F32),
                   jax.ShapeDtypeStruct((t, h), F32),
                   jax.ShapeDtypeStruct((h, t), F32)],
        compiler_params=_cparams("parallel"),
        name="gdn_gates",
    )(x, wab, wabt, a_log.reshape(1, h), dt_bias.reshape(1, h),
      a_log.reshape(h, 1), dt_bias.reshape(h, 1))


def _gdn_conv_kernel(cur_ref, prev_ref, w_ref, q_ref, k_ref, v_ref):
    i = pl.program_id(1)
    x = cur_ref[0]
    tl = x.shape[0]
    prev = jnp.where(i > 0, prev_ref[0], 0.0)
    xx = jnp.concatenate([prev, x], axis=0)
    w = w_ref[...]
    y = None
    for j in range(CONV_WIDTH):
        off = SUBLANES - (CONV_WIDTH - 1) + j
        term = xx[off:off + tl, :] * w[j:j + 1, :]
        y = term if y is None else y + term
    y = _silu(y)
    for h in range(GDN_HEADS):
        sl = slice(h * GDN_DK, (h + 1) * GDN_DK)
        qh = y[:, sl]
        q_ref[0, :, sl] = qh * lax.rsqrt(jnp.sum(qh * qh, -1, keepdims=True) + EPS) * (GDN_DK ** -0.5)
        kh = y[:, D_QK + h * GDN_DK:D_QK + (h + 1) * GDN_DK]
        k_ref[0, :, sl] = kh * lax.rsqrt(jnp.sum(kh * kh, -1, keepdims=True) + EPS)
    v_ref[0] = y[:, 2 * D_QK:]


def _gdn_conv(proj, conv_w, tl=256):
    b, l, _ = proj.shape
    tl = min(tl, l)
    c3 = GDN_CONV_CH
    per = tl // SUBLANES
    out = jax.ShapeDtypeStruct((b, l, D_QK), F32)
    return pl.pallas_call(
        _gdn_conv_kernel,
        grid=(b, l // tl),
        in_specs=[pl.BlockSpec((1, tl, c3), lambda bi, i: (bi, i, 0)),
                  pl.BlockSpec((1, SUBLANES, c3), lambda bi, i: (bi, jnp.maximum(i * per - 1, 0), 0)),
                  pl.BlockSpec((CONV_WIDTH, c3), lambda bi, i: (0, 0))],
        out_specs=[pl.BlockSpec((1, tl, D_QK), lambda bi, i: (bi, i, 0))] * 3,
        out_shape=[out, out, out],
        compiler_params=_cparams("parallel", "parallel"),
        name="gdn_conv",
    )(proj, proj, conv_w)


def _split_bf16(x):
    hi = x.astype(BF16)
    lo = (x - hi.astype(F32)).astype(BF16)
    return hi, lo


def _dot_split(a, b):
    (ah, al), (bh, bl) = a, b
    dot = functools.partial(jnp.dot, preferred_element_type=F32)
    return dot(ah, bh) + (dot(ah, bl) + dot(al, bh))


def _gdn_prep_kernel(q_ref, k_ref, v_ref, gc_ref, beta_ref, gct_ref, wq_ref, u_ref, attn_ref, kgt_ref,
                     *, chunks):
    c = CHUNK
    row = lax.broadcasted_iota(jnp.int32, (c, c), 0)
    col = lax.broadcasted_iota(jnp.int32, (c, c), 1)
    incl = row >= col
    strict = row > col
    eye = jnp.where(row == col, 1.0, 0.0).astype(F32)
    items = [(ci, h) for ci in range(chunks) for h in range(GDN_HEADS)]

    st = {}
    for ci, h in items:
        rows = slice(ci * c, (ci + 1) * c)
        sl = slice(h * GDN_DK, (h + 1) * GDN_DK)
        q = q_ref[0, rows, sl]
        k = k_ref[0, rows, sl]
        v = v_ref[0, rows, sl]
        gcol = gc_ref[0, rows, h:h + 1]
        grow = gct_ref[0, ci, h:h + 1, :]
        bcol = beta_ref[0, rows, h:h + 1]
        decay = jnp.where(incl, jnp.exp(jnp.where(incl, gcol - grow, 0.0)), 0.0)
        kb = k * bcol
        eg = jnp.exp(gcol)
        g_last = gcol[c - 1:c, :]
        sc = _bdot_nt(jnp.concatenate([kb, q], axis=0), k)
        a = jnp.where(strict, sc[:c] * decay, 0.0)
        attn_ref[0, ci, h] = (sc[c:] * decay).astype(attn_ref.dtype)
        kgt_ref[0, ci, h] = (k * jnp.exp(g_last - gcol)).T.astype(kgt_ref.dtype)
        wq_ref[0, ci, h, c:, :] = (q * eg).astype(wq_ref.dtype)
        st[ci, h] = dict(m=-a, rhs=jnp.concatenate([v * bcol, kb * eg], axis=1))

    for it in items:
        st[it]["p"] = eye + st[it]["m"]
    span = 2
    while span < c:
        for it in items:
            ms = _split_bf16(st[it]["m"])
            st[it]["m"] = _dot_split(ms, ms)
        for it in items:
            st[it]["p"] = st[it]["p"] + _dot_split(_split_bf16(st[it]["p"]), _split_bf16(st[it]["m"]))
        span *= 2

    for ci, h in items:
        sol = _dot_split(_split_bf16(st[ci, h]["p"]), _split_bf16(st[ci, h]["rhs"]))
        u_ref[0, ci * c:(ci + 1) * c, h * GDN_DV:(h + 1) * GDN_DV] = sol[:, :GDN_DV]
        wq_ref[0, ci, h, :c, :] = sol[:, GDN_DV:].astype(wq_ref.dtype)


def _gdn_prep(q, k, v, gc, beta, gct, chunks=2):
    b, l, _ = q.shape
    n = l // CHUNK
    chunks = min(chunks, n)
    h = GDN_HEADS
    rows = chunks * CHUNK
    blk = pl.BlockSpec((1, rows, D_QK), lambda bi, i: (bi, i, 0))
    gate = pl.BlockSpec((1, rows, h), lambda bi, i: (bi, i, 0))
    return pl.pallas_call(
        functools.partial(_gdn_prep_kernel, chunks=chunks),
        grid=(b, n // chunks),
        in_specs=[blk, blk, blk, gate, gate,
                  pl.BlockSpec((1, chunks, h, CHUNK), lambda bi, i: (bi, i, 0, 0))],
        out_specs=[pl.BlockSpec((1, chunks, h, 2 * CHUNK, GDN_DK), lambda bi, i: (bi, i, 0, 0, 0)),
                   pl.BlockSpec((1, rows, D_VA), lambda bi, i: (bi, i, 0)),
                   pl.BlockSpec((1, chunks, h, CHUNK, CHUNK), lambda bi, i: (bi, i, 0, 0, 0)),
                   pl.BlockSpec((1, chunks, h, GDN_DK, CHUNK), lambda bi, i: (bi, i, 0, 0, 0))],
        out_shape=[jax.ShapeDtypeStruct((b, n, h, 2 * CHUNK, GDN_DK), BF16),
                   jax.ShapeDtypeStruct((b, l, D_VA), F32),
                   jax.ShapeDtypeStruct((b, n, h, CHUNK, CHUNK), BF16),
                   jax.ShapeDtypeStruct((b, n, h, GDN_DK, CHUNK), BF16)],
        compiler_params=_cparams("parallel", "parallel"),
        name="gdn_prep",
    )(q, k, v, gc, beta, gct)


def _gdn_state_kernel(wq_ref, u_ref, attn_ref, kgt_ref, gc_ref, z_ref, onw_ref, o_ref, s_ref, *, chunks):
    @pl.when(pl.program_id(1) == 0)
    def _():
        s_ref[...] = jnp.zeros_like(s_ref)

    c = CHUNK
    onw = onw_ref[...]
    dot = functools.partial(jnp.dot, preferred_element_type=F32)
    heads = range(GDN_HEADS)
    for ci in range(chunks):
        rows = slice(ci * c, (ci + 1) * c)
        s_old = [s_ref[h] for h in heads]
        r = [dot(wq_ref[0, ci, h], s_old[h].astype(BF16)) for h in heads]
        v_new = [(u_ref[0, rows, h * GDN_DV:(h + 1) * GDN_DV] - r[h][:c]).astype(BF16) for h in heads]
        for h in heads:
            decay_last = jnp.exp(gc_ref[0, (ci + 1) * c - 1:(ci + 1) * c, h:h + 1])
            s_ref[h] = s_old[h] * decay_last + dot(kgt_ref[0, ci, h], v_new[h])
        for h in heads:
            sl = slice(h * GDN_DV, (h + 1) * GDN_DV)
            o = r[h][c:] + dot(attn_ref[0, ci, h], v_new[h])
            o_ref[0, rows, sl] = (_rms_norm(o, onw) * _silu(z_ref[0, rows, sl])).astype(o_ref.dtype)


def _gdn_state(wq, u, attn, kgt, gc, proj, onorm_w, chunks=4):
    b, n, h = wq.shape[:3]
    l = n * CHUNK
    chunks = min(chunks, n)
    rows = chunks * CHUNK
    z_block = GDN_CONV_CH // D_VA
    return pl.pallas_call(
        functools.partial(_gdn_state_kernel, chunks=chunks),
        grid=(b, n // chunks),
        in_specs=[pl.BlockSpec((1, chunks, h, 2 * CHUNK, GDN_DK), lambda bi, i: (bi, i, 0, 0, 0)),
                  pl.BlockSpec((1, rows, D_VA), lambda bi, i: (bi, i, 0)),
                  pl.BlockSpec((1, chunks, h, CHUNK, CHUNK), lambda bi, i: (bi, i, 0, 0, 0)),
                  pl.BlockSpec((1, chunks, h, GDN_DK, CHUNK), lambda bi, i: (bi, i, 0, 0, 0)),
                  pl.BlockSpec((1, rows, h), lambda bi, i: (bi, i, 0)),
                  pl.BlockSpec((1, rows, D_VA), lambda bi, i: (bi, i, z_block)),
                  pl.BlockSpec((1, GDN_DV), lambda bi, i: (0, 0))],
        out_specs=pl.BlockSpec((1, rows, D_VA), lambda bi, i: (bi, i, 0)),
        out_shape=jax.ShapeDtypeStruct((b, l, D_VA), BF16),
        scratch_shapes=[pltpu.VMEM((h, GDN_DK, GDN_DV), F32)],
        compiler_params=_cparams("parallel", "arbitrary"),
        name="gdn_state",
    )(wq, u, attn, kgt, gc, proj, onorm_w.reshape(1, GDN_DV))


def _gated_deltanet_ln(hid, w_in, conv_w, a_log, dt_bias, onorm_w, w_out, ln_g, ln_b):
    b, l, d = hid.shape
    h = GDN_HEADS
    n = l // CHUNK
    x = hid.reshape(b * l, d)
    n_main = GDN_CONV_CH + D_VA
    proj = _matmul(x, w_in[:, :n_main].astype(BF16), F32, 512, 1024).reshape(b, l, n_main)
    gc, beta, gct = _gdn_gates(x, w_in[:, n_main:], a_log, dt_bias)
    gc = gc.reshape(b, l, h)
    gct = gct.reshape(h, b, n, CHUNK).transpose(1, 2, 0, 3)
    q, k, v = _gdn_conv(proj, conv_w)
    wq, u, attn, kgt = _gdn_prep(q, k, v, gc, beta.reshape(b, l, h), gct)
    o = _gdn_state(wq, u, attn, kgt, gc, proj, onorm_w)
    out = _matmul_ln(o.reshape(b * l, D_VA), w_out.astype(BF16), x, ln_g, ln_b)
    return out.reshape(b, l, d)


def _rope_lane_tables(l):
    inv = 1.0 / (ROPE_THETA ** (jnp.arange(0, D_ROPE, 2, dtype=F32) / D_ROPE))
    ang = jnp.arange(l, dtype=F32)[:, None] * inv[None, :]
    cos, sin = jnp.cos(ang), jnp.sin(ang)
    zero = jnp.zeros((l, LANES - D_ROPE), F32)
    return (jnp.concatenate([cos, cos, zero], -1),
            jnp.concatenate([-sin, sin, zero], -1))


def _rope_weight_groups(w_rope):
    kdim = w_rope.shape[0]
    half = D_ROPE // 2
    zero = jnp.zeros((kdim, LANES - D_ROPE), w_rope.dtype)
    x1, x2 = w_rope[:, :half], w_rope[:, half:]
    return jnp.concatenate([x1, x2, zero, x2, x1, zero], axis=-1)


def _mla_kv_kernel(x_ref, wd_ref, nw_ref, wu_ref, cos_ref, sin_ref, k_ref, v_ref):
    x = x_ref[0]
    ckv = _bdot(x, wd_ref[...])
    c = _rms_norm(ckv[:, :KV_LORA], nw_ref[...])
    k_rope = (ckv[:, KV_LORA:KV_LORA + LANES] * cos_ref[...]
              + ckv[:, KV_LORA + LANES:] * sin_ref[...]).astype(k_ref.dtype)
    kv = _bdot(c, wu_ref[...])
    per = D_NOPE + D_V
    for h in range(MLA_HEADS):
        k_ref[0, h, :, :D_NOPE] = kv[:, h * per:h * per + D_NOPE].astype(k_ref.dtype)
        k_ref[0, h, :, D_NOPE:] = k_rope
        v_ref[0, h] = kv[:, h * per + D_NOPE:(h + 1) * per].astype(v_ref.dtype)


def _mla_kv(hid, w_dkv, kv_norm_w, w_ukv, cos_t, sin_t, tl=512):
    b, l, d = hid.shape
    tl = min(tl, l)
    wd = jnp.concatenate([w_dkv[:, :KV_LORA], _rope_weight_groups(w_dkv[:, KV_LORA:])], -1).astype(BF16)
    nd = wd.shape[1]
    hh = MLA_HEADS
    return pl.pallas_call(
        _mla_kv_kernel,
        grid=(b, l // tl),
        in_specs=[pl.BlockSpec((1, tl, d), lambda bi, i: (bi, i, 0)),
                  pl.BlockSpec((d, nd), lambda bi, i: (0, 0)),
                  pl.BlockSpec((1, KV_LORA), lambda bi, i: (0, 0)),
                  pl.BlockSpec((KV_LORA, hh * (D_NOPE + D_V)), lambda bi, i: (0, 0)),
                  pl.BlockSpec((tl, LANES), lambda bi, i: (i, 0)),
                  pl.BlockSpec((tl, LANES), lambda bi, i: (i, 0))],
        out_specs=[pl.BlockSpec((1, hh, tl, D_NOPE + LANES), lambda bi, i: (bi, 0, i, 0)),
                   pl.BlockSpec((1, hh, tl, D_V), lambda bi, i: (bi, 0, i, 0))],
        out_shape=[jax.ShapeDtypeStruct((b, hh, l, D_NOPE + LANES), BF16),
                   jax.ShapeDtypeStruct((b, hh, l, D_V), BF16)],
        compiler_params=_cparams("parallel", "parallel"),
        name="mla_kv",
    )(hid, wd, kv_norm_w.reshape(1, KV_LORA), w_ukv.astype(BF16), cos_t, sin_t)


def _mla_cq_kernel(x_ref, w_ref, nw_ref, o_ref):
    o_ref[...] = _rms_norm(_bdot(x_ref[...], w_ref[...]), nw_ref[...]).astype(o_ref.dtype)


def _mla_cq(x, w_dq, qnorm_w, tm=512):
    m, k = x.shape
    tm = min(tm, m)
    return pl.pallas_call(
        _mla_cq_kernel,
        grid=(m // tm,),
        in_specs=[pl.BlockSpec((tm, k), lambda i: (i, 0)),
                  pl.BlockSpec((k, Q_LORA), lambda i: (0, 0)),
                  pl.BlockSpec((1, Q_LORA), lambda i: (0, 0))],
        out_specs=pl.BlockSpec((tm, Q_LORA), lambda i: (i, 0)),
        out_shape=jax.ShapeDtypeStruct((m, Q_LORA), BF16),
        compiler_params=_cparams("parallel"),
        name="mla_cq",
    )(x, w_dq.astype(BF16), qnorm_w.reshape(1, Q_LORA))


def _mla_q_kernel(c_ref, w_ref, cos_ref, sin_ref, q_ref):
    c = c_ref[0]
    scale = (D_NOPE + D_ROPE) ** -0.5 * LOG2_E
    per = D_NOPE + 2 * LANES
    for h in range(MLA_HEADS):
        qh = _bdot(c, w_ref[:, h * per:(h + 1) * per])
        rope = qh[:, D_NOPE:D_NOPE + LANES] * cos_ref[...] + qh[:, D_NOPE + LANES:] * sin_ref[...]
        q_ref[0, h, :, :D_NOPE] = (qh[:, :D_NOPE] * scale).astype(q_ref.dtype)
        q_ref[0, h, :, D_NOPE:] = (rope * scale).astype(q_ref.dtype)


def _mla_q(cq, w_uq, cos_t, sin_t, tl=512):
    b, l, _ = cq.shape
    tl = min(tl, l)
    hh = MLA_HEADS
    per_in = D_NOPE + D_ROPE
    groups = []
    for h in range(hh):
        wh = w_uq[:, h * per_in:(h + 1) * per_in]
        groups += [wh[:, :D_NOPE], _rope_weight_groups(wh[:, D_NOPE:])]
    w = jnp.concatenate(groups, -1).astype(BF16)
    return pl.pallas_call(
        _mla_q_kernel,
        grid=(b, l // tl),
        in_specs=[pl.BlockSpec((1, tl, Q_LORA), lambda bi, i: (bi, i, 0)),
                  pl.BlockSpec(w.shape, lambda bi, i: (0, 0)),
                  pl.BlockSpec((tl, LANES), lambda bi, i: (i, 0)),
                  pl.BlockSpec((tl, LANES), lambda bi, i: (i, 0))],
        out_specs=pl.BlockSpec((1, hh, tl, D_NOPE + LANES), lambda bi, i: (bi, 0, i, 0)),
        out_shape=jax.ShapeDtypeStruct((b, hh, l, D_NOPE + LANES), BF16),
        compiler_params=_cparams("parallel", "parallel"),
        name="mla_q",
    )(cq, w, cos_t, sin_t)


def _mla_attn_kernel(q_ref, k_ref, v_ref, o_ref, *, tk):
    qi = pl.program_id(2)
    q = q_ref[0, 0]
    tq = q.shape[0]
    per = tq // tk

    groups = ATTN_ROW_GROUPS
    rows = tq // groups
    qs = [q[g * rows:(g + 1) * rows] for g in range(groups)]

    def block(j, carry, mask_offset=None):
        at = pl.ds(pl.multiple_of(j * tk, tk), tk)
        kb = k_ref[0, 0, at, :]
        vb = v_ref[0, 0, at, :]
        ss = [_bdot_nt(qg, kb) for qg in qs]
        if mask_offset is not None:
            qpos = lax.broadcasted_iota(jnp.int32, (rows, tk), 0)
            kpos = lax.broadcasted_iota(jnp.int32, (rows, tk), 1)
            ss = [jnp.where(kpos + mask_offset <= qpos + g * rows, s, -jnp.inf) for g, s in enumerate(ss)]
        m_new = [jnp.maximum(c[0], jnp.max(s, -1, keepdims=True)) for c, s in zip(carry, ss)]
        ps = [jnp.exp2(s - m) for s, m in zip(ss, m_new)]
        out = []
        for (m, l, acc), mn, p in zip(carry, m_new, ps):
            alpha = jnp.exp2(m - mn)
            out.append((mn, alpha * l + jnp.sum(p, -1, keepdims=True), alpha * acc + _bdot(p, vb)))
        return tuple(out)

    first = qi * per
    init = (jnp.full((rows, 1), -jnp.inf, F32), jnp.zeros((rows, 1), F32), jnp.zeros((rows, D_V), F32))
    carry = lax.fori_loop(0, first, block, (init,) * groups)
    for d in range(per):
        carry = block(first + d, carry, mask_offset=d * tk)
    for g, (_, l, acc) in enumerate(carry):
        o_ref[0, g * rows:(g + 1) * rows, :] = (acc / l).astype(o_ref.dtype)


def _mla_attn(q, k, v, tq=1024, tk=1024):
    b, hh, l, dq = q.shape
    tq = min(tq, l)
    tk = min(tk, tq)
    return pl.pallas_call(
        functools.partial(_mla_attn_kernel, tk=tk),
        grid=(b, hh, l // tq),
        in_specs=[pl.BlockSpec((1, 1, tq, dq), lambda bi, h, i: (bi, h, i, 0)),
                  pl.BlockSpec((1, 1, l, dq), lambda bi, h, i: (bi, h, 0, 0)),
                  pl.BlockSpec((1, 1, l, D_V), lambda bi, h, i: (bi, h, 0, 0))],
        out_specs=pl.BlockSpec((1, tq, D_V), lambda bi, h, i: (bi, i, h)),
        out_shape=jax.ShapeDtypeStruct((b, l, hh * D_V), BF16),
        compiler_params=_cparams("parallel", "parallel", "parallel"),
        name="mla_attn",
    )(q, k, v)


def _mla_ln(hid, w_dq, qnorm_w, w_uq, w_o, k, v, cos_t, sin_t, ln_g, ln_b):
    b, l, d = hid.shape
    x = hid.reshape(b * l, d)
    cq = _mla_cq(x, w_dq, qnorm_w).reshape(b, l, Q_LORA)
    q = _mla_q(cq, w_uq, cos_t, sin_t)
    o = _mla_attn(q, k, v)
    out = _matmul_ln(o.reshape(b * l, MLA_HEADS * D_V), w_o.astype(BF16), x, ln_g, ln_b)
    return out.reshape(b, l, d)


def _first_argmax(x, ids, n):
    m = jnp.max(x, axis=0, keepdims=True)
    first = jnp.min(jnp.where(x == m, ids, n), axis=0, keepdims=True)
    return m, first


def _router_kernel(x_ref, wt_ref, bias_ref, idx_ref, w_ref, rank_ref, cnt_out_ref, cnt_ref, *, steps_per_tile):
    @pl.when(pl.program_id(0) % steps_per_tile == 0)
    def _():
        cnt_ref[...] = jnp.zeros_like(cnt_ref)

    x = x_ref[...]
    t = x.shape[0]
    logits = lax.dot_general(wt_ref[...], x, (((1,), (1,)), ((), ())),
                             precision=HIGHEST, preferred_element_type=F32)
    scores = jax.nn.sigmoid(logits)
    biased = scores + bias_ref[...]
    neg = -jnp.inf
    sub = lax.broadcasted_iota(jnp.int32, (GROUP_SIZE, t), 0).astype(F32)
    gscores = []
    for g in range(N_GROUPS):
        xg = biased[g * GROUP_SIZE:(g + 1) * GROUP_SIZE, :]
        m1, i1 = _first_argmax(xg, sub, float(GROUP_SIZE))
        m2 = jnp.max(jnp.where(sub == i1, neg, xg), axis=0, keepdims=True)
        gscores.append(m1 + m2)
    gs = jnp.concatenate(gscores, axis=0)
    gid = lax.broadcasted_iota(jnp.int32, (N_GROUPS, t), 0).astype(F32)
    gsel = jnp.zeros((N_GROUPS, t), F32)
    for _ in range(TOPK_GROUPS):
        _, gi = _first_argmax(gs, gid, float(N_GROUPS))
        hit = gid == gi
        gsel = jnp.where(hit, 1.0, gsel)
        gs = jnp.where(hit, neg, gs)
    eid = lax.broadcasted_iota(jnp.int32, (N_EXPERTS, t), 0).astype(F32)
    allowed = jnp.concatenate(
        [jnp.broadcast_to(gsel[g:g + 1, :], (GROUP_SIZE, t)) for g in range(N_GROUPS)], axis=0)
    cand = jnp.where(allowed > 0.0, biased, neg)
    idxs, ws, hits = [], [], []
    for _ in range(TOP_K):
        _, ei = _first_argmax(cand, eid, float(N_EXPERTS))
        hit = eid == ei
        idxs.append(ei)
        hits.append(hit)
        ws.append(jnp.sum(jnp.where(hit, scores, 0.0), axis=0, keepdims=True))
        cand = jnp.where(hit, neg, cand)
    w = jnp.concatenate(ws, axis=0)
    w = w / (jnp.sum(w, axis=0, keepdims=True) + 1e-20) * ROUTED_SCALE
    idx_ref[...] = jnp.concatenate(idxs, axis=0).astype(jnp.int32)
    w_ref[...] = w
    chosen = jnp.zeros((N_EXPERTS, t), F32)
    for hit in hits:
        chosen = jnp.where(hit, 1.0, chosen)
    before = (lax.broadcasted_iota(jnp.int32, (t, t), 0) < lax.broadcasted_iota(jnp.int32, (t, t), 1))
    prior = jnp.dot(chosen.astype(BF16), jnp.where(before, 1.0, 0.0).astype(BF16),
                    preferred_element_type=F32) + cnt_ref[...]
    rank_ref[...] = jnp.concatenate(
        [jnp.sum(jnp.where(hit, prior, 0.0), axis=0, keepdims=True) for hit in hits], axis=0).astype(jnp.int32)
    cnt_ref[...] += jnp.sum(chosen, axis=1, keepdims=True)
    cnt_out_ref[0] = cnt_ref[...]


def _router(x, w_router, bias, tile_tokens, tr=512):
    t, d = x.shape
    tr = min(tr, tile_tokens)
    steps_per_tile = tile_tokens // tr
    n_tiles = t // tile_tokens
    kt = pl.BlockSpec((TOP_K, tr), lambda i: (0, i))
    idx, w, rank, counts = pl.pallas_call(
        functools.partial(_router_kernel, steps_per_tile=steps_per_tile),
        grid=(t // tr,),
        in_specs=[pl.BlockSpec((tr, d), lambda i: (i, 0)),
                  pl.BlockSpec((N_EXPERTS, d), lambda i: (0, 0)),
                  pl.BlockSpec((N_EXPERTS, 1), lambda i: (0, 0))],
        out_specs=[kt, kt, kt,
                   pl.BlockSpec((1, N_EXPERTS, 1), lambda i: (i // steps_per_tile, 0, 0))],
        out_shape=[jax.ShapeDtypeStruct((TOP_K, t), jnp.int32),
                   jax.ShapeDtypeStruct((TOP_K, t), F32),
                   jax.ShapeDtypeStruct((TOP_K, t), jnp.int32),
                   jax.ShapeDtypeStruct((n_tiles, N_EXPERTS, 1), F32)],
        scratch_shapes=[pltpu.VMEM((N_EXPERTS, 1), F32)],
        compiler_params=_cparams("arbitrary"),
        name="moe_router",
    )(x, w_router.T, bias.reshape(N_EXPERTS, 1))
    return idx, w, rank, counts.reshape(n_tiles, N_EXPERTS).astype(jnp.int32)


def _steps_per_tile(tile_tokens):
    return tile_tokens * TOP_K // STEP_ROWS + N_EXPERTS


def _plan_kernel(fill_lo_ref, fill_hi_ref, pos_ref, src_ref, *, tile_tokens, part):
    i = pl.program_id(0)
    q = pl.program_id(1)
    empty = tile_tokens * TOP_K
    group = 16

    @pl.when(q == 0)
    def _():
        def fill_range(g, carry):
            lo = fill_lo_ref[i * N_EXPERTS + g]
            hi = fill_hi_ref[i * N_EXPERTS + g]

            def fill(b, carry):
                for d in range(group):
                    src_ref[0, 0, jnp.maximum(hi - 1 - b * group - d, 0)] = empty
                return carry
            return lax.fori_loop(0, (hi - lo + group - 1) // group, fill, carry)
        lax.fori_loop(0, N_EXPERTS, fill_range, 0)

    def place(b, carry):
        base = b * group
        slots = [pos_ref[0, 0, base + d] for d in range(group)]
        for d in range(group):
            src_ref[0, 0, slots[d]] = q * part + base + d
        return carry
    lax.fori_loop(0, part // group, place, 0)


def _dispatch_plan(idx_t, w_t, rank_t, counts, tile_tokens):
    t = idx_t.shape[1]
    n_tiles = t // tile_tokens
    steps = _steps_per_tile(tile_tokens)
    padded = (counts + STEP_ROWS - 1) // STEP_ROWS * STEP_ROWS
    pad_end = jnp.cumsum(padded, axis=1)
    pad_start = (pad_end - padded).astype(jnp.int32)
    n_used = (pad_end[:, -1] // STEP_ROWS).astype(jnp.int32)
    starts = jnp.arange(steps, dtype=jnp.int32) * STEP_ROWS
    step_e = jnp.sum(starts[None, :, None] >= pad_end[:, None, :], axis=-1).astype(jnp.int32)
    step_e = jnp.minimum(step_e, N_EXPERTS - 1)
    last = jnp.take_along_axis(step_e, jnp.maximum(n_used - 1, 0)[:, None], axis=1)
    step_e = jnp.where(starts[None, :] // STEP_ROWS < n_used[:, None], step_e, last)

    start_of = jnp.repeat(pad_start.T, tile_tokens, axis=1)
    experts = jnp.arange(N_EXPERTS, dtype=jnp.int32)[None, :, None]
    pos_t = rank_t + jnp.sum(jnp.where(idx_t[:, None, :] == experts, start_of[None], 0), axis=1)
    fill_lo = (pad_start + counts).astype(jnp.int32)
    fill_hi = pad_end.astype(jnp.int32)

    parts = PLAN_PARTS
    part = tile_tokens * TOP_K // parts
    n_slots = steps * STEP_ROWS
    flat = lambda a: a.T.reshape(n_tiles * parts, 1, part)
    src = pl.pallas_call(
        functools.partial(_plan_kernel, tile_tokens=tile_tokens, part=part),
        grid_spec=pltpu.PrefetchScalarGridSpec(
            num_scalar_prefetch=2,
            grid=(n_tiles, parts),
            in_specs=[pl.BlockSpec((1, 1, part), lambda i, q, lo, hi: (i * parts + q, 0, 0),
                                   memory_space=pltpu.SMEM)],
            out_specs=pl.BlockSpec((1, 1, n_slots), lambda i, q, lo, hi: (i, 0, 0),
                                   memory_space=pltpu.SMEM, pipeline_mode=pl.Buffered(1)),
        ),
        out_shape=jax.ShapeDtypeStruct((n_tiles, 1, n_slots), jnp.int32),
        compiler_params=_cparams("parallel", "arbitrary"),
        name="moe_plan",
    )(fill_lo.reshape(-1), fill_hi.reshape(-1), flat(pos_t))
    src = jnp.where(jnp.arange(n_slots, dtype=jnp.int32)[None, None, :] < pad_end[:, -1][:, None, None],
                    src, tile_tokens * TOP_K)
    rows = src & -SUBLANES
    w_flat = jnp.pad(w_t.T.reshape(n_tiles, tile_tokens * TOP_K), ((0, 0), (0, SUBLANES)))
    slot_w = jnp.take_along_axis(w_flat, src[:, 0, :], axis=1)
    return rows, slot_w, step_e.reshape(-1), n_used, steps


def _experts_kernel(step_e_ref, n_used_ref, row_ref, sw_ref, x_ref, wg_ref, wu_ref, wd_ref, acc_ref, *stage_refs):
    n_sub = BLOCKS_PER_STEP
    xs_refs = stage_refs[:n_sub]
    ys_refs = stage_refs[n_sub:]
    i = pl.program_id(0)
    j = pl.program_id(1)
    rows = EXPERT_ROWS
    pitch = STAGE_PITCH

    @pl.when(j == 0)
    def _():
        acc_ref[...] = jnp.zeros_like(acc_ref)

    def token_rows(slot):
        return pl.ds(pl.multiple_of(row_ref[0, 0, slot], SUBLANES), SUBLANES)

    @pl.when(j < n_used_ref[i])
    def _():
        for s in range(n_sub):
            for r in range(rows):
                xs_refs[s][pl.ds(r, ROW_CHUNKS, stride=pitch), :] = x_ref[0, token_rows(s * rows + r), :]
        eye = (lax.broadcasted_iota(jnp.int32, (rows, rows), 0)
               == lax.broadcasted_iota(jnp.int32, (rows, rows), 1))
        ys = []
        for s in range(n_sub):
            x = jnp.concatenate([xs_refs[s][pl.ds(c * pitch, rows), :] for c in range(ROW_CHUNKS)], axis=1)
            hid = _silu(_bdot(x, wg_ref[0])) * _bdot(x, wu_ref[0])
            w_row = sw_ref[0, :, s * rows:(s + 1) * rows]
            w_col = jnp.sum(jnp.where(eye, w_row, 0.0), axis=1, keepdims=True)
            ys.append(_bdot(hid, wd_ref[0]) * w_col)
        for s in range(n_sub):
            for c in range(ROW_CHUNKS):
                ys_refs[s][pl.ds(c * pitch, rows), :] = ys[s][:, c * LANES:(c + 1) * LANES]
        for s in range(n_sub):
            for r0 in range(0, rows, SCATTER_BATCH):
                new = [acc_ref[0, token_rows(s * rows + r), :] + ys_refs[s][pl.ds(r, ROW_CHUNKS, stride=pitch), :]
                       for r in range(r0, r0 + SCATTER_BATCH)]
                for r, val in zip(range(r0, r0 + SCATTER_BATCH), new):
                    acc_ref[0, token_rows(s * rows + r), :] = val


def _routed_experts(x, idx_t, w_t, rank_t, counts, w_gate, w_up, w_down, tile_tokens):
    t, d = x.shape
    n_tiles = t // tile_tokens
    rows, slot_w, step_e, n_used, steps = _dispatch_plan(idx_t, w_t, rank_t, counts, tile_tokens)
    rows = rows.reshape(n_tiles * steps, 1, STEP_ROWS)
    slot_w = slot_w.reshape(n_tiles * steps, 1, STEP_ROWS)
    slab_rows = (tile_tokens + 1) * ROW_CHUNKS
    x_slabs = jnp.pad(x.reshape(n_tiles, tile_tokens, d), ((0, 0), (0, 1), (0, 0))).reshape(n_tiles, slab_rows, LANES)

    def used_step(i, j, se, nu):
        return (i * steps + jnp.minimum(j, jnp.maximum(nu[i] - 1, 0)), 0, 0)

    pick = lambda i, j, se, nu: (se[i * steps + j], 0, 0)
    tile = lambda i, j, se, nu: (i, 0, 0)
    resident = pl.Buffered(1)
    stage = pltpu.VMEM((ROW_CHUNKS * STAGE_PITCH, LANES), F32)
    grid_spec = pltpu.PrefetchScalarGridSpec(
        num_scalar_prefetch=2,
        grid=(n_tiles, steps),
        in_specs=[
            pl.BlockSpec((1, 1, STEP_ROWS), used_step, memory_space=pltpu.SMEM),
            pl.BlockSpec((1, 1, STEP_ROWS), used_step),
            pl.BlockSpec((1, slab_rows, LANES), tile, pipeline_mode=resident),
            pl.BlockSpec((1, d, D_EXPERT), pick),
            pl.BlockSpec((1, d, D_EXPERT), pick),
            pl.BlockSpec((1, D_EXPERT, d), pick),
        ],
        out_specs=pl.BlockSpec((1, slab_rows, LANES), tile, pipeline_mode=resident),
        scratch_shapes=[stage] * (2 * BLOCKS_PER_STEP),
    )
    return pl.pallas_call(
        _experts_kernel,
        grid_spec=grid_spec,
        out_shape=jax.ShapeDtypeStruct((n_tiles, slab_rows, LANES), F32),
        compiler_params=_cparams("parallel", "arbitrary"),
        name="moe_experts",
    )(step_e, n_used, rows, slot_w, x_slabs, w_gate, w_up, w_down)


def _moe_out_kernel(x_ref, r_ref, wg_ref, wu_ref, wd_ref, g_ref, b_ref, o_ref):
    x = x_ref[...]
    tm = x.shape[0]
    hid = _silu(_bdot(x, wg_ref[...])) * _bdot(x, wu_ref[...])
    routed = jnp.concatenate([r_ref[0, pl.ds(c, tm, stride=ROW_CHUNKS), :] for c in range(ROW_CHUNKS)], axis=1)
    y = routed + _bdot(hid, wd_ref[...])
    o_ref[...] = _layer_norm(DN_ALPHA * x + y, g_ref[...], b_ref[...])


def _moe_out(x, routed, ws_gate, ws_up, ws_down, g, b, tile_tokens, tm=512):
    t, d = x.shape
    tm = min(tm, tile_tokens)
    per_tile = tile_tokens // tm
    return pl.pallas_call(
        _moe_out_kernel,
        grid=(t // tm,),
        in_specs=[pl.BlockSpec((tm, d), lambda i: (i, 0)),
                  pl.BlockSpec((1, tm * ROW_CHUNKS, LANES), lambda i: (i // per_tile, i % per_tile, 0)),
                  pl.BlockSpec((d, D_SHARED), lambda i: (0, 0)),
                  pl.BlockSpec((d, D_SHARED), lambda i: (0, 0)),
                  pl.BlockSpec((D_SHARED, d), lambda i: (0, 0)),
                  pl.BlockSpec((1, d), lambda i: (0, 0)),
                  pl.BlockSpec((1, d), lambda i: (0, 0))],
        out_specs=pl.BlockSpec((tm, d), lambda i: (i, 0)),
        out_shape=jax.ShapeDtypeStruct((t, d), F32),
        compiler_params=_cparams("parallel"),
        name="moe_out",
    )(x, routed, ws_gate.astype(BF16), ws_up.astype(BF16), ws_down.astype(BF16),
      g.reshape(1, d), b.reshape(1, d))


def _moe_ln(hid, w_router, router_bias, w_gate, w_up, w_down, ws_gate, ws_up, ws_down, ln_g, ln_b):
    b, l, d = hid.shape
    x = hid.reshape(b * l, d)
    tile_tokens = min(MOE_TILE_TOKENS, b * l)
    idx_t, w_t, rank_t, counts = _router(x, w_router, router_bias, tile_tokens)
    routed = _routed_experts(x, idx_t, w_t, rank_t, counts, w_gate.astype(BF16), w_up.astype(BF16),
                             w_down.astype(BF16), tile_tokens)
    return _moe_out(x, routed, ws_gate, ws_up, ws_down, ln_g, ln_b, tile_tokens).reshape(b, l, d)


def kernel(x, a_w_in, a_conv_w, a_A_log, a_dt_bias, a_onorm_w, a_w_out, b_w_dq, b_qnorm_w, b_w_uq, b_w_o,
           kv_w_dkv, kv_norm_w, kv_w_ukv, ln1_g, ln1_b, ln2_g, ln2_b, moe_w_router, moe_router_bias,
           moe_w_gate, moe_w_up, moe_w_down, moe_ws_gate, moe_ws_up, moe_ws_down):
    l = x.shape[1]
    cos_t, sin_t = _rope_lane_tables(l)
    h = x
    k = v = None
    for layer in range(DEPTH):
        if layer < N_A_LAYERS:
            i = layer
            h = _gated_deltanet_ln(h, a_w_in[i], a_conv_w[i], a_A_log[i], a_dt_bias[i], a_onorm_w[i],
                                   a_w_out[i], ln1_g[layer], ln1_b[layer])
        else:
            i = layer - N_A_LAYERS
            h = _mla_ln(h, b_w_dq[i], b_qnorm_w[i], b_w_uq[i], b_w_o[i], k, v, cos_t, sin_t,
                        ln1_g[layer], ln1_b[layer])
        h = _moe_ln(h, moe_w_router[layer], moe_router_bias[layer], moe_w_gate[layer], moe_w_up[layer],
                    moe_w_down[layer], moe_ws_gate[layer], moe_ws_up[layer], moe_ws_down[layer],
                    ln2_g[layer], ln2_b[layer])
        if layer == N_A_LAYERS - 1:
            k, v = _mla_kv(h, kv_w_dkv, kv_norm_w, kv_w_ukv, cos_t, sin_t)
    return h
```

```python
import functools

import jax
import jax.numpy as jnp
import numpy as np
from jax import lax
from jax.experimental import pallas as pl
from jax.experimental.pallas import tpu as pltpu

F32 = jnp.float32
BF16 = jnp.bfloat16
HIGHEST = lax.Precision.HIGHEST

D_MODEL = 1024
DEPTH = 4
N_A_LAYERS = DEPTH // 2
GDN_HEADS = 8
GDN_DK = 128
GDN_DV = 128
D_QK = GDN_HEADS * GDN_DK
D_VA = GDN_HEADS * GDN_DV
GDN_CONV_CH = 2 * D_QK + D_VA
CONV_WIDTH = 4
CHUNK = 64
MLA_HEADS = 8
Q_LORA = 512
KV_LORA = 256
D_NOPE = 128
D_ROPE = 64
D_V = 128
ROPE_THETA = 10000.0
N_EXPERTS = 64
TOP_K = 8
N_GROUPS = 8
GROUP_SIZE = N_EXPERTS // N_GROUPS
TOPK_GROUPS = 4
D_EXPERT = 256
D_SHARED = 256
ROUTED_SCALE = 2.5
DN_ALPHA = (2 * DEPTH) ** 0.25
EPS = 1e-6
LN_EPS = 1e-5
LOG2_E = 1.4426950408889634

LANES = 128
SUBLANES = 8
VMEM_LIMIT_BYTES = 56 * 1024 * 1024

EXPERT_ROWS = 128
ROW_CHUNKS = D_MODEL // LANES
STAGE_PITCH = EXPERT_ROWS + 1
SCATTER_BATCH = 16
BLOCKS_PER_STEP = 2
STEP_ROWS = BLOCKS_PER_STEP * EXPERT_ROWS
MOE_TILE_TOKENS = 4096
PLAN_PARTS = 4
ATTN_ROW_GROUPS = 4
assert TOP_K == SUBLANES


def _cparams(*sem):
    return pltpu.CompilerParams(dimension_semantics=sem, vmem_limit_bytes=VMEM_LIMIT_BYTES)


def _silu(x):
    return x * jax.nn.sigmoid(x)


def _layer_norm(x, g, b):
    mu = jnp.mean(x, -1, keepdims=True)
    xc = x - mu
    var = jnp.mean(xc * xc, -1, keepdims=True)
    return xc * lax.rsqrt(var + LN_EPS) * g + b


def _rms_norm(x, w):
    return x * lax.rsqrt(jnp.mean(x * x, -1, keepdims=True) + EPS) * w


def _bdot(a, b):
    return jnp.dot(a.astype(BF16), b.astype(BF16), preferred_element_type=F32)


def _bdot_nt(a, b):
    return lax.dot_general(a.astype(BF16), b.astype(BF16), (((1,), (1,)), ((), ())),
                           preferred_element_type=F32)


def _hdot(a, b):
    return jnp.dot(a, b, precision=HIGHEST, preferred_element_type=F32)


def _matmul_kernel(x_ref, w_ref, o_ref):
    o_ref[...] = _bdot(x_ref[...], w_ref[...]).astype(o_ref.dtype)


def _matmul(x, w, out_dtype, tm, tn):
    m, k = x.shape
    n = w.shape[1]
    tm = min(tm, m)
    tn = min(tn, n)
    return pl.pallas_call(
        _matmul_kernel,
        grid=(m // tm, n // tn),
        in_specs=[pl.BlockSpec((tm, k), lambda i, j: (i, 0)),
                  pl.BlockSpec((k, tn), lambda i, j: (0, j))],
        out_specs=pl.BlockSpec((tm, tn), lambda i, j: (i, j)),
        out_shape=jax.ShapeDtypeStruct((m, n), out_dtype),
        compiler_params=_cparams("parallel", "parallel"),
        name="matmul",
    )(x, w)


def _matmul_ln_kernel(x_ref, w_ref, r_ref, g_ref, b_ref, o_ref, slab_ref):
    mix = _bdot(x_ref[...], w_ref[...])
    out = _layer_norm(DN_ALPHA * r_ref[...] + mix, g_ref[...], b_ref[...])
    o_ref[...] = out
    tm = out.shape[0]
    for c in range(ROW_CHUNKS):
        slab_ref[pl.ds(c, tm, stride=ROW_CHUNKS), :] = out[:, c * LANES:(c + 1) * LANES]


def _matmul_ln(x, w, res, g, b, tm=512):
    m, k = x.shape
    n = w.shape[1]
    tm = min(tm, m)
    chunks = n // LANES
    return pl.pallas_call(
        _matmul_ln_kernel,
        grid=(m // tm,),
        in_specs=[pl.BlockSpec((tm, k), lambda i: (i, 0)),
                  pl.BlockSpec((k, n), lambda i: (0, 0)),
                  pl.BlockSpec((tm, n), lambda i: (i, 0)),
                  pl.BlockSpec((1, n), lambda i: (0, 0)),
                  pl.BlockSpec((1, n), lambda i: (0, 0))],
        out_specs=[pl.BlockSpec((tm, n), lambda i: (i, 0)),
                   pl.BlockSpec((tm * chunks, LANES), lambda i: (i, 0))],
        out_shape=[jax.ShapeDtypeStruct((m, n), F32),
                   jax.ShapeDtypeStruct((m * chunks, LANES), F32)],
        compiler_params=_cparams("parallel"),
        name="matmul_ln",
    )(x, w, res, g.reshape(1, n), b.reshape(1, n))


def _softplus(x):
    return jnp.maximum(x, 0.0) + jnp.log1p(jnp.exp(-jnp.abs(x)))


def _gdn_gates_kernel(x_ref, wab_ref, wabt_ref, alog_ref, dtb_ref, alogt_ref, dtbt_ref,
                      gc_ref, beta_ref, gct_ref):
    x = x_ref[...]
    tl = x.shape[0]
    h = GDN_HEADS
    ab = _hdot(x, wab_ref[...])
    abt = lax.dot_general(wabt_ref[...], x, (((1,), (1,)), ((), ())),
                          precision=HIGHEST, preferred_element_type=F32)
    g = -jnp.exp(alog_ref[...]) * _softplus(ab[:, :h] + dtb_ref[...])
    gt = -jnp.exp(alogt_ref[...]) * _softplus(abt[:h, :] + dtbt_ref[...])
    beta_ref[...] = jax.nn.sigmoid(ab[:, h:2 * h])
    row = lax.broadcasted_iota(jnp.int32, (tl, tl), 0)
    col = lax.broadcasted_iota(jnp.int32, (tl, tl), 1)
    shift = CHUNK.bit_length() - 1
    same = (row >> shift) == (col >> shift)
    lower = jnp.where(same & (col <= row), 1.0, 0.0).astype(F32)
    upper = jnp.where(same & (row <= col), 1.0, 0.0).astype(F32)
    gc_ref[...] = _hdot(lower, g)
    gct_ref[...] = _hdot(gt, upper)


def _gdn_gates(x, w_ab, a_log, dt_bias, tl=512):
    t = x.shape[0]
    tl = min(tl, t)
    h = GDN_HEADS
    wab = jnp.zeros((D_MODEL, LANES), F32).at[:, :2 * h].set(w_ab)
    wabt = w_ab.T
    return pl.pallas_call(
        _gdn_gates_kernel,
        grid=(t // tl,),
        in_specs=[pl.BlockSpec((tl, D_MODEL), lambda i: (i, 0)),
                  pl.BlockSpec((D_MODEL, LANES), lambda i: (0, 0)),
                  pl.BlockSpec((2 * h, D_MODEL), lambda i: (0, 0)),
                  pl.BlockSpec((1, h), lambda i: (0, 0)),
                  pl.BlockSpec((1, h), lambda i: (0, 0)),
                  pl.BlockSpec((h, 1), lambda i: (0, 0)),
                  pl.BlockSpec((h, 1), lambda i: (0, 0))],
        out_specs=[pl.BlockSpec((tl, h), lambda i: (i, 0)),
                   pl.BlockSpec((tl, h), lambda i: (i, 0)),
                   pl.BlockSpec((h, tl), lambda i: (0, i))],
        out_shape=[jax.ShapeDtypeStruct((t, h), F32),
                   jax.ShapeDtypeStruct((t, h), F32),
                   jax.ShapeDtypeStruct((h, t), F32)],
        compiler_params=_cparams("parallel"),
        name="gdn_gates",
    )(x, wab, wabt, a_log.reshape(1, h), dt_bias.reshape(1, h),
      a_log.reshape(h, 1), dt_bias.reshape(h, 1))


def _gdn_conv_kernel(cur_ref, prev_ref, w_ref, q_ref, k_ref, v_ref):
    i = pl.program_id(1)
    x = cur_ref[0]
    tl = x.shape[0]
    prev = jnp.where(i > 0, prev_ref[0], 0.0)
    xx = jnp.concatenate([prev, x], axis=0)
    w = w_ref[...]
    y = None
    for j in range(CONV_WIDTH):
        off = SUBLANES - (CONV_WIDTH - 1) + j
        term = xx[off:off + tl, :] * w[j:j + 1, :]
        y = term if y is None else y + term
    y = _silu(y)
    for h in range(GDN_HEADS):
        sl = slice(h * GDN_DK, (h + 1) * GDN_DK)
        qh = y[:, sl]
        q_ref[0, :, sl] = qh * lax.rsqrt(jnp.sum(qh * qh, -1, keepdims=True) + EPS) * (GDN_DK ** -0.5)
        kh = y[:, D_QK + h * GDN_DK:D_QK + (h + 1) * GDN_DK]
        k_ref[0, :, sl] = kh * lax.rsqrt(jnp.sum(kh * kh, -1, keepdims=True) + EPS)
    v_ref[0] = y[:, 2 * D_QK:]


def _gdn_conv(proj, conv_w, tl=256):
    b, l, _ = proj.shape
    tl = min(tl, l)
    c3 = GDN_CONV_CH
    per = tl // SUBLANES
    out = jax.ShapeDtypeStruct((b, l, D_QK), F32)
    return pl.pallas_call(
        _gdn_conv_kernel,
        grid=(b, l // tl),
        in_specs=[pl.BlockSpec((1, tl, c3), lambda bi, i: (bi, i, 0)),
                  pl.BlockSpec((1, SUBLANES, c3), lambda bi, i: (bi, jnp.maximum(i * per - 1, 0), 0)),
                  pl.BlockSpec((CONV_WIDTH, c3), lambda bi, i: (0, 0))],
        out_specs=[pl.BlockSpec((1, tl, D_QK), lambda bi, i: (bi, i, 0))] * 3,
        out_shape=[out, out, out],
        compiler_params=_cparams("parallel", "parallel"),
        name="gdn_conv",
    )(proj, proj, conv_w)


def _split_bf16(x):
    hi = x.astype(BF16)
    lo = (x - hi.astype(F32)).astype(BF16)
    return hi, lo


def _dot_split(a, b):
    (ah, al), (bh, bl) = a, b
    dot = functools.partial(jnp.dot, preferred_element_type=F32)
    return dot(ah, bh) + (dot(ah, bl) + dot(al, bh))


def _gdn_prep_kernel(q_ref, k_ref, v_ref, gc_ref, beta_ref, gct_ref, wq_ref, u_ref, attn_ref, kgt_ref,
                     *, chunks):
    c = CHUNK
    row = lax.broadcasted_iota(jnp.int32, (c, c), 0)
    col = lax.broadcasted_iota(jnp.int32, (c, c), 1)
    incl = row >= col
    strict = row > col
    eye = jnp.where(row == col, 1.0, 0.0).astype(F32)
    items = [(ci, h) for ci in range(chunks) for h in range(GDN_HEADS)]

    st = {}
    for ci, h in items:
        rows = slice(ci * c, (ci + 1) * c)
        sl = slice(h * GDN_DK, (h + 1) * GDN_DK)
        q = q_ref[0, rows, sl]
        k = k_ref[0, rows, sl]
        v = v_ref[0, rows, sl]
        gcol = gc_ref[0, rows, h:h + 1]
        grow = gct_ref[0, ci, h:h + 1, :]
        bcol = beta_ref[0, rows, h:h + 1]
        decay = jnp.where(incl, jnp.exp(jnp.where(incl, gcol - grow, 0.0)), 0.0)
        kb = k * bcol
        eg = jnp.exp(gcol)
        g_last = gcol[c - 1:c, :]
        sc = _bdot_nt(jnp.concatenate([kb, q], axis=0), k)
        a = jnp.where(strict, sc[:c] * decay, 0.0)
        attn_ref[0, ci, h] = (sc[c:] * decay).astype(attn_ref.dtype)
        kgt_ref[0, ci, h] = (k * jnp.exp(g_last - gcol)).T.astype(kgt_ref.dtype)
        wq_ref[0, ci, h, c:, :] = (q * eg).astype(wq_ref.dtype)
        st[ci, h] = dict(m=-a, rhs=jnp.concatenate([v * bcol, kb * eg], axis=1))

    for it in items:
        st[it]["p"] = eye + st[it]["m"]
    span = 2
    while span < c:
        for it in items:
            ms = _split_bf16(st[it]["m"])
            st[it]["m"] = _dot_split(ms, ms)
        for it in items:
            st[it]["p"] = st[it]["p"] + _dot_split(_split_bf16(st[it]["p"]), _split_bf16(st[it]["m"]))
        span *= 2

    for ci, h in items:
        sol = _dot_split(_split_bf16(st[ci, h]["p"]), _split_bf16(st[ci, h]["rhs"]))
        u_ref[0, ci * c:(ci + 1) * c, h * GDN_DV:(h + 1) * GDN_DV] = sol[:, :GDN_DV]
        wq_ref[0, ci, h, :c, :] = sol[:, GDN_DV:].astype(wq_ref.dtype)


def _gdn_prep(q, k, v, gc, beta, gct, chunks=2):
    b, l, _ = q.shape
    n = l // CHUNK
    chunks = min(chunks, n)
    h = GDN_HEADS
    rows = chunks * CHUNK
    blk = pl.BlockSpec((1, rows, D_QK), lambda bi, i: (bi, i, 0))
    gate = pl.BlockSpec((1, rows, h), lambda bi, i: (bi, i, 0))
    return pl.pallas_call(
        functools.partial(_gdn_prep_kernel, chunks=chunks),
        grid=(b, n // chunks),
        in_specs=[blk, blk, blk, gate, gate,
                  pl.BlockSpec((1, chunks, h, CHUNK), lambda bi, i: (bi, i, 0, 0))],
        out_specs=[pl.BlockSpec((1, chunks, h, 2 * CHUNK, GDN_DK), lambda bi, i: (bi, i, 0, 0, 0)),
                   pl.BlockSpec((1, rows, D_VA), lambda bi, i: (bi, i, 0)),
                   pl.BlockSpec((1, chunks, h, CHUNK, CHUNK), lambda bi, i: (bi, i, 0, 0, 0)),
                   pl.BlockSpec((1, chunks, h, GDN_DK, CHUNK), lambda bi, i: (bi, i, 0, 0, 0))],
        out_shape=[jax.ShapeDtypeStruct((b, n, h, 2 * CHUNK, GDN_DK), BF16),
                   jax.ShapeDtypeStruct((b, l, D_VA), F32),
                   jax.ShapeDtypeStruct((b, n, h, CHUNK, CHUNK), BF16),
                   jax.ShapeDtypeStruct((b, n, h, GDN_DK, CHUNK), BF16)],
        compiler_params=_cparams("parallel", "parallel"),
        name="gdn_prep",
    )(q, k, v, gc, beta, gct)


def _gdn_state_kernel(wq_ref, u_ref, attn_ref, kgt_ref, gc_ref, z_ref, onw_ref, o_ref, s_ref, *, chunks):
    @pl.when(pl.program_id(1) == 0)
    def _():
        s_ref[...] = jnp.zeros_like(s_ref)

    c = CHUNK
    onw = onw_ref[...]
    dot = functools.partial(jnp.dot, preferred_element_type=F32)
    heads = range(GDN_HEADS)
    for ci in range(chunks):
        rows = slice(ci * c, (ci + 1) * c)
        s_old = [s_ref[h] for h in heads]
        r = [dot(wq_ref[0, ci, h], s_old[h].astype(BF16)) for h in heads]
        v_new = [(u_ref[0, rows, h * GDN_DV:(h + 1) * GDN_DV] - r[h][:c]).astype(BF16) for h in heads]
        for h in heads:
            decay_last = jnp.exp(gc_ref[0, (ci + 1) * c - 1:(ci + 1) * c, h:h + 1])
            s_ref[h] = s_old[h] * decay_last + dot(kgt_ref[0, ci, h], v_new[h])
        for h in heads:
            sl = slice(h * GDN_DV, (h + 1) * GDN_DV)
            o = r[h][c:] + dot(attn_ref[0, ci, h], v_new[h])
            o_ref[0, rows, sl] = (_rms_norm(o, onw) * _silu(z_ref[0, rows, sl])).astype(o_ref.dtype)


def _gdn_state(wq, u, attn, kgt, gc, proj, onorm_w, chunks=4):
    b, n, h = wq.shape[:3]
    l = n * CHUNK
    chunks = min(chunks, n)
    rows = chunks * CHUNK
    z_block = GDN_CONV_CH // D_VA
    return pl.pallas_call(
        functools.partial(_gdn_state_kernel, chunks=chunks),
        grid=(b, n // chunks),
        in_specs=[pl.BlockSpec((1, chunks, h, 2 * CHUNK, GDN_DK), lambda bi, i: (bi, i, 0, 0, 0)),
                  pl.BlockSpec((1, rows, D_VA), lambda bi, i: (bi, i, 0)),
                  pl.BlockSpec((1, chunks, h, CHUNK, CHUNK), lambda bi, i: (bi, i, 0, 0, 0)),
                  pl.BlockSpec((1, chunks, h, GDN_DK, CHUNK), lambda bi, i: (bi, i, 0, 0, 0)),
                  pl.BlockSpec((1, rows, h), lambda bi, i: (bi, i, 0)),
                  pl.BlockSpec((1, rows, D_VA), lambda bi, i: (bi, i, z_block)),
                  pl.BlockSpec((1, GDN_DV), lambda bi, i: (0, 0))],
        out_specs=pl.BlockSpec((1, rows, D_VA), lambda bi, i: (bi, i, 0)),
        out_shape=jax.ShapeDtypeStruct((b, l, D_VA), BF16),
        scratch_shapes=[pltpu.VMEM((h, GDN_DK, GDN_DV), F32)],
        compiler_params=_cparams("parallel", "arbitrary"),
        name="gdn_state",
    )(wq, u, attn, kgt, gc, proj, onorm_w.reshape(1, GDN_DV))


def _gated_deltanet_ln(hid, w_in, conv_w, a_log, dt_bias, onorm_w, w_out, ln_g, ln_b):
    b, l, d = hid.shape
    h = GDN_HEADS
    n = l // CHUNK
    x = hid.reshape(b * l, d)
    n_main = GDN_CONV_CH + D_VA
    proj = _matmul(x, w_in[:, :n_main].astype(BF16), F32, 512, 1024).reshape(b, l, n_main)
    gc, beta, gct = _gdn_gates(x, w_in[:, n_main:], a_log, dt_bias)
    gc = gc.reshape(b, l, h)
    gct = gct.reshape(h, b, n, CHUNK).transpose(1, 2, 0, 3)
    q, k, v = _gdn_conv(proj, conv_w)
    wq, u, attn, kgt = _gdn_prep(q, k, v, gc, beta.reshape(b, l, h), gct)
    o = _gdn_state(wq, u, attn, kgt, gc, proj, onorm_w)
    out, slabs = _matmul_ln(o.reshape(b * l, D_VA), w_out.astype(BF16), x, ln_g, ln_b)
    return out.reshape(b, l, d), slabs


def _rope_lane_tables(l):
    inv = 1.0 / (ROPE_THETA ** (jnp.arange(0, D_ROPE, 2, dtype=F32) / D_ROPE))
    ang = jnp.arange(l, dtype=F32)[:, None] * inv[None, :]
    cos, sin = jnp.cos(ang), jnp.sin(ang)
    zero = jnp.zeros((l, LANES - D_ROPE), F32)
    return (jnp.concatenate([cos, cos, zero], -1),
            jnp.concatenate([-sin, sin, zero], -1))


def _rope_weight_groups(w_rope):
    kdim = w_rope.shape[0]
    half = D_ROPE // 2
    zero = jnp.zeros((kdim, LANES - D_ROPE), w_rope.dtype)
    x1, x2 = w_rope[:, :half], w_rope[:, half:]
    return jnp.concatenate([x1, x2, zero, x2, x1, zero], axis=-1)


def _mla_kv_kernel(x_ref, wd_ref, nw_ref, wu_ref, cos_ref, sin_ref, k_ref, v_ref):
    x = x_ref[0]
    ckv = _bdot(x, wd_ref[...])
    c = _rms_norm(ckv[:, :KV_LORA], nw_ref[...])
    k_rope_t = (ckv[:, KV_LORA:KV_LORA + LANES] * cos_ref[...]
                + ckv[:, KV_LORA + LANES:] * sin_ref[...]).T.astype(k_ref.dtype)
    kv = _bdot(c, wu_ref[...])
    per = D_NOPE + D_V
    for h in range(MLA_HEADS):
        k_ref[0, h, :D_NOPE, :] = kv[:, h * per:h * per + D_NOPE].T.astype(k_ref.dtype)
        k_ref[0, h, D_NOPE:, :] = k_rope_t
        v_ref[0, h] = kv[:, h * per + D_NOPE:(h + 1) * per].astype(v_ref.dtype)


def _mla_kv(hid, w_dkv, kv_norm_w, w_ukv, cos_t, sin_t, tl=512):
    b, l, d = hid.shape
    tl = min(tl, l)
    wd = jnp.concatenate([w_dkv[:, :KV_LORA], _rope_weight_groups(w_dkv[:, KV_LORA:])], -1).astype(BF16)
    nd = wd.shape[1]
    hh = MLA_HEADS
    return pl.pallas_call(
        _mla_kv_kernel,
        grid=(b, l // tl),
        in_specs=[pl.BlockSpec((1, tl, d), lambda bi, i: (bi, i, 0)),
                  pl.BlockSpec((d, nd), lambda bi, i: (0, 0)),
                  pl.BlockSpec((1, KV_LORA), lambda bi, i: (0, 0)),
                  pl.BlockSpec((KV_LORA, hh * (D_NOPE + D_V)), lambda bi, i: (0, 0)),
                  pl.BlockSpec((tl, LANES), lambda bi, i: (i, 0)),
                  pl.BlockSpec((tl, LANES), lambda bi, i: (i, 0))],
        out_specs=[pl.BlockSpec((1, hh, D_NOPE + LANES, tl), lambda bi, i: (bi, 0, 0, i)),
                   pl.BlockSpec((1, hh, tl, D_V), lambda bi, i: (bi, 0, i, 0))],
        out_shape=[jax.ShapeDtypeStruct((b, hh, D_NOPE + LANES, l), BF16),
                   jax.ShapeDtypeStruct((b, hh, l, D_V), BF16)],
        compiler_params=_cparams("parallel", "parallel"),
        name="mla_kv",
    )(hid, wd, kv_norm_w.reshape(1, KV_LORA), w_ukv.astype(BF16), cos_t, sin_t)


def _mla_cq_kernel(x_ref, w_ref, nw_ref, o_ref):
    o_ref[...] = _rms_norm(_bdot(x_ref[...], w_ref[...]), nw_ref[...]).astype(o_ref.dtype)


def _mla_cq(x, w_dq, qnorm_w, tm=512):
    m, k = x.shape
    tm = min(tm, m)
    return pl.pallas_call(
        _mla_cq_kernel,
        grid=(m // tm,),
        in_specs=[pl.BlockSpec((tm, k), lambda i: (i, 0)),
                  pl.BlockSpec((k, Q_LORA), lambda i: (0, 0)),
                  pl.BlockSpec((1, Q_LORA), lambda i: (0, 0))],
        out_specs=pl.BlockSpec((tm, Q_LORA), lambda i: (i, 0)),
        out_shape=jax.ShapeDtypeStruct((m, Q_LORA), BF16),
        compiler_params=_cparams("parallel"),
        name="mla_cq",
    )(x, w_dq.astype(BF16), qnorm_w.reshape(1, Q_LORA))


def _mla_q_kernel(c_ref, w_ref, cos_ref, sin_ref, q_ref):
    c = c_ref[0]
    scale = (D_NOPE + D_ROPE) ** -0.5 * LOG2_E
    per = D_NOPE + 2 * LANES
    for h in range(MLA_HEADS):
        qh = _bdot(c, w_ref[:, h * per:(h + 1) * per])
        rope = qh[:, D_NOPE:D_NOPE + LANES] * cos_ref[...] + qh[:, D_NOPE + LANES:] * sin_ref[...]
        q_ref[0, h, :, :D_NOPE] = (qh[:, :D_NOPE] * scale).astype(q_ref.dtype)
        q_ref[0, h, :, D_NOPE:] = (rope * scale).astype(q_ref.dtype)


def _mla_q(cq, w_uq, cos_t, sin_t, tl=512):
    b, l, _ = cq.shape
    tl = min(tl, l)
    hh = MLA_HEADS
    per_in = D_NOPE + D_ROPE
    groups = []
    for h in range(hh):
        wh = w_uq[:, h * per_in:(h + 1) * per_in]
        groups += [wh[:, :D_NOPE], _rope_weight_groups(wh[:, D_NOPE:])]
    w = jnp.concatenate(groups, -1).astype(BF16)
    return pl.pallas_call(
        _mla_q_kernel,
        grid=(b, l // tl),
        in_specs=[pl.BlockSpec((1, tl, Q_LORA), lambda bi, i: (bi, i, 0)),
                  pl.BlockSpec(w.shape, lambda bi, i: (0, 0)),
                  pl.BlockSpec((tl, LANES), lambda bi, i: (i, 0)),
                  pl.BlockSpec((tl, LANES), lambda bi, i: (i, 0))],
        out_specs=pl.BlockSpec((1, hh, tl, D_NOPE + LANES), lambda bi, i: (bi, 0, i, 0)),
        out_shape=jax.ShapeDtypeStruct((b, hh, l, D_NOPE + LANES), BF16),
        compiler_params=_cparams("parallel", "parallel"),
        name="mla_q",
    )(cq, w, cos_t, sin_t)


def _mla_attn_kernel(q_ref, k_ref, v_ref, o_ref, *, tk):
    qi = pl.program_id(2)
    q = q_ref[0, 0]
    tq = q.shape[0]
    per = tq // tk

    groups = ATTN_ROW_GROUPS
    rows = tq // groups
    qs = [q[g * rows:(g + 1) * rows] for g in range(groups)]

    def block(j, carry, mask_offset=None):
        at = pl.ds(pl.multiple_of(j * tk, tk), tk)
        kb = k_ref[0, 0, :, at]
        vb = v_ref[0, 0, at, :]
        ss = [_bdot(qg, kb) for qg in qs]
        if mask_offset is not None:
            qpos = lax.broadcasted_iota(jnp.int32, (rows, tk), 0)
            kpos = lax.broadcasted_iota(jnp.int32, (rows, tk), 1)
            ss = [jnp.where(kpos + mask_offset <= qpos + g * rows, s, -jnp.inf) for g, s in enumerate(ss)]
        m_new = [jnp.maximum(c[0], jnp.max(s, -1, keepdims=True)) for c, s in zip(carry, ss)]
        ps = [jnp.exp2(s - m) for s, m in zip(ss, m_new)]
        out = []
        for (m, l, acc), mn, p in zip(carry, m_new, ps):
            alpha = jnp.exp2(m - mn)
            out.append((mn, alpha * l + jnp.sum(p, -1, keepdims=True), alpha * acc + _bdot(p, vb)))
        return tuple(out)

    first = qi * per
    init = (jnp.full((rows, 1), -jnp.inf, F32), jnp.zeros((rows, 1), F32), jnp.zeros((rows, D_V), F32))
    carry = lax.fori_loop(0, first, block, (init,) * groups)
    for d in range(per):
        carry = block(first + d, carry, mask_offset=d * tk)
    for g, (_, l, acc) in enumerate(carry):
        o_ref[0, g * rows:(g + 1) * rows, :] = (acc / l).astype(o_ref.dtype)


def _mla_attn(q, k, v, tq=1024, tk=1024):
    b, hh, l, dq = q.shape
    tq = min(tq, l)
    tk = min(tk, tq)
    return pl.pallas_call(
        functools.partial(_mla_attn_kernel, tk=tk),
        grid=(b, hh, l // tq),
        in_specs=[pl.BlockSpec((1, 1, tq, dq), lambda bi, h, i: (bi, h, i, 0)),
                  pl.BlockSpec((1, 1, dq, l), lambda bi, h, i: (bi, h, 0, 0)),
                  pl.BlockSpec((1, 1, l, D_V), lambda bi, h, i: (bi, h, 0, 0))],
        out_specs=pl.BlockSpec((1, tq, D_V), lambda bi, h, i: (bi, i, h)),
        out_shape=jax.ShapeDtypeStruct((b, l, hh * D_V), BF16),
        compiler_params=_cparams("parallel", "parallel", "parallel"),
        name="mla_attn",
    )(q, k, v)


def _mla_ln(hid, w_dq, qnorm_w, w_uq, w_o, k, v, cos_t, sin_t, ln_g, ln_b):
    b, l, d = hid.shape
    x = hid.reshape(b * l, d)
    cq = _mla_cq(x, w_dq, qnorm_w).reshape(b, l, Q_LORA)
    q = _mla_q(cq, w_uq, cos_t, sin_t)
    o = _mla_attn(q, k, v)
    out, slabs = _matmul_ln(o.reshape(b * l, MLA_HEADS * D_V), w_o.astype(BF16), x, ln_g, ln_b)
    return out.reshape(b, l, d), slabs


def _first_argmax(x, ids, n):
    m = jnp.max(x, axis=0, keepdims=True)
    first = jnp.min(jnp.where(x == m, ids, n), axis=0, keepdims=True)
    return m, first


def _router_kernel(x_ref, wt_ref, bias_ref, idx_ref, w_ref, rank_ref, cnt_out_ref, cnt_ref, *, steps_per_tile):
    @pl.when(pl.program_id(0) % steps_per_tile == 0)
    def _():
        cnt_ref[...] = jnp.zeros_like(cnt_ref)

    x = x_ref[...]
    t = x.shape[0]
    logits = lax.dot_general(wt_ref[...], x, (((1,), (1,)), ((), ())),
                             precision=HIGHEST, preferred_element_type=F32)
    scores = jax.nn.sigmoid(logits)
    biased = scores + bias_ref[...]
    neg = -jnp.inf
    sub = lax.broadcasted_iota(jnp.int32, (GROUP_SIZE, t), 0).astype(F32)
    gscores = []
    for g in range(N_GROUPS):
        xg = biased[g * GROUP_SIZE:(g + 1) * GROUP_SIZE, :]
        m1, i1 = _first_argmax(xg, sub, float(GROUP_SIZE))
        m2 = jnp.max(jnp.where(sub == i1, neg, xg), axis=0, keepdims=True)
        gscores.append(m1 + m2)
    gs = jnp.concatenate(gscores, axis=0)
    gid = lax.broadcasted_iota(jnp.int32, (N_GROUPS, t), 0).astype(F32)
    gsel = jnp.zeros((N_GROUPS, t), F32)
    for _ in range(TOPK_GROUPS):
        _, gi = _first_argmax(gs, gid, float(N_GROUPS))
        hit = gid == gi
        gsel = jnp.where(hit, 1.0, gsel)
        gs = jnp.where(hit, neg, gs)
    eid = lax.broadcasted_iota(jnp.int32, (N_EXPERTS, t), 0).astype(F32)
    allowed = jnp.concatenate(
        [jnp.broadcast_to(gsel[g:g + 1, :], (GROUP_SIZE, t)) for g in range(N_GROUPS)], axis=0)
    cand = jnp.where(allowed > 0.0, biased, neg)
    idxs, ws, hits = [], [], []
    for _ in range(TOP_K):
        _, ei = _first_argmax(cand, eid, float(N_EXPERTS))
        hit = eid == ei
        idxs.append(ei)
        hits.append(hit)
        ws.append(jnp.sum(jnp.where(hit, scores, 0.0), axis=0, keepdims=True))
        cand = jnp.where(hit, neg, cand)
    w = jnp.concatenate(ws, axis=0)
    w = w / (jnp.sum(w, axis=0, keepdims=True) + 1e-20) * ROUTED_SCALE
    idx_ref[...] = jnp.concatenate(idxs, axis=0).astype(jnp.int32)
    w_ref[...] = w
    chosen = jnp.zeros((N_EXPERTS, t), F32)
    for hit in hits:
        chosen = jnp.where(hit, 1.0, chosen)
    before = (lax.broadcasted_iota(jnp.int32, (t, t), 0) < lax.broadcasted_iota(jnp.int32, (t, t), 1))
    prior = jnp.dot(chosen.astype(BF16), jnp.where(before, 1.0, 0.0).astype(BF16),
                    preferred_element_type=F32) + cnt_ref[...]
    rank_ref[...] = jnp.concatenate(
        [jnp.sum(jnp.where(hit, prior, 0.0), axis=0, keepdims=True) for hit in hits], axis=0).astype(jnp.int32)
    cnt_ref[...] += jnp.sum(chosen, axis=1, keepdims=True)
    cnt_out_ref[0] = cnt_ref[...]


def _router(x, w_router, bias, tile_tokens, tr=512):
    t, d = x.shape
    tr = min(tr, tile_tokens)
    steps_per_tile = tile_tokens // tr
    n_tiles = t // tile_tokens
    kt = pl.BlockSpec((TOP_K, tr), lambda i: (0, i))
    idx, w, rank, counts = pl.pallas_call(
        functools.partial(_router_kernel, steps_per_tile=steps_per_tile),
        grid=(t // tr,),
        in_specs=[pl.BlockSpec((tr, d), lambda i: (i, 0)),
                  pl.BlockSpec((N_EXPERTS, d), lambda i: (0, 0)),
                  pl.BlockSpec((N_EXPERTS, 1), lambda i: (0, 0))],
        out_specs=[kt, kt, kt,
                   pl.BlockSpec((1, N_EXPERTS, 1), lambda i: (i // steps_per_tile, 0, 0))],
        out_shape=[jax.ShapeDtypeStruct((TOP_K, t), jnp.int32),
                   jax.ShapeDtypeStruct((TOP_K, t), F32),
                   jax.ShapeDtypeStruct((TOP_K, t), jnp.int32),
                   jax.ShapeDtypeStruct((n_tiles, N_EXPERTS, 1), F32)],
        scratch_shapes=[pltpu.VMEM((N_EXPERTS, 1), F32)],
        compiler_params=_cparams("arbitrary"),
        name="moe_router",
    )(x, w_router.T, bias.reshape(N_EXPERTS, 1))
    return idx, w, rank, counts.reshape(n_tiles, N_EXPERTS).astype(jnp.int32)


def _steps_per_tile(tile_tokens):
    return tile_tokens * TOP_K // STEP_ROWS + N_EXPERTS


def _plan_kernel(fill_lo_ref, fill_hi_ref, pos_ref, src_ref, *, tile_tokens, part):
    i = pl.program_id(0)
    q = pl.program_id(1)
    empty = tile_tokens * TOP_K
    group = 16

    @pl.when(q == 0)
    def _():
        def fill_range(g, carry):
            lo = fill_lo_ref[i * N_EXPERTS + g]
            hi = fill_hi_ref[i * N_EXPERTS + g]

            def fill(b, carry):
                for d in range(group):
                    src_ref[0, 0, jnp.maximum(hi - 1 - b * group - d, 0)] = empty
                return carry
            return lax.fori_loop(0, (hi - lo + group - 1) // group, fill, carry)
        lax.fori_loop(0, N_EXPERTS, fill_range, 0)

    def place(b, carry):
        base = b * group
        slots = [pos_ref[0, 0, base + d] for d in range(group)]
        for d in range(group):
            src_ref[0, 0, slots[d]] = q * part + base + d
        return carry
    lax.fori_loop(0, part // group, place, 0)


def _dispatch_plan(idx_t, w_t, rank_t, counts, tile_tokens):
    t = idx_t.shape[1]
    n_tiles = t // tile_tokens
    steps = _steps_per_tile(tile_tokens)
    padded = (counts + STEP_ROWS - 1) // STEP_ROWS * STEP_ROWS
    pad_end = jnp.cumsum(padded, axis=1)
    pad_start = (pad_end - padded).astype(jnp.int32)
    n_used = (pad_end[:, -1] // STEP_ROWS).astype(jnp.int32)
    starts = jnp.arange(steps, dtype=jnp.int32) * STEP_ROWS
    step_e = jnp.sum(starts[None, :, None] >= pad_end[:, None, :], axis=-1).astype(jnp.int32)
    step_e = jnp.minimum(step_e, N_EXPERTS - 1)
    last = jnp.take_along_axis(step_e, jnp.maximum(n_used - 1, 0)[:, None], axis=1)
    step_e = jnp.where(starts[None, :] // STEP_ROWS < n_used[:, None], step_e, last)

    start_of = jnp.repeat(pad_start.T, tile_tokens, axis=1)
    experts = jnp.arange(N_EXPERTS, dtype=jnp.int32)[None, :, None]
    pos_t = rank_t + jnp.sum(jnp.where(idx_t[:, None, :] == experts, start_of[None], 0), axis=1)
    fill_lo = (pad_start + counts).astype(jnp.int32)
    fill_hi = pad_end.astype(jnp.int32)

    parts = PLAN_PARTS
    part = tile_tokens * TOP_K // parts
    n_slots = steps * STEP_ROWS
    flat = lambda a: a.T.reshape(n_tiles * parts, 1, part)
    src = pl.pallas_call(
        functools.partial(_plan_kernel, tile_tokens=tile_tokens, part=part),
        grid_spec=pltpu.PrefetchScalarGridSpec(
            num_scalar_prefetch=2,
            grid=(n_tiles, parts),
            in_specs=[pl.BlockSpec((1, 1, part), lambda i, q, lo, hi: (i * parts + q, 0, 0),
                                   memory_space=pltpu.SMEM)],
            out_specs=pl.BlockSpec((1, 1, n_slots), lambda i, q, lo, hi: (i, 0, 0),
                                   memory_space=pltpu.SMEM, pipeline_mode=pl.Buffered(1)),
        ),
        out_shape=jax.ShapeDtypeStruct((n_tiles, 1, n_slots), jnp.int32),
        compiler_params=_cparams("parallel", "arbitrary"),
        name="moe_plan",
    )(fill_lo.reshape(-1), fill_hi.reshape(-1), flat(pos_t))
    src = jnp.where(jnp.arange(n_slots, dtype=jnp.int32)[None, None, :] < pad_end[:, -1][:, None, None],
                    src, tile_tokens * TOP_K)
    rows = src & -SUBLANES
    w_flat = jnp.pad(w_t.T.reshape(n_tiles, tile_tokens * TOP_K), ((0, 0), (0, SUBLANES)))
    slot_w = jnp.take_along_axis(w_flat, src[:, 0, :], axis=1)
    held = jnp.clip(jnp.take_along_axis(fill_lo, step_e, axis=1) - starts[None, :], 0, STEP_ROWS)
    live = jnp.where(starts[None, :] // STEP_ROWS < n_used[:, None], (held + EXPERT_ROWS - 1) // EXPERT_ROWS, 0)
    return rows, slot_w, step_e.reshape(-1), n_used, live.reshape(-1).astype(jnp.int32), steps


def _experts_kernel(step_e_ref, n_used_ref, live_ref, row_ref, sw_ref, x_ref, wg_ref, wu_ref, wd_ref, acc_ref,
                    *stage_refs):
    n_sub = BLOCKS_PER_STEP
    xs_refs = stage_refs[:n_sub]
    ys_refs = stage_refs[n_sub:]
    i = pl.program_id(0)
    j = pl.program_id(1)
    rows = EXPERT_ROWS
    pitch = STAGE_PITCH

    @pl.when(j == 0)
    def _():
        acc_ref[...] = jnp.zeros_like(acc_ref)

    last_row = x_ref.shape[1] - SUBLANES

    def token_rows(slot, limit=None):
        row = row_ref[0, 0, slot]
        if limit is not None:
            row = jnp.minimum(row, limit)
        return pl.ds(pl.multiple_of(row, SUBLANES), SUBLANES)

    def run(subs):
        for s in subs:
            for r in range(rows):
                xs_refs[s][pl.ds(r, ROW_CHUNKS, stride=pitch), :] = x_ref[0, token_rows(s * rows + r, last_row), :]
        eye = (lax.broadcasted_iota(jnp.int32, (rows, rows), 0)
               == lax.broadcasted_iota(jnp.int32, (rows, rows), 1))
        ys = {}
        for s in subs:
            x = jnp.concatenate([xs_refs[s][pl.ds(c * pitch, rows), :] for c in range(ROW_CHUNKS)], axis=1)
            hid = _silu(_bdot(x, wg_ref[0])) * _bdot(x, wu_ref[0])
            w_row = sw_ref[0, :, s * rows:(s + 1) * rows]
            w_col = jnp.sum(jnp.where(eye, w_row, 0.0), axis=1, keepdims=True)
            ys[s] = _bdot(hid, wd_ref[0]) * w_col
        for s in subs:
            for c in range(ROW_CHUNKS):
                ys_refs[s][pl.ds(c * pitch, rows), :] = ys[s][:, c * LANES:(c + 1) * LANES]
        for s in subs:
            for r0 in range(0, rows, SCATTER_BATCH):
                new = [acc_ref[0, token_rows(s * rows + r), :] + ys_refs[s][pl.ds(r, ROW_CHUNKS, stride=pitch), :]
                       for r in range(r0, r0 + SCATTER_BATCH)]
                for r, val in zip(range(r0, r0 + SCATTER_BATCH), new):
                    acc_ref[0, token_rows(s * rows + r), :] = val

    live = live_ref[i * pl.num_programs(1) + j]
    for n_live in range(1, n_sub + 1):
        pl.when(live == n_live)(functools.partial(run, range(n_live)))


def _routed_experts(x_slabs, idx_t, w_t, rank_t, counts, w_gate, w_up, w_down, tile_tokens):
    d = D_MODEL
    t = x_slabs.shape[0] // ROW_CHUNKS
    n_tiles = t // tile_tokens
    rows, slot_w, step_e, n_used, live, steps = _dispatch_plan(idx_t, w_t, rank_t, counts, tile_tokens)
    rows = rows.reshape(n_tiles * steps, 1, STEP_ROWS)
    slot_w = slot_w.reshape(n_tiles * steps, 1, STEP_ROWS)
    x_slabs = x_slabs.reshape(n_tiles, tile_tokens * ROW_CHUNKS, LANES)
    slab_rows = (tile_tokens + 1) * ROW_CHUNKS

    def used_step(i, j, se, nu, lv):
        return (i * steps + jnp.minimum(j, jnp.maximum(nu[i] - 1, 0)), 0, 0)

    pick = lambda i, j, se, nu, lv: (se[i * steps + j], 0, 0)
    tile = lambda i, j, se, nu, lv: (i, 0, 0)
    resident = pl.Buffered(1)
    stage = pltpu.VMEM((ROW_CHUNKS * STAGE_PITCH, LANES), F32)
    grid_spec = pltpu.PrefetchScalarGridSpec(
        num_scalar_prefetch=3,
        grid=(n_tiles, steps),
        in_specs=[
            pl.BlockSpec((1, 1, STEP_ROWS), used_step, memory_space=pltpu.SMEM),
            pl.BlockSpec((1, 1, STEP_ROWS), used_step),
            pl.BlockSpec((1, tile_tokens * ROW_CHUNKS, LANES), tile, pipeline_mode=resident),
            pl.BlockSpec((1, d, D_EXPERT), pick),
            pl.BlockSpec((1, d, D_EXPERT), pick),
            pl.BlockSpec((1, D_EXPERT, d), pick),
        ],
        out_specs=pl.BlockSpec((1, slab_rows, LANES), tile, pipeline_mode=resident),
        scratch_shapes=[stage] * (2 * BLOCKS_PER_STEP),
    )
    return pl.pallas_call(
        _experts_kernel,
        grid_spec=grid_spec,
        out_shape=jax.ShapeDtypeStruct((n_tiles, slab_rows, LANES), F32),
        compiler_params=_cparams("parallel", "arbitrary"),
        name="moe_experts",
    )(step_e, n_used, live, rows, slot_w, x_slabs, w_gate, w_up, w_down)


def _moe_out_kernel(x_ref, r_ref, wg_ref, wu_ref, wd_ref, g_ref, b_ref, o_ref):
    x = x_ref[...]
    tm = x.shape[0]
    hid = _silu(_bdot(x, wg_ref[...])) * _bdot(x, wu_ref[...])
    routed = jnp.concatenate([r_ref[0, pl.ds(c, tm, stride=ROW_CHUNKS), :] for c in range(ROW_CHUNKS)], axis=1)
    y = routed + _bdot(hid, wd_ref[...])
    o_ref[...] = _layer_norm(DN_ALPHA * x + y, g_ref[...], b_ref[...])


def _moe_out(x, routed, ws_gate, ws_up, ws_down, g, b, tile_tokens, tm=512):
    t, d = x.shape
    tm = min(tm, tile_tokens)
    per_tile = tile_tokens // tm
    return pl.pallas_call(
        _moe_out_kernel,
        grid=(t // tm,),
        in_specs=[pl.BlockSpec((tm, d), lambda i: (i, 0)),
                  pl.BlockSpec((1, tm * ROW_CHUNKS, LANES), lambda i: (i // per_tile, i % per_tile, 0)),
                  pl.BlockSpec((d, D_SHARED), lambda i: (0, 0)),
                  pl.BlockSpec((d, D_SHARED), lambda i: (0, 0)),
                  pl.BlockSpec((D_SHARED, d), lambda i: (0, 0)),
                  pl.BlockSpec((1, d), lambda i: (0, 0)),
                  pl.BlockSpec((1, d), lambda i: (0, 0))],
        out_specs=pl.BlockSpec((tm, d), lambda i: (i, 0)),
        out_shape=jax.ShapeDtypeStruct((t, d), F32),
        compiler_params=_cparams("parallel"),
        name="moe_out",
    )(x, routed, ws_gate.astype(BF16), ws_up.astype(BF16), ws_down.astype(BF16),
      g.reshape(1, d), b.reshape(1, d))


def _moe_ln(hid, hid_slabs, w_router, router_bias, w_gate, w_up, w_down, ws_gate, ws_up, ws_down, ln_g, ln_b):
    b, l, d = hid.shape
    x = hid.reshape(b * l, d)
    tile_tokens = min(MOE_TILE_TOKENS, b * l)
    idx_t, w_t, rank_t, counts = _router(x, w_router, router_bias, tile_tokens)
    routed = _routed_experts(hid_slabs, idx_t, w_t, rank_t, counts, w_gate.astype(BF16), w_up.astype(BF16),
                             w_down.astype(BF16), tile_tokens)
    return _moe_out(x, routed, ws_gate, ws_up, ws_down, ln_g, ln_b, tile_tokens).reshape(b, l, d)


def kernel(x, a_w_in, a_conv_w, a_A_log, a_dt_bias, a_onorm_w, a_w_out, b_w_dq, b_qnorm_w, b_w_uq, b_w_o,
           kv_w_dkv, kv_norm_w, kv_w_ukv, ln1_g, ln1_b, ln2_g, ln2_b, moe_w_router, moe_router_bias,
           moe_w_gate, moe_w_up, moe_w_down, moe_ws_gate, moe_ws_up, moe_ws_down):
    l = x.shape[1]
    cos_t, sin_t = _rope_lane_tables(l)
    h = x
    k = v = None
    for layer in range(DEPTH):
        if layer < N_A_LAYERS:
            i = layer
            h, slabs = _gated_deltanet_ln(h, a_w_in[i], a_conv_w[i], a_A_log[i], a_dt_bias[i], a_onorm_w[i],
                                          a_w_out[i], ln1_g[layer], ln1_b[layer])
        else:
            i = layer - N_A_LAYERS
            h, slabs = _mla_ln(h, b_w_dq[i], b_qnorm_w[i], b_w_uq[i], b_w_o[i], k, v, cos_t, sin_t,
                               ln1_g[layer], ln1_b[layer])
        h = _moe_ln(h, slabs, moe_w_router[layer], moe_router_bias[layer], moe_w_gate[layer], moe_w_up[layer],
                    moe_w_down[layer], moe_ws_gate[layer], moe_ws_up[layer], moe_ws_down[layer],
                    ln2_g[layer], ln2_b[layer])
        if layer == N_A_LAYERS - 1:
            k, v = _mla_kv(h, kv_w_dkv, kv_norm_w, kv_w_ukv, cos_t, sin_t)
    return h
```

```python
import functools

import jax
import jax.numpy as jnp
import numpy as np
from jax import lax
from jax.experimental import pallas as pl
from jax.experimental.pallas import tpu as pltpu

F32 = jnp.float32
BF16 = jnp.bfloat16
HIGHEST = lax.Precision.HIGHEST

D_MODEL = 1024
DEPTH = 4
N_A_LAYERS = DEPTH // 2
GDN_HEADS = 8
GDN_DK = 128
GDN_DV = 128
D_QK = GDN_HEADS * GDN_DK
D_VA = GDN_HEADS * GDN_DV
GDN_CONV_CH = 2 * D_QK + D_VA
CONV_WIDTH = 4
CHUNK = 64
MLA_HEADS = 8
Q_LORA = 512
KV_LORA = 256
D_NOPE = 128
D_ROPE = 64
D_V = 128
ROPE_THETA = 10000.0
N_EXPERTS = 64
TOP_K = 8
N_GROUPS = 8
GROUP_SIZE = N_EXPERTS // N_GROUPS
TOPK_GROUPS = 4
D_EXPERT = 256
D_SHARED = 256
ROUTED_SCALE = 2.5
DN_ALPHA = (2 * DEPTH) ** 0.25
EPS = 1e-6
LN_EPS = 1e-5
LOG2_E = 1.4426950408889634

LANES = 128
SUBLANES = 8
VMEM_LIMIT_BYTES = 56 * 1024 * 1024

EXPERT_ROWS = 128
ROW_CHUNKS = D_MODEL // LANES
STAGE_PITCH = EXPERT_ROWS + 1
SCATTER_BATCH = 16
BLOCKS_PER_STEP = 2
STEP_ROWS = BLOCKS_PER_STEP * EXPERT_ROWS
MOE_TILE_TOKENS = 4096
PLAN_PARTS = 4
ATTN_ROW_GROUPS = 4
assert TOP_K == SUBLANES


def _cparams(*sem):
    return pltpu.CompilerParams(dimension_semantics=sem, vmem_limit_bytes=VMEM_LIMIT_BYTES)


def _silu(x):
    return x * jax.nn.sigmoid(x)


def _layer_norm(x, g, b):
    mu = jnp.mean(x, -1, keepdims=True)
    xc = x - mu
    var = jnp.mean(xc * xc, -1, keepdims=True)
    return xc * lax.rsqrt(var + LN_EPS) * g + b


def _rms_norm(x, w):
    return x * lax.rsqrt(jnp.mean(x * x, -1, keepdims=True) + EPS) * w


def _bdot(a, b):
    return jnp.dot(a.astype(BF16), b.astype(BF16), preferred_element_type=F32)


def _bdot_nt(a, b):
    return lax.dot_general(a.astype(BF16), b.astype(BF16), (((1,), (1,)), ((), ())),
                           preferred_element_type=F32)


def _hdot(a, b):
    return jnp.dot(a, b, precision=HIGHEST, preferred_element_type=F32)


def _matmul_kernel(x_ref, w_ref, o_ref):
    o_ref[...] = _bdot(x_ref[...], w_ref[...]).astype(o_ref.dtype)


def _matmul(x, w, out_dtype, tm, tn):
    m, k = x.shape
    n = w.shape[1]
    tm = min(tm, m)
    tn = min(tn, n)
    return pl.pallas_call(
        _matmul_kernel,
        grid=(m // tm, n // tn),
        in_specs=[pl.BlockSpec((tm, k), lambda i, j: (i, 0)),
                  pl.BlockSpec((k, tn), lambda i, j: (0, j))],
        out_specs=pl.BlockSpec((tm, tn), lambda i, j: (i, j)),
        out_shape=jax.ShapeDtypeStruct((m, n), out_dtype),
        compiler_params=_cparams("parallel", "parallel"),
        name="matmul",
    )(x, w)


def _matmul_ln_kernel(x_ref, w_ref, r_ref, g_ref, b_ref, o_ref, slab_ref):
    mix = _bdot(x_ref[...], w_ref[...])
    out = _layer_norm(DN_ALPHA * r_ref[...] + mix, g_ref[...], b_ref[...])
    o_ref[...] = out
    tm = out.shape[0]
    for c in range(ROW_CHUNKS):
        slab_ref[pl.ds(c, tm, stride=ROW_CHUNKS), :] = out[:, c * LANES:(c + 1) * LANES]


def _matmul_ln(x, w, res, g, b, tm=512):
    m, k = x.shape
    n = w.shape[1]
    tm = min(tm, m)
    chunks = n // LANES
    return pl.pallas_call(
        _matmul_ln_kernel,
        grid=(m // tm,),
        in_specs=[pl.BlockSpec((tm, k), lambda i: (i, 0)),
                  pl.BlockSpec((k, n), lambda i: (0, 0)),
                  pl.BlockSpec((tm, n), lambda i: (i, 0)),
                  pl.BlockSpec((1, n), lambda i: (0, 0)),
                  pl.BlockSpec((1, n), lambda i: (0, 0))],
        out_specs=[pl.BlockSpec((tm, n), lambda i: (i, 0)),
                   pl.BlockSpec((tm * chunks, LANES), lambda i: (i, 0))],
        out_shape=[jax.ShapeDtypeStruct((m, n), F32),
                   jax.ShapeDtypeStruct((m * chunks, LANES), F32)],
        compiler_params=_cparams("parallel"),
        name="matmul_ln",
    )(x, w, res, g.reshape(1, n), b.reshape(1, n))


def _softplus(x):
    return jnp.maximum(x, 0.0) + jnp.log1p(jnp.exp(-jnp.abs(x)))


def _gdn_gates_kernel(x_ref, wab_ref, wabt_ref, alog_ref, dtb_ref, alogt_ref, dtbt_ref,
                      gc_ref, beta_ref, gct_ref):
    x = x_ref[...]
    tl = x.shape[0]
    h = GDN_HEADS
    ab = _bdot(x, wab_ref[...])
    abt = _bdot_nt(wabt_ref[...], x)
    g = -jnp.exp(alog_ref[...]) * _softplus(ab[:, :h] + dtb_ref[...])
    gt = -jnp.exp(alogt_ref[...]) * _softplus(abt[:h, :] + dtbt_ref[...])
    beta_ref[...] = jax.nn.sigmoid(ab[:, h:2 * h])
    row = lax.broadcasted_iota(jnp.int32, (tl, tl), 0)
    col = lax.broadcasted_iota(jnp.int32, (tl, tl), 1)
    shift = CHUNK.bit_length() - 1
    same = (row >> shift) == (col >> shift)
    lower = jnp.where(same & (col <= row), 1.0, 0.0).astype(F32)
    upper = jnp.where(same & (row <= col), 1.0, 0.0).astype(F32)
    gc_ref[...] = _hdot(lower, g)
    gct_ref[...] = _hdot(gt, upper)


def _gdn_gates(x, w_ab, a_log, dt_bias, tl=512):
    t = x.shape[0]
    tl = min(tl, t)
    h = GDN_HEADS
    wab = jnp.zeros((D_MODEL, LANES), F32).at[:, :2 * h].set(w_ab)
    wabt = w_ab.T
    return pl.pallas_call(
        _gdn_gates_kernel,
        grid=(t // tl,),
        in_specs=[pl.BlockSpec((tl, D_MODEL), lambda i: (i, 0)),
                  pl.BlockSpec((D_MODEL, LANES), lambda i: (0, 0)),
                  pl.BlockSpec((2 * h, D_MODEL), lambda i: (0, 0)),
                  pl.BlockSpec((1, h), lambda i: (0, 0)),
                  pl.BlockSpec((1, h), lambda i: (0, 0)),
                  pl.BlockSpec((h, 1), lambda i: (0, 0)),
                  pl.BlockSpec((h, 1), lambda i: (0, 0))],
        out_specs=[pl.BlockSpec((tl, h), lambda i: (i, 0)),
                   pl.BlockSpec((tl, h), lambda i: (i, 0)),
                   pl.BlockSpec((h, tl), lambda i: (0, i))],
        out_shape=[jax.ShapeDtypeStruct((t, h), F32),
                   jax.ShapeDtypeStruct((t, h), F32),
                   jax.ShapeDtypeStruct((h, t), F32)],
        compiler_params=_cparams("parallel"),
        name="gdn_gates",
    )(x, wab, wabt, a_log.reshape(1, h), dt_bias.reshape(1, h),
      a_log.reshape(h, 1), dt_bias.reshape(h, 1))


def _gdn_conv_kernel(cur_ref, prev_ref, w_ref, q_ref, k_ref, v_ref):
    i = pl.program_id(1)
    x = cur_ref[0]
    tl = x.shape[0]
    prev = jnp.where(i > 0, prev_ref[0], 0.0)
    xx = jnp.concatenate([prev, x], axis=0)
    w = w_ref[...]
    y = None
    for j in range(CONV_WIDTH):
        off = SUBLANES - (CONV_WIDTH - 1) + j
        term = xx[off:off + tl, :] * w[j:j + 1, :]
        y = term if y is None else y + term
    y = _silu(y)
    for h in range(GDN_HEADS):
        sl = slice(h * GDN_DK, (h + 1) * GDN_DK)
        qh = y[:, sl]
        q_ref[0, :, sl] = qh * lax.rsqrt(jnp.sum(qh * qh, -1, keepdims=True) + EPS) * (GDN_DK ** -0.5)
        kh = y[:, D_QK + h * GDN_DK:D_QK + (h + 1) * GDN_DK]
        k_ref[0, :, sl] = kh * lax.rsqrt(jnp.sum(kh * kh, -1, keepdims=True) + EPS)
    v_ref[0] = y[:, 2 * D_QK:]


def _gdn_conv(proj, conv_w, tl=256):
    b, l, _ = proj.shape
    tl = min(tl, l)
    c3 = GDN_CONV_CH
    per = tl // SUBLANES
    out = jax.ShapeDtypeStruct((b, l, D_QK), F32)
    return pl.pallas_call(
        _gdn_conv_kernel,
        grid=(b, l // tl),
        in_specs=[pl.BlockSpec((1, tl, c3), lambda bi, i: (bi, i, 0)),
                  pl.BlockSpec((1, SUBLANES, c3), lambda bi, i: (bi, jnp.maximum(i * per - 1, 0), 0)),
                  pl.BlockSpec((CONV_WIDTH, c3), lambda bi, i: (0, 0))],
        out_specs=[pl.BlockSpec((1, tl, D_QK), lambda bi, i: (bi, i, 0))] * 3,
        out_shape=[out, out, out],
        compiler_params=_cparams("parallel", "parallel"),
        name="gdn_conv",
    )(proj, proj, conv_w)


def _split_bf16(x):
    hi = x.astype(BF16)
    lo = (x - hi.astype(F32)).astype(BF16)
    return hi, lo


def _dot_split(a, b):
    (ah, al), (bh, bl) = a, b
    dot = functools.partial(jnp.dot, preferred_element_type=F32)
    return dot(ah, bh) + (dot(ah, bl) + dot(al, bh))


def _gdn_prep_kernel(q_ref, k_ref, v_ref, gc_ref, beta_ref, gct_ref, wq_ref, u_ref, attn_ref, kgt_ref,
                     *, chunks):
    c = CHUNK
    row = lax.broadcasted_iota(jnp.int32, (c, c), 0)
    col = lax.broadcasted_iota(jnp.int32, (c, c), 1)
    incl = row >= col
    strict = row > col
    eye = jnp.where(row == col, 1.0, 0.0).astype(F32)
    items = [(ci, h) for ci in range(chunks) for h in range(GDN_HEADS)]

    st = {}
    for ci, h in items:
        rows = slice(ci * c, (ci + 1) * c)
        sl = slice(h * GDN_DK, (h + 1) * GDN_DK)
        q = q_ref[0, rows, sl]
        k = k_ref[0, rows, sl]
        v = v_ref[0, rows, sl]
        gcol = gc_ref[0, rows, h:h + 1]
        grow = gct_ref[0, ci, h:h + 1, :]
        bcol = beta_ref[0, rows, h:h + 1]
        decay = jnp.where(incl, jnp.exp(jnp.where(incl, gcol - grow, 0.0)), 0.0)
        kb = k * bcol
        eg = jnp.exp(gcol)
        g_last = gcol[c - 1:c, :]
        sc = _bdot_nt(jnp.concatenate([kb, q], axis=0), k)
        a = jnp.where(strict, sc[:c] * decay, 0.0)
        attn_ref[0, ci, h] = (sc[c:] * decay).astype(attn_ref.dtype)
        kgt_ref[0, ci, h] = (k * jnp.exp(g_last - gcol)).T.astype(kgt_ref.dtype)
        wq_ref[0, ci, h, c:, :] = (q * eg).astype(wq_ref.dtype)
        st[ci, h] = dict(m=-a, rhs=jnp.concatenate([v * bcol, kb * eg], axis=1))

    for it in items:
        st[it]["p"] = eye + st[it]["m"]
    span = 2
    while span < c:
        for it in items:
            ms = _split_bf16(st[it]["m"])
            st[it]["m"] = _dot_split(ms, ms)
        for it in items:
            st[it]["p"] = st[it]["p"] + _dot_split(_split_bf16(st[it]["p"]), _split_bf16(st[it]["m"]))
        span *= 2

    for ci, h in items:
        sol = _dot_split(_split_bf16(st[ci, h]["p"]), _split_bf16(st[ci, h]["rhs"]))
        u_ref[0, ci * c:(ci + 1) * c, h * GDN_DV:(h + 1) * GDN_DV] = sol[:, :GDN_DV]
        wq_ref[0, ci, h, :c, :] = sol[:, GDN_DV:].astype(wq_ref.dtype)


def _gdn_prep(q, k, v, gc, beta, gct, chunks=2):
    b, l, _ = q.shape
    n = l // CHUNK
    chunks = min(chunks, n)
    h = GDN_HEADS
    rows = chunks * CHUNK
    blk = pl.BlockSpec((1, rows, D_QK), lambda bi, i: (bi, i, 0))
    gate = pl.BlockSpec((1, rows, h), lambda bi, i: (bi, i, 0))
    return pl.pallas_call(
        functools.partial(_gdn_prep_kernel, chunks=chunks),
        grid=(b, n // chunks),
        in_specs=[blk, blk, blk, gate, gate,
                  pl.BlockSpec((1, chunks, h, CHUNK), lambda bi, i: (bi, i, 0, 0))],
        out_specs=[pl.BlockSpec((1, chunks, h, 2 * CHUNK, GDN_DK), lambda bi, i: (bi, i, 0, 0, 0)),
                   pl.BlockSpec((1, rows, D_VA), lambda bi, i: (bi, i, 0)),
                   pl.BlockSpec((1, chunks, h, CHUNK, CHUNK), lambda bi, i: (bi, i, 0, 0, 0)),
                   pl.BlockSpec((1, chunks, h, GDN_DK, CHUNK), lambda bi, i: (bi, i, 0, 0, 0))],
        out_shape=[jax.ShapeDtypeStruct((b, n, h, 2 * CHUNK, GDN_DK), BF16),
                   jax.ShapeDtypeStruct((b, l, D_VA), F32),
                   jax.ShapeDtypeStruct((b, n, h, CHUNK, CHUNK), BF16),
                   jax.ShapeDtypeStruct((b, n, h, GDN_DK, CHUNK), BF16)],
        compiler_params=_cparams("parallel", "parallel"),
        name="gdn_prep",
    )(q, k, v, gc, beta, gct)


def _gdn_state_kernel(wq_ref, u_ref, attn_ref, kgt_ref, gc_ref, z_ref, onw_ref, o_ref, s_ref, *, chunks):
    @pl.when(pl.program_id(1) == 0)
    def _():
        s_ref[...] = jnp.zeros_like(s_ref)

    c = CHUNK
    onw = onw_ref[...]
    dot = functools.partial(jnp.dot, preferred_element_type=F32)
    heads = range(GDN_HEADS)
    for ci in range(chunks):
        rows = slice(ci * c, (ci + 1) * c)
        s_old = [s_ref[h] for h in heads]
        r = [dot(wq_ref[0, ci, h], s_old[h].astype(BF16)) for h in heads]
        v_new = [(u_ref[0, rows, h * GDN_DV:(h + 1) * GDN_DV] - r[h][:c]).astype(BF16) for h in heads]
        for h in heads:
            decay_last = jnp.exp(gc_ref[0, (ci + 1) * c - 1:(ci + 1) * c, h:h + 1])
            s_ref[h] = s_old[h] * decay_last + dot(kgt_ref[0, ci, h], v_new[h])
        for h in heads:
            sl = slice(h * GDN_DV, (h + 1) * GDN_DV)
            o = r[h][c:] + dot(attn_ref[0, ci, h], v_new[h])
            o_ref[0, rows, sl] = (_rms_norm(o, onw) * _silu(z_ref[0, rows, sl])).astype(o_ref.dtype)


def _gdn_state(wq, u, attn, kgt, gc, proj, onorm_w, chunks=4):
    b, n, h = wq.shape[:3]
    l = n * CHUNK
    chunks = min(chunks, n)
    rows = chunks * CHUNK
    z_block = GDN_CONV_CH // D_VA
    return pl.pallas_call(
        functools.partial(_gdn_state_kernel, chunks=chunks),
        grid=(b, n // chunks),
        in_specs=[pl.BlockSpec((1, chunks, h, 2 * CHUNK, GDN_DK), lambda bi, i: (bi, i, 0, 0, 0)),
                  pl.BlockSpec((1, rows, D_VA), lambda bi, i: (bi, i, 0)),
                  pl.BlockSpec((1, chunks, h, CHUNK, CHUNK), lambda bi, i: (bi, i, 0, 0, 0)),
                  pl.BlockSpec((1, chunks, h, GDN_DK, CHUNK), lambda bi, i: (bi, i, 0, 0, 0)),
                  pl.BlockSpec((1, rows, h), lambda bi, i: (bi, i, 0)),
                  pl.BlockSpec((1, rows, D_VA), lambda bi, i: (bi, i, z_block)),
                  pl.BlockSpec((1, GDN_DV), lambda bi, i: (0, 0))],
        out_specs=pl.BlockSpec((1, rows, D_VA), lambda bi, i: (bi, i, 0)),
        out_shape=jax.ShapeDtypeStruct((b, l, D_VA), BF16),
        scratch_shapes=[pltpu.VMEM((h, GDN_DK, GDN_DV), F32)],
        compiler_params=_cparams("parallel", "arbitrary"),
        name="gdn_state",
    )(wq, u, attn, kgt, gc, proj, onorm_w.reshape(1, GDN_DV))


def _gated_deltanet_ln(hid, w_in, conv_w, a_log, dt_bias, onorm_w, w_out, ln_g, ln_b):
    b, l, d = hid.shape
    h = GDN_HEADS
    n = l // CHUNK
    x = hid.reshape(b * l, d)
    n_main = GDN_CONV_CH + D_VA
    proj = _matmul(x, w_in[:, :n_main].astype(BF16), F32, 512, 1024).reshape(b, l, n_main)
    gc, beta, gct = _gdn_gates(x, w_in[:, n_main:], a_log, dt_bias)
    gc = gc.reshape(b, l, h)
    gct = gct.reshape(h, b, n, CHUNK).transpose(1, 2, 0, 3)
    q, k, v = _gdn_conv(proj, conv_w)
    wq, u, attn, kgt = _gdn_prep(q, k, v, gc, beta.reshape(b, l, h), gct)
    o = _gdn_state(wq, u, attn, kgt, gc, proj, onorm_w)
    out, slabs = _matmul_ln(o.reshape(b * l, D_VA), w_out.astype(BF16), x, ln_g, ln_b)
    return out.reshape(b, l, d), slabs


def _rope_lane_tables(l):
    inv = 1.0 / (ROPE_THETA ** (jnp.arange(0, D_ROPE, 2, dtype=F32) / D_ROPE))
    ang = jnp.arange(l, dtype=F32)[:, None] * inv[None, :]
    cos, sin = jnp.cos(ang), jnp.sin(ang)
    zero = jnp.zeros((l, LANES - D_ROPE), F32)
    return (jnp.concatenate([cos, cos, zero], -1),
            jnp.concatenate([-sin, sin, zero], -1))


def _rope_weight_groups(w_rope):
    kdim = w_rope.shape[0]
    half = D_ROPE // 2
    zero = jnp.zeros((kdim, LANES - D_ROPE), w_rope.dtype)
    x1, x2 = w_rope[:, :half], w_rope[:, half:]
    return jnp.concatenate([x1, x2, zero, x2, x1, zero], axis=-1)


def _mla_kv_kernel(x_ref, wd_ref, nw_ref, wu_ref, cos_ref, sin_ref, k_ref, v_ref):
    x = x_ref[0]
    ckv = _bdot(x, wd_ref[...])
    c = _rms_norm(ckv[:, :KV_LORA], nw_ref[...])
    k_rope_t = (ckv[:, KV_LORA:KV_LORA + LANES] * cos_ref[...]
                + ckv[:, KV_LORA + LANES:] * sin_ref[...]).T.astype(k_ref.dtype)
    kv = _bdot(c, wu_ref[...])
    per = D_NOPE + D_V
    ones_col = jnp.where(lax.broadcasted_iota(jnp.int32, (x.shape[0], LANES), 1) == 0, 1.0, 0.0).astype(v_ref.dtype)
    for h in range(MLA_HEADS):
        k_ref[0, h, :D_NOPE, :] = kv[:, h * per:h * per + D_NOPE].T.astype(k_ref.dtype)
        k_ref[0, h, D_NOPE:, :] = k_rope_t
        v_ref[0, h, :, :D_V] = kv[:, h * per + D_NOPE:(h + 1) * per].astype(v_ref.dtype)
        v_ref[0, h, :, D_V:] = ones_col


def _mla_kv(hid, w_dkv, kv_norm_w, w_ukv, cos_t, sin_t, tl=512):
    b, l, d = hid.shape
    tl = min(tl, l)
    wd = jnp.concatenate([w_dkv[:, :KV_LORA], _rope_weight_groups(w_dkv[:, KV_LORA:])], -1).astype(BF16)
    nd = wd.shape[1]
    hh = MLA_HEADS
    return pl.pallas_call(
        _mla_kv_kernel,
        grid=(b, l // tl),
        in_specs=[pl.BlockSpec((1, tl, d), lambda bi, i: (bi, i, 0)),
                  pl.BlockSpec((d, nd), lambda bi, i: (0, 0)),
                  pl.BlockSpec((1, KV_LORA), lambda bi, i: (0, 0)),
                  pl.BlockSpec((KV_LORA, hh * (D_NOPE + D_V)), lambda bi, i: (0, 0)),
                  pl.BlockSpec((tl, LANES), lambda bi, i: (i, 0)),
                  pl.BlockSpec((tl, LANES), lambda bi, i: (i, 0))],
        out_specs=[pl.BlockSpec((1, hh, D_NOPE + LANES, tl), lambda bi, i: (bi, 0, 0, i)),
                   pl.BlockSpec((1, hh, tl, D_V + LANES), lambda bi, i: (bi, 0, i, 0))],
        out_shape=[jax.ShapeDtypeStruct((b, hh, D_NOPE + LANES, l), BF16),
                   jax.ShapeDtypeStruct((b, hh, l, D_V + LANES), BF16)],
        compiler_params=_cparams("parallel", "parallel"),
        name="mla_kv",
    )(hid, wd, kv_norm_w.reshape(1, KV_LORA), w_ukv.astype(BF16), cos_t, sin_t)


def _mla_cq_kernel(x_ref, w_ref, nw_ref, o_ref):
    o_ref[...] = _rms_norm(_bdot(x_ref[...], w_ref[...]), nw_ref[...]).astype(o_ref.dtype)


def _mla_cq(x, w_dq, qnorm_w, tm=512):
    m, k = x.shape
    tm = min(tm, m)
    return pl.pallas_call(
        _mla_cq_kernel,
        grid=(m // tm,),
        in_specs=[pl.BlockSpec((tm, k), lambda i: (i, 0)),
                  pl.BlockSpec((k, Q_LORA), lambda i: (0, 0)),
                  pl.BlockSpec((1, Q_LORA), lambda i: (0, 0))],
        out_specs=pl.BlockSpec((tm, Q_LORA), lambda i: (i, 0)),
        out_shape=jax.ShapeDtypeStruct((m, Q_LORA), BF16),
        compiler_params=_cparams("parallel"),
        name="mla_cq",
    )(x, w_dq.astype(BF16), qnorm_w.reshape(1, Q_LORA))


def _mla_q_kernel(c_ref, w_ref, cos_ref, sin_ref, q_ref):
    c = c_ref[0]
    scale = (D_NOPE + D_ROPE) ** -0.5 * LOG2_E
    per = D_NOPE + 2 * LANES
    for h in range(MLA_HEADS):
        qh = _bdot(c, w_ref[:, h * per:(h + 1) * per])
        rope = qh[:, D_NOPE:D_NOPE + LANES] * cos_ref[...] + qh[:, D_NOPE + LANES:] * sin_ref[...]
        q_ref[0, h, :, :D_NOPE] = (qh[:, :D_NOPE] * scale).astype(q_ref.dtype)
        q_ref[0, h, :, D_NOPE:] = (rope * scale).astype(q_ref.dtype)


def _mla_q(cq, w_uq, cos_t, sin_t, tl=512):
    b, l, _ = cq.shape
    tl = min(tl, l)
    hh = MLA_HEADS
    per_in = D_NOPE + D_ROPE
    groups = []
    for h in range(hh):
        wh = w_uq[:, h * per_in:(h + 1) * per_in]
        groups += [wh[:, :D_NOPE], _rope_weight_groups(wh[:, D_NOPE:])]
    w = jnp.concatenate(groups, -1).astype(BF16)
    return pl.pallas_call(
        _mla_q_kernel,
        grid=(b, l // tl),
        in_specs=[pl.BlockSpec((1, tl, Q_LORA), lambda bi, i: (bi, i, 0)),
                  pl.BlockSpec(w.shape, lambda bi, i: (0, 0)),
                  pl.BlockSpec((tl, LANES), lambda bi, i: (i, 0)),
                  pl.BlockSpec((tl, LANES), lambda bi, i: (i, 0))],
        out_specs=pl.BlockSpec((1, hh, tl, D_NOPE + LANES), lambda bi, i: (bi, 0, i, 0)),
        out_shape=jax.ShapeDtypeStruct((b, hh, l, D_NOPE + LANES), BF16),
        compiler_params=_cparams("parallel", "parallel"),
        name="mla_q",
    )(cq, w, cos_t, sin_t)


def _mla_attn_kernel(q_ref, k_ref, v_ref, o_ref, *, tk):
    qi = pl.program_id(2)
    q = q_ref[0, 0]
    tq = q.shape[0]
    per = tq // tk

    groups = ATTN_ROW_GROUPS
    rows = tq // groups
    qs = [q[g * rows:(g + 1) * rows] for g in range(groups)]

    def block(j, carry, mask_offset=None):
        at = pl.ds(pl.multiple_of(j * tk, tk), tk)
        kb = k_ref[0, 0, :, at]
        vb = v_ref[0, 0, at, :]
        ss = [_bdot(qg, kb) for qg in qs]
        if mask_offset is not None:
            qpos = lax.broadcasted_iota(jnp.int32, (rows, tk), 0)
            kpos = lax.broadcasted_iota(jnp.int32, (rows, tk), 1)
            ss = [jnp.where(kpos + mask_offset <= qpos + g * rows, s, -jnp.inf) for g, s in enumerate(ss)]
        m_new = [jnp.maximum(c[0], jnp.max(s, -1, keepdims=True)) for c, s in zip(carry, ss)]
        ps = [jnp.exp2(s - m) for s, m in zip(ss, m_new)]
        return tuple((mn, jnp.exp2(m - mn) * acc + _bdot(p, vb)) for (m, acc), mn, p in zip(carry, m_new, ps))

    first = qi * per
    init = (jnp.full((rows, 1), -jnp.inf, F32), jnp.zeros((rows, D_V + LANES), F32))
    carry = lax.fori_loop(0, first, block, (init,) * groups)
    for d in range(per):
        carry = block(first + d, carry, mask_offset=d * tk)
    for g, (_, acc) in enumerate(carry):
        o_ref[0, g * rows:(g + 1) * rows, :] = (acc[:, :D_V] / acc[:, D_V:D_V + 1]).astype(o_ref.dtype)


def _mla_attn(q, k, v, tq=1024, tk=1024):
    b, hh, l, dq = q.shape
    tq = min(tq, l)
    tk = min(tk, tq)
    return pl.pallas_call(
        functools.partial(_mla_attn_kernel, tk=tk),
        grid=(b, hh, l // tq),
        in_specs=[pl.BlockSpec((1, 1, tq, dq), lambda bi, h, i: (bi, h, i, 0)),
                  pl.BlockSpec((1, 1, dq, l), lambda bi, h, i: (bi, h, 0, 0)),
                  pl.BlockSpec((1, 1, l, D_V + LANES), lambda bi, h, i: (bi, h, 0, 0))],
        out_specs=pl.BlockSpec((1, tq, D_V), lambda bi, h, i: (bi, i, h)),
        out_shape=jax.ShapeDtypeStruct((b, l, hh * D_V), BF16),
        compiler_params=_cparams("parallel", "parallel", "parallel"),
        name="mla_attn",
    )(q, k, v)


def _mla_ln(hid, w_dq, qnorm_w, w_uq, w_o, k, v, cos_t, sin_t, ln_g, ln_b):
    b, l, d = hid.shape
    x = hid.reshape(b * l, d)
    cq = _mla_cq(x, w_dq, qnorm_w).reshape(b, l, Q_LORA)
    q = _mla_q(cq, w_uq, cos_t, sin_t)
    o = _mla_attn(q, k, v)
    out, slabs = _matmul_ln(o.reshape(b * l, MLA_HEADS * D_V), w_o.astype(BF16), x, ln_g, ln_b)
    return out.reshape(b, l, d), slabs


def _first_argmax(x, ids, n):
    m = jnp.max(x, axis=0, keepdims=True)
    first = jnp.min(jnp.where(x == m, ids, n), axis=0, keepdims=True)
    return m, first


def _router_kernel(x_ref, wt_ref, bias_ref, idx_ref, w_ref, rank_ref, cnt_out_ref, cnt_ref, *, steps_per_tile):
    @pl.when(pl.program_id(0) % steps_per_tile == 0)
    def _():
        cnt_ref[...] = jnp.zeros_like(cnt_ref)

    x = x_ref[...]
    t = x.shape[0]
    logits = lax.dot_general(wt_ref[...], x, (((1,), (1,)), ((), ())),
                             precision=HIGHEST, preferred_element_type=F32)
    scores = jax.nn.sigmoid(logits)
    biased = scores + bias_ref[...]
    neg = -jnp.inf
    sub = lax.broadcasted_iota(jnp.int32, (GROUP_SIZE, t), 0).astype(F32)
    gscores = []
    for g in range(N_GROUPS):
        xg = biased[g * GROUP_SIZE:(g + 1) * GROUP_SIZE, :]
        m1, i1 = _first_argmax(xg, sub, float(GROUP_SIZE))
        m2 = jnp.max(jnp.where(sub == i1, neg, xg), axis=0, keepdims=True)
        gscores.append(m1 + m2)
    gs = jnp.concatenate(gscores, axis=0)
    gid = lax.broadcasted_iota(jnp.int32, (N_GROUPS, t), 0).astype(F32)
    gsel = jnp.zeros((N_GROUPS, t), F32)
    for _ in range(TOPK_GROUPS):
        _, gi = _first_argmax(gs, gid, float(N_GROUPS))
        hit = gid == gi
        gsel = jnp.where(hit, 1.0, gsel)
        gs = jnp.where(hit, neg, gs)
    eid = lax.broadcasted_iota(jnp.int32, (N_EXPERTS, t), 0).astype(F32)
    allowed = jnp.concatenate(
        [jnp.broadcast_to(gsel[g:g + 1, :], (GROUP_SIZE, t)) for g in range(N_GROUPS)], axis=0)
    cand = jnp.where(allowed > 0.0, biased, neg)
    idxs, ws, hits = [], [], []
    for _ in range(TOP_K):
        _, ei = _first_argmax(cand, eid, float(N_EXPERTS))
        hit = eid == ei
        idxs.append(ei)
        hits.append(hit)
        ws.append(jnp.sum(jnp.where(hit, scores, 0.0), axis=0, keepdims=True))
        cand = jnp.where(hit, neg, cand)
    w = jnp.concatenate(ws, axis=0)
    w = w / (jnp.sum(w, axis=0, keepdims=True) + 1e-20) * ROUTED_SCALE
    idx_ref[...] = jnp.concatenate(idxs, axis=0).astype(jnp.int32)
    w_ref[...] = w
    chosen = jnp.zeros((N_EXPERTS, t), F32)
    for hit in hits:
        chosen = jnp.where(hit, 1.0, chosen)
    before = (lax.broadcasted_iota(jnp.int32, (t, t), 0) < lax.broadcasted_iota(jnp.int32, (t, t), 1))
    prior = jnp.dot(chosen.astype(BF16), jnp.where(before, 1.0, 0.0).astype(BF16),
                    preferred_element_type=F32) + cnt_ref[...]
    rank_ref[...] = jnp.concatenate(
        [jnp.sum(jnp.where(hit, prior, 0.0), axis=0, keepdims=True) for hit in hits], axis=0).astype(jnp.int32)
    cnt_ref[...] += jnp.sum(chosen, axis=1, keepdims=True)
    cnt_out_ref[0] = cnt_ref[...]


def _router(x, w_router, bias, tile_tokens, tr=512):
    t, d = x.shape
    tr = min(tr, tile_tokens)
    steps_per_tile = tile_tokens // tr
    n_tiles = t // tile_tokens
    kt = pl.BlockSpec((TOP_K, tr), lambda i: (0, i))
    idx, w, rank, counts = pl.pallas_call(
        functools.partial(_router_kernel, steps_per_tile=steps_per_tile),
        grid=(t // tr,),
        in_specs=[pl.BlockSpec((tr, d), lambda i: (i, 0)),
                  pl.BlockSpec((N_EXPERTS, d), lambda i: (0, 0)),
                  pl.BlockSpec((N_EXPERTS, 1), lambda i: (0, 0))],
        out_specs=[kt, kt, kt,
                   pl.BlockSpec((1, N_EXPERTS, 1), lambda i: (i // steps_per_tile, 0, 0))],
        out_shape=[jax.ShapeDtypeStruct((TOP_K, t), jnp.int32),
                   jax.ShapeDtypeStruct((TOP_K, t), F32),
                   jax.ShapeDtypeStruct((TOP_K, t), jnp.int32),
                   jax.ShapeDtypeStruct((n_tiles, N_EXPERTS, 1), F32)],
        scratch_shapes=[pltpu.VMEM((N_EXPERTS, 1), F32)],
        compiler_params=_cparams("arbitrary"),
        name="moe_router",
    )(x, w_router.T, bias.reshape(N_EXPERTS, 1))
    return idx, w, rank, counts.reshape(n_tiles, N_EXPERTS).astype(jnp.int32)


def _steps_per_tile(tile_tokens):
    return tile_tokens * TOP_K // STEP_ROWS + N_EXPERTS


def _plan_kernel(fill_lo_ref, fill_hi_ref, pos_ref, src_ref, *, tile_tokens, part):
    i = pl.program_id(0)
    q = pl.program_id(1)
    empty = tile_tokens * TOP_K
    group = 16

    @pl.when(q == 0)
    def _():
        def fill_range(g, carry):
            lo = fill_lo_ref[i * N_EXPERTS + g]
            hi = fill_hi_ref[i * N_EXPERTS + g]

            def fill(b, carry):
                for d in range(group):
                    src_ref[0, 0, jnp.maximum(hi - 1 - b * group - d, 0)] = empty
                return carry
            return lax.fori_loop(0, (hi - lo + group - 1) // group, fill, carry)
        lax.fori_loop(0, N_EXPERTS, fill_range, 0)

    def place(b, carry):
        base = b * group
        slots = [pos_ref[0, 0, base + d] for d in range(group)]
        for d in range(group):
            src_ref[0, 0, slots[d]] = q * part + base + d
        return carry
    lax.fori_loop(0, part // group, place, 0)


def _dispatch_plan(idx_t, w_t, rank_t, counts, tile_tokens):
    t = idx_t.shape[1]
    n_tiles = t // tile_tokens
    steps = _steps_per_tile(tile_tokens)
    padded = (counts + STEP_ROWS - 1) // STEP_ROWS * STEP_ROWS
    pad_end = jnp.cumsum(padded, axis=1)
    pad_start = (pad_end - padded).astype(jnp.int32)
    n_used = (pad_end[:, -1] // STEP_ROWS).astype(jnp.int32)
    starts = jnp.arange(steps, dtype=jnp.int32) * STEP_ROWS
    step_e = jnp.sum(starts[None, :, None] >= pad_end[:, None, :], axis=-1).astype(jnp.int32)
    step_e = jnp.minimum(step_e, N_EXPERTS - 1)
    last = jnp.take_along_axis(step_e, jnp.maximum(n_used - 1, 0)[:, None], axis=1)
    step_e = jnp.where(starts[None, :] // STEP_ROWS < n_used[:, None], step_e, last)

    start_of = jnp.repeat(pad_start.T, tile_tokens, axis=1)
    experts = jnp.arange(N_EXPERTS, dtype=jnp.int32)[None, :, None]
    pos_t = rank_t + jnp.sum(jnp.where(idx_t[:, None, :] == experts, start_of[None], 0), axis=1)
    fill_lo = (pad_start + counts).astype(jnp.int32)
    fill_hi = pad_end.astype(jnp.int32)

    parts = PLAN_PARTS
    part = tile_tokens * TOP_K // parts
    n_slots = steps * STEP_ROWS
    flat = lambda a: a.T.reshape(n_tiles * parts, 1, part)
    src = pl.pallas_call(
        functools.partial(_plan_kernel, tile_tokens=tile_tokens, part=part),
        grid_spec=pltpu.PrefetchScalarGridSpec(
            num_scalar_prefetch=2,
            grid=(n_tiles, parts),
            in_specs=[pl.BlockSpec((1, 1, part), lambda i, q, lo, hi: (i * parts + q, 0, 0),
                                   memory_space=pltpu.SMEM)],
            out_specs=pl.BlockSpec((1, 1, n_slots), lambda i, q, lo, hi: (i, 0, 0),
                                   memory_space=pltpu.SMEM, pipeline_mode=pl.Buffered(1)),
        ),
        out_shape=jax.ShapeDtypeStruct((n_tiles, 1, n_slots), jnp.int32),
        compiler_params=_cparams("parallel", "arbitrary"),
        name="moe_plan",
    )(fill_lo.reshape(-1), fill_hi.reshape(-1), flat(pos_t))
    src = jnp.where(jnp.arange(n_slots, dtype=jnp.int32)[None, None, :] < pad_end[:, -1][:, None, None],
                    src, tile_tokens * TOP_K)
    rows = src & -SUBLANES
    w_flat = jnp.pad(w_t.T.reshape(n_tiles, tile_tokens * TOP_K), ((0, 0), (0, SUBLANES)))
    slot_w = jnp.take_along_axis(w_flat, src[:, 0, :], axis=1)
    held = jnp.clip(jnp.take_along_axis(fill_lo, step_e, axis=1) - starts[None, :], 0, STEP_ROWS)
    live = jnp.where(starts[None, :] // STEP_ROWS < n_used[:, None], (held + EXPERT_ROWS - 1) // EXPERT_ROWS, 0)
    return rows, slot_w, step_e.reshape(-1), n_used, live.reshape(-1).astype(jnp.int32), steps


def _experts_kernel(step_e_ref, n_used_ref, live_ref, row_ref, sw_ref, x_ref, wgu_ref, wd_ref, acc_ref,
                    *stage_refs):
    n_sub = BLOCKS_PER_STEP
    xs_refs = stage_refs[:n_sub]
    ys_refs = stage_refs[n_sub:]
    i = pl.program_id(0)
    j = pl.program_id(1)
    rows = EXPERT_ROWS
    pitch = STAGE_PITCH

    @pl.when(j == 0)
    def _():
        acc_ref[...] = jnp.zeros_like(acc_ref)

    last_row = x_ref.shape[1] - SUBLANES

    def token_rows(slot, limit=None):
        row = row_ref[0, 0, slot]
        if limit is not None:
            row = jnp.minimum(row, limit)
        return pl.ds(pl.multiple_of(row, SUBLANES), SUBLANES)

    def run(subs):
        for s in subs:
            for r in range(rows):
                xs_refs[s][pl.ds(r, ROW_CHUNKS, stride=pitch), :] = x_ref[0, token_rows(s * rows + r, last_row), :]
        eye = (lax.broadcasted_iota(jnp.int32, (rows, rows), 0)
               == lax.broadcasted_iota(jnp.int32, (rows, rows), 1))
        ys = {}
        for s in subs:
            x = jnp.concatenate([xs_refs[s][pl.ds(c * pitch, rows), :] for c in range(ROW_CHUNKS)], axis=1)
            gu = _bdot(x, wgu_ref[0, 0])
            hid = _silu(gu[:, :D_EXPERT]) * gu[:, D_EXPERT:]
            w_row = sw_ref[0, :, s * rows:(s + 1) * rows]
            w_col = jnp.sum(jnp.where(eye, w_row, 0.0), axis=1, keepdims=True)
            ys[s] = _bdot(hid, wd_ref[0, 0]) * w_col
        for s in subs:
            for c in range(ROW_CHUNKS):
                ys_refs[s][pl.ds(c * pitch, rows), :] = ys[s][:, c * LANES:(c + 1) * LANES]
        for s in subs:
            for r0 in range(0, rows, SCATTER_BATCH):
                new = [acc_ref[0, token_rows(s * rows + r), :] + ys_refs[s][pl.ds(r, ROW_CHUNKS, stride=pitch), :]
                       for r in range(r0, r0 + SCATTER_BATCH)]
                for r, val in zip(range(r0, r0 + SCATTER_BATCH), new):
                    acc_ref[0, token_rows(s * rows + r), :] = val

    live = live_ref[i * pl.num_programs(1) + j]
    for n_live in range(1, n_sub + 1):
        pl.when(live == n_live)(functools.partial(run, range(n_live)))


def _routed_experts(x_slabs, idx_t, w_t, rank_t, counts, w_gate_up, w_down, layer, tile_tokens):
    d = D_MODEL
    t = x_slabs.shape[0] // ROW_CHUNKS
    n_tiles = t // tile_tokens
    rows, slot_w, step_e, n_used, live, steps = _dispatch_plan(idx_t, w_t, rank_t, counts, tile_tokens)
    rows = rows.reshape(n_tiles * steps, 1, STEP_ROWS)
    slot_w = slot_w.reshape(n_tiles * steps, 1, STEP_ROWS)
    x_slabs = x_slabs.reshape(n_tiles, tile_tokens * ROW_CHUNKS, LANES)
    slab_rows = (tile_tokens + 1) * ROW_CHUNKS

    def used_step(i, j, se, nu, lv):
        return (i * steps + jnp.minimum(j, jnp.maximum(nu[i] - 1, 0)), 0, 0)

    pick = lambda i, j, se, nu, lv: (layer, se[i * steps + j], 0, 0)
    tile = lambda i, j, se, nu, lv: (i, 0, 0)
    resident = pl.Buffered(1)
    stage = pltpu.VMEM((ROW_CHUNKS * STAGE_PITCH, LANES), F32)
    grid_spec = pltpu.PrefetchScalarGridSpec(
        num_scalar_prefetch=3,
        grid=(n_tiles, steps),
        in_specs=[
            pl.BlockSpec((1, 1, STEP_ROWS), used_step, memory_space=pltpu.SMEM),
            pl.BlockSpec((1, 1, STEP_ROWS), used_step),
            pl.BlockSpec((1, tile_tokens * ROW_CHUNKS, LANES), tile, pipeline_mode=resident),
            pl.BlockSpec((1, 1, d, 2 * D_EXPERT), pick),
            pl.BlockSpec((1, 1, D_EXPERT, d), pick),
        ],
        out_specs=pl.BlockSpec((1, slab_rows, LANES), tile, pipeline_mode=resident),
        scratch_shapes=[stage] * (2 * BLOCKS_PER_STEP),
    )
    return pl.pallas_call(
        _experts_kernel,
        grid_spec=grid_spec,
        out_shape=jax.ShapeDtypeStruct((n_tiles, slab_rows, LANES), F32),
        compiler_params=_cparams("parallel", "arbitrary"),
        name="moe_experts",
    )(step_e, n_used, live, rows, slot_w, x_slabs, w_gate_up, w_down)


def _moe_out_kernel(x_ref, r_ref, wg_ref, wu_ref, wd_ref, g_ref, b_ref, o_ref):
    x = x_ref[...]
    tm = x.shape[0]
    hid = _silu(_bdot(x, wg_ref[...])) * _bdot(x, wu_ref[...])
    routed = jnp.concatenate([r_ref[0, pl.ds(c, tm, stride=ROW_CHUNKS), :] for c in range(ROW_CHUNKS)], axis=1)
    y = routed + _bdot(hid, wd_ref[...])
    o_ref[...] = _layer_norm(DN_ALPHA * x + y, g_ref[...], b_ref[...])


def _moe_out(x, routed, ws_gate, ws_up, ws_down, g, b, tile_tokens, tm=512):
    t, d = x.shape
    tm = min(tm, tile_tokens)
    per_tile = tile_tokens // tm
    return pl.pallas_call(
        _moe_out_kernel,
        grid=(t // tm,),
        in_specs=[pl.BlockSpec((tm, d), lambda i: (i, 0)),
                  pl.BlockSpec((1, tm * ROW_CHUNKS, LANES), lambda i: (i // per_tile, i % per_tile, 0)),
                  pl.BlockSpec((d, D_SHARED), lambda i: (0, 0)),
                  pl.BlockSpec((d, D_SHARED), lambda i: (0, 0)),
                  pl.BlockSpec((D_SHARED, d), lambda i: (0, 0)),
                  pl.BlockSpec((1, d), lambda i: (0, 0)),
                  pl.BlockSpec((1, d), lambda i: (0, 0))],
        out_specs=pl.BlockSpec((tm, d), lambda i: (i, 0)),
        out_shape=jax.ShapeDtypeStruct((t, d), F32),
        compiler_params=_cparams("parallel"),
        name="moe_out",
    )(x, routed, ws_gate.astype(BF16), ws_up.astype(BF16), ws_down.astype(BF16),
      g.reshape(1, d), b.reshape(1, d))


def _moe_ln(hid, hid_slabs, w_router, router_bias, w_gate_up, w_down, layer, ws_gate, ws_up, ws_down, ln_g, ln_b):
    b, l, d = hid.shape
    x = hid.reshape(b * l, d)
    tile_tokens = min(MOE_TILE_TOKENS, b * l)
    idx_t, w_t, rank_t, counts = _router(x, w_router, router_bias, tile_tokens)
    routed = _routed_experts(hid_slabs, idx_t, w_t, rank_t, counts, w_gate_up, w_down, layer, tile_tokens)
    return _moe_out(x, routed, ws_gate, ws_up, ws_down, ln_g, ln_b, tile_tokens).reshape(b, l, d)


def kernel(x, a_w_in, a_conv_w, a_A_log, a_dt_bias, a_onorm_w, a_w_out, b_w_dq, b_qnorm_w, b_w_uq, b_w_o,
           kv_w_dkv, kv_norm_w, kv_w_ukv, ln1_g, ln1_b, ln2_g, ln2_b, moe_w_router, moe_router_bias,
           moe_w_gate, moe_w_up, moe_w_down, moe_ws_gate, moe_ws_up, moe_ws_down):
    l = x.shape[1]
    cos_t, sin_t = _rope_lane_tables(l)
    w_gate_up = jnp.concatenate([moe_w_gate, moe_w_up], axis=-1).astype(BF16)
    w_down = moe_w_down.astype(BF16)
    h = x
    k = v = None
    for layer in range(DEPTH):
        if layer < N_A_LAYERS:
            i = layer
            h, slabs = _gated_deltanet_ln(h, a_w_in[i], a_conv_w[i], a_A_log[i], a_dt_bias[i], a_onorm_w[i],
                                          a_w_out[i], ln1_g[layer], ln1_b[layer])
        else:
            i = layer - N_A_LAYERS
            h, slabs = _mla_ln(h, b_w_dq[i], b_qnorm_w[i], b_w_uq[i], b_w_o[i], k, v, cos_t, sin_t,
                               ln1_g[layer], ln1_b[layer])
        h = _moe_ln(h, slabs, moe_w_router[layer], moe_router_bias[layer], w_gate_up, w_down, layer,
                    moe_ws_gate[layer], moe_ws_up[layer], moe_ws_down[layer], ln2_g[layer], ln2_b[layer])
        if layer == N_A_LAYERS - 1:
            k, v = _mla_kv(h, kv_w_dkv, kv_norm_w, kv_w_ukv, cos_t, sin_t)
    return h
```

```python
import functools

import jax
import jax.numpy as jnp
import numpy as np
from jax import lax
from jax.experimental import pallas as pl
from jax.experimental.pallas import tpu as pltpu

F32 = jnp.float32
BF16 = jnp.bfloat16
HIGHEST = lax.Precision.HIGHEST

D_MODEL = 1024
DEPTH = 4
N_A_LAYERS = DEPTH // 2
GDN_HEADS = 8
GDN_DK = 128
GDN_DV = 128
D_QK = GDN_HEADS * GDN_DK
D_VA = GDN_HEADS * GDN_DV
GDN_CONV_CH = 2 * D_QK + D_VA
CONV_WIDTH = 4
CHUNK = 64
MLA_HEADS = 8
Q_LORA = 512
KV_LORA = 256
D_NOPE = 128
D_ROPE = 64
D_V = 128
ROPE_THETA = 10000.0
N_EXPERTS = 64
TOP_K = 8
N_GROUPS = 8
GROUP_SIZE = N_EXPERTS // N_GROUPS
TOPK_GROUPS = 4
D_EXPERT = 256
D_SHARED = 256
ROUTED_SCALE = 2.5
DN_ALPHA = (2 * DEPTH) ** 0.25
EPS = 1e-6
LN_EPS = 1e-5
LOG2_E = 1.4426950408889634

LANES = 128
SUBLANES = 8
VMEM_LIMIT_BYTES = 56 * 1024 * 1024

EXPERT_ROWS = 128
ROW_CHUNKS = D_MODEL // LANES
STAGE_PITCH = EXPERT_ROWS + 1
SCATTER_BATCH = 8
BLOCKS_PER_STEP = 2
STEP_ROWS = BLOCKS_PER_STEP * EXPERT_ROWS
MOE_TILE_TOKENS = 4096
ATTN_ROW_GROUPS = 4
assert TOP_K == SUBLANES


def _cparams(*sem):
    return pltpu.CompilerParams(dimension_semantics=sem, vmem_limit_bytes=VMEM_LIMIT_BYTES)


def _silu(x):
    return x * jax.nn.sigmoid(x)


def _layer_norm(x, g, b):
    mu = jnp.mean(x, -1, keepdims=True)
    xc = x - mu
    var = jnp.mean(xc * xc, -1, keepdims=True)
    return xc * lax.rsqrt(var + LN_EPS) * g + b


def _rms_norm(x, w):
    return x * lax.rsqrt(jnp.mean(x * x, -1, keepdims=True) + EPS) * w


def _bdot(a, b):
    return jnp.dot(a.astype(BF16), b.astype(BF16), preferred_element_type=F32)


def _bdot_nt(a, b):
    return lax.dot_general(a.astype(BF16), b.astype(BF16), (((1,), (1,)), ((), ())),
                           preferred_element_type=F32)


def _hdot(a, b):
    return jnp.dot(a, b, precision=HIGHEST, preferred_element_type=F32)


def _matmul_kernel(x_ref, w_ref, o_ref):
    o_ref[...] = _bdot(x_ref[...], w_ref[...]).astype(o_ref.dtype)


def _matmul(x, w, out_dtype, tm, tn):
    m, k = x.shape
    n = w.shape[1]
    tm = min(tm, m)
    tn = min(tn, n)
    return pl.pallas_call(
        _matmul_kernel,
        grid=(m // tm, n // tn),
        in_specs=[pl.BlockSpec((tm, k), lambda i, j: (i, 0)),
                  pl.BlockSpec((k, tn), lambda i, j: (0, j))],
        out_specs=pl.BlockSpec((tm, tn), lambda i, j: (i, j)),
        out_shape=jax.ShapeDtypeStruct((m, n), out_dtype),
        compiler_params=_cparams("parallel", "parallel"),
        name="matmul",
    )(x, w)


def _matmul_ln_kernel(x_ref, w_ref, r_ref, g_ref, b_ref, o_ref, slab_ref):
    mix = _bdot(x_ref[...], w_ref[...])
    out = _layer_norm(DN_ALPHA * r_ref[...] + mix, g_ref[...], b_ref[...])
    o_ref[...] = out
    tm = out.shape[0]
    for c in range(ROW_CHUNKS):
        slab_ref[pl.ds(c, tm, stride=ROW_CHUNKS), :] = out[:, c * LANES:(c + 1) * LANES]


def _matmul_ln(x, w, res, g, b, tm=512):
    m, k = x.shape
    n = w.shape[1]
    tm = min(tm, m)
    chunks = n // LANES
    return pl.pallas_call(
        _matmul_ln_kernel,
        grid=(m // tm,),
        in_specs=[pl.BlockSpec((tm, k), lambda i: (i, 0)),
                  pl.BlockSpec((k, n), lambda i: (0, 0)),
                  pl.BlockSpec((tm, n), lambda i: (i, 0)),
                  pl.BlockSpec((1, n), lambda i: (0, 0)),
                  pl.BlockSpec((1, n), lambda i: (0, 0))],
        out_specs=[pl.BlockSpec((tm, n), lambda i: (i, 0)),
                   pl.BlockSpec((tm * chunks, LANES), lambda i: (i, 0))],
        out_shape=[jax.ShapeDtypeStruct((m, n), F32),
                   jax.ShapeDtypeStruct((m * chunks, LANES), F32)],
        compiler_params=_cparams("parallel"),
        name="matmul_ln",
    )(x, w, res, g.reshape(1, n), b.reshape(1, n))


def _softplus(x):
    return jnp.maximum(x, 0.0) + jnp.log1p(jnp.exp(-jnp.abs(x)))


def _gdn_gates_kernel(x_ref, wab_ref, wabt_ref, alog_ref, dtb_ref, alogt_ref, dtbt_ref,
                      gc_ref, beta_ref, gct_ref):
    x = x_ref[...]
    tl = x.shape[0]
    h = GDN_HEADS
    ab = _bdot(x, wab_ref[...])
    abt = _bdot_nt(wabt_ref[...], x)
    g = -jnp.exp(alog_ref[...]) * _softplus(ab[:, :h] + dtb_ref[...])
    gt = -jnp.exp(alogt_ref[...]) * _softplus(abt[:h, :] + dtbt_ref[...])
    beta_ref[...] = jax.nn.sigmoid(ab[:, h:2 * h])
    row = lax.broadcasted_iota(jnp.int32, (tl, tl), 0)
    col = lax.broadcasted_iota(jnp.int32, (tl, tl), 1)
    shift = CHUNK.bit_length() - 1
    same = (row >> shift) == (col >> shift)
    lower = jnp.where(same & (col <= row), 1.0, 0.0).astype(F32)
    upper = jnp.where(same & (row <= col), 1.0, 0.0).astype(F32)
    gc_ref[...] = _hdot(lower, g)
    gct_ref[...] = _hdot(gt, upper)


def _gdn_gates(x, w_ab, a_log, dt_bias, tl=512):
    t = x.shape[0]
    tl = min(tl, t)
    h = GDN_HEADS
    wab = jnp.zeros((D_MODEL, LANES), F32).at[:, :2 * h].set(w_ab)
    wabt = w_ab.T
    return pl.pallas_call(
        _gdn_gates_kernel,
        grid=(t // tl,),
        in_specs=[pl.BlockSpec((tl, D_MODEL), lambda i: (i, 0)),
                  pl.BlockSpec((D_MODEL, LANES), lambda i: (0, 0)),
                  pl.BlockSpec((2 * h, D_MODEL), lambda i: (0, 0)),
                  pl.BlockSpec((1, h), lambda i: (0, 0)),
                  pl.BlockSpec((1, h), lambda i: (0, 0)),
                  pl.BlockSpec((h, 1), lambda i: (0, 0)),
                  pl.BlockSpec((h, 1), lambda i: (0, 0))],
        out_specs=[pl.BlockSpec((tl, h), lambda i: (i, 0)),
                   pl.BlockSpec((tl, h), lambda i: (i, 0)),
                   pl.BlockSpec((h, tl), lambda i: (0, i))],
        out_shape=[jax.ShapeDtypeStruct((t, h), F32),
                   jax.ShapeDtypeStruct((t, h), F32),
                   jax.ShapeDtypeStruct((h, t), F32)],
        compiler_params=_cparams("parallel"),
        name="gdn_gates",
    )(x, wab, wabt, a_log.reshape(1, h), dt_bias.reshape(1, h),
      a_log.reshape(h, 1), dt_bias.reshape(h, 1))


def _gdn_conv_kernel(cur_ref, prev_ref, w_ref, q_ref, k_ref, v_ref):
    i = pl.program_id(1)
    x = cur_ref[0]
    tl = x.shape[0]
    prev = jnp.where(i > 0, prev_ref[0], 0.0)
    xx = jnp.concatenate([prev, x], axis=0)
    w = w_ref[...]
    y = None
    for j in range(CONV_WIDTH):
        off = SUBLANES - (CONV_WIDTH - 1) + j
        term = xx[off:off + tl, :] * w[j:j + 1, :]
        y = term if y is None else y + term
    y = _silu(y)
    for h in range(GDN_HEADS):
        sl = slice(h * GDN_DK, (h + 1) * GDN_DK)
        qh = y[:, sl]
        q_ref[0, :, sl] = qh * lax.rsqrt(jnp.sum(qh * qh, -1, keepdims=True) + EPS) * (GDN_DK ** -0.5)
        kh = y[:, D_QK + h * GDN_DK:D_QK + (h + 1) * GDN_DK]
        k_ref[0, :, sl] = kh * lax.rsqrt(jnp.sum(kh * kh, -1, keepdims=True) + EPS)
    v_ref[0] = y[:, 2 * D_QK:]


def _gdn_conv(proj, conv_w, tl=256):
    b, l, _ = proj.shape
    tl = min(tl, l)
    c3 = GDN_CONV_CH
    per = tl // SUBLANES
    out = jax.ShapeDtypeStruct((b, l, D_QK), F32)
    return pl.pallas_call(
        _gdn_conv_kernel,
        grid=(b, l // tl),
        in_specs=[pl.BlockSpec((1, tl, c3), lambda bi, i: (bi, i, 0)),
                  pl.BlockSpec((1, SUBLANES, c3), lambda bi, i: (bi, jnp.maximum(i * per - 1, 0), 0)),
                  pl.BlockSpec((CONV_WIDTH, c3), lambda bi, i: (0, 0))],
        out_specs=[pl.BlockSpec((1, tl, D_QK), lambda bi, i: (bi, i, 0))] * 3,
        out_shape=[out, out, out],
        compiler_params=_cparams("parallel", "parallel"),
        name="gdn_conv",
    )(proj, proj, conv_w)


def _split_bf16(x):
    hi = x.astype(BF16)
    lo = (x - hi.astype(F32)).astype(BF16)
    return hi, lo


def _dot_split(a, b):
    (ah, al), (bh, bl) = a, b
    dot = functools.partial(jnp.dot, preferred_element_type=F32)
    return dot(ah, bh) + (dot(ah, bl) + dot(al, bh))


def _gdn_prep_kernel(q_ref, k_ref, v_ref, gc_ref, beta_ref, gct_ref, wq_ref, u_ref, attn_ref, kgt_ref,
                     *, chunks):
    c = CHUNK
    row = lax.broadcasted_iota(jnp.int32, (c, c), 0)
    col = lax.broadcasted_iota(jnp.int32, (c, c), 1)
    incl = row >= col
    strict = row > col
    eye = jnp.where(row == col, 1.0, 0.0).astype(F32)
    items = [(ci, h) for ci in range(chunks) for h in range(GDN_HEADS)]

    st = {}
    for ci, h in items:
        rows = slice(ci * c, (ci + 1) * c)
        sl = slice(h * GDN_DK, (h + 1) * GDN_DK)
        q = q_ref[0, rows, sl]
        k = k_ref[0, rows, sl]
        v = v_ref[0, rows, sl]
        gcol = gc_ref[0, rows, h:h + 1]
        grow = gct_ref[0, ci, h:h + 1, :]
        bcol = beta_ref[0, rows, h:h + 1]
        decay = jnp.where(incl, jnp.exp(jnp.where(incl, gcol - grow, 0.0)), 0.0)
        kb = k * bcol
        eg = jnp.exp(gcol)
        g_last = gcol[c - 1:c, :]
        sc = _bdot_nt(jnp.concatenate([kb, q], axis=0), k)
        a = jnp.where(strict, sc[:c] * decay, 0.0)
        attn_ref[0, ci, h] = (sc[c:] * decay).astype(attn_ref.dtype)
        kgt_ref[0, ci, h] = (k * jnp.exp(g_last - gcol)).T.astype(kgt_ref.dtype)
        wq_ref[0, ci, h, c:, :] = (q * eg).astype(wq_ref.dtype)
        st[ci, h] = dict(m=-a, rhs=jnp.concatenate([v * bcol, kb * eg], axis=1))

    for it in items:
        st[it]["p"] = eye + st[it]["m"]
    span = 2
    while span < c:
        for it in items:
            ms = _split_bf16(st[it]["m"])
            st[it]["m"] = _dot_split(ms, ms)
        for it in items:
            st[it]["p"] = st[it]["p"] + _dot_split(_split_bf16(st[it]["p"]), _split_bf16(st[it]["m"]))
        span *= 2

    for ci, h in items:
        sol = _dot_split(_split_bf16(st[ci, h]["p"]), _split_bf16(st[ci, h]["rhs"]))
        u_ref[0, ci * c:(ci + 1) * c, h * GDN_DV:(h + 1) * GDN_DV] = sol[:, :GDN_DV]
        wq_ref[0, ci, h, :c, :] = sol[:, GDN_DV:].astype(wq_ref.dtype)


def _gdn_prep(q, k, v, gc, beta, gct, chunks=2):
    b, l, _ = q.shape
    n = l // CHUNK
    chunks = min(chunks, n)
    h = GDN_HEADS
    rows = chunks * CHUNK
    blk = pl.BlockSpec((1, rows, D_QK), lambda bi, i: (bi, i, 0))
    gate = pl.BlockSpec((1, rows, h), lambda bi, i: (bi, i, 0))
    return pl.pallas_call(
        functools.partial(_gdn_prep_kernel, chunks=chunks),
        grid=(b, n // chunks),
        in_specs=[blk, blk, blk, gate, gate,
                  pl.BlockSpec((1, chunks, h, CHUNK), lambda bi, i: (bi, i, 0, 0))],
        out_specs=[pl.BlockSpec((1, chunks, h, 2 * CHUNK, GDN_DK), lambda bi, i: (bi, i, 0, 0, 0)),
                   pl.BlockSpec((1, rows, D_VA), lambda bi, i: (bi, i, 0)),
                   pl.BlockSpec((1, chunks, h, CHUNK, CHUNK), lambda bi, i: (bi, i, 0, 0, 0)),
                   pl.BlockSpec((1, chunks, h, GDN_DK, CHUNK), lambda bi, i: (bi, i, 0, 0, 0))],
        out_shape=[jax.ShapeDtypeStruct((b, n, h, 2 * CHUNK, GDN_DK), BF16),
                   jax.ShapeDtypeStruct((b, l, D_VA), F32),
                   jax.ShapeDtypeStruct((b, n, h, CHUNK, CHUNK), BF16),
                   jax.ShapeDtypeStruct((b, n, h, GDN_DK, CHUNK), BF16)],
        compiler_params=_cparams("parallel", "parallel"),
        name="gdn_prep",
    )(q, k, v, gc, beta, gct)


def _gdn_state_kernel(wq_ref, u_ref, attn_ref, kgt_ref, gc_ref, z_ref, onw_ref, o_ref, s_ref, *, chunks):
    @pl.when(pl.program_id(1) == 0)
    def _():
        s_ref[...] = jnp.zeros_like(s_ref)

    c = CHUNK
    onw = onw_ref[...]
    dot = functools.partial(jnp.dot, preferred_element_type=F32)
    heads = range(GDN_HEADS)
    for ci in range(chunks):
        rows = slice(ci * c, (ci + 1) * c)
        s_old = [s_ref[h] for h in heads]
        r = [dot(wq_ref[0, ci, h], s_old[h].astype(BF16)) for h in heads]
        v_new = [(u_ref[0, rows, h * GDN_DV:(h + 1) * GDN_DV] - r[h][:c]).astype(BF16) for h in heads]
        for h in heads:
            decay_last = jnp.exp(gc_ref[0, (ci + 1) * c - 1:(ci + 1) * c, h:h + 1])
            s_ref[h] = s_old[h] * decay_last + dot(kgt_ref[0, ci, h], v_new[h])
        for h in heads:
            sl = slice(h * GDN_DV, (h + 1) * GDN_DV)
            o = r[h][c:] + dot(attn_ref[0, ci, h], v_new[h])
            o_ref[0, rows, sl] = (_rms_norm(o, onw) * _silu(z_ref[0, rows, sl])).astype(o_ref.dtype)


def _gdn_state(wq, u, attn, kgt, gc, proj, onorm_w, chunks=4):
    b, n, h = wq.shape[:3]
    l = n * CHUNK
    chunks = min(chunks, n)
    rows = chunks * CHUNK
    z_block = GDN_CONV_CH // D_VA
    return pl.pallas_call(
        functools.partial(_gdn_state_kernel, chunks=chunks),
        grid=(b, n // chunks),
        in_specs=[pl.BlockSpec((1, chunks, h, 2 * CHUNK, GDN_DK), lambda bi, i: (bi, i, 0, 0, 0)),
                  pl.BlockSpec((1, rows, D_VA), lambda bi, i: (bi, i, 0)),
                  pl.BlockSpec((1, chunks, h, CHUNK, CHUNK), lambda bi, i: (bi, i, 0, 0, 0)),
                  pl.BlockSpec((1, chunks, h, GDN_DK, CHUNK), lambda bi, i: (bi, i, 0, 0, 0)),
                  pl.BlockSpec((1, rows, h), lambda bi, i: (bi, i, 0)),
                  pl.BlockSpec((1, rows, D_VA), lambda bi, i: (bi, i, z_block)),
                  pl.BlockSpec((1, GDN_DV), lambda bi, i: (0, 0))],
        out_specs=pl.BlockSpec((1, rows, D_VA), lambda bi, i: (bi, i, 0)),
        out_shape=jax.ShapeDtypeStruct((b, l, D_VA), BF16),
        scratch_shapes=[pltpu.VMEM((h, GDN_DK, GDN_DV), F32)],
        compiler_params=_cparams("parallel", "arbitrary"),
        name="gdn_state",
    )(wq, u, attn, kgt, gc, proj, onorm_w.reshape(1, GDN_DV))


def _gated_deltanet_ln(hid, w_in, conv_w, a_log, dt_bias, onorm_w, w_out, ln_g, ln_b):
    b, l, d = hid.shape
    h = GDN_HEADS
    n = l // CHUNK
    x = hid.reshape(b * l, d)
    n_main = GDN_CONV_CH + D_VA
    proj = _matmul(x, w_in[:, :n_main].astype(BF16), F32, 1024, 1024).reshape(b, l, n_main)
    gc, beta, gct = _gdn_gates(x, w_in[:, n_main:], a_log, dt_bias)
    gc = gc.reshape(b, l, h)
    gct = gct.reshape(h, b, n, CHUNK).transpose(1, 2, 0, 3)
    q, k, v = _gdn_conv(proj, conv_w)
    wq, u, attn, kgt = _gdn_prep(q, k, v, gc, beta.reshape(b, l, h), gct)
    o = _gdn_state(wq, u, attn, kgt, gc, proj, onorm_w)
    out, slabs = _matmul_ln(o.reshape(b * l, D_VA), w_out.astype(BF16), x, ln_g, ln_b)
    return out.reshape(b, l, d), slabs


def _rope_lane_tables(l):
    inv = 1.0 / (ROPE_THETA ** (jnp.arange(0, D_ROPE, 2, dtype=F32) / D_ROPE))
    ang = jnp.arange(l, dtype=F32)[:, None] * inv[None, :]
    cos, sin = jnp.cos(ang), jnp.sin(ang)
    zero = jnp.zeros((l, LANES - D_ROPE), F32)
    return (jnp.concatenate([cos, cos, zero], -1),
            jnp.concatenate([-sin, sin, zero], -1))


def _rope_weight_groups(w_rope):
    kdim = w_rope.shape[0]
    half = D_ROPE // 2
    zero = jnp.zeros((kdim, LANES - D_ROPE), w_rope.dtype)
    x1, x2 = w_rope[:, :half], w_rope[:, half:]
    return jnp.concatenate([x1, x2, zero, x2, x1, zero], axis=-1)


def _mla_kv_kernel(x_ref, wd_ref, nw_ref, wu_ref, cos_ref, sin_ref, k_ref, v_ref):
    x = x_ref[0]
    ckv = _bdot(x, wd_ref[...])
    c = _rms_norm(ckv[:, :KV_LORA], nw_ref[...])
    k_rope_t = (ckv[:, KV_LORA:KV_LORA + LANES] * cos_ref[...]
                + ckv[:, KV_LORA + LANES:] * sin_ref[...]).T.astype(k_ref.dtype)
    kv = _bdot(c, wu_ref[...])
    per = D_NOPE + D_V
    ones_col = jnp.where(lax.broadcasted_iota(jnp.int32, (x.shape[0], LANES), 1) == 0, 1.0, 0.0).astype(v_ref.dtype)
    for h in range(MLA_HEADS):
        k_ref[0, h, :D_NOPE, :] = kv[:, h * per:h * per + D_NOPE].T.astype(k_ref.dtype)
        k_ref[0, h, D_NOPE:, :] = k_rope_t
        v_ref[0, h, :, :D_V] = kv[:, h * per + D_NOPE:(h + 1) * per].astype(v_ref.dtype)
        v_ref[0, h, :, D_V:] = ones_col


def _mla_kv(hid, w_dkv, kv_norm_w, w_ukv, cos_t, sin_t, tl=512):
    b, l, d = hid.shape
    tl = min(tl, l)
    wd = jnp.concatenate([w_dkv[:, :KV_LORA], _rope_weight_groups(w_dkv[:, KV_LORA:])], -1).astype(BF16)
    nd = wd.shape[1]
    hh = MLA_HEADS
    return pl.pallas_call(
        _mla_kv_kernel,
        grid=(b, l // tl),
        in_specs=[pl.BlockSpec((1, tl, d), lambda bi, i: (bi, i, 0)),
                  pl.BlockSpec((d, nd), lambda bi, i: (0, 0)),
                  pl.BlockSpec((1, KV_LORA), lambda bi, i: (0, 0)),
                  pl.BlockSpec((KV_LORA, hh * (D_NOPE + D_V)), lambda bi, i: (0, 0)),
                  pl.BlockSpec((tl, LANES), lambda bi, i: (i, 0)),
                  pl.BlockSpec((tl, LANES), lambda bi, i: (i, 0))],
        out_specs=[pl.BlockSpec((1, hh, D_NOPE + LANES, tl), lambda bi, i: (bi, 0, 0, i)),
                   pl.BlockSpec((1, hh, tl, D_V + LANES), lambda bi, i: (bi, 0, i, 0))],
        out_shape=[jax.ShapeDtypeStruct((b, hh, D_NOPE + LANES, l), BF16),
                   jax.ShapeDtypeStruct((b, hh, l, D_V + LANES), BF16)],
        compiler_params=_cparams("parallel", "parallel"),
        name="mla_kv",
    )(hid, wd, kv_norm_w.reshape(1, KV_LORA), w_ukv.astype(BF16), cos_t, sin_t)


def _mla_cq_kernel(x_ref, w_ref, nw_ref, o_ref):
    o_ref[...] = _rms_norm(_bdot(x_ref[...], w_ref[...]), nw_ref[...]).astype(o_ref.dtype)


def _mla_cq(x, w_dq, qnorm_w, tm=512):
    m, k = x.shape
    tm = min(tm, m)
    return pl.pallas_call(
        _mla_cq_kernel,
        grid=(m // tm,),
        in_specs=[pl.BlockSpec((tm, k), lambda i: (i, 0)),
                  pl.BlockSpec((k, Q_LORA), lambda i: (0, 0)),
                  pl.BlockSpec((1, Q_LORA), lambda i: (0, 0))],
        out_specs=pl.BlockSpec((tm, Q_LORA), lambda i: (i, 0)),
        out_shape=jax.ShapeDtypeStruct((m, Q_LORA), BF16),
        compiler_params=_cparams("parallel"),
        name="mla_cq",
    )(x, w_dq.astype(BF16), qnorm_w.reshape(1, Q_LORA))


def _mla_q_kernel(c_ref, w_ref, cos_ref, sin_ref, q_ref):
    c = c_ref[0]
    scale = (D_NOPE + D_ROPE) ** -0.5 * LOG2_E
    per = D_NOPE + 2 * LANES
    for h in range(MLA_HEADS):
        qh = _bdot(c, w_ref[:, h * per:(h + 1) * per])
        rope = qh[:, D_NOPE:D_NOPE + LANES] * cos_ref[...] + qh[:, D_NOPE + LANES:] * sin_ref[...]
        q_ref[0, h, :, :D_NOPE] = (qh[:, :D_NOPE] * scale).astype(q_ref.dtype)
        q_ref[0, h, :, D_NOPE:] = (rope * scale).astype(q_ref.dtype)


def _mla_q(cq, w_uq, cos_t, sin_t, tl=512):
    b, l, _ = cq.shape
    tl = min(tl, l)
    hh = MLA_HEADS
    per_in = D_NOPE + D_ROPE
    groups = []
    for h in range(hh):
        wh = w_uq[:, h * per_in:(h + 1) * per_in]
        groups += [wh[:, :D_NOPE], _rope_weight_groups(wh[:, D_NOPE:])]
    w = jnp.concatenate(groups, -1).astype(BF16)
    return pl.pallas_call(
        _mla_q_kernel,
        grid=(b, l // tl),
        in_specs=[pl.BlockSpec((1, tl, Q_LORA), lambda bi, i: (bi, i, 0)),
                  pl.BlockSpec(w.shape, lambda bi, i: (0, 0)),
                  pl.BlockSpec((tl, LANES), lambda bi, i: (i, 0)),
                  pl.BlockSpec((tl, LANES), lambda bi, i: (i, 0))],
        out_specs=pl.BlockSpec((1, hh, tl, D_NOPE + LANES), lambda bi, i: (bi, 0, i, 0)),
        out_shape=jax.ShapeDtypeStruct((b, hh, l, D_NOPE + LANES), BF16),
        compiler_params=_cparams("parallel", "parallel"),
        name="mla_q",
    )(cq, w, cos_t, sin_t)


def _mla_attn_kernel(q_ref, k_ref, v_ref, o_ref, *, tk):
    qi = pl.program_id(2)
    q = q_ref[0, 0]
    tq = q.shape[0]
    per = tq // tk

    groups = ATTN_ROW_GROUPS
    rows = tq // groups
    qs = [q[g * rows:(g + 1) * rows] for g in range(groups)]

    def block(j, carry, mask_offset=None):
        at = pl.ds(pl.multiple_of(j * tk, tk), tk)
        kb = k_ref[0, 0, :, at]
        vb = v_ref[0, 0, at, :]
        ss = [_bdot(qg, kb) for qg in qs]
        if mask_offset is not None:
            qpos = lax.broadcasted_iota(jnp.int32, (rows, tk), 0)
            kpos = lax.broadcasted_iota(jnp.int32, (rows, tk), 1)
            ss = [jnp.where(kpos + mask_offset <= qpos + g * rows, s, -jnp.inf) for g, s in enumerate(ss)]
        m_new = [jnp.maximum(c[0], jnp.max(s, -1, keepdims=True)) for c, s in zip(carry, ss)]
        ps = [jnp.exp2(s - m) for s, m in zip(ss, m_new)]
        return tuple((mn, jnp.exp2(m - mn) * acc + _bdot(p, vb)) for (m, acc), mn, p in zip(carry, m_new, ps))

    first = qi * per
    init = (jnp.full((rows, 1), -jnp.inf, F32), jnp.zeros((rows, D_V + LANES), F32))
    carry = lax.fori_loop(0, first, block, (init,) * groups)
    for d in range(per):
        carry = block(first + d, carry, mask_offset=d * tk)
    for g, (_, acc) in enumerate(carry):
        o_ref[0, g * rows:(g + 1) * rows, :] = (acc[:, :D_V] / acc[:, D_V:D_V + 1]).astype(o_ref.dtype)


def _mla_attn(q, k, v, tq=1024, tk=1024):
    b, hh, l, dq = q.shape
    tq = min(tq, l)
    tk = min(tk, tq)
    return pl.pallas_call(
        functools.partial(_mla_attn_kernel, tk=tk),
        grid=(b, hh, l // tq),
        in_specs=[pl.BlockSpec((1, 1, tq, dq), lambda bi, h, i: (bi, h, i, 0)),
                  pl.BlockSpec((1, 1, dq, l), lambda bi, h, i: (bi, h, 0, 0)),
                  pl.BlockSpec((1, 1, l, D_V + LANES), lambda bi, h, i: (bi, h, 0, 0))],
        out_specs=pl.BlockSpec((1, tq, D_V), lambda bi, h, i: (bi, i, h)),
        out_shape=jax.ShapeDtypeStruct((b, l, hh * D_V), BF16),
        compiler_params=_cparams("parallel", "parallel", "parallel"),
        name="mla_attn",
    )(q, k, v)


def _mla_ln(hid, w_dq, qnorm_w, w_uq, w_o, k, v, cos_t, sin_t, ln_g, ln_b):
    b, l, d = hid.shape
    x = hid.reshape(b * l, d)
    cq = _mla_cq(x, w_dq, qnorm_w).reshape(b, l, Q_LORA)
    q = _mla_q(cq, w_uq, cos_t, sin_t)
    o = _mla_attn(q, k, v)
    out, slabs = _matmul_ln(o.reshape(b * l, MLA_HEADS * D_V), w_o.astype(BF16), x, ln_g, ln_b)
    return out.reshape(b, l, d), slabs


def _first_argmax(x, ids, n):
    m = jnp.max(x, axis=0, keepdims=True)
    first = jnp.min(jnp.where(x == m, ids, n), axis=0, keepdims=True)
    return m, first


def _router_kernel(x_ref, wt_ref, bias_ref, idx_ref, w_ref, rank_ref, cnt_out_ref, cnt_ref, *, steps_per_tile):
    @pl.when(pl.program_id(0) % steps_per_tile == 0)
    def _():
        cnt_ref[...] = jnp.zeros_like(cnt_ref)

    x = x_ref[...]
    t = x.shape[0]
    logits = lax.dot_general(wt_ref[...], x, (((1,), (1,)), ((), ())),
                             precision=HIGHEST, preferred_element_type=F32)
    scores = jax.nn.sigmoid(logits)
    biased = scores + bias_ref[...]
    neg = -jnp.inf
    sub = lax.broadcasted_iota(jnp.int32, (GROUP_SIZE, t), 0).astype(F32)
    gscores = []
    for g in range(N_GROUPS):
        xg = biased[g * GROUP_SIZE:(g + 1) * GROUP_SIZE, :]
        m1, i1 = _first_argmax(xg, sub, float(GROUP_SIZE))
        m2 = jnp.max(jnp.where(sub == i1, neg, xg), axis=0, keepdims=True)
        gscores.append(m1 + m2)
    gs = jnp.concatenate(gscores, axis=0)
    gid = lax.broadcasted_iota(jnp.int32, (N_GROUPS, t), 0).astype(F32)
    gsel = jnp.zeros((N_GROUPS, t), F32)
    for _ in range(TOPK_GROUPS):
        _, gi = _first_argmax(gs, gid, float(N_GROUPS))
        hit = gid == gi
        gsel = jnp.where(hit, 1.0, gsel)
        gs = jnp.where(hit, neg, gs)
    eid = lax.broadcasted_iota(jnp.int32, (N_EXPERTS, t), 0).astype(F32)
    allowed = jnp.concatenate(
        [jnp.broadcast_to(gsel[g:g + 1, :], (GROUP_SIZE, t)) for g in range(N_GROUPS)], axis=0)
    cand = jnp.where(allowed > 0.0, biased, neg)
    idxs, ws, hits = [], [], []
    for _ in range(TOP_K):
        _, ei = _first_argmax(cand, eid, float(N_EXPERTS))
        hit = eid == ei
        idxs.append(ei)
        hits.append(hit)
        ws.append(jnp.sum(jnp.where(hit, scores, 0.0), axis=0, keepdims=True))
        cand = jnp.where(hit, neg, cand)
    w = jnp.concatenate(ws, axis=0)
    w = w / (jnp.sum(w, axis=0, keepdims=True) + 1e-20) * ROUTED_SCALE
    idx_ref[...] = jnp.concatenate(idxs, axis=0).astype(jnp.int32)
    w_ref[...] = w
    chosen = jnp.zeros((N_EXPERTS, t), F32)
    for hit in hits:
        chosen = jnp.where(hit, 1.0, chosen)
    before = (lax.broadcasted_iota(jnp.int32, (t, t), 0) < lax.broadcasted_iota(jnp.int32, (t, t), 1))
    prior = jnp.dot(chosen.astype(BF16), jnp.where(before, 1.0, 0.0).astype(BF16),
                    preferred_element_type=F32) + cnt_ref[...]
    rank_ref[...] = jnp.concatenate(
        [jnp.sum(jnp.where(hit, prior, 0.0), axis=0, keepdims=True) for hit in hits], axis=0).astype(jnp.int32)
    cnt_ref[...] += jnp.sum(chosen, axis=1, keepdims=True)
    cnt_out_ref[0] = cnt_ref[...]


def _router(x, w_router, bias, tile_tokens, tr=512):
    t, d = x.shape
    tr = min(tr, tile_tokens)
    steps_per_tile = tile_tokens // tr
    n_tiles = t // tile_tokens
    kt = pl.BlockSpec((TOP_K, tr), lambda i: (0, i))
    idx, w, rank, counts = pl.pallas_call(
        functools.partial(_router_kernel, steps_per_tile=steps_per_tile),
        grid=(t // tr,),
        in_specs=[pl.BlockSpec((tr, d), lambda i: (i, 0)),
                  pl.BlockSpec((N_EXPERTS, d), lambda i: (0, 0)),
                  pl.BlockSpec((N_EXPERTS, 1), lambda i: (0, 0))],
        out_specs=[kt, kt, kt,
                   pl.BlockSpec((1, N_EXPERTS, 1), lambda i: (i // steps_per_tile, 0, 0))],
        out_shape=[jax.ShapeDtypeStruct((TOP_K, t), jnp.int32),
                   jax.ShapeDtypeStruct((TOP_K, t), F32),
                   jax.ShapeDtypeStruct((TOP_K, t), jnp.int32),
                   jax.ShapeDtypeStruct((n_tiles, N_EXPERTS, 1), F32)],
        scratch_shapes=[pltpu.VMEM((N_EXPERTS, 1), F32)],
        compiler_params=_cparams("arbitrary"),
        name="moe_router",
    )(x, w_router.T, bias.reshape(N_EXPERTS, 1))
    return idx, w, rank, counts.reshape(n_tiles, N_EXPERTS).astype(jnp.int32)


def _steps_per_tile(tile_tokens):
    return tile_tokens * TOP_K // STEP_ROWS + N_EXPERTS


def _plan_kernel(fill_lo_ref, fill_hi_ref, pos_ref, src_ref, *, tile_tokens):
    i = pl.program_id(0)
    n_assign = tile_tokens * TOP_K
    empty = n_assign
    group = 16

    def fill_range(g, carry):
        lo = fill_lo_ref[i * N_EXPERTS + g]
        hi = fill_hi_ref[i * N_EXPERTS + g]

        def fill(b, carry):
            for d in range(group):
                src_ref[0, 0, jnp.maximum(hi - 1 - b * group - d, 0)] = empty
            return carry
        return lax.fori_loop(0, (hi - lo + group - 1) // group, fill, carry)
    lax.fori_loop(0, N_EXPERTS, fill_range, 0)

    def place(b, carry):
        base = b * group
        slots = [pos_ref[0, 0, base + d] for d in range(group)]
        for d in range(group):
            src_ref[0, 0, slots[d]] = base + d
        return carry
    lax.fori_loop(0, n_assign // group, place, 0)


def _dispatch_plan(idx_t, w_t, rank_t, counts, tile_tokens):
    t = idx_t.shape[1]
    n_tiles = t // tile_tokens
    steps = _steps_per_tile(tile_tokens)
    padded = (counts + STEP_ROWS - 1) // STEP_ROWS * STEP_ROWS
    pad_end = jnp.cumsum(padded, axis=1)
    pad_start = (pad_end - padded).astype(jnp.int32)
    n_used = (pad_end[:, -1] // STEP_ROWS).astype(jnp.int32)
    starts = jnp.arange(steps, dtype=jnp.int32) * STEP_ROWS
    step_e = jnp.sum(starts[None, :, None] >= pad_end[:, None, :], axis=-1).astype(jnp.int32)
    step_e = jnp.minimum(step_e, N_EXPERTS - 1)
    last = jnp.take_along_axis(step_e, jnp.maximum(n_used - 1, 0)[:, None], axis=1)
    step_e = jnp.where(starts[None, :] // STEP_ROWS < n_used[:, None], step_e, last)

    start_of = jnp.repeat(pad_start.T, tile_tokens, axis=1)
    experts = jnp.arange(N_EXPERTS, dtype=jnp.int32)[None, :, None]
    pos_t = rank_t + jnp.sum(jnp.where(idx_t[:, None, :] == experts, start_of[None], 0), axis=1)
    fill_lo = (pad_start + counts).astype(jnp.int32)
    fill_hi = pad_end.astype(jnp.int32)

    n_assign = tile_tokens * TOP_K
    n_slots = steps * STEP_ROWS
    single = pl.Buffered(1)
    src = pl.pallas_call(
        functools.partial(_plan_kernel, tile_tokens=tile_tokens),
        grid_spec=pltpu.PrefetchScalarGridSpec(
            num_scalar_prefetch=2,
            grid=(n_tiles,),
            in_specs=[pl.BlockSpec((1, 1, n_assign), lambda i, lo, hi: (i, 0, 0),
                                   memory_space=pltpu.SMEM, pipeline_mode=single)],
            out_specs=pl.BlockSpec((1, 1, n_slots), lambda i, lo, hi: (i, 0, 0),
                                   memory_space=pltpu.SMEM, pipeline_mode=single),
        ),
        out_shape=jax.ShapeDtypeStruct((n_tiles, 1, n_slots), jnp.int32),
        compiler_params=_cparams("parallel"),
        name="moe_plan",
    )(fill_lo.reshape(-1), fill_hi.reshape(-1), pos_t.T.reshape(n_tiles, 1, n_assign))
    src = jnp.where(jnp.arange(n_slots, dtype=jnp.int32)[None, None, :] < pad_end[:, -1][:, None, None],
                    src, tile_tokens * TOP_K)
    rows = src & -SUBLANES
    w_flat = jnp.pad(w_t.T.reshape(n_tiles, tile_tokens * TOP_K), ((0, 0), (0, SUBLANES)))
    slot_w = jnp.take_along_axis(w_flat, src[:, 0, :], axis=1)
    held = jnp.clip(jnp.take_along_axis(fill_lo, step_e, axis=1) - starts[None, :], 0, STEP_ROWS)
    live = jnp.where(starts[None, :] // STEP_ROWS < n_used[:, None], (held + EXPERT_ROWS - 1) // EXPERT_ROWS, 0)
    return rows, slot_w, step_e.reshape(-1), n_used, live.reshape(-1).astype(jnp.int32), steps


def _experts_kernel(step_e_ref, n_used_ref, live_ref, row_ref, sw_ref, x_ref, wg_ref, wu_ref, wd_ref, acc_ref,
                    *stage_refs):
    n_sub = BLOCKS_PER_STEP
    xs_refs = stage_refs[:n_sub]
    ys_refs = stage_refs[n_sub:]
    i = pl.program_id(0)
    j = pl.program_id(1)
    rows = EXPERT_ROWS
    pitch = STAGE_PITCH

    @pl.when(j == 0)
    def _():
        acc_ref[...] = jnp.zeros_like(acc_ref)

    last_row = x_ref.shape[1] - SUBLANES

    def token_rows(slot, limit=None):
        row = row_ref[0, 0, slot]
        if limit is not None:
            row = jnp.minimum(row, limit)
        return pl.ds(pl.multiple_of(row, SUBLANES), SUBLANES)

    def run(subs):
        for s in subs:
            for r in range(rows):
                xs_refs[s][pl.ds(r, ROW_CHUNKS, stride=pitch), :] = x_ref[0, token_rows(s * rows + r, last_row), :]
        eye = (lax.broadcasted_iota(jnp.int32, (rows, rows), 0)
               == lax.broadcasted_iota(jnp.int32, (rows, rows), 1))
        ys = {}
        for s in subs:
            x = jnp.concatenate([xs_refs[s][pl.ds(c * pitch, rows), :] for c in range(ROW_CHUNKS)], axis=1)
            hid = _silu(_bdot(x, wg_ref[0, 0])) * _bdot(x, wu_ref[0, 0])
            w_row = sw_ref[0, :, s * rows:(s + 1) * rows]
            w_col = jnp.sum(jnp.where(eye, w_row, 0.0), axis=1, keepdims=True)
            ys[s] = _bdot(hid, wd_ref[0, 0]) * w_col
        for s in subs:
            for c in range(ROW_CHUNKS):
                ys_refs[s][pl.ds(c * pitch, rows), :] = ys[s][:, c * LANES:(c + 1) * LANES]
        for s in subs:
            for r0 in range(0, rows, SCATTER_BATCH):
                batch = range(r0, r0 + SCATTER_BATCH)
                ats = [token_rows(s * rows + r) for r in batch]
                new = [acc_ref[0, at, :] + ys_refs[s][pl.ds(r, ROW_CHUNKS, stride=pitch), :]
                       for at, r in zip(ats, batch)]
                for at, val in zip(ats, new):
                    acc_ref[0, at, :] = val

    live = live_ref[i * pl.num_programs(1) + j]
    for n_live in range(1, n_sub + 1):
        pl.when(live == n_live)(functools.partial(run, range(n_live)))


def _routed_experts(x_slabs, idx_t, w_t, rank_t, counts, w_gate_up, w_down, layer, tile_tokens):
    d = D_MODEL
    t = x_slabs.shape[0] // ROW_CHUNKS
    n_tiles = t // tile_tokens
    rows, slot_w, step_e, n_used, live, steps = _dispatch_plan(idx_t, w_t, rank_t, counts, tile_tokens)
    rows = rows.reshape(n_tiles * steps, 1, STEP_ROWS)
    slot_w = slot_w.reshape(n_tiles * steps, 1, STEP_ROWS)
    x_slabs = x_slabs.reshape(n_tiles, tile_tokens * ROW_CHUNKS, LANES)
    slab_rows = (tile_tokens + 1) * ROW_CHUNKS

    def used_step(i, j, se, nu, lv):
        return (i * steps + jnp.minimum(j, jnp.maximum(nu[i] - 1, 0)), 0, 0)

    pick = lambda i, j, se, nu, lv: (layer, se[i * steps + j], 0, 0)
    tile = lambda i, j, se, nu, lv: (i, 0, 0)
    resident = pl.Buffered(1)
    stage = pltpu.VMEM((ROW_CHUNKS * STAGE_PITCH, LANES), F32)
    grid_spec = pltpu.PrefetchScalarGridSpec(
        num_scalar_prefetch=3,
        grid=(n_tiles, steps),
        in_specs=[
            pl.BlockSpec((1, 1, STEP_ROWS), used_step, memory_space=pltpu.SMEM),
            pl.BlockSpec((1, 1, STEP_ROWS), used_step),
            pl.BlockSpec((1, tile_tokens * ROW_CHUNKS, LANES), tile, pipeline_mode=resident),
            pl.BlockSpec((1, 1, d, D_EXPERT), pick),
            pl.BlockSpec((1, 1, d, D_EXPERT), pick),
            pl.BlockSpec((1, 1, D_EXPERT, d), pick),
        ],
        out_specs=pl.BlockSpec((1, slab_rows, LANES), tile, pipeline_mode=resident),
        scratch_shapes=[stage] * (2 * BLOCKS_PER_STEP),
    )
    return pl.pallas_call(
        _experts_kernel,
        grid_spec=grid_spec,
        out_shape=jax.ShapeDtypeStruct((n_tiles, slab_rows, LANES), F32),
        compiler_params=_cparams("parallel", "arbitrary"),
        name="moe_experts",
    )(step_e, n_used, live, rows, slot_w, x_slabs, *w_gate_up, w_down)


def _moe_out_kernel(x_ref, r_ref, wg_ref, wu_ref, wd_ref, g_ref, b_ref, o_ref):
    x = x_ref[...]
    tm = x.shape[0]
    hid = _silu(_bdot(x, wg_ref[...])) * _bdot(x, wu_ref[...])
    routed = jnp.concatenate([r_ref[0, pl.ds(c, tm, stride=ROW_CHUNKS), :] for c in range(ROW_CHUNKS)], axis=1)
    y = routed + _bdot(hid, wd_ref[...])
    o_ref[...] = _layer_norm(DN_ALPHA * x + y, g_ref[...], b_ref[...])


def _moe_out(x, routed, ws_gate, ws_up, ws_down, g, b, tile_tokens, tm=512):
    t, d = x.shape
    tm = min(tm, tile_tokens)
    per_tile = tile_tokens // tm
    return pl.pallas_call(
        _moe_out_kernel,
        grid=(t // tm,),
        in_specs=[pl.BlockSpec((tm, d), lambda i: (i, 0)),
                  pl.BlockSpec((1, tm * ROW_CHUNKS, LANES), lambda i: (i // per_tile, i % per_tile, 0)),
                  pl.BlockSpec((d, D_SHARED), lambda i: (0, 0)),
                  pl.BlockSpec((d, D_SHARED), lambda i: (0, 0)),
                  pl.BlockSpec((D_SHARED, d), lambda i: (0, 0)),
                  pl.BlockSpec((1, d), lambda i: (0, 0)),
                  pl.BlockSpec((1, d), lambda i: (0, 0))],
        out_specs=pl.BlockSpec((tm, d), lambda i: (i, 0)),
        out_shape=jax.ShapeDtypeStruct((t, d), F32),
        compiler_params=_cparams("parallel"),
        name="moe_out",
    )(x, routed, ws_gate.astype(BF16), ws_up.astype(BF16), ws_down.astype(BF16),
      g.reshape(1, d), b.reshape(1, d))


def _moe_ln(hid, hid_slabs, w_router, router_bias, w_gate_up, w_down, layer, ws_gate, ws_up, ws_down, ln_g, ln_b):
    b, l, d = hid.shape
    x = hid.reshape(b * l, d)
    tile_tokens = min(MOE_TILE_TOKENS, b * l)
    idx_t, w_t, rank_t, counts = _router(x, w_router, router_bias, tile_tokens)
    routed = _routed_experts(hid_slabs, idx_t, w_t, rank_t, counts, w_gate_up, w_down, layer, tile_tokens)
    return _moe_out(x, routed, ws_gate, ws_up, ws_down, ln_g, ln_b, tile_tokens).reshape(b, l, d)


def kernel(x, a_w_in, a_conv_w, a_A_log, a_dt_bias, a_onorm_w, a_w_out, b_w_dq, b_qnorm_w, b_w_uq, b_w_o,
           kv_w_dkv, kv_norm_w, kv_w_ukv, ln1_g, ln1_b, ln2_g, ln2_b, moe_w_router, moe_router_bias,
           moe_w_gate, moe_w_up, moe_w_down, moe_ws_gate, moe_ws_up, moe_ws_down):
    l = x.shape[1]
    cos_t, sin_t = _rope_lane_tables(l)
    w_gate_up = (moe_w_gate.astype(BF16), moe_w_up.astype(BF16))
    w_down = moe_w_down.astype(BF16)
    h = x
    k = v = None
    for layer in range(DEPTH):
        if layer < N_A_LAYERS:
            i = layer
            h, slabs = _gated_deltanet_ln(h, a_w_in[i], a_conv_w[i], a_A_log[i], a_dt_bias[i], a_onorm_w[i],
                                          a_w_out[i], ln1_g[layer], ln1_b[layer])
        else:
            i = layer - N_A_LAYERS
            h, slabs = _mla_ln(h, b_w_dq[i], b_qnorm_w[i], b_w_uq[i], b_w_o[i], k, v, cos_t, sin_t,
                               ln1_g[layer], ln1_b[layer])
        h = _moe_ln(h, slabs, moe_w_router[layer], moe_router_bias[layer], w_gate_up, w_down, layer,
                    moe_ws_gate[layer], moe_ws_up[layer], moe_ws_down[layer], ln2_g[layer], ln2_b[layer])
        if layer == N_A_LAYERS - 1:
            k, v = _mla_kv(h, kv_w_dkv, kv_norm_w, kv_w_ukv, cos_t, sin_t)
    return h
```

```python
import functools

import jax
import jax.numpy as jnp
import numpy as np
from jax import lax
from jax.experimental import pallas as pl
from jax.experimental.pallas import tpu as pltpu

F32 = jnp.float32
BF16 = jnp.bfloat16
HIGHEST = lax.Precision.HIGHEST

D_MODEL = 1024
DEPTH = 4
N_A_LAYERS = DEPTH // 2
GDN_HEADS = 8
GDN_DK = 128
GDN_DV = 128
D_QK = GDN_HEADS * GDN_DK
D_VA = GDN_HEADS * GDN_DV
GDN_CONV_CH = 2 * D_QK + D_VA
CONV_WIDTH = 4
CHUNK = 64
MLA_HEADS = 8
Q_LORA = 512
KV_LORA = 256
D_NOPE = 128
D_ROPE = 64
D_V = 128
ROPE_THETA = 10000.0
N_EXPERTS = 64
TOP_K = 8
N_GROUPS = 8
GROUP_SIZE = N_EXPERTS // N_GROUPS
TOPK_GROUPS = 4
D_EXPERT = 256
D_SHARED = 256
ROUTED_SCALE = 2.5
DN_ALPHA = (2 * DEPTH) ** 0.25
EPS = 1e-6
LN_EPS = 1e-5
LOG2_E = 1.4426950408889634

LANES = 128
SUBLANES = 8
VMEM_LIMIT_BYTES = 56 * 1024 * 1024

EXPERT_ROWS = 128
ROW_CHUNKS = D_MODEL // LANES
STAGE_PITCH = EXPERT_ROWS + 1
SCATTER_BATCH = 8
BLOCKS_PER_STEP = 2
STEP_ROWS = BLOCKS_PER_STEP * EXPERT_ROWS
MOE_TILE_TOKENS = 4096
ATTN_ROW_GROUPS = 4
assert TOP_K == SUBLANES
assert D_QK == D_VA


def _cparams(*sem):
    return pltpu.CompilerParams(dimension_semantics=sem, vmem_limit_bytes=VMEM_LIMIT_BYTES)


def _silu(x):
    return x * jax.nn.sigmoid(x)


def _layer_norm(x, g, b):
    mu = jnp.mean(x, -1, keepdims=True)
    xc = x - mu
    var = jnp.mean(xc * xc, -1, keepdims=True)
    return xc * lax.rsqrt(var + LN_EPS) * g + b


def _rms_norm(x, w):
    return x * lax.rsqrt(jnp.mean(x * x, -1, keepdims=True) + EPS) * w


def _bdot(a, b):
    return jnp.dot(a.astype(BF16), b.astype(BF16), preferred_element_type=F32)


def _bdot_nt(a, b):
    return lax.dot_general(a.astype(BF16), b.astype(BF16), (((1,), (1,)), ((), ())),
                           preferred_element_type=F32)


def _hdot(a, b):
    return jnp.dot(a, b, precision=HIGHEST, preferred_element_type=F32)


def _matmul_ln_kernel(x_ref, w_ref, r_ref, g_ref, b_ref, o_ref, slab_ref):
    mix = _bdot(x_ref[...], w_ref[...])
    out = _layer_norm(DN_ALPHA * r_ref[...] + mix, g_ref[...], b_ref[...])
    o_ref[...] = out
    tm = out.shape[0]
    for c in range(ROW_CHUNKS):
        slab_ref[pl.ds(c, tm, stride=ROW_CHUNKS), :] = out[:, c * LANES:(c + 1) * LANES]


def _matmul_ln(x, w, res, g, b, tm=512):
    m, k = x.shape
    n = w.shape[1]
    tm = min(tm, m)
    chunks = n // LANES
    return pl.pallas_call(
        _matmul_ln_kernel,
        grid=(m // tm,),
        in_specs=[pl.BlockSpec((tm, k), lambda i: (i, 0)),
                  pl.BlockSpec((k, n), lambda i: (0, 0)),
                  pl.BlockSpec((tm, n), lambda i: (i, 0)),
                  pl.BlockSpec((1, n), lambda i: (0, 0)),
                  pl.BlockSpec((1, n), lambda i: (0, 0))],
        out_specs=[pl.BlockSpec((tm, n), lambda i: (i, 0)),
                   pl.BlockSpec((tm * chunks, LANES), lambda i: (i, 0))],
        out_shape=[jax.ShapeDtypeStruct((m, n), F32),
                   jax.ShapeDtypeStruct((m * chunks, LANES), F32)],
        compiler_params=_cparams("parallel"),
        name="matmul_ln",
    )(x, w, res, g.reshape(1, n), b.reshape(1, n))


def _softplus(x):
    return jnp.maximum(x, 0.0) + jnp.log1p(jnp.exp(-jnp.abs(x)))


def _gdn_gates_kernel(x_ref, wab_ref, wabt_ref, alog_ref, dtb_ref, alogt_ref, dtbt_ref,
                      gc_ref, beta_ref, gct_ref):
    x = x_ref[...]
    tl = x.shape[0]
    h = GDN_HEADS
    ab = _bdot(x, wab_ref[...])
    abt = _bdot_nt(wabt_ref[...], x)
    g = -jnp.exp(alog_ref[...]) * _softplus(ab[:, :h] + dtb_ref[...])
    gt = -jnp.exp(alogt_ref[...]) * _softplus(abt[:h, :] + dtbt_ref[...])
    beta_ref[...] = jax.nn.sigmoid(ab[:, h:2 * h])
    row = lax.broadcasted_iota(jnp.int32, (tl, tl), 0)
    col = lax.broadcasted_iota(jnp.int32, (tl, tl), 1)
    shift = CHUNK.bit_length() - 1
    same = (row >> shift) == (col >> shift)
    lower = jnp.where(same & (col <= row), 1.0, 0.0).astype(F32)
    upper = jnp.where(same & (row <= col), 1.0, 0.0).astype(F32)
    gc_ref[...] = _hdot(lower, g)
    gct_ref[...] = _hdot(gt, upper)


def _gdn_gates(x, w_ab, a_log, dt_bias, tl=512):
    t = x.shape[0]
    tl = min(tl, t)
    h = GDN_HEADS
    wab = jnp.zeros((D_MODEL, LANES), F32).at[:, :2 * h].set(w_ab)
    wabt = w_ab.T
    return pl.pallas_call(
        _gdn_gates_kernel,
        grid=(t // tl,),
        in_specs=[pl.BlockSpec((tl, D_MODEL), lambda i: (i, 0)),
                  pl.BlockSpec((D_MODEL, LANES), lambda i: (0, 0)),
                  pl.BlockSpec((2 * h, D_MODEL), lambda i: (0, 0)),
                  pl.BlockSpec((1, h), lambda i: (0, 0)),
                  pl.BlockSpec((1, h), lambda i: (0, 0)),
                  pl.BlockSpec((h, 1), lambda i: (0, 0)),
                  pl.BlockSpec((h, 1), lambda i: (0, 0))],
        out_specs=[pl.BlockSpec((tl, h), lambda i: (i, 0)),
                   pl.BlockSpec((tl, h), lambda i: (i, 0)),
                   pl.BlockSpec((h, tl), lambda i: (0, i))],
        out_shape=[jax.ShapeDtypeStruct((t, h), F32),
                   jax.ShapeDtypeStruct((t, h), F32),
                   jax.ShapeDtypeStruct((h, t), F32)],
        compiler_params=_cparams("parallel"),
        name="gdn_gates",
    )(x, wab, wabt, a_log.reshape(1, h), dt_bias.reshape(1, h),
      a_log.reshape(h, 1), dt_bias.reshape(h, 1))


def _gdn_inproj_kernel(x_ref, prev_ref, w_ref, cw_ref, o_ref):
    i = pl.program_id(1)
    j = pl.program_id(2)
    tm = x_ref.shape[1]
    xb = jnp.concatenate([prev_ref[0], x_ref[0]], axis=0).astype(BF16)
    p = jnp.dot(xb, w_ref[...], preferred_element_type=F32)

    @pl.when(j == 3)
    def _():
        o_ref[0] = p[SUBLANES:]

    @pl.when(j < 3)
    def _():
        context = jnp.where(i > 0, p[:SUBLANES], 0.0)
        xx = jnp.concatenate([context, p[SUBLANES:]], axis=0)
        cw = cw_ref[...]
        y = xx[SUBLANES:] * cw[CONV_WIDTH - 1:CONV_WIDTH, :]
        for back in range(1, CONV_WIDTH):
            tap = CONV_WIDTH - 1 - back
            y = y + pltpu.roll(xx, back, axis=0)[SUBLANES:] * cw[tap:tap + 1, :]
        y = _silu(y)

        @pl.when(j == 2)
        def _():
            o_ref[0] = y

        @pl.when(j < 2)
        def _():
            scale = jnp.where(j == 0, GDN_DK ** -0.5, 1.0)
            for h in range(GDN_HEADS):
                sl = slice(h * GDN_DK, (h + 1) * GDN_DK)
                yh = y[:, sl]
                o_ref[0, :, sl] = yh * lax.rsqrt(jnp.sum(yh * yh, -1, keepdims=True) + EPS) * scale


def _gdn_inproj(hid, w_qkvz, conv_w, tm=1024):
    b, l, d = hid.shape
    tm = min(tm, l)
    per = tm // SUBLANES
    n_col = w_qkvz.shape[1] // D_QK
    return pl.pallas_call(
        _gdn_inproj_kernel,
        grid=(b, l // tm, n_col),
        in_specs=[pl.BlockSpec((1, tm, d), lambda bi, i, j: (bi, i, 0)),
                  pl.BlockSpec((1, SUBLANES, d), lambda bi, i, j: (bi, jnp.maximum(i * per - 1, 0), 0)),
                  pl.BlockSpec((d, D_QK), lambda bi, i, j: (0, j)),
                  pl.BlockSpec((CONV_WIDTH, D_QK), lambda bi, i, j: (0, jnp.minimum(j, n_col - 2)))],
        out_specs=pl.BlockSpec((1, tm, D_QK), lambda bi, i, j: (bi, i, j)),
        out_shape=jax.ShapeDtypeStruct((b, l, n_col * D_QK), F32),
        compiler_params=_cparams("parallel", "parallel", "arbitrary"),
        name="gdn_inproj",
    )(hid, hid, w_qkvz, conv_w)


def _split_bf16(x):
    hi = x.astype(BF16)
    lo = (x - hi.astype(F32)).astype(BF16)
    return hi, lo


def _dot_split(a, b):
    (ah, al), (bh, bl) = a, b
    dot = functools.partial(jnp.dot, preferred_element_type=F32)
    return dot(ah, bh) + (dot(ah, bl) + dot(al, bh))


def _gdn_prep_kernel(q_ref, k_ref, v_ref, gc_ref, beta_ref, gct_ref, wq_ref, u_ref, attn_ref, kgt_ref,
                     *, chunks):
    c = CHUNK
    row = lax.broadcasted_iota(jnp.int32, (c, c), 0)
    col = lax.broadcasted_iota(jnp.int32, (c, c), 1)
    incl = row >= col
    strict = row > col
    eye = jnp.where(row == col, 1.0, 0.0).astype(F32)
    items = [(ci, h) for ci in range(chunks) for h in range(GDN_HEADS)]

    st = {}
    for ci, h in items:
        rows = slice(ci * c, (ci + 1) * c)
        sl = slice(h * GDN_DK, (h + 1) * GDN_DK)
        q = q_ref[0, rows, sl]
        k = k_ref[0, rows, sl]
        v = v_ref[0, rows, sl]
        gcol = gc_ref[0, rows, h:h + 1]
        grow = gct_ref[0, ci, h:h + 1, :]
        bcol = beta_ref[0, rows, h:h + 1]
        decay = jnp.where(incl, jnp.exp(jnp.where(incl, gcol - grow, 0.0)), 0.0)
        kb = k * bcol
        eg = jnp.exp(gcol)
        g_last = gcol[c - 1:c, :]
        sc = _bdot_nt(jnp.concatenate([kb, q], axis=0), k)
        a = jnp.where(strict, sc[:c] * decay, 0.0)
        attn_ref[0, ci, h] = (sc[c:] * decay).astype(attn_ref.dtype)
        kgt_ref[0, ci, h] = (k * jnp.exp(g_last - gcol)).T.astype(kgt_ref.dtype)
        wq_ref[0, ci, h, c:, :] = (q * eg).astype(wq_ref.dtype)
        st[ci, h] = dict(m=-a, rhs=jnp.concatenate([v * bcol, kb * eg], axis=1))

    for it in items:
        st[it]["p"] = eye + st[it]["m"]
    span = 2
    while span < c:
        for it in items:
            ms = _split_bf16(st[it]["m"])
            st[it]["m"] = _dot_split(ms, ms)
        for it in items:
            st[it]["p"] = st[it]["p"] + _dot_split(_split_bf16(st[it]["p"]), _split_bf16(st[it]["m"]))
        span *= 2

    for ci, h in items:
        sol = _dot_split(_split_bf16(st[ci, h]["p"]), _split_bf16(st[ci, h]["rhs"]))
        u_ref[0, ci * c:(ci + 1) * c, h * GDN_DV:(h + 1) * GDN_DV] = sol[:, :GDN_DV]
        wq_ref[0, ci, h, :c, :] = sol[:, GDN_DV:].astype(wq_ref.dtype)


def _gdn_prep(qkvz, gc, beta, gct, chunks=2):
    b, l, _ = qkvz.shape
    n = l // CHUNK
    chunks = min(chunks, n)
    h = GDN_HEADS
    rows = chunks * CHUNK
    col = lambda c: pl.BlockSpec((1, rows, D_QK), lambda bi, i: (bi, i, c))
    gate = pl.BlockSpec((1, rows, h), lambda bi, i: (bi, i, 0))
    q = k = v = qkvz
    return pl.pallas_call(
        functools.partial(_gdn_prep_kernel, chunks=chunks),
        grid=(b, n // chunks),
        in_specs=[col(0), col(1), col(2), gate, gate,
                  pl.BlockSpec((1, chunks, h, CHUNK), lambda bi, i: (bi, i, 0, 0))],
        out_specs=[pl.BlockSpec((1, chunks, h, 2 * CHUNK, GDN_DK), lambda bi, i: (bi, i, 0, 0, 0)),
                   pl.BlockSpec((1, rows, D_VA), lambda bi, i: (bi, i, 0)),
                   pl.BlockSpec((1, chunks, h, CHUNK, CHUNK), lambda bi, i: (bi, i, 0, 0, 0)),
                   pl.BlockSpec((1, chunks, h, GDN_DK, CHUNK), lambda bi, i: (bi, i, 0, 0, 0))],
        out_shape=[jax.ShapeDtypeStruct((b, n, h, 2 * CHUNK, GDN_DK), BF16),
                   jax.ShapeDtypeStruct((b, l, D_VA), F32),
                   jax.ShapeDtypeStruct((b, n, h, CHUNK, CHUNK), BF16),
                   jax.ShapeDtypeStruct((b, n, h, GDN_DK, CHUNK), BF16)],
        compiler_params=_cparams("parallel", "parallel"),
        name="gdn_prep",
    )(q, k, v, gc, beta, gct)


def _gdn_state_kernel(wq_ref, u_ref, attn_ref, kgt_ref, gc_ref, z_ref, onw_ref, o_ref, s_ref, *, chunks):
    @pl.when(pl.program_id(1) == 0)
    def _():
        s_ref[...] = jnp.zeros_like(s_ref)

    c = CHUNK
    onw = onw_ref[...]
    dot = functools.partial(jnp.dot, preferred_element_type=F32)
    heads = range(GDN_HEADS)
    for ci in range(chunks):
        rows = slice(ci * c, (ci + 1) * c)
        s_old = [s_ref[h] for h in heads]
        r = [dot(wq_ref[0, ci, h], s_old[h].astype(BF16)) for h in heads]
        v_new = [(u_ref[0, rows, h * GDN_DV:(h + 1) * GDN_DV] - r[h][:c]).astype(BF16) for h in heads]
        for h in heads:
            decay_last = jnp.exp(gc_ref[0, (ci + 1) * c - 1:(ci + 1) * c, h:h + 1])
            s_ref[h] = s_old[h] * decay_last + dot(kgt_ref[0, ci, h], v_new[h])
        for h in heads:
            sl = slice(h * GDN_DV, (h + 1) * GDN_DV)
            o = r[h][c:] + dot(attn_ref[0, ci, h], v_new[h])
            o_ref[0, rows, sl] = (_rms_norm(o, onw) * _silu(z_ref[0, rows, sl])).astype(o_ref.dtype)


def _gdn_state(wq, u, attn, kgt, gc, proj, onorm_w, chunks=4):
    b, n, h = wq.shape[:3]
    l = n * CHUNK
    chunks = min(chunks, n)
    rows = chunks * CHUNK
    z_block = GDN_CONV_CH // D_VA
    return pl.pallas_call(
        functools.partial(_gdn_state_kernel, chunks=chunks),
        grid=(b, n // chunks),
        in_specs=[pl.BlockSpec((1, chunks, h, 2 * CHUNK, GDN_DK), lambda bi, i: (bi, i, 0, 0, 0)),
                  pl.BlockSpec((1, rows, D_VA), lambda bi, i: (bi, i, 0)),
                  pl.BlockSpec((1, chunks, h, CHUNK, CHUNK), lambda bi, i: (bi, i, 0, 0, 0)),
                  pl.BlockSpec((1, chunks, h, GDN_DK, CHUNK), lambda bi, i: (bi, i, 0, 0, 0)),
                  pl.BlockSpec((1, rows, h), lambda bi, i: (bi, i, 0)),
                  pl.BlockSpec((1, rows, D_VA), lambda bi, i: (bi, i, z_block)),
                  pl.BlockSpec((1, GDN_DV), lambda bi, i: (0, 0))],
        out_specs=pl.BlockSpec((1, rows, D_VA), lambda bi, i: (bi, i, 0)),
        out_shape=jax.ShapeDtypeStruct((b, l, D_VA), BF16),
        scratch_shapes=[pltpu.VMEM((h, GDN_DK, GDN_DV), F32)],
        compiler_params=_cparams("parallel", "arbitrary"),
        name="gdn_state",
    )(wq, u, attn, kgt, gc, proj, onorm_w.reshape(1, GDN_DV))


def _gated_deltanet_ln(hid, w_in, conv_w, a_log, dt_bias, onorm_w, w_out, ln_g, ln_b):
    b, l, d = hid.shape
    h = GDN_HEADS
    n = l // CHUNK
    x = hid.reshape(b * l, d)
    n_main = GDN_CONV_CH + D_VA
    qkvz = _gdn_inproj(hid, w_in[:, :n_main].astype(BF16), conv_w)
    gc, beta, gct = _gdn_gates(x, w_in[:, n_main:], a_log, dt_bias)
    gc = gc.reshape(b, l, h)
    gct = gct.reshape(h, b, n, CHUNK).transpose(1, 2, 0, 3)
    wq, u, attn, kgt = _gdn_prep(qkvz, gc, beta.reshape(b, l, h), gct)
    o = _gdn_state(wq, u, attn, kgt, gc, qkvz, onorm_w)
    out, slabs = _matmul_ln(o.reshape(b * l, D_VA), w_out.astype(BF16), x, ln_g, ln_b)
    return out.reshape(b, l, d), slabs


def _rope_lane_tables(l):
    inv = 1.0 / (ROPE_THETA ** (jnp.arange(0, D_ROPE, 2, dtype=F32) / D_ROPE))
    ang = jnp.arange(l, dtype=F32)[:, None] * inv[None, :]
    cos, sin = jnp.cos(ang), jnp.sin(ang)
    zero = jnp.zeros((l, LANES - D_ROPE), F32)
    return (jnp.concatenate([cos, cos, zero], -1),
            jnp.concatenate([-sin, sin, zero], -1))


def _rope_weight_groups(w_rope):
    kdim = w_rope.shape[0]
    half = D_ROPE // 2
    zero = jnp.zeros((kdim, LANES - D_ROPE), w_rope.dtype)
    x1, x2 = w_rope[:, :half], w_rope[:, half:]
    return jnp.concatenate([x1, x2, zero, x2, x1, zero], axis=-1)


def _mla_kv_kernel(x_ref, wd_ref, nw_ref, wu_ref, cos_ref, sin_ref, k_ref, v_ref):
    x = x_ref[0]
    ckv = _bdot(x, wd_ref[...])
    c = _rms_norm(ckv[:, :KV_LORA], nw_ref[...])
    k_rope_t = (ckv[:, KV_LORA:KV_LORA + LANES] * cos_ref[...]
                + ckv[:, KV_LORA + LANES:] * sin_ref[...]).T.astype(k_ref.dtype)
    kv = _bdot(c, wu_ref[...])
    per = D_NOPE + D_V
    ones_col = jnp.where(lax.broadcasted_iota(jnp.int32, (x.shape[0], LANES), 1) == 0, 1.0, 0.0).astype(v_ref.dtype)
    for h in range(MLA_HEADS):
        k_ref[0, h, :D_NOPE, :] = kv[:, h * per:h * per + D_NOPE].T.astype(k_ref.dtype)
        k_ref[0, h, D_NOPE:, :] = k_rope_t
        v_ref[0, h, :, :D_V] = kv[:, h * per + D_NOPE:(h + 1) * per].astype(v_ref.dtype)
        v_ref[0, h, :, D_V:] = ones_col


def _mla_kv(hid, w_dkv, kv_norm_w, w_ukv, cos_t, sin_t, tl=512):
    b, l, d = hid.shape
    tl = min(tl, l)
    wd = jnp.concatenate([w_dkv[:, :KV_LORA], _rope_weight_groups(w_dkv[:, KV_LORA:])], -1).astype(BF16)
    nd = wd.shape[1]
    hh = MLA_HEADS
    return pl.pallas_call(
        _mla_kv_kernel,
        grid=(b, l // tl),
        in_specs=[pl.BlockSpec((1, tl, d), lambda bi, i: (bi, i, 0)),
                  pl.BlockSpec((d, nd), lambda bi, i: (0, 0)),
                  pl.BlockSpec((1, KV_LORA), lambda bi, i: (0, 0)),
                  pl.BlockSpec((KV_LORA, hh * (D_NOPE + D_V)), lambda bi, i: (0, 0)),
                  pl.BlockSpec((tl, LANES), lambda bi, i: (i, 0)),
                  pl.BlockSpec((tl, LANES), lambda bi, i: (i, 0))],
        out_specs=[pl.BlockSpec((1, hh, D_NOPE + LANES, tl), lambda bi, i: (bi, 0, 0, i)),
                   pl.BlockSpec((1, hh, tl, D_V + LANES), lambda bi, i: (bi, 0, i, 0))],
        out_shape=[jax.ShapeDtypeStruct((b, hh, D_NOPE + LANES, l), BF16),
                   jax.ShapeDtypeStruct((b, hh, l, D_V + LANES), BF16)],
        compiler_params=_cparams("parallel", "parallel"),
        name="mla_kv",
    )(hid, wd, kv_norm_w.reshape(1, KV_LORA), w_ukv.astype(BF16), cos_t, sin_t)


def _mla_cq_kernel(x_ref, w_ref, nw_ref, o_ref):
    o_ref[...] = _rms_norm(_bdot(x_ref[...], w_ref[...]), nw_ref[...]).astype(o_ref.dtype)


def _mla_cq(x, w_dq, qnorm_w, tm=512):
    m, k = x.shape
    tm = min(tm, m)
    return pl.pallas_call(
        _mla_cq_kernel,
        grid=(m // tm,),
        in_specs=[pl.BlockSpec((tm, k), lambda i: (i, 0)),
                  pl.BlockSpec((k, Q_LORA), lambda i: (0, 0)),
                  pl.BlockSpec((1, Q_LORA), lambda i: (0, 0))],
        out_specs=pl.BlockSpec((tm, Q_LORA), lambda i: (i, 0)),
        out_shape=jax.ShapeDtypeStruct((m, Q_LORA), BF16),
        compiler_params=_cparams("parallel"),
        name="mla_cq",
    )(x, w_dq.astype(BF16), qnorm_w.reshape(1, Q_LORA))


def _mla_q_kernel(c_ref, w_ref, cos_ref, sin_ref, q_ref):
    c = c_ref[0]
    scale = (D_NOPE + D_ROPE) ** -0.5 * LOG2_E
    per = D_NOPE + 2 * LANES
    for h in range(MLA_HEADS):
        qh = _bdot(c, w_ref[:, h * per:(h + 1) * per])
        rope = qh[:, D_NOPE:D_NOPE + LANES] * cos_ref[...] + qh[:, D_NOPE + LANES:] * sin_ref[...]
        q_ref[0, h, :, :D_NOPE] = (qh[:, :D_NOPE] * scale).astype(q_ref.dtype)
        q_ref[0, h, :, D_NOPE:] = (rope * scale).astype(q_ref.dtype)


def _mla_q(cq, w_uq, cos_t, sin_t, tl=512):
    b, l, _ = cq.shape
    tl = min(tl, l)
    hh = MLA_HEADS
    per_in = D_NOPE + D_ROPE
    groups = []
    for h in range(hh):
        wh = w_uq[:, h * per_in:(h + 1) * per_in]
        groups += [wh[:, :D_NOPE], _rope_weight_groups(wh[:, D_NOPE:])]
    w = jnp.concatenate(groups, -1).astype(BF16)
    return pl.pallas_call(
        _mla_q_kernel,
        grid=(b, l // tl),
        in_specs=[pl.BlockSpec((1, tl, Q_LORA), lambda bi, i: (bi, i, 0)),
                  pl.BlockSpec(w.shape, lambda bi, i: (0, 0)),
                  pl.BlockSpec((tl, LANES), lambda bi, i: (i, 0)),
                  pl.BlockSpec((tl, LANES), lambda bi, i: (i, 0))],
        out_specs=pl.BlockSpec((1, hh, tl, D_NOPE + LANES), lambda bi, i: (bi, 0, i, 0)),
        out_shape=jax.ShapeDtypeStruct((b, hh, l, D_NOPE + LANES), BF16),
        compiler_params=_cparams("parallel", "parallel"),
        name="mla_q",
    )(cq, w, cos_t, sin_t)


def _mla_attn_kernel(q_ref, k_ref, v_ref, o_ref, *, tk):
    qi = pl.program_id(2)
    q = q_ref[0, 0]
    tq = q.shape[0]
    per = tq // tk

    groups = ATTN_ROW_GROUPS
    rows = tq // groups
    qs = [q[g * rows:(g + 1) * rows] for g in range(groups)]

    def block(j, carry, mask_offset=None):
        at = pl.ds(pl.multiple_of(j * tk, tk), tk)
        kb = k_ref[0, 0, :, at]
        vb = v_ref[0, 0, at, :]
        ss = [_bdot(qg, kb) for qg in qs]
        if mask_offset is not None:
            qpos = lax.broadcasted_iota(jnp.int32, (rows, tk), 0)
            kpos = lax.broadcasted_iota(jnp.int32, (rows, tk), 1)
            ss = [jnp.where(kpos + mask_offset <= qpos + g * rows, s, -jnp.inf) for g, s in enumerate(ss)]
        m_new = [jnp.maximum(c[0], jnp.max(s, -1, keepdims=True)) for c, s in zip(carry, ss)]
        ps = [jnp.exp2(s - m) for s, m in zip(ss, m_new)]
        return tuple((mn, jnp.exp2(m - mn) * acc + _bdot(p, vb)) for (m, acc), mn, p in zip(carry, m_new, ps))

    first = qi * per
    init = (jnp.full((rows, 1), -jnp.inf, F32), jnp.zeros((rows, D_V + LANES), F32))
    carry = lax.fori_loop(0, first, block, (init,) * groups)
    for d in range(per):
        carry = block(first + d, carry, mask_offset=d * tk)
    for g, (_, acc) in enumerate(carry):
        o_ref[0, g * rows:(g + 1) * rows, :] = (acc[:, :D_V] / acc[:, D_V:D_V + 1]).astype(o_ref.dtype)


def _mla_attn(q, k, v, tq=1024, tk=1024):
    b, hh, l, dq = q.shape
    tq = min(tq, l)
    tk = min(tk, tq)
    return pl.pallas_call(
        functools.partial(_mla_attn_kernel, tk=tk),
        grid=(b, hh, l // tq),
        in_specs=[pl.BlockSpec((1, 1, tq, dq), lambda bi, h, i: (bi, h, i, 0)),
                  pl.BlockSpec((1, 1, dq, l), lambda bi, h, i: (bi, h, 0, 0)),
                  pl.BlockSpec((1, 1, l, D_V + LANES), lambda bi, h, i: (bi, h, 0, 0))],
        out_specs=pl.BlockSpec((1, tq, D_V), lambda bi, h, i: (bi, i, h)),
        out_shape=jax.ShapeDtypeStruct((b, l, hh * D_V), BF16),
        compiler_params=_cparams("parallel", "parallel", "parallel"),
        name="mla_attn",
    )(q, k, v)


def _mla_ln(hid, w_dq, qnorm_w, w_uq, w_o, k, v, cos_t, sin_t, ln_g, ln_b):
    b, l, d = hid.shape
    x = hid.reshape(b * l, d)
    cq = _mla_cq(x, w_dq, qnorm_w).reshape(b, l, Q_LORA)
    q = _mla_q(cq, w_uq, cos_t, sin_t)
    o = _mla_attn(q, k, v)
    out, slabs = _matmul_ln(o.reshape(b * l, MLA_HEADS * D_V), w_o.astype(BF16), x, ln_g, ln_b)
    return out.reshape(b, l, d), slabs


def _first_argmax(x, ids, n):
    m = jnp.max(x, axis=0, keepdims=True)
    first = jnp.min(jnp.where(x == m, ids, n), axis=0, keepdims=True)
    return m, first


def _router_kernel(x_ref, wt_ref, bias_ref, idx_ref, w_ref, rank_ref, cnt_out_ref, cnt_ref, *, steps_per_tile):
    @pl.when(pl.program_id(0) % steps_per_tile == 0)
    def _():
        cnt_ref[...] = jnp.zeros_like(cnt_ref)

    x = x_ref[...]
    t = x.shape[0]
    logits = lax.dot_general(wt_ref[...], x, (((1,), (1,)), ((), ())),
                             precision=HIGHEST, preferred_element_type=F32)
    scores = jax.nn.sigmoid(logits)
    biased = scores + bias_ref[...]
    neg = -jnp.inf
    sub = lax.broadcasted_iota(jnp.int32, (GROUP_SIZE, t), 0).astype(F32)
    gscores = []
    for g in range(N_GROUPS):
        xg = biased[g * GROUP_SIZE:(g + 1) * GROUP_SIZE, :]
        m1, i1 = _first_argmax(xg, sub, float(GROUP_SIZE))
        m2 = jnp.max(jnp.where(sub == i1, neg, xg), axis=0, keepdims=True)
        gscores.append(m1 + m2)
    gs = jnp.concatenate(gscores, axis=0)
    gid = lax.broadcasted_iota(jnp.int32, (N_GROUPS, t), 0).astype(F32)
    gsel = jnp.zeros((N_GROUPS, t), F32)
    for _ in range(TOPK_GROUPS):
        _, gi = _first_argmax(gs, gid, float(N_GROUPS))
        hit = gid == gi
        gsel = jnp.where(hit, 1.0, gsel)
        gs = jnp.where(hit, neg, gs)
    eid = lax.broadcasted_iota(jnp.int32, (N_EXPERTS, t), 0).astype(F32)
    allowed = jnp.concatenate(
        [jnp.broadcast_to(gsel[g:g + 1, :], (GROUP_SIZE, t)) for g in range(N_GROUPS)], axis=0)
    cand = jnp.where(allowed > 0.0, biased, neg)
    idxs, ws, hits = [], [], []
    for _ in range(TOP_K):
        _, ei = _first_argmax(cand, eid, float(N_EXPERTS))
        hit = eid == ei
        idxs.append(ei)
        hits.append(hit)
        ws.append(jnp.sum(jnp.where(hit, scores, 0.0), axis=0, keepdims=True))
        cand = jnp.where(hit, neg, cand)
    w = jnp.concatenate(ws, axis=0)
    w = w / (jnp.sum(w, axis=0, keepdims=True) + 1e-20) * ROUTED_SCALE
    idx_ref[...] = jnp.concatenate(idxs, axis=0).astype(jnp.int32)
    w_ref[...] = w
    chosen = jnp.zeros((N_EXPERTS, t), F32)
    for hit in hits:
        chosen = jnp.where(hit, 1.0, chosen)
    before = (lax.broadcasted_iota(jnp.int32, (t, t), 0) < lax.broadcasted_iota(jnp.int32, (t, t), 1))
    prior = jnp.dot(chosen.astype(BF16), jnp.where(before, 1.0, 0.0).astype(BF16),
                    preferred_element_type=F32) + cnt_ref[...]
    rank_ref[...] = jnp.concatenate(
        [jnp.sum(jnp.where(hit, prior, 0.0), axis=0, keepdims=True) for hit in hits], axis=0).astype(jnp.int32)
    cnt_ref[...] += jnp.sum(chosen, axis=1, keepdims=True)
    cnt_out_ref[0] = cnt_ref[...]


def _router(x, w_router, bias, tile_tokens, tr=512):
    t, d = x.shape
    tr = min(tr, tile_tokens)
    steps_per_tile = tile_tokens // tr
    n_tiles = t // tile_tokens
    kt = pl.BlockSpec((TOP_K, tr), lambda i: (0, i))
    idx, w, rank, counts = pl.pallas_call(
        functools.partial(_router_kernel, steps_per_tile=steps_per_tile),
        grid=(t // tr,),
        in_specs=[pl.BlockSpec((tr, d), lambda i: (i, 0)),
                  pl.BlockSpec((N_EXPERTS, d), lambda i: (0, 0)),
                  pl.BlockSpec((N_EXPERTS, 1), lambda i: (0, 0))],
        out_specs=[kt, kt, kt,
                   pl.BlockSpec((1, N_EXPERTS, 1), lambda i: (i // steps_per_tile, 0, 0))],
        out_shape=[jax.ShapeDtypeStruct((TOP_K, t), jnp.int32),
                   jax.ShapeDtypeStruct((TOP_K, t), F32),
                   jax.ShapeDtypeStruct((TOP_K, t), jnp.int32),
                   jax.ShapeDtypeStruct((n_tiles, N_EXPERTS, 1), F32)],
        scratch_shapes=[pltpu.VMEM((N_EXPERTS, 1), F32)],
        compiler_params=_cparams("arbitrary"),
        name="moe_router",
    )(x, w_router.T, bias.reshape(N_EXPERTS, 1))
    return idx, w, rank, counts.reshape(n_tiles, N_EXPERTS).astype(jnp.int32)


def _steps_per_tile(tile_tokens):
    return tile_tokens * TOP_K // STEP_ROWS + N_EXPERTS


def _plan_kernel(fill_lo_ref, fill_hi_ref, pos_ref, src_ref, *, tile_tokens):
    i = pl.program_id(0)
    n_assign = tile_tokens * TOP_K
    empty = n_assign
    group = 16

    def fill_range(g, carry):
        lo = fill_lo_ref[i * N_EXPERTS + g]
        hi = fill_hi_ref[i * N_EXPERTS + g]

        def fill(b, carry):
            for d in range(group):
                src_ref[0, 0, jnp.maximum(hi - 1 - b * group - d, 0)] = empty
            return carry
        return lax.fori_loop(0, (hi - lo + group - 1) // group, fill, carry)
    lax.fori_loop(0, N_EXPERTS, fill_range, 0)

    def place(b, carry):
        base = b * group
        slots = [pos_ref[0, 0, base + d] for d in range(group)]
        for d in range(group):
            src_ref[0, 0, slots[d]] = base + d
        return carry
    lax.fori_loop(0, n_assign // group, place, 0)


def _dispatch_plan(idx_t, w_t, rank_t, counts, tile_tokens):
    t = idx_t.shape[1]
    n_tiles = t // tile_tokens
    steps = _steps_per_tile(tile_tokens)
    padded = (counts + STEP_ROWS - 1) // STEP_ROWS * STEP_ROWS
    pad_end = jnp.cumsum(padded, axis=1)
    pad_start = (pad_end - padded).astype(jnp.int32)
    n_used = (pad_end[:, -1] // STEP_ROWS).astype(jnp.int32)
    starts = jnp.arange(steps, dtype=jnp.int32) * STEP_ROWS
    step_e = jnp.sum(starts[None, :, None] >= pad_end[:, None, :], axis=-1).astype(jnp.int32)
    step_e = jnp.minimum(step_e, N_EXPERTS - 1)
    last = jnp.take_along_axis(step_e, jnp.maximum(n_used - 1, 0)[:, None], axis=1)
    step_e = jnp.where(starts[None, :] // STEP_ROWS < n_used[:, None], step_e, last)

    start_of = jnp.repeat(pad_start.T, tile_tokens, axis=1)
    experts = jnp.arange(N_EXPERTS, dtype=jnp.int32)[None, :, None]
    pos_t = rank_t + jnp.sum(jnp.where(idx_t[:, None, :] == experts, start_of[None], 0), axis=1)
    fill_lo = (pad_start + counts).astype(jnp.int32)
    fill_hi = pad_end.astype(jnp.int32)

    n_assign = tile_tokens * TOP_K
    n_slots = steps * STEP_ROWS
    single = pl.Buffered(1)
    src = pl.pallas_call(
        functools.partial(_plan_kernel, tile_tokens=tile_tokens),
        grid_spec=pltpu.PrefetchScalarGridSpec(
            num_scalar_prefetch=2,
            grid=(n_tiles,),
            in_specs=[pl.BlockSpec((1, 1, n_assign), lambda i, lo, hi: (i, 0, 0),
                                   memory_space=pltpu.SMEM, pipeline_mode=single)],
            out_specs=pl.BlockSpec((1, 1, n_slots), lambda i, lo, hi: (i, 0, 0),
                                   memory_space=pltpu.SMEM, pipeline_mode=single),
        ),
        out_shape=jax.ShapeDtypeStruct((n_tiles, 1, n_slots), jnp.int32),
        compiler_params=_cparams("parallel"),
        name="moe_plan",
    )(fill_lo.reshape(-1), fill_hi.reshape(-1), pos_t.T.reshape(n_tiles, 1, n_assign))
    src = jnp.where(jnp.arange(n_slots, dtype=jnp.int32)[None, None, :] < pad_end[:, -1][:, None, None],
                    src, tile_tokens * TOP_K)
    rows = src & -SUBLANES
    w_flat = jnp.pad(w_t.T.reshape(n_tiles, tile_tokens * TOP_K), ((0, 0), (0, SUBLANES)))
    slot_w = jnp.take_along_axis(w_flat, src[:, 0, :], axis=1)
    held = jnp.clip(jnp.take_along_axis(fill_lo, step_e, axis=1) - starts[None, :], 0, STEP_ROWS)
    live = jnp.where(starts[None, :] // STEP_ROWS < n_used[:, None], (held + EXPERT_ROWS - 1) // EXPERT_ROWS, 0)
    return rows, slot_w, step_e.reshape(-1), n_used, live.reshape(-1).astype(jnp.int32), steps


def _experts_kernel(step_e_ref, n_used_ref, live_ref, row_ref, sw_ref, x_ref, wg_ref, wu_ref, wd_ref, acc_ref,
                    *stage_refs):
    n_sub = BLOCKS_PER_STEP
    xs_refs = stage_refs[:n_sub]
    ys_refs = stage_refs[n_sub:]
    i = pl.program_id(0)
    j = pl.program_id(1)
    rows = EXPERT_ROWS
    pitch = STAGE_PITCH

    @pl.when(j == 0)
    def _():
        acc_ref[...] = jnp.zeros_like(acc_ref)

    last_row = x_ref.shape[1] - SUBLANES

    def token_rows(slot, limit=None):
        row = row_ref[0, 0, slot]
        if limit is not None:
            row = jnp.minimum(row, limit)
        return pl.ds(pl.multiple_of(row, SUBLANES), SUBLANES)

    def run(subs):
        for s in subs:
            for r in range(rows):
                xs_refs[s][pl.ds(r, ROW_CHUNKS, stride=pitch), :] = x_ref[0, token_rows(s * rows + r, last_row), :]
        eye = (lax.broadcasted_iota(jnp.int32, (rows, rows), 0)
               == lax.broadcasted_iota(jnp.int32, (rows, rows), 1))
        ys = {}
        for s in subs:
            x = jnp.concatenate([xs_refs[s][pl.ds(c * pitch, rows), :] for c in range(ROW_CHUNKS)], axis=1)
            hid = _silu(_bdot(x, wg_ref[0, 0])) * _bdot(x, wu_ref[0, 0])
            w_row = sw_ref[0, :, s * rows:(s + 1) * rows]
            w_col = jnp.sum(jnp.where(eye, w_row, 0.0), axis=1, keepdims=True)
            ys[s] = _bdot(hid, wd_ref[0, 0]) * w_col
        for s in subs:
            for c in range(ROW_CHUNKS):
                ys_refs[s][pl.ds(c * pitch, rows), :] = ys[s][:, c * LANES:(c + 1) * LANES]
        for s in subs:
            for r0 in range(0, rows, SCATTER_BATCH):
                batch = range(r0, r0 + SCATTER_BATCH)
                ats = [token_rows(s * rows + r) for r in batch]
                new = [acc_ref[0, at, :] + ys_refs[s][pl.ds(r, ROW_CHUNKS, stride=pitch), :]
                       for at, r in zip(ats, batch)]
                for at, val in zip(ats, new):
                    acc_ref[0, at, :] = val

    live = live_ref[i * pl.num_programs(1) + j]
    for n_live in range(1, n_sub + 1):
        pl.when(live == n_live)(functools.partial(run, range(n_live)))


def _routed_experts(x_slabs, idx_t, w_t, rank_t, counts, w_gate_up, w_down, layer, tile_tokens):
    d = D_MODEL
    t = x_slabs.shape[0] // ROW_CHUNKS
    n_tiles = t // tile_tokens
    rows, slot_w, step_e, n_used, live, steps = _dispatch_plan(idx_t, w_t, rank_t, counts, tile_tokens)
    rows = rows.reshape(n_tiles * steps, 1, STEP_ROWS)
    slot_w = slot_w.reshape(n_tiles * steps, 1, STEP_ROWS)
    x_slabs = x_slabs.reshape(n_tiles, tile_tokens * ROW_CHUNKS, LANES)
    slab_rows = (tile_tokens + 1) * ROW_CHUNKS

    def used_step(i, j, se, nu, lv):
        return (i * steps + jnp.minimum(j, jnp.maximum(nu[i] - 1, 0)), 0, 0)

    pick = lambda i, j, se, nu, lv: (layer, se[i * steps + j], 0, 0)
    tile = lambda i, j, se, nu, lv: (i, 0, 0)
    resident = pl.Buffered(1)
    stage = pltpu.VMEM((ROW_CHUNKS * STAGE_PITCH, LANES), F32)
    grid_spec = pltpu.PrefetchScalarGridSpec(
        num_scalar_prefetch=3,
        grid=(n_tiles, steps),
        in_specs=[
            pl.BlockSpec((1, 1, STEP_ROWS), used_step, memory_space=pltpu.SMEM),
            pl.BlockSpec((1, 1, STEP_ROWS), used_step),
            pl.BlockSpec((1, tile_tokens * ROW_CHUNKS, LANES), tile, pipeline_mode=resident),
            pl.BlockSpec((1, 1, d, D_EXPERT), pick),
            pl.BlockSpec((1, 1, d, D_EXPERT), pick),
            pl.BlockSpec((1, 1, D_EXPERT, d), pick),
        ],
        out_specs=pl.BlockSpec((1, slab_rows, LANES), tile, pipeline_mode=resident),
        scratch_shapes=[stage] * (2 * BLOCKS_PER_STEP),
    )
    return pl.pallas_call(
        _experts_kernel,
        grid_spec=grid_spec,
        out_shape=jax.ShapeDtypeStruct((n_tiles, slab_rows, LANES), F32),
        compiler_params=_cparams("parallel", "arbitrary"),
        name="moe_experts",
    )(step_e, n_used, live, rows, slot_w, x_slabs, *w_gate_up, w_down)


def _moe_out_kernel(x_ref, r_ref, wg_ref, wu_ref, wd_ref, g_ref, b_ref, o_ref):
    x = x_ref[...]
    tm = x.shape[0]
    hid = _silu(_bdot(x, wg_ref[...])) * _bdot(x, wu_ref[...])
    routed = jnp.concatenate([r_ref[0, pl.ds(c, tm, stride=ROW_CHUNKS), :] for c in range(ROW_CHUNKS)], axis=1)
    y = routed + _bdot(hid, wd_ref[...])
    o_ref[...] = _layer_norm(DN_ALPHA * x + y, g_ref[...], b_ref[...])


def _moe_out(x, routed, ws_gate, ws_up, ws_down, g, b, tile_tokens, tm=512):
    t, d = x.shape
    tm = min(tm, tile_tokens)
    per_tile = tile_tokens // tm
    return pl.pallas_call(
        _moe_out_kernel,
        grid=(t // tm,),
        in_specs=[pl.BlockSpec((tm, d), lambda i: (i, 0)),
                  pl.BlockSpec((1, tm * ROW_CHUNKS, LANES), lambda i: (i // per_tile, i % per_tile, 0)),
                  pl.BlockSpec((d, D_SHARED), lambda i: (0, 0)),
                  pl.BlockSpec((d, D_SHARED), lambda i: (0, 0)),
                  pl.BlockSpec((D_SHARED, d), lambda i: (0, 0)),
                  pl.BlockSpec((1, d), lambda i: (0, 0)),
                  pl.BlockSpec((1, d), lambda i: (0, 0))],
        out_specs=pl.BlockSpec((tm, d), lambda i: (i, 0)),
        out_shape=jax.ShapeDtypeStruct((t, d), F32),
        compiler_params=_cparams("parallel"),
        name="moe_out",
    )(x, routed, ws_gate.astype(BF16), ws_up.astype(BF16), ws_down.astype(BF16),
      g.reshape(1, d), b.reshape(1, d))


def _moe_ln(hid, hid_slabs, w_router, router_bias, w_gate_up, w_down, layer, ws_gate, ws_up, ws_down, ln_g, ln_b):
    b, l, d = hid.shape
    x = hid.reshape(b * l, d)
    tile_tokens = min(MOE_TILE_TOKENS, b * l)
    idx_t, w_t, rank_t, counts = _router(x, w_router, router_bias, tile_tokens)
    routed = _routed_experts(hid_slabs, idx_t, w_t, rank_t, counts, w_gate_up, w_down, layer, tile_tokens)
    return _moe_out(x, routed, ws_gate, ws_up, ws_down, ln_g, ln_b, tile_tokens).reshape(b, l, d)


def kernel(x, a_w_in, a_conv_w, a_A_log, a_dt_bias, a_onorm_w, a_w_out, b_w_dq, b_qnorm_w, b_w_uq, b_w_o,
           kv_w_dkv, kv_norm_w, kv_w_ukv, ln1_g, ln1_b, ln2_g, ln2_b, moe_w_router, moe_router_bias,
           moe_w_gate, moe_w_up, moe_w_down, moe_ws_gate, moe_ws_up, moe_ws_down):
    l = x.shape[1]
    cos_t, sin_t = _rope_lane_tables(l)
    w_gate_up = (moe_w_gate.astype(BF16), moe_w_up.astype(BF16))
    w_down = moe_w_down.astype(BF16)
    h = x
    k = v = None
    for layer in range(DEPTH):
        if layer < N_A_LAYERS:
            i = layer
            h, slabs = _gated_deltanet_ln(h, a_w_in[i], a_conv_w[i], a_A_log[i], a_dt_bias[i], a_onorm_w[i],
                                          a_w_out[i], ln1_g[layer], ln1_b[layer])
        else:
            i = layer - N_A_LAYERS
            h, slabs = _mla_ln(h, b_w_dq[i], b_qnorm_w[i], b_w_uq[i], b_w_o[i], k, v, cos_t, sin_t,
                               ln1_g[layer], ln1_b[layer])
        h = _moe_ln(h, slabs, moe_w_router[layer], moe_router_bias[layer], w_gate_up, w_down, layer,
                    moe_ws_gate[layer], moe_ws_up[layer], moe_ws_down[layer], ln2_g[layer], ln2_b[layer])
        if layer == N_A_LAYERS - 1:
            k, v = _mla_kv(h, kv_w_dkv, kv_norm_w, kv_w_ukv, cos_t, sin_t)
    return h
```

```python
import functools

import jax
import jax.numpy as jnp
import numpy as np
from jax import lax
from jax.experimental import pallas as pl
from jax.experimental.pallas import tpu as pltpu

F32 = jnp.float32
BF16 = jnp.bfloat16
HIGHEST = lax.Precision.HIGHEST

D_MODEL = 1024
DEPTH = 4
N_A_LAYERS = DEPTH // 2
GDN_HEADS = 8
GDN_DK = 128
GDN_DV = 128
D_QK = GDN_HEADS * GDN_DK
D_VA = GDN_HEADS * GDN_DV
GDN_CONV_CH = 2 * D_QK + D_VA
CONV_WIDTH = 4
CHUNK = 64
MLA_HEADS = 8
Q_LORA = 512
KV_LORA = 256
D_NOPE = 128
D_ROPE = 64
D_V = 128
ROPE_THETA = 10000.0
N_EXPERTS = 64
TOP_K = 8
N_GROUPS = 8
GROUP_SIZE = N_EXPERTS // N_GROUPS
TOPK_GROUPS = 4
D_EXPERT = 256
D_SHARED = 256
ROUTED_SCALE = 2.5
DN_ALPHA = (2 * DEPTH) ** 0.25
EPS = 1e-6
LN_EPS = 1e-5
LOG2_E = 1.4426950408889634

LANES = 128
SUBLANES = 8
VMEM_LIMIT_BYTES = 56 * 1024 * 1024

EXPERT_ROWS = 128
ROW_CHUNKS = D_MODEL // LANES
STAGE_PITCH = EXPERT_ROWS + 1
SCATTER_BATCH = 8
BLOCKS_PER_STEP = 2
STEP_ROWS = BLOCKS_PER_STEP * EXPERT_ROWS
MOE_TILE_TOKENS = 4096
ATTN_ROW_GROUPS = 4
assert TOP_K == SUBLANES


def _cparams(*sem):
    return pltpu.CompilerParams(dimension_semantics=sem, vmem_limit_bytes=VMEM_LIMIT_BYTES)


def _silu(x):
    return x * jax.nn.sigmoid(x)


def _layer_norm(x, g, b):
    mu = jnp.mean(x, -1, keepdims=True)
    xc = x - mu
    var = jnp.mean(xc * xc, -1, keepdims=True)
    return xc * lax.rsqrt(var + LN_EPS) * g + b


def _rms_norm(x, w):
    return x * lax.rsqrt(jnp.mean(x * x, -1, keepdims=True) + EPS) * w


def _bdot(a, b):
    return jnp.dot(a.astype(BF16), b.astype(BF16), preferred_element_type=F32)


def _bdot_nt(a, b):
    return lax.dot_general(a.astype(BF16), b.astype(BF16), (((1,), (1,)), ((), ())),
                           preferred_element_type=F32)


def _hdot(a, b):
    return jnp.dot(a, b, precision=HIGHEST, preferred_element_type=F32)


def _matmul_kernel(x_ref, w_ref, o_ref):
    o_ref[...] = _bdot(x_ref[...], w_ref[...]).astype(o_ref.dtype)


def _matmul(x, w, out_dtype, tm, tn):
    m, k = x.shape
    n = w.shape[1]
    tm = min(tm, m)
    tn = min(tn, n)
    return pl.pallas_call(
        _matmul_kernel,
        grid=(m // tm, n // tn),
        in_specs=[pl.BlockSpec((tm, k), lambda i, j: (i, 0)),
                  pl.BlockSpec((k, tn), lambda i, j: (0, j))],
        out_specs=pl.BlockSpec((tm, tn), lambda i, j: (i, j)),
        out_shape=jax.ShapeDtypeStruct((m, n), out_dtype),
        compiler_params=_cparams("parallel", "parallel"),
        name="matmul",
    )(x, w)


def _matmul_ln_kernel(x_ref, w_ref, r_ref, g_ref, b_ref, o_ref, slab_ref):
    mix = _bdot(x_ref[...], w_ref[...])
    out = _layer_norm(DN_ALPHA * r_ref[...] + mix, g_ref[...], b_ref[...])
    o_ref[...] = out
    tm = out.shape[0]
    for c in range(ROW_CHUNKS):
        slab_ref[pl.ds(c, tm, stride=ROW_CHUNKS), :] = out[:, c * LANES:(c + 1) * LANES]


def _matmul_ln(x, w, res, g, b, tm=512):
    m, k = x.shape
    n = w.shape[1]
    tm = min(tm, m)
    chunks = n // LANES
    return pl.pallas_call(
        _matmul_ln_kernel,
        grid=(m // tm,),
        in_specs=[pl.BlockSpec((tm, k), lambda i: (i, 0)),
                  pl.BlockSpec((k, n), lambda i: (0, 0)),
                  pl.BlockSpec((tm, n), lambda i: (i, 0)),
                  pl.BlockSpec((1, n), lambda i: (0, 0)),
                  pl.BlockSpec((1, n), lambda i: (0, 0))],
        out_specs=[pl.BlockSpec((tm, n), lambda i: (i, 0)),
                   pl.BlockSpec((tm * chunks, LANES), lambda i: (i, 0))],
        out_shape=[jax.ShapeDtypeStruct((m, n), F32),
                   jax.ShapeDtypeStruct((m * chunks, LANES), F32)],
        compiler_params=_cparams("parallel"),
        name="matmul_ln",
    )(x, w, res, g.reshape(1, n), b.reshape(1, n))


def _softplus(x):
    return jnp.maximum(x, 0.0) + jnp.log1p(jnp.exp(-jnp.abs(x)))


def _gdn_gates_kernel(x_ref, wab_ref, wabt_ref, alog_ref, dtb_ref, alogt_ref, dtbt_ref,
                      gc_ref, beta_ref, gct_ref):
    x = x_ref[...]
    tl = x.shape[0]
    h = GDN_HEADS
    ab = _bdot(x, wab_ref[...])
    abt = _bdot_nt(wabt_ref[...], x)
    g = -jnp.exp(alog_ref[...]) * _softplus(ab[:, :h] + dtb_ref[...])
    gt = -jnp.exp(alogt_ref[...]) * _softplus(abt[:h, :] + dtbt_ref[...])
    beta_ref[...] = jax.nn.sigmoid(ab[:, h:2 * h])
    row = lax.broadcasted_iota(jnp.int32, (tl, tl), 0)
    col = lax.broadcasted_iota(jnp.int32, (tl, tl), 1)
    shift = CHUNK.bit_length() - 1
    same = (row >> shift) == (col >> shift)
    lower = jnp.where(same & (col <= row), 1.0, 0.0).astype(F32)
    upper = jnp.where(same & (row <= col), 1.0, 0.0).astype(F32)
    gc_ref[...] = _hdot(lower, g)
    gct_ref[...] = _hdot(gt, upper)


def _gdn_gates(x, w_ab, a_log, dt_bias, tl=512):
    t = x.shape[0]
    tl = min(tl, t)
    h = GDN_HEADS
    wab = jnp.zeros((D_MODEL, LANES), F32).at[:, :2 * h].set(w_ab)
    wabt = w_ab.T
    return pl.pallas_call(
        _gdn_gates_kernel,
        grid=(t // tl,),
        in_specs=[pl.BlockSpec((tl, D_MODEL), lambda i: (i, 0)),
                  pl.BlockSpec((D_MODEL, LANES), lambda i: (0, 0)),
                  pl.BlockSpec((2 * h, D_MODEL), lambda i: (0, 0)),
                  pl.BlockSpec((1, h), lambda i: (0, 0)),
                  pl.BlockSpec((1, h), lambda i: (0, 0)),
                  pl.BlockSpec((h, 1), lambda i: (0, 0)),
                  pl.BlockSpec((h, 1), lambda i: (0, 0))],
        out_specs=[pl.BlockSpec((tl, h), lambda i: (i, 0)),
                   pl.BlockSpec((tl, h), lambda i: (i, 0)),
                   pl.BlockSpec((h, tl), lambda i: (0, i))],
        out_shape=[jax.ShapeDtypeStruct((t, h), F32),
                   jax.ShapeDtypeStruct((t, h), F32),
                   jax.ShapeDtypeStruct((h, t), F32)],
        compiler_params=_cparams("parallel"),
        name="gdn_gates",
    )(x, wab, wabt, a_log.reshape(1, h), dt_bias.reshape(1, h),
      a_log.reshape(h, 1), dt_bias.reshape(h, 1))


def _gdn_conv_kernel(cur_ref, prev_ref, w_ref, q_ref, k_ref, v_ref):
    i = pl.program_id(1)
    x = cur_ref[0]
    tl = x.shape[0]
    prev = jnp.where(i > 0, prev_ref[0], 0.0)
    xx = jnp.concatenate([prev, x], axis=0)
    w = w_ref[...]
    y = None
    for j in range(CONV_WIDTH):
        off = SUBLANES - (CONV_WIDTH - 1) + j
        term = xx[off:off + tl, :] * w[j:j + 1, :]
        y = term if y is None else y + term
    y = _silu(y)
    for h in range(GDN_HEADS):
        sl = slice(h * GDN_DK, (h + 1) * GDN_DK)
        qh = y[:, sl]
        q_ref[0, :, sl] = qh * lax.rsqrt(jnp.sum(qh * qh, -1, keepdims=True) + EPS) * (GDN_DK ** -0.5)
        kh = y[:, D_QK + h * GDN_DK:D_QK + (h + 1) * GDN_DK]
        k_ref[0, :, sl] = kh * lax.rsqrt(jnp.sum(kh * kh, -1, keepdims=True) + EPS)
    v_ref[0] = y[:, 2 * D_QK:]


def _gdn_conv(proj, conv_w, tl=256):
    b, l, _ = proj.shape
    tl = min(tl, l)
    c3 = GDN_CONV_CH
    per = tl // SUBLANES
    out = jax.ShapeDtypeStruct((b, l, D_QK), F32)
    return pl.pallas_call(
        _gdn_conv_kernel,
        grid=(b, l // tl),
        in_specs=[pl.BlockSpec((1, tl, c3), lambda bi, i: (bi, i, 0)),
                  pl.BlockSpec((1, SUBLANES, c3), lambda bi, i: (bi, jnp.maximum(i * per - 1, 0), 0)),
                  pl.BlockSpec((CONV_WIDTH, c3), lambda bi, i: (0, 0))],
        out_specs=[pl.BlockSpec((1, tl, D_QK), lambda bi, i: (bi, i, 0))] * 3,
        out_shape=[out, out, out],
        compiler_params=_cparams("parallel", "parallel"),
        name="gdn_conv",
    )(proj, proj, conv_w)


def _split_bf16(x):
    hi = x.astype(BF16)
    lo = (x - hi.astype(F32)).astype(BF16)
    return hi, lo


def _dot_split(a, b):
    (ah, al), (bh, bl) = a, b
    dot = functools.partial(jnp.dot, preferred_element_type=F32)
    return dot(ah, bh) + (dot(ah, bl) + dot(al, bh))


def _gdn_prep_kernel(q_ref, k_ref, v_ref, gc_ref, beta_ref, gct_ref, wq_ref, u_ref, attn_ref, kgt_ref,
                     *, chunks):
    c = CHUNK
    row = lax.broadcasted_iota(jnp.int32, (c, c), 0)
    col = lax.broadcasted_iota(jnp.int32, (c, c), 1)
    incl = row >= col
    strict = row > col
    eye = jnp.where(row == col, 1.0, 0.0).astype(F32)
    items = [(ci, h) for ci in range(chunks) for h in range(GDN_HEADS)]

    st = {}
    for ci, h in items:
        rows = slice(ci * c, (ci + 1) * c)
        sl = slice(h * GDN_DK, (h + 1) * GDN_DK)
        q = q_ref[0, rows, sl]
        k = k_ref[0, rows, sl]
        v = v_ref[0, rows, sl]
        gcol = gc_ref[0, rows, h:h + 1]
        grow = gct_ref[0, ci, h:h + 1, :]
        bcol = beta_ref[0, rows, h:h + 1]
        decay = jnp.where(incl, jnp.exp(jnp.where(incl, gcol - grow, 0.0)), 0.0)
        kb = k * bcol
        eg = jnp.exp(gcol)
        g_last = gcol[c - 1:c, :]
        sc = _bdot_nt(jnp.concatenate([kb, q], axis=0), k)
        a = jnp.where(strict, sc[:c] * decay, 0.0)
        attn_ref[0, ci, h] = (sc[c:] * decay).astype(attn_ref.dtype)
        kgt_ref[0, ci, h] = (k * jnp.exp(g_last - gcol)).T.astype(kgt_ref.dtype)
        wq_ref[0, ci, h, c:, :] = (q * eg).astype(wq_ref.dtype)
        st[ci, h] = dict(m=-a, rhs=jnp.concatenate([v * bcol, kb * eg], axis=1))

    for it in items:
        st[it]["p"] = eye + st[it]["m"]
    span = 2
    while span < c:
        for it in items:
            ms = _split_bf16(st[it]["m"])
            st[it]["m"] = _dot_split(ms, ms)
        for it in items:
            st[it]["p"] = st[it]["p"] + _dot_split(_split_bf16(st[it]["p"]), _split_bf16(st[it]["m"]))
        span *= 2

    for ci, h in items:
        sol = _dot_split(_split_bf16(st[ci, h]["p"]), _split_bf16(st[ci, h]["rhs"]))
        u_ref[0, ci * c:(ci + 1) * c, h * GDN_DV:(h + 1) * GDN_DV] = sol[:, :GDN_DV]
        wq_ref[0, ci, h, :c, :] = sol[:, GDN_DV:].astype(wq_ref.dtype)


def _gdn_prep(q, k, v, gc, beta, gct, chunks=2):
    b, l, _ = q.shape
    n = l // CHUNK
    chunks = min(chunks, n)
    h = GDN_HEADS
    rows = chunks * CHUNK
    blk = pl.BlockSpec((1, rows, D_QK), lambda bi, i: (bi, i, 0))
    gate = pl.BlockSpec((1, rows, h), lambda bi, i: (bi, i, 0))
    return pl.pallas_call(
        functools.partial(_gdn_prep_kernel, chunks=chunks),
        grid=(b, n // chunks),
        in_specs=[blk, blk, blk, gate, gate,
                  pl.BlockSpec((1, chunks, h, CHUNK), lambda bi, i: (bi, i, 0, 0))],
        out_specs=[pl.BlockSpec((1, chunks, h, 2 * CHUNK, GDN_DK), lambda bi, i: (bi, i, 0, 0, 0)),
                   pl.BlockSpec((1, rows, D_VA), lambda bi, i: (bi, i, 0)),
                   pl.BlockSpec((1, chunks, h, CHUNK, CHUNK), lambda bi, i: (bi, i, 0, 0, 0)),
                   pl.BlockSpec((1, chunks, h, GDN_DK, CHUNK), lambda bi, i: (bi, i, 0, 0, 0))],
        out_shape=[jax.ShapeDtypeStruct((b, n, h, 2 * CHUNK, GDN_DK), BF16),
                   jax.ShapeDtypeStruct((b, l, D_VA), F32),
                   jax.ShapeDtypeStruct((b, n, h, CHUNK, CHUNK), BF16),
                   jax.ShapeDtypeStruct((b, n, h, GDN_DK, CHUNK), BF16)],
        compiler_params=_cparams("parallel", "parallel"),
        name="gdn_prep",
    )(q, k, v, gc, beta, gct)


def _gdn_state_kernel(wq_ref, u_ref, attn_ref, kgt_ref, gc_ref, z_ref, onw_ref, o_ref, s_ref, *, chunks):
    @pl.when(pl.program_id(1) == 0)
    def _():
        s_ref[...] = jnp.zeros_like(s_ref)

    c = CHUNK
    onw = onw_ref[...]
    dot = functools.partial(jnp.dot, preferred_element_type=F32)
    heads = range(GDN_HEADS)
    for ci in range(chunks):
        rows = slice(ci * c, (ci + 1) * c)
        s_old = [s_ref[h] for h in heads]
        r = [dot(wq_ref[0, ci, h], s_old[h].astype(BF16)) for h in heads]
        v_new = [(u_ref[0, rows, h * GDN_DV:(h + 1) * GDN_DV] - r[h][:c]).astype(BF16) for h in heads]
        for h in heads:
            decay_last = jnp.exp(gc_ref[0, (ci + 1) * c - 1:(ci + 1) * c, h:h + 1])
            s_ref[h] = s_old[h] * decay_last + dot(kgt_ref[0, ci, h], v_new[h])
        for h in heads:
            sl = slice(h * GDN_DV, (h + 1) * GDN_DV)
            o = r[h][c:] + dot(attn_ref[0, ci, h], v_new[h])
            o_ref[0, rows, sl] = (_rms_norm(o, onw) * _silu(z_ref[0, rows, sl])).astype(o_ref.dtype)


def _gdn_state(wq, u, attn, kgt, gc, proj, onorm_w, chunks=4):
    b, n, h = wq.shape[:3]
    l = n * CHUNK
    chunks = min(chunks, n)
    rows = chunks * CHUNK
    z_block = GDN_CONV_CH // D_VA
    return pl.pallas_call(
        functools.partial(_gdn_state_kernel, chunks=chunks),
        grid=(b, n // chunks),
        in_specs=[pl.BlockSpec((1, chunks, h, 2 * CHUNK, GDN_DK), lambda bi, i: (bi, i, 0, 0, 0)),
                  pl.BlockSpec((1, rows, D_VA), lambda bi, i: (bi, i, 0)),
                  pl.BlockSpec((1, chunks, h, CHUNK, CHUNK), lambda bi, i: (bi, i, 0, 0, 0)),
                  pl.BlockSpec((1, chunks, h, GDN_DK, CHUNK), lambda bi, i: (bi, i, 0, 0, 0)),
                  pl.BlockSpec((1, rows, h), lambda bi, i: (bi, i, 0)),
                  pl.BlockSpec((1, rows, D_VA), lambda bi, i: (bi, i, z_block)),
                  pl.BlockSpec((1, GDN_DV), lambda bi, i: (0, 0))],
        out_specs=pl.BlockSpec((1, rows, D_VA), lambda bi, i: (bi, i, 0)),
        out_shape=jax.ShapeDtypeStruct((b, l, D_VA), BF16),
        scratch_shapes=[pltpu.VMEM((h, GDN_DK, GDN_DV), F32)],
        compiler_params=_cparams("parallel", "arbitrary"),
        name="gdn_state",
    )(wq, u, attn, kgt, gc, proj, onorm_w.reshape(1, GDN_DV))


def _gated_deltanet_ln(hid, w_in, conv_w, a_log, dt_bias, onorm_w, w_out, ln_g, ln_b):
    b, l, d = hid.shape
    h = GDN_HEADS
    n = l // CHUNK
    x = hid.reshape(b * l, d)
    n_main = GDN_CONV_CH + D_VA
    proj = _matmul(x, w_in[:, :n_main].astype(BF16), F32, 1024, 1024).reshape(b, l, n_main)
    gc, beta, gct = _gdn_gates(x, w_in[:, n_main:], a_log, dt_bias)
    gc = gc.reshape(b, l, h)
    gct = gct.reshape(h, b, n, CHUNK).transpose(1, 2, 0, 3)
    q, k, v = _gdn_conv(proj, conv_w)
    wq, u, attn, kgt = _gdn_prep(q, k, v, gc, beta.reshape(b, l, h), gct)
    o = _gdn_state(wq, u, attn, kgt, gc, proj, onorm_w)
    out, slabs = _matmul_ln(o.reshape(b * l, D_VA), w_out.astype(BF16), x, ln_g, ln_b)
    return out.reshape(b, l, d), slabs


def _rope_lane_tables(l):
    inv = 1.0 / (ROPE_THETA ** (jnp.arange(0, D_ROPE, 2, dtype=F32) / D_ROPE))
    ang = jnp.arange(l, dtype=F32)[:, None] * inv[None, :]
    cos, sin = jnp.cos(ang), jnp.sin(ang)
    zero = jnp.zeros((l, LANES - D_ROPE), F32)
    return (jnp.concatenate([cos, cos, zero], -1),
            jnp.concatenate([-sin, sin, zero], -1))


def _rope_weight_groups(w_rope):
    kdim = w_rope.shape[0]
    half = D_ROPE // 2
    zero = jnp.zeros((kdim, LANES - D_ROPE), w_rope.dtype)
    x1, x2 = w_rope[:, :half], w_rope[:, half:]
    return jnp.concatenate([x1, x2, zero, x2, x1, zero], axis=-1)


def _mla_kv_kernel(x_ref, wd_ref, nw_ref, wu_ref, cos_ref, sin_ref, k_ref, v_ref):
    x = x_ref[0]
    ckv = _bdot(x, wd_ref[...])
    c = _rms_norm(ckv[:, :KV_LORA], nw_ref[...])
    k_rope_t = (ckv[:, KV_LORA:KV_LORA + LANES] * cos_ref[...]
                + ckv[:, KV_LORA + LANES:] * sin_ref[...]).T.astype(k_ref.dtype)
    kv = _bdot(c, wu_ref[...])
    per = D_NOPE + D_V
    ones_col = jnp.where(lax.broadcasted_iota(jnp.int32, (x.shape[0], LANES), 1) == 0, 1.0, 0.0).astype(v_ref.dtype)
    for h in range(MLA_HEADS):
        k_ref[0, h, :D_NOPE, :] = kv[:, h * per:h * per + D_NOPE].T.astype(k_ref.dtype)
        k_ref[0, h, D_NOPE:, :] = k_rope_t
        v_ref[0, h, :, :D_V] = kv[:, h * per + D_NOPE:(h + 1) * per].astype(v_ref.dtype)
        v_ref[0, h, :, D_V:] = ones_col


def _mla_kv(hid, w_dkv, kv_norm_w, w_ukv, cos_t, sin_t, tl=512):
    b, l, d = hid.shape
    tl = min(tl, l)
    wd = jnp.concatenate([w_dkv[:, :KV_LORA], _rope_weight_groups(w_dkv[:, KV_LORA:])], -1).astype(BF16)
    nd = wd.shape[1]
    hh = MLA_HEADS
    return pl.pallas_call(
        _mla_kv_kernel,
        grid=(b, l // tl),
        in_specs=[pl.BlockSpec((1, tl, d), lambda bi, i: (bi, i, 0)),
                  pl.BlockSpec((d, nd), lambda bi, i: (0, 0)),
                  pl.BlockSpec((1, KV_LORA), lambda bi, i: (0, 0)),
                  pl.BlockSpec((KV_LORA, hh * (D_NOPE + D_V)), lambda bi, i: (0, 0)),
                  pl.BlockSpec((tl, LANES), lambda bi, i: (i, 0)),
                  pl.BlockSpec((tl, LANES), lambda bi, i: (i, 0))],
        out_specs=[pl.BlockSpec((1, hh, D_NOPE + LANES, tl), lambda bi, i: (bi, 0, 0, i)),
                   pl.BlockSpec((1, hh, tl, D_V + LANES), lambda bi, i: (bi, 0, i, 0))],
        out_shape=[jax.ShapeDtypeStruct((b, hh, D_NOPE + LANES, l), BF16),
                   jax.ShapeDtypeStruct((b, hh, l, D_V + LANES), BF16)],
        compiler_params=_cparams("parallel", "parallel"),
        name="mla_kv",
    )(hid, wd, kv_norm_w.reshape(1, KV_LORA), w_ukv.astype(BF16), cos_t, sin_t)


def _mla_cq_kernel(x_ref, w_ref, nw_ref, o_ref):
    o_ref[...] = _rms_norm(_bdot(x_ref[...], w_ref[...]), nw_ref[...]).astype(o_ref.dtype)


def _mla_cq(x, w_dq, qnorm_w, tm=512):
    m, k = x.shape
    tm = min(tm, m)
    return pl.pallas_call(
        _mla_cq_kernel,
        grid=(m // tm,),
        in_specs=[pl.BlockSpec((tm, k), lambda i: (i, 0)),
                  pl.BlockSpec((k, Q_LORA), lambda i: (0, 0)),
                  pl.BlockSpec((1, Q_LORA), lambda i: (0, 0))],
        out_specs=pl.BlockSpec((tm, Q_LORA), lambda i: (i, 0)),
        out_shape=jax.ShapeDtypeStruct((m, Q_LORA), BF16),
        compiler_params=_cparams("parallel"),
        name="mla_cq",
    )(x, w_dq.astype(BF16), qnorm_w.reshape(1, Q_LORA))


def _mla_q_kernel(c_ref, w_ref, cos_ref, sin_ref, q_ref):
    c = c_ref[0]
    scale = (D_NOPE + D_ROPE) ** -0.5 * LOG2_E
    per = D_NOPE + 2 * LANES
    for h in range(MLA_HEADS):
        qh = _bdot(c, w_ref[:, h * per:(h + 1) * per])
        rope = qh[:, D_NOPE:D_NOPE + LANES] * cos_ref[...] + qh[:, D_NOPE + LANES:] * sin_ref[...]
        q_ref[0, h, :, :D_NOPE] = (qh[:, :D_NOPE] * scale).astype(q_ref.dtype)
        q_ref[0, h, :, D_NOPE:] = (rope * scale).astype(q_ref.dtype)


def _mla_q(cq, w_uq, cos_t, sin_t, tl=512):
    b, l, _ = cq.shape
    tl = min(tl, l)
    hh = MLA_HEADS
    per_in = D_NOPE + D_ROPE
    groups = []
    for h in range(hh):
        wh = w_uq[:, h * per_in:(h + 1) * per_in]
        groups += [wh[:, :D_NOPE], _rope_weight_groups(wh[:, D_NOPE:])]
    w = jnp.concatenate(groups, -1).astype(BF16)
    return pl.pallas_call(
        _mla_q_kernel,
        grid=(b, l // tl),
        in_specs=[pl.BlockSpec((1, tl, Q_LORA), lambda bi, i: (bi, i, 0)),
                  pl.BlockSpec(w.shape, lambda bi, i: (0, 0)),
                  pl.BlockSpec((tl, LANES), lambda bi, i: (i, 0)),
                  pl.BlockSpec((tl, LANES), lambda bi, i: (i, 0))],
        out_specs=pl.BlockSpec((1, hh, tl, D_NOPE + LANES), lambda bi, i: (bi, 0, i, 0)),
        out_shape=jax.ShapeDtypeStruct((b, hh, l, D_NOPE + LANES), BF16),
        compiler_params=_cparams("parallel", "parallel"),
        name="mla_q",
    )(cq, w, cos_t, sin_t)


def _mla_attn_kernel(q_ref, k_ref, v_ref, o_ref, *, tk):
    qi = pl.program_id(2)
    q = q_ref[0, 0]
    tq = q.shape[0]
    per = tq // tk

    groups = ATTN_ROW_GROUPS
    rows = tq // groups
    qs = [q[g * rows:(g + 1) * rows] for g in range(groups)]

    def block(j, carry, mask_offset=None):
        at = pl.ds(pl.multiple_of(j * tk, tk), tk)
        kb = k_ref[0, 0, :, at]
        vb = v_ref[0, 0, at, :]
        ss = [_bdot(qg, kb) for qg in qs]
        if mask_offset is not None:
            qpos = lax.broadcasted_iota(jnp.int32, (rows, tk), 0)
            kpos = lax.broadcasted_iota(jnp.int32, (rows, tk), 1)
            ss = [jnp.where(kpos + mask_offset <= qpos + g * rows, s, -jnp.inf) for g, s in enumerate(ss)]
        m_new = [jnp.maximum(c[0], jnp.max(s, -1, keepdims=True)) for c, s in zip(carry, ss)]
        ps = [jnp.exp2(s - m) for s, m in zip(ss, m_new)]
        return tuple((mn, jnp.exp2(m - mn) * acc + _bdot(p, vb)) for (m, acc), mn, p in zip(carry, m_new, ps))

    first = qi * per
    init = (jnp.full((rows, 1), -jnp.inf, F32), jnp.zeros((rows, D_V + LANES), F32))
    carry = lax.fori_loop(0, first, block, (init,) * groups)
    for d in range(per):
        carry = block(first + d, carry, mask_offset=d * tk)
    for g, (_, acc) in enumerate(carry):
        o_ref[0, g * rows:(g + 1) * rows, :] = (acc[:, :D_V] / acc[:, D_V:D_V + 1]).astype(o_ref.dtype)


def _mla_attn(q, k, v, tq=1024, tk=1024):
    b, hh, l, dq = q.shape
    tq = min(tq, l)
    tk = min(tk, tq)
    return pl.pallas_call(
        functools.partial(_mla_attn_kernel, tk=tk),
        grid=(b, hh, l // tq),
        in_specs=[pl.BlockSpec((1, 1, tq, dq), lambda bi, h, i: (bi, h, i, 0)),
                  pl.BlockSpec((1, 1, dq, l), lambda bi, h, i: (bi, h, 0, 0)),
                  pl.BlockSpec((1, 1, l, D_V + LANES), lambda bi, h, i: (bi, h, 0, 0))],
        out_specs=pl.BlockSpec((1, tq, D_V), lambda bi, h, i: (bi, i, h)),
        out_shape=jax.ShapeDtypeStruct((b, l, hh * D_V), BF16),
        compiler_params=_cparams("parallel", "parallel", "parallel"),
        name="mla_attn",
    )(q, k, v)


def _mla_ln(hid, w_dq, qnorm_w, w_uq, w_o, k, v, cos_t, sin_t, ln_g, ln_b):
    b, l, d = hid.shape
    x = hid.reshape(b * l, d)
    cq = _mla_cq(x, w_dq, qnorm_w).reshape(b, l, Q_LORA)
    q = _mla_q(cq, w_uq, cos_t, sin_t)
    o = _mla_attn(q, k, v)
    out, slabs = _matmul_ln(o.reshape(b * l, MLA_HEADS * D_V), w_o.astype(BF16), x, ln_g, ln_b)
    return out.reshape(b, l, d), slabs


def _first_argmax(x, ids, n):
    m = jnp.max(x, axis=0, keepdims=True)
    first = jnp.min(jnp.where(x == m, ids, n), axis=0, keepdims=True)
    return m, first


def _router_kernel(x_ref, wt_ref, bias_ref, idx_ref, w_ref, rank_ref, cnt_out_ref, cnt_ref, *, steps_per_tile):
    @pl.when(pl.program_id(0) % steps_per_tile == 0)
    def _():
        cnt_ref[...] = jnp.zeros_like(cnt_ref)

    x = x_ref[...]
    t = x.shape[0]
    logits = lax.dot_general(wt_ref[...], x, (((1,), (1,)), ((), ())),
                             precision=HIGHEST, preferred_element_type=F32)
    scores = jax.nn.sigmoid(logits)
    biased = scores + bias_ref[...]
    neg = -jnp.inf
    sub = lax.broadcasted_iota(jnp.int32, (GROUP_SIZE, t), 0).astype(F32)
    gscores = []
    for g in range(N_GROUPS):
        xg = biased[g * GROUP_SIZE:(g + 1) * GROUP_SIZE, :]
        m1, i1 = _first_argmax(xg, sub, float(GROUP_SIZE))
        m2 = jnp.max(jnp.where(sub == i1, neg, xg), axis=0, keepdims=True)
        gscores.append(m1 + m2)
    gs = jnp.concatenate(gscores, axis=0)
    gid = lax.broadcasted_iota(jnp.int32, (N_GROUPS, t), 0).astype(F32)
    gsel = jnp.zeros((N_GROUPS, t), F32)
    for _ in range(TOPK_GROUPS):
        _, gi = _first_argmax(gs, gid, float(N_GROUPS))
        hit = gid == gi
        gsel = jnp.where(hit, 1.0, gsel)
        gs = jnp.where(hit, neg, gs)
    eid = lax.broadcasted_iota(jnp.int32, (N_EXPERTS, t), 0).astype(F32)
    allowed = jnp.concatenate(
        [jnp.broadcast_to(gsel[g:g + 1, :], (GROUP_SIZE, t)) for g in range(N_GROUPS)], axis=0)
    cand = jnp.where(allowed > 0.0, biased, neg)
    idxs, ws, hits = [], [], []
    for _ in range(TOP_K):
        _, ei = _first_argmax(cand, eid, float(N_EXPERTS))
        hit = eid == ei
        idxs.append(ei)
        hits.append(hit)
        ws.append(jnp.sum(jnp.where(hit, scores, 0.0), axis=0, keepdims=True))
        cand = jnp.where(hit, neg, cand)
    w = jnp.concatenate(ws, axis=0)
    w = w / (jnp.sum(w, axis=0, keepdims=True) + 1e-20) * ROUTED_SCALE
    idx_ref[...] = jnp.concatenate(idxs, axis=0).astype(jnp.int32)
    w_ref[...] = w
    chosen = jnp.zeros((N_EXPERTS, t), F32)
    for hit in hits:
        chosen = jnp.where(hit, 1.0, chosen)
    before = (lax.broadcasted_iota(jnp.int32, (t, t), 0) < lax.broadcasted_iota(jnp.int32, (t, t), 1))
    prior = jnp.dot(chosen.astype(BF16), jnp.where(before, 1.0, 0.0).astype(BF16),
                    preferred_element_type=F32) + cnt_ref[...]
    rank_ref[...] = jnp.concatenate(
        [jnp.sum(jnp.where(hit, prior, 0.0), axis=0, keepdims=True) for hit in hits], axis=0).astype(jnp.int32)
    cnt_ref[...] += jnp.sum(chosen, axis=1, keepdims=True)
    cnt_out_ref[0] = cnt_ref[...]


def _router(x, w_router, bias, tile_tokens, tr=512):
    t, d = x.shape
    tr = min(tr, tile_tokens)
    steps_per_tile = tile_tokens // tr
    n_tiles = t // tile_tokens
    kt = pl.BlockSpec((TOP_K, tr), lambda i: (0, i))
    idx, w, rank, counts = pl.pallas_call(
        functools.partial(_router_kernel, steps_per_tile=steps_per_tile),
        grid=(t // tr,),
        in_specs=[pl.BlockSpec((tr, d), lambda i: (i, 0)),
                  pl.BlockSpec((N_EXPERTS, d), lambda i: (0, 0)),
                  pl.BlockSpec((N_EXPERTS, 1), lambda i: (0, 0))],
        out_specs=[kt, kt, kt,
                   pl.BlockSpec((1, N_EXPERTS, 1), lambda i: (i // steps_per_tile, 0, 0))],
        out_shape=[jax.ShapeDtypeStruct((TOP_K, t), jnp.int32),
                   jax.ShapeDtypeStruct((TOP_K, t), F32),
                   jax.ShapeDtypeStruct((TOP_K, t), jnp.int32),
                   jax.ShapeDtypeStruct((n_tiles, N_EXPERTS, 1), F32)],
        scratch_shapes=[pltpu.VMEM((N_EXPERTS, 1), F32)],
        compiler_params=_cparams("arbitrary"),
        name="moe_router",
    )(x, w_router.T, bias.reshape(N_EXPERTS, 1))
    return idx, w, rank, counts.reshape(n_tiles, N_EXPERTS).astype(jnp.int32)


def _steps_per_tile(tile_tokens):
    return tile_tokens * TOP_K // STEP_ROWS + N_EXPERTS


def _plan_kernel(fill_lo_ref, fill_hi_ref, pos_ref, src_ref, *, tile_tokens):
    i = pl.program_id(0)
    n_assign = tile_tokens * TOP_K
    empty = n_assign
    group = 16

    def fill_range(g, carry):
        lo = fill_lo_ref[i * N_EXPERTS + g]
        hi = fill_hi_ref[i * N_EXPERTS + g]

        def fill(b, carry):
            for d in range(group):
                src_ref[0, 0, jnp.maximum(hi - 1 - b * group - d, 0)] = empty
            return carry
        return lax.fori_loop(0, (hi - lo + group - 1) // group, fill, carry)
    lax.fori_loop(0, N_EXPERTS, fill_range, 0)

    def place(b, carry):
        base = b * group
        slots = [pos_ref[0, 0, base + d] for d in range(group)]
        for d in range(group):
            src_ref[0, 0, slots[d]] = base + d
        return carry
    lax.fori_loop(0, n_assign // group, place, 0)


def _dispatch_plan(idx_t, w_t, rank_t, counts, tile_tokens):
    t = idx_t.shape[1]
    n_tiles = t // tile_tokens
    steps = _steps_per_tile(tile_tokens)
    padded = (counts + STEP_ROWS - 1) // STEP_ROWS * STEP_ROWS
    pad_end = jnp.cumsum(padded, axis=1)
    pad_start = (pad_end - padded).astype(jnp.int32)
    n_used = (pad_end[:, -1] // STEP_ROWS).astype(jnp.int32)
    starts = jnp.arange(steps, dtype=jnp.int32) * STEP_ROWS
    step_e = jnp.sum(starts[None, :, None] >= pad_end[:, None, :], axis=-1).astype(jnp.int32)
    step_e = jnp.minimum(step_e, N_EXPERTS - 1)
    last = jnp.take_along_axis(step_e, jnp.maximum(n_used - 1, 0)[:, None], axis=1)
    step_e = jnp.where(starts[None, :] // STEP_ROWS < n_used[:, None], step_e, last)

    start_of = jnp.repeat(pad_start.T, tile_tokens, axis=1)
    experts = jnp.arange(N_EXPERTS, dtype=jnp.int32)[None, :, None]
    pos_t = rank_t + jnp.sum(jnp.where(idx_t[:, None, :] == experts, start_of[None], 0), axis=1)
    fill_lo = (pad_start + counts).astype(jnp.int32)
    fill_hi = pad_end.astype(jnp.int32)

    n_assign = tile_tokens * TOP_K
    n_slots = steps * STEP_ROWS
    single = pl.Buffered(1)
    src = pl.pallas_call(
        functools.partial(_plan_kernel, tile_tokens=tile_tokens),
        grid_spec=pltpu.PrefetchScalarGridSpec(
            num_scalar_prefetch=2,
            grid=(n_tiles,),
            in_specs=[pl.BlockSpec((1, 1, n_assign), lambda i, lo, hi: (i, 0, 0),
                                   memory_space=pltpu.SMEM, pipeline_mode=single)],
            out_specs=pl.BlockSpec((1, 1, n_slots), lambda i, lo, hi: (i, 0, 0),
                                   memory_space=pltpu.SMEM, pipeline_mode=single),
        ),
        out_shape=jax.ShapeDtypeStruct((n_tiles, 1, n_slots), jnp.int32),
        compiler_params=_cparams("parallel"),
        name="moe_plan",
    )(fill_lo.reshape(-1), fill_hi.reshape(-1), pos_t.T.reshape(n_tiles, 1, n_assign))
    src = jnp.where(jnp.arange(n_slots, dtype=jnp.int32)[None, None, :] < pad_end[:, -1][:, None, None],
                    src, tile_tokens * TOP_K)
    rows = src & -SUBLANES
    w_flat = jnp.pad(w_t.T.reshape(n_tiles, tile_tokens * TOP_K), ((0, 0), (0, SUBLANES)))
    slot_w = jnp.take_along_axis(w_flat, src[:, 0, :], axis=1)
    held = jnp.clip(jnp.take_along_axis(fill_lo, step_e, axis=1) - starts[None, :], 0, STEP_ROWS)
    live = jnp.where(starts[None, :] // STEP_ROWS < n_used[:, None], (held + EXPERT_ROWS - 1) // EXPERT_ROWS, 0)
    return rows, slot_w, step_e.reshape(-1), n_used, live.reshape(-1).astype(jnp.int32), steps


def _experts_kernel(step_e_ref, n_used_ref, live_ref, row_ref, sw_ref, x_ref, wg_ref, wu_ref, wd_ref, acc_ref,
                    *stage_refs):
    n_sub = BLOCKS_PER_STEP
    xs_refs = stage_refs[:n_sub]
    ys_refs = stage_refs[n_sub:]
    i = pl.program_id(0)
    j = pl.program_id(1)
    rows = EXPERT_ROWS
    pitch = STAGE_PITCH

    @pl.when(j == 0)
    def _():
        acc_ref[...] = jnp.zeros_like(acc_ref)

    last_row = x_ref.shape[1] - SUBLANES

    def token_rows(slot, limit=None):
        row = row_ref[0, 0, slot]
        if limit is not None:
            row = jnp.minimum(row, limit)
        return pl.ds(pl.multiple_of(row, SUBLANES), SUBLANES)

    def run(subs):
        for s in subs:
            for r in range(rows):
                xs_refs[s][pl.ds(r, ROW_CHUNKS, stride=pitch), :] = x_ref[0, token_rows(s * rows + r, last_row), :]
        wg, wu, wd = (ref[0, 0].astype(BF16) for ref in (wg_ref, wu_ref, wd_ref))
        eye = (lax.broadcasted_iota(jnp.int32, (rows, rows), 0)
               == lax.broadcasted_iota(jnp.int32, (rows, rows), 1))
        ys = {}
        for s in subs:
            x = jnp.concatenate([xs_refs[s][pl.ds(c * pitch, rows), :] for c in range(ROW_CHUNKS)], axis=1)
            hid = _silu(_bdot(x, wg)) * _bdot(x, wu)
            w_row = sw_ref[0, :, s * rows:(s + 1) * rows]
            w_col = jnp.sum(jnp.where(eye, w_row, 0.0), axis=1, keepdims=True)
            ys[s] = _bdot(hid, wd) * w_col
        for s in subs:
            for c in range(ROW_CHUNKS):
                ys_refs[s][pl.ds(c * pitch, rows), :] = ys[s][:, c * LANES:(c + 1) * LANES]
        for s in subs:
            for r0 in range(0, rows, SCATTER_BATCH):
                batch = range(r0, r0 + SCATTER_BATCH)
                ats = [token_rows(s * rows + r) for r in batch]
                new = [acc_ref[0, at, :] + ys_refs[s][pl.ds(r, ROW_CHUNKS, stride=pitch), :]
                       for at, r in zip(ats, batch)]
                for at, val in zip(ats, new):
                    acc_ref[0, at, :] = val

    live = live_ref[i * pl.num_programs(1) + j]
    for n_live in range(1, n_sub + 1):
        pl.when(live == n_live)(functools.partial(run, range(n_live)))


def _routed_experts(x_slabs, idx_t, w_t, rank_t, counts, w_gate_up, w_down, layer, tile_tokens):
    d = D_MODEL
    t = x_slabs.shape[0] // ROW_CHUNKS
    n_tiles = t // tile_tokens
    rows, slot_w, step_e, n_used, live, steps = _dispatch_plan(idx_t, w_t, rank_t, counts, tile_tokens)
    rows = rows.reshape(n_tiles * steps, 1, STEP_ROWS)
    slot_w = slot_w.reshape(n_tiles * steps, 1, STEP_ROWS)
    x_slabs = x_slabs.reshape(n_tiles, tile_tokens * ROW_CHUNKS, LANES)
    slab_rows = (tile_tokens + 1) * ROW_CHUNKS

    def used_step(i, j, se, nu, lv):
        return (i * steps + jnp.minimum(j, jnp.maximum(nu[i] - 1, 0)), 0, 0)

    pick = lambda i, j, se, nu, lv: (layer, se[i * steps + j], 0, 0)
    tile = lambda i, j, se, nu, lv: (i, 0, 0)
    resident = pl.Buffered(1)
    stage = pltpu.VMEM((ROW_CHUNKS * STAGE_PITCH, LANES), F32)
    grid_spec = pltpu.PrefetchScalarGridSpec(
        num_scalar_prefetch=3,
        grid=(n_tiles, steps),
        in_specs=[
            pl.BlockSpec((1, 1, STEP_ROWS), used_step, memory_space=pltpu.SMEM),
            pl.BlockSpec((1, 1, STEP_ROWS), used_step),
            pl.BlockSpec((1, tile_tokens * ROW_CHUNKS, LANES), tile, pipeline_mode=resident),
            pl.BlockSpec((1, 1, d, D_EXPERT), pick),
            pl.BlockSpec((1, 1, d, D_EXPERT), pick),
            pl.BlockSpec((1, 1, D_EXPERT, d), pick),
        ],
        out_specs=pl.BlockSpec((1, slab_rows, LANES), tile, pipeline_mode=resident),
        scratch_shapes=[stage] * (2 * BLOCKS_PER_STEP),
    )
    return pl.pallas_call(
        _experts_kernel,
        grid_spec=grid_spec,
        out_shape=jax.ShapeDtypeStruct((n_tiles, slab_rows, LANES), F32),
        compiler_params=_cparams("parallel", "arbitrary"),
        name="moe_experts",
    )(step_e, n_used, live, rows, slot_w, x_slabs, *w_gate_up, w_down)


def _moe_out_kernel(x_ref, r_ref, wg_ref, wu_ref, wd_ref, g_ref, b_ref, o_ref):
    x = x_ref[...]
    tm = x.shape[0]
    hid = _silu(_bdot(x, wg_ref[...])) * _bdot(x, wu_ref[...])
    routed = jnp.concatenate([r_ref[0, pl.ds(c, tm, stride=ROW_CHUNKS), :] for c in range(ROW_CHUNKS)], axis=1)
    y = routed + _bdot(hid, wd_ref[...])
    o_ref[...] = _layer_norm(DN_ALPHA * x + y, g_ref[...], b_ref[...])


def _moe_out(x, routed, ws_gate, ws_up, ws_down, g, b, tile_tokens, tm=512):
    t, d = x.shape
    tm = min(tm, tile_tokens)
    per_tile = tile_tokens // tm
    return pl.pallas_call(
        _moe_out_kernel,
        grid=(t // tm,),
        in_specs=[pl.BlockSpec((tm, d), lambda i: (i, 0)),
                  pl.BlockSpec((1, tm * ROW_CHUNKS, LANES), lambda i: (i // per_tile, i % per_tile, 0)),
                  pl.BlockSpec((d, D_SHARED), lambda i: (0, 0)),
                  pl.BlockSpec((d, D_SHARED), lambda i: (0, 0)),
                  pl.BlockSpec((D_SHARED, d), lambda i: (0, 0)),
                  pl.BlockSpec((1, d), lambda i: (0, 0)),
                  pl.BlockSpec((1, d), lambda i: (0, 0))],
        out_specs=pl.BlockSpec((tm, d), lambda i: (i, 0)),
        out_shape=jax.ShapeDtypeStruct((t, d), F32),
        compiler_params=_cparams("parallel"),
        name="moe_out",
    )(x, routed, ws_gate.astype(BF16), ws_up.astype(BF16), ws_down.astype(BF16),
      g.reshape(1, d), b.reshape(1, d))


def _moe_ln(hid, hid_slabs, w_router, router_bias, w_gate_up, w_down, layer, ws_gate, ws_up, ws_down, ln_g, ln_b):
    b, l, d = hid.shape
    x = hid.reshape(b * l, d)
    tile_tokens = min(MOE_TILE_TOKENS, b * l)
    idx_t, w_t, rank_t, counts = _router(x, w_router, router_bias, tile_tokens)
    routed = _routed_experts(hid_slabs, idx_t, w_t, rank_t, counts, w_gate_up, w_down, layer, tile_tokens)
    return _moe_out(x, routed, ws_gate, ws_up, ws_down, ln_g, ln_b, tile_tokens).reshape(b, l, d)


def kernel(x, a_w_in, a_conv_w, a_A_log, a_dt_bias, a_onorm_w, a_w_out, b_w_dq, b_qnorm_w, b_w_uq, b_w_o,
           kv_w_dkv, kv_norm_w, kv_w_ukv, ln1_g, ln1_b, ln2_g, ln2_b, moe_w_router, moe_router_bias,
           moe_w_gate, moe_w_up, moe_w_down, moe_ws_gate, moe_ws_up, moe_ws_down):
    l = x.shape[1]
    cos_t, sin_t = _rope_lane_tables(l)
    w_gate_up = (moe_w_gate, moe_w_up)
    w_down = moe_w_down
    h = x
    k = v = None
    for layer in range(DEPTH):
        if layer < N_A_LAYERS:
            i = layer
            h, slabs = _gated_deltanet_ln(h, a_w_in[i], a_conv_w[i], a_A_log[i], a_dt_bias[i], a_onorm_w[i],
                                          a_w_out[i], ln1_g[layer], ln1_b[layer])
        else:
            i = layer - N_A_LAYERS
            h, slabs = _mla_ln(h, b_w_dq[i], b_qnorm_w[i], b_w_uq[i], b_w_o[i], k, v, cos_t, sin_t,
                               ln1_g[layer], ln1_b[layer])
        h = _moe_ln(h, slabs, moe_w_router[layer], moe_router_bias[layer], w_gate_up, w_down, layer,
                    moe_ws_gate[layer], moe_ws_up[layer], moe_ws_down[layer], ln2_g[layer], ln2_b[layer])
        if layer == N_A_LAYERS - 1:
            k, v = _mla_kv(h, kv_w_dkv, kv_norm_w, kv_w_ukv, cos_t, sin_t)
    return h
```

```python
import functools

import jax
import jax.numpy as jnp
import numpy as np
from jax import lax
from jax.experimental import pallas as pl
from jax.experimental.pallas import tpu as pltpu

F32 = jnp.float32
BF16 = jnp.bfloat16
HIGHEST = lax.Precision.HIGHEST

D_MODEL = 1024
DEPTH = 4
N_A_LAYERS = DEPTH // 2
GDN_HEADS = 8
GDN_DK = 128
GDN_DV = 128
D_QK = GDN_HEADS * GDN_DK
D_VA = GDN_HEADS * GDN_DV
GDN_CONV_CH = 2 * D_QK + D_VA
CONV_WIDTH = 4
CHUNK = 64
MLA_HEADS = 8
Q_LORA = 512
KV_LORA = 256
D_NOPE = 128
D_ROPE = 64
D_V = 128
ROPE_THETA = 10000.0
N_EXPERTS = 64
TOP_K = 8
N_GROUPS = 8
GROUP_SIZE = N_EXPERTS // N_GROUPS
TOPK_GROUPS = 4
D_EXPERT = 256
D_SHARED = 256
ROUTED_SCALE = 2.5
DN_ALPHA = (2 * DEPTH) ** 0.25
EPS = 1e-6
LN_EPS = 1e-5
LOG2_E = 1.4426950408889634

LANES = 128
SUBLANES = 8
VMEM_LIMIT_BYTES = 56 * 1024 * 1024

EXPERT_ROWS = 128
ROW_CHUNKS = D_MODEL // LANES
STAGE_PITCH = EXPERT_ROWS + 1
SCATTER_BATCH = 8
BLOCKS_PER_STEP = 2
STEP_ROWS = BLOCKS_PER_STEP * EXPERT_ROWS
MOE_TILE_TOKENS = 4096
ATTN_ROW_GROUPS = 4
assert TOP_K == SUBLANES


def _cparams(*sem):
    return pltpu.CompilerParams(dimension_semantics=sem, vmem_limit_bytes=VMEM_LIMIT_BYTES)


def _silu(x):
    return x * jax.nn.sigmoid(x)


def _layer_norm(x, g, b):
    mu = jnp.mean(x, -1, keepdims=True)
    xc = x - mu
    var = jnp.mean(xc * xc, -1, keepdims=True)
    return xc * lax.rsqrt(var + LN_EPS) * g + b


def _rms_norm(x, w):
    return x * lax.rsqrt(jnp.mean(x * x, -1, keepdims=True) + EPS) * w


def _bdot(a, b):
    return jnp.dot(a.astype(BF16), b.astype(BF16), preferred_element_type=F32)


def _bdot_nt(a, b):
    return lax.dot_general(a.astype(BF16), b.astype(BF16), (((1,), (1,)), ((), ())),
                           preferred_element_type=F32)


def _hdot(a, b):
    return jnp.dot(a, b, precision=HIGHEST, preferred_element_type=F32)


def _matmul_kernel(x_ref, w_ref, o_ref):
    o_ref[...] = _bdot(x_ref[...], w_ref[...]).astype(o_ref.dtype)


def _matmul(x, w, out_dtype, tm, tn):
    m, k = x.shape
    n = w.shape[1]
    tm = min(tm, m)
    tn = min(tn, n)
    return pl.pallas_call(
        _matmul_kernel,
        grid=(m // tm, n // tn),
        in_specs=[pl.BlockSpec((tm, k), lambda i, j: (i, 0)),
                  pl.BlockSpec((k, tn), lambda i, j: (0, j))],
        out_specs=pl.BlockSpec((tm, tn), lambda i, j: (i, j)),
        out_shape=jax.ShapeDtypeStruct((m, n), out_dtype),
        compiler_params=_cparams("parallel", "parallel"),
        name="matmul",
    )(x, w)


def _matmul_ln_kernel(x_ref, w_ref, r_ref, g_ref, b_ref, o_ref, slab_ref):
    mix = _bdot(x_ref[...], w_ref[...])
    out = _layer_norm(DN_ALPHA * r_ref[...] + mix, g_ref[...], b_ref[...])
    o_ref[...] = out
    tm = out.shape[0]
    for c in range(ROW_CHUNKS):
        slab_ref[pl.ds(c, tm, stride=ROW_CHUNKS), :] = out[:, c * LANES:(c + 1) * LANES]


def _matmul_ln(x, w, res, g, b, tm=512):
    m, k = x.shape
    n = w.shape[1]
    tm = min(tm, m)
    chunks = n // LANES
    return pl.pallas_call(
        _matmul_ln_kernel,
        grid=(m // tm,),
        in_specs=[pl.BlockSpec((tm, k), lambda i: (i, 0)),
                  pl.BlockSpec((k, n), lambda i: (0, 0)),
                  pl.BlockSpec((tm, n), lambda i: (i, 0)),
                  pl.BlockSpec((1, n), lambda i: (0, 0)),
                  pl.BlockSpec((1, n), lambda i: (0, 0))],
        out_specs=[pl.BlockSpec((tm, n), lambda i: (i, 0)),
                   pl.BlockSpec((tm * chunks, LANES), lambda i: (i, 0))],
        out_shape=[jax.ShapeDtypeStruct((m, n), F32),
                   jax.ShapeDtypeStruct((m * chunks, LANES), F32)],
        compiler_params=_cparams("parallel"),
        name="matmul_ln",
    )(x, w, res, g.reshape(1, n), b.reshape(1, n))


def _softplus(x):
    return jnp.maximum(x, 0.0) + jnp.log1p(jnp.exp(-jnp.abs(x)))


def _gdn_gates_kernel(x_ref, wab_ref, wabt_ref, alog_ref, dtb_ref, alogt_ref, dtbt_ref,
                      gc_ref, beta_ref, gct_ref):
    x = x_ref[...]
    tl = x.shape[0]
    h = GDN_HEADS
    ab = _bdot(x, wab_ref[...])
    abt = _bdot_nt(wabt_ref[...], x)
    g = -jnp.exp(alog_ref[...]) * _softplus(ab[:, :h] + dtb_ref[...])
    gt = -jnp.exp(alogt_ref[...]) * _softplus(abt[:h, :] + dtbt_ref[...])
    beta_ref[...] = jax.nn.sigmoid(ab[:, h:2 * h])
    row = lax.broadcasted_iota(jnp.int32, (tl, tl), 0)
    col = lax.broadcasted_iota(jnp.int32, (tl, tl), 1)
    shift = CHUNK.bit_length() - 1
    same = (row >> shift) == (col >> shift)
    lower = jnp.where(same & (col <= row), 1.0, 0.0).astype(F32)
    upper = jnp.where(same & (row <= col), 1.0, 0.0).astype(F32)
    gc_ref[...] = _hdot(lower, g)
    gct_ref[...] = _hdot(gt, upper)


def _gdn_gates(x, w_ab, a_log, dt_bias, tl=512):
    t = x.shape[0]
    tl = min(tl, t)
    h = GDN_HEADS
    wab = jnp.zeros((D_MODEL, LANES), F32).at[:, :2 * h].set(w_ab)
    wabt = w_ab.T
    return pl.pallas_call(
        _gdn_gates_kernel,
        grid=(t // tl,),
        in_specs=[pl.BlockSpec((tl, D_MODEL), lambda i: (i, 0)),
                  pl.BlockSpec((D_MODEL, LANES), lambda i: (0, 0)),
                  pl.BlockSpec((2 * h, D_MODEL), lambda i: (0, 0)),
                  pl.BlockSpec((1, h), lambda i: (0, 0)),
                  pl.BlockSpec((1, h), lambda i: (0, 0)),
                  pl.BlockSpec((h, 1), lambda i: (0, 0)),
                  pl.BlockSpec((h, 1), lambda i: (0, 0))],
        out_specs=[pl.BlockSpec((tl, h), lambda i: (i, 0)),
                   pl.BlockSpec((tl, h), lambda i: (i, 0)),
                   pl.BlockSpec((h, tl), lambda i: (0, i))],
        out_shape=[jax.ShapeDtypeStruct((t, h), F32),
                   jax.ShapeDtypeStruct((t, h), F32),
                   jax.ShapeDtypeStruct((h, t), F32)],
        compiler_params=_cparams("parallel"),
        name="gdn_gates",
    )(x, wab, wabt, a_log.reshape(1, h), dt_bias.reshape(1, h),
      a_log.reshape(h, 1), dt_bias.reshape(h, 1))


def _gdn_conv_kernel(cur_ref, prev_ref, w_ref, q_ref, k_ref, v_ref):
    i = pl.program_id(1)
    x = cur_ref[0]
    tl = x.shape[0]
    prev = jnp.where(i > 0, prev_ref[0], 0.0)
    xx = jnp.concatenate([prev, x], axis=0)
    w = w_ref[...]
    y = None
    for j in range(CONV_WIDTH):
        off = SUBLANES - (CONV_WIDTH - 1) + j
        term = xx[off:off + tl, :] * w[j:j + 1, :]
        y = term if y is None else y + term
    y = _silu(y)
    for h in range(GDN_HEADS):
        sl = slice(h * GDN_DK, (h + 1) * GDN_DK)
        qh = y[:, sl]
        q_ref[0, :, sl] = qh * lax.rsqrt(jnp.sum(qh * qh, -1, keepdims=True) + EPS) * (GDN_DK ** -0.5)
        kh = y[:, D_QK + h * GDN_DK:D_QK + (h + 1) * GDN_DK]
        k_ref[0, :, sl] = kh * lax.rsqrt(jnp.sum(kh * kh, -1, keepdims=True) + EPS)
    v_ref[0] = y[:, 2 * D_QK:]


def _gdn_conv(proj, conv_w, tl=256):
    b, l, _ = proj.shape
    tl = min(tl, l)
    c3 = GDN_CONV_CH
    per = tl // SUBLANES
    out = jax.ShapeDtypeStruct((b, l, D_QK), F32)
    return pl.pallas_call(
        _gdn_conv_kernel,
        grid=(b, l // tl),
        in_specs=[pl.BlockSpec((1, tl, c3), lambda bi, i: (bi, i, 0)),
                  pl.BlockSpec((1, SUBLANES, c3), lambda bi, i: (bi, jnp.maximum(i * per - 1, 0), 0)),
                  pl.BlockSpec((CONV_WIDTH, c3), lambda bi, i: (0, 0))],
        out_specs=[pl.BlockSpec((1, tl, D_QK), lambda bi, i: (bi, i, 0))] * 3,
        out_shape=[out, out, out],
        compiler_params=_cparams("parallel", "parallel"),
        name="gdn_conv",
    )(proj, proj, conv_w)


def _split_bf16(x):
    hi = x.astype(BF16)
    lo = (x - hi.astype(F32)).astype(BF16)
    return hi, lo


def _dot_split(a, b):
    (ah, al), (bh, bl) = a, b
    dot = functools.partial(jnp.dot, preferred_element_type=F32)
    return dot(ah, bh) + (dot(ah, bl) + dot(al, bh))


def _gdn_prep_kernel(q_ref, k_ref, v_ref, gc_ref, beta_ref, gct_ref, wq_ref, u_ref, attn_ref, kgt_ref,
                     *, chunks):
    c = CHUNK
    row = lax.broadcasted_iota(jnp.int32, (c, c), 0)
    col = lax.broadcasted_iota(jnp.int32, (c, c), 1)
    incl = row >= col
    strict = row > col
    eye = jnp.where(row == col, 1.0, 0.0).astype(F32)
    items = [(ci, h) for ci in range(chunks) for h in range(GDN_HEADS)]

    st = {}
    for ci, h in items:
        rows = slice(ci * c, (ci + 1) * c)
        sl = slice(h * GDN_DK, (h + 1) * GDN_DK)
        q = q_ref[0, rows, sl]
        k = k_ref[0, rows, sl]
        v = v_ref[0, rows, sl]
        gcol = gc_ref[0, rows, h:h + 1]
        grow = gct_ref[0, ci, h:h + 1, :]
        bcol = beta_ref[0, rows, h:h + 1]
        decay = jnp.where(incl, jnp.exp(jnp.where(incl, gcol - grow, 0.0)), 0.0)
        kb = k * bcol
        eg = jnp.exp(gcol)
        g_last = gcol[c - 1:c, :]
        sc = _bdot_nt(jnp.concatenate([kb, q], axis=0), k)
        a = jnp.where(strict, sc[:c] * decay, 0.0)
        attn_ref[0, ci, h] = (sc[c:] * decay).astype(attn_ref.dtype)
        kgt_ref[0, ci, h] = (k * jnp.exp(g_last - gcol)).T.astype(kgt_ref.dtype)
        wq_ref[0, ci, h, c:, :] = (q * eg).astype(wq_ref.dtype)
        st[ci, h] = dict(m=-a, rhs=jnp.concatenate([v * bcol, kb * eg], axis=1))

    for it in items:
        st[it]["p"] = eye + st[it]["m"]
    span = 2
    while span < c:
        for it in items:
            ms = _split_bf16(st[it]["m"])
            st[it]["m"] = _dot_split(ms, ms)
        for it in items:
            st[it]["p"] = st[it]["p"] + _dot_split(_split_bf16(st[it]["p"]), _split_bf16(st[it]["m"]))
        span *= 2

    for ci, h in items:
        sol = _dot_split(_split_bf16(st[ci, h]["p"]), _split_bf16(st[ci, h]["rhs"]))
        u_ref[0, ci * c:(ci + 1) * c, h * GDN_DV:(h + 1) * GDN_DV] = sol[:, :GDN_DV]
        wq_ref[0, ci, h, :c, :] = sol[:, GDN_DV:].astype(wq_ref.dtype)


def _gdn_prep(q, k, v, gc, beta, gct, chunks=2):
    b, l, _ = q.shape
    n = l // CHUNK
    chunks = min(chunks, n)
    h = GDN_HEADS
    rows = chunks * CHUNK
    blk = pl.BlockSpec((1, rows, D_QK), lambda bi, i: (bi, i, 0))
    gate = pl.BlockSpec((1, rows, h), lambda bi, i: (bi, i, 0))
    return pl.pallas_call(
        functools.partial(_gdn_prep_kernel, chunks=chunks),
        grid=(b, n // chunks),
        in_specs=[blk, blk, blk, gate, gate,
                  pl.BlockSpec((1, chunks, h, CHUNK), lambda bi, i: (bi, i, 0, 0))],
        out_specs=[pl.BlockSpec((1, chunks, h, 2 * CHUNK, GDN_DK), lambda bi, i: (bi, i, 0, 0, 0)),
                   pl.BlockSpec((1, rows, D_VA), lambda bi, i: (bi, i, 0)),
                   pl.BlockSpec((1, chunks, h, CHUNK, CHUNK), lambda bi, i: (bi, i, 0, 0, 0)),
                   pl.BlockSpec((1, chunks, h, GDN_DK, CHUNK), lambda bi, i: (bi, i, 0, 0, 0))],
        out_shape=[jax.ShapeDtypeStruct((b, n, h, 2 * CHUNK, GDN_DK), BF16),
                   jax.ShapeDtypeStruct((b, l, D_VA), F32),
                   jax.ShapeDtypeStruct((b, n, h, CHUNK, CHUNK), BF16),
                   jax.ShapeDtypeStruct((b, n, h, GDN_DK, CHUNK), BF16)],
        compiler_params=_cparams("parallel", "parallel"),
        name="gdn_prep",
    )(q, k, v, gc, beta, gct)


def _gdn_state_kernel(wq_ref, u_ref, attn_ref, kgt_ref, gc_ref, z_ref, onw_ref, o_ref, s_ref, *, chunks):
    @pl.when(pl.program_id(1) == 0)
    def _():
        s_ref[...] = jnp.zeros_like(s_ref)

    c = CHUNK
    onw = onw_ref[...]
    dot = functools.partial(jnp.dot, preferred_element_type=F32)
    heads = range(GDN_HEADS)
    for ci in range(chunks):
        rows = slice(ci * c, (ci + 1) * c)
        s_old = [s_ref[h] for h in heads]
        r = [dot(wq_ref[0, ci, h], s_old[h].astype(BF16)) for h in heads]
        v_new = [(u_ref[0, rows, h * GDN_DV:(h + 1) * GDN_DV] - r[h][:c]).astype(BF16) for h in heads]
        for h in heads:
            decay_last = jnp.exp(gc_ref[0, (ci + 1) * c - 1:(ci + 1) * c, h:h + 1])
            s_ref[h] = s_old[h] * decay_last + dot(kgt_ref[0, ci, h], v_new[h])
        for h in heads:
            sl = slice(h * GDN_DV, (h + 1) * GDN_DV)
            o = r[h][c:] + dot(attn_ref[0, ci, h], v_new[h])
            o_ref[0, rows, sl] = (_rms_norm(o, onw) * _silu(z_ref[0, rows, sl])).astype(o_ref.dtype)


def _gdn_state(wq, u, attn, kgt, gc, proj, onorm_w, chunks=8):
    b, n, h = wq.shape[:3]
    l = n * CHUNK
    chunks = min(chunks, n)
    rows = chunks * CHUNK
    z_block = GDN_CONV_CH // D_VA
    return pl.pallas_call(
        functools.partial(_gdn_state_kernel, chunks=chunks),
        grid=(b, n // chunks),
        in_specs=[pl.BlockSpec((1, chunks, h, 2 * CHUNK, GDN_DK), lambda bi, i: (bi, i, 0, 0, 0)),
                  pl.BlockSpec((1, rows, D_VA), lambda bi, i: (bi, i, 0)),
                  pl.BlockSpec((1, chunks, h, CHUNK, CHUNK), lambda bi, i: (bi, i, 0, 0, 0)),
                  pl.BlockSpec((1, chunks, h, GDN_DK, CHUNK), lambda bi, i: (bi, i, 0, 0, 0)),
                  pl.BlockSpec((1, rows, h), lambda bi, i: (bi, i, 0)),
                  pl.BlockSpec((1, rows, D_VA), lambda bi, i: (bi, i, z_block)),
                  pl.BlockSpec((1, GDN_DV), lambda bi, i: (0, 0))],
        out_specs=pl.BlockSpec((1, rows, D_VA), lambda bi, i: (bi, i, 0)),
        out_shape=jax.ShapeDtypeStruct((b, l, D_VA), BF16),
        scratch_shapes=[pltpu.VMEM((h, GDN_DK, GDN_DV), F32)],
        compiler_params=_cparams("parallel", "arbitrary"),
        name="gdn_state",
    )(wq, u, attn, kgt, gc, proj, onorm_w.reshape(1, GDN_DV))


def _gated_deltanet_ln(hid, w_in, conv_w, a_log, dt_bias, onorm_w, w_out, ln_g, ln_b):
    b, l, d = hid.shape
    h = GDN_HEADS
    n = l // CHUNK
    x = hid.reshape(b * l, d)
    n_main = GDN_CONV_CH + D_VA
    proj = _matmul(x, w_in[:, :n_main].astype(BF16), F32, 1024, 1024).reshape(b, l, n_main)
    gc, beta, gct = _gdn_gates(x, w_in[:, n_main:], a_log, dt_bias)
    gc = gc.reshape(b, l, h)
    gct = gct.reshape(h, b, n, CHUNK).transpose(1, 2, 0, 3)
    q, k, v = _gdn_conv(proj, conv_w)
    wq, u, attn, kgt = _gdn_prep(q, k, v, gc, beta.reshape(b, l, h), gct)
    o = _gdn_state(wq, u, attn, kgt, gc, proj, onorm_w)
    out, slabs = _matmul_ln(o.reshape(b * l, D_VA), w_out.astype(BF16), x, ln_g, ln_b)
    return out.reshape(b, l, d), slabs


def _rope_lane_tables(l):
    inv = 1.0 / (ROPE_THETA ** (jnp.arange(0, D_ROPE, 2, dtype=F32) / D_ROPE))
    ang = jnp.arange(l, dtype=F32)[:, None] * inv[None, :]
    cos, sin = jnp.cos(ang), jnp.sin(ang)
    zero = jnp.zeros((l, LANES - D_ROPE), F32)
    return (jnp.concatenate([cos, cos, zero], -1),
            jnp.concatenate([-sin, sin, zero], -1))


def _rope_weight_groups(w_rope):
    kdim = w_rope.shape[0]
    half = D_ROPE // 2
    zero = jnp.zeros((kdim, LANES - D_ROPE), w_rope.dtype)
    x1, x2 = w_rope[:, :half], w_rope[:, half:]
    return jnp.concatenate([x1, x2, zero, x2, x1, zero], axis=-1)


def _mla_kv_kernel(x_ref, wd_ref, nw_ref, wu_ref, cos_ref, sin_ref, k_ref, v_ref):
    x = x_ref[0]
    ckv = _bdot(x, wd_ref[...])
    c = _rms_norm(ckv[:, :KV_LORA], nw_ref[...])
    k_rope_t = (ckv[:, KV_LORA:KV_LORA + LANES] * cos_ref[...]
                + ckv[:, KV_LORA + LANES:] * sin_ref[...]).T.astype(k_ref.dtype)
    kv = _bdot(c, wu_ref[...])
    per = D_NOPE + D_V
    ones_col = jnp.where(lax.broadcasted_iota(jnp.int32, (x.shape[0], LANES), 1) == 0, 1.0, 0.0).astype(v_ref.dtype)
    for h in range(MLA_HEADS):
        k_ref[0, h, :D_NOPE, :] = kv[:, h * per:h * per + D_NOPE].T.astype(k_ref.dtype)
        k_ref[0, h, D_NOPE:, :] = k_rope_t
        v_ref[0, h, :, :D_V] = kv[:, h * per + D_NOPE:(h + 1) * per].astype(v_ref.dtype)
        v_ref[0, h, :, D_V:] = ones_col


def _mla_kv(hid, w_dkv, kv_norm_w, w_ukv, cos_t, sin_t, tl=512):
    b, l, d = hid.shape
    tl = min(tl, l)
    wd = jnp.concatenate([w_dkv[:, :KV_LORA], _rope_weight_groups(w_dkv[:, KV_LORA:])], -1).astype(BF16)
    nd = wd.shape[1]
    hh = MLA_HEADS
    return pl.pallas_call(
        _mla_kv_kernel,
        grid=(b, l // tl),
        in_specs=[pl.BlockSpec((1, tl, d), lambda bi, i: (bi, i, 0)),
                  pl.BlockSpec((d, nd), lambda bi, i: (0, 0)),
                  pl.BlockSpec((1, KV_LORA), lambda bi, i: (0, 0)),
                  pl.BlockSpec((KV_LORA, hh * (D_NOPE + D_V)), lambda bi, i: (0, 0)),
                  pl.BlockSpec((tl, LANES), lambda bi, i: (i, 0)),
                  pl.BlockSpec((tl, LANES), lambda bi, i: (i, 0))],
        out_specs=[pl.BlockSpec((1, hh, D_NOPE + LANES, tl), lambda bi, i: (bi, 0, 0, i)),
                   pl.BlockSpec((1, hh, tl, D_V + LANES), lambda bi, i: (bi, 0, i, 0))],
        out_shape=[jax.ShapeDtypeStruct((b, hh, D_NOPE + LANES, l), BF16),
                   jax.ShapeDtypeStruct((b, hh, l, D_V + LANES), BF16)],
        compiler_params=_cparams("parallel", "parallel"),
        name="mla_kv",
    )(hid, wd, kv_norm_w.reshape(1, KV_LORA), w_ukv.astype(BF16), cos_t, sin_t)


def _mla_cq_kernel(x_ref, w_ref, nw_ref, o_ref):
    o_ref[...] = _rms_norm(_bdot(x_ref[...], w_ref[...]), nw_ref[...]).astype(o_ref.dtype)


def _mla_cq(x, w_dq, qnorm_w, tm=512):
    m, k = x.shape
    tm = min(tm, m)
    return pl.pallas_call(
        _mla_cq_kernel,
        grid=(m // tm,),
        in_specs=[pl.BlockSpec((tm, k), lambda i: (i, 0)),
                  pl.BlockSpec((k, Q_LORA), lambda i: (0, 0)),
                  pl.BlockSpec((1, Q_LORA), lambda i: (0, 0))],
        out_specs=pl.BlockSpec((tm, Q_LORA), lambda i: (i, 0)),
        out_shape=jax.ShapeDtypeStruct((m, Q_LORA), BF16),
        compiler_params=_cparams("parallel"),
        name="mla_cq",
    )(x, w_dq.astype(BF16), qnorm_w.reshape(1, Q_LORA))


def _mla_q_kernel(c_ref, w_ref, cos_ref, sin_ref, q_ref):
    c = c_ref[0]
    scale = (D_NOPE + D_ROPE) ** -0.5 * LOG2_E
    per = D_NOPE + 2 * LANES
    for h in range(MLA_HEADS):
        qh = _bdot(c, w_ref[:, h * per:(h + 1) * per])
        rope = qh[:, D_NOPE:D_NOPE + LANES] * cos_ref[...] + qh[:, D_NOPE + LANES:] * sin_ref[...]
        q_ref[0, h, :, :D_NOPE] = (qh[:, :D_NOPE] * scale).astype(q_ref.dtype)
        q_ref[0, h, :, D_NOPE:] = (rope * scale).astype(q_ref.dtype)


def _mla_q(cq, w_uq, cos_t, sin_t, tl=512):
    b, l, _ = cq.shape
    tl = min(tl, l)
    hh = MLA_HEADS
    per_in = D_NOPE + D_ROPE
    groups = []
    for h in range(hh):
        wh = w_uq[:, h * per_in:(h + 1) * per_in]
        groups += [wh[:, :D_NOPE], _rope_weight_groups(wh[:, D_NOPE:])]
    w = jnp.concatenate(groups, -1).astype(BF16)
    return pl.pallas_call(
        _mla_q_kernel,
        grid=(b, l // tl),
        in_specs=[pl.BlockSpec((1, tl, Q_LORA), lambda bi, i: (bi, i, 0)),
                  pl.BlockSpec(w.shape, lambda bi, i: (0, 0)),
                  pl.BlockSpec((tl, LANES), lambda bi, i: (i, 0)),
                  pl.BlockSpec((tl, LANES), lambda bi, i: (i, 0))],
        out_specs=pl.BlockSpec((1, hh, tl, D_NOPE + LANES), lambda bi, i: (bi, 0, i, 0)),
        out_shape=jax.ShapeDtypeStruct((b, hh, l, D_NOPE + LANES), BF16),
        compiler_params=_cparams("parallel", "parallel"),
        name="mla_q",
    )(cq, w, cos_t, sin_t)


def _mla_attn_kernel(q_ref, k_ref, v_ref, o_ref, *, tk):
    qi = pl.program_id(2)
    q = q_ref[0, 0]
    tq = q.shape[0]
    per = tq // tk

    groups = ATTN_ROW_GROUPS
    rows = tq // groups
    qs = [q[g * rows:(g + 1) * rows] for g in range(groups)]

    def block(j, carry, mask_offset=None):
        at = pl.ds(pl.multiple_of(j * tk, tk), tk)
        kb = k_ref[0, 0, :, at]
        vb = v_ref[0, 0, at, :]
        if mask_offset is None:
            widths = [tk] * groups
            ss = [_bdot(qg, kb) for qg in qs]
        else:
            widths = [min(max((g + 1) * rows - mask_offset, LANES), tk) for g in range(groups)]
            ss = []
            for g, (qg, wd) in enumerate(zip(qs, widths)):
                qpos = lax.broadcasted_iota(jnp.int32, (rows, wd), 0)
                kpos = lax.broadcasted_iota(jnp.int32, (rows, wd), 1)
                ss.append(jnp.where(kpos + mask_offset <= qpos + g * rows, _bdot(qg, kb[:, :wd]), -jnp.inf))
        m_new = [jnp.maximum(c[0], jnp.max(s, -1, keepdims=True)) for c, s in zip(carry, ss)]
        ps = [jnp.exp2(s - m) for s, m in zip(ss, m_new)]
        return tuple((mn, jnp.exp2(m - mn) * acc + _bdot(p, vb[:wd]))
                     for (m, acc), mn, p, wd in zip(carry, m_new, ps, widths))

    first = qi * per
    init = (jnp.full((rows, 1), -jnp.inf, F32), jnp.zeros((rows, D_V + LANES), F32))
    carry = lax.fori_loop(0, first, block, (init,) * groups)
    for d in range(per):
        carry = block(first + d, carry, mask_offset=d * tk)
    for g, (_, acc) in enumerate(carry):
        o_ref[0, g * rows:(g + 1) * rows, :] = (acc[:, :D_V] / acc[:, D_V:D_V + 1]).astype(o_ref.dtype)


def _mla_attn(q, k, v, tq=1024, tk=1024):
    b, hh, l, dq = q.shape
    tq = min(tq, l)
    tk = min(tk, tq)
    return pl.pallas_call(
        functools.partial(_mla_attn_kernel, tk=tk),
        grid=(b, hh, l // tq),
        in_specs=[pl.BlockSpec((1, 1, tq, dq), lambda bi, h, i: (bi, h, i, 0)),
                  pl.BlockSpec((1, 1, dq, l), lambda bi, h, i: (bi, h, 0, 0)),
                  pl.BlockSpec((1, 1, l, D_V + LANES), lambda bi, h, i: (bi, h, 0, 0))],
        out_specs=pl.BlockSpec((1, tq, D_V), lambda bi, h, i: (bi, i, h)),
        out_shape=jax.ShapeDtypeStruct((b, l, hh * D_V), BF16),
        compiler_params=_cparams("parallel", "parallel", "parallel"),
        name="mla_attn",
    )(q, k, v)


def _mla_ln(hid, w_dq, qnorm_w, w_uq, w_o, k, v, cos_t, sin_t, ln_g, ln_b):
    b, l, d = hid.shape
    x = hid.reshape(b * l, d)
    cq = _mla_cq(x, w_dq, qnorm_w).reshape(b, l, Q_LORA)
    q = _mla_q(cq, w_uq, cos_t, sin_t)
    o = _mla_attn(q, k, v)
    out, slabs = _matmul_ln(o.reshape(b * l, MLA_HEADS * D_V), w_o.astype(BF16), x, ln_g, ln_b)
    return out.reshape(b, l, d), slabs


def _first_argmax(x, ids, n):
    m = jnp.max(x, axis=0, keepdims=True)
    first = jnp.min(jnp.where(x == m, ids, n), axis=0, keepdims=True)
    return m, first


def _router_kernel(x_ref, wt_ref, bias_ref, idx_ref, w_ref, rank_ref, cnt_out_ref, cnt_ref, *, steps_per_tile):
    @pl.when(pl.program_id(0) % steps_per_tile == 0)
    def _():
        cnt_ref[...] = jnp.zeros_like(cnt_ref)

    x = x_ref[...]
    t = x.shape[0]
    logits = lax.dot_general(wt_ref[...], x, (((1,), (1,)), ((), ())),
                             precision=HIGHEST, preferred_element_type=F32)
    scores = jax.nn.sigmoid(logits)
    biased = scores + bias_ref[...]
    neg = -jnp.inf
    sub = lax.broadcasted_iota(jnp.int32, (GROUP_SIZE, t), 0).astype(F32)
    gscores = []
    for g in range(N_GROUPS):
        xg = biased[g * GROUP_SIZE:(g + 1) * GROUP_SIZE, :]
        m1, i1 = _first_argmax(xg, sub, float(GROUP_SIZE))
        m2 = jnp.max(jnp.where(sub == i1, neg, xg), axis=0, keepdims=True)
        gscores.append(m1 + m2)
    gs = jnp.concatenate(gscores, axis=0)
    gid = lax.broadcasted_iota(jnp.int32, (N_GROUPS, t), 0).astype(F32)
    gsel = jnp.zeros((N_GROUPS, t), F32)
    for _ in range(TOPK_GROUPS):
        _, gi = _first_argmax(gs, gid, float(N_GROUPS))
        hit = gid == gi
        gsel = jnp.where(hit, 1.0, gsel)
        gs = jnp.where(hit, neg, gs)
    eid = lax.broadcasted_iota(jnp.int32, (N_EXPERTS, t), 0).astype(F32)
    allowed = jnp.concatenate(
        [jnp.broadcast_to(gsel[g:g + 1, :], (GROUP_SIZE, t)) for g in range(N_GROUPS)], axis=0)
    cand = jnp.where(allowed > 0.0, biased, neg)
    idxs, ws, hits = [], [], []
    for _ in range(TOP_K):
        _, ei = _first_argmax(cand, eid, float(N_EXPERTS))
        hit = eid == ei
        idxs.append(ei)
        hits.append(hit)
        ws.append(jnp.sum(jnp.where(hit, scores, 0.0), axis=0, keepdims=True))
        cand = jnp.where(hit, neg, cand)
    w = jnp.concatenate(ws, axis=0)
    w = w / (jnp.sum(w, axis=0, keepdims=True) + 1e-20) * ROUTED_SCALE
    idx_ref[...] = jnp.concatenate(idxs, axis=0).astype(jnp.int32)
    w_ref[...] = w
    chosen = jnp.zeros((N_EXPERTS, t), F32)
    for hit in hits:
        chosen = jnp.where(hit, 1.0, chosen)
    before = (lax.broadcasted_iota(jnp.int32, (t, t), 0) < lax.broadcasted_iota(jnp.int32, (t, t), 1))
    prior = jnp.dot(chosen.astype(BF16), jnp.where(before, 1.0, 0.0).astype(BF16),
                    preferred_element_type=F32) + cnt_ref[...]
    rank_ref[...] = jnp.concatenate(
        [jnp.sum(jnp.where(hit, prior, 0.0), axis=0, keepdims=True) for hit in hits], axis=0).astype(jnp.int32)
    cnt_ref[...] += jnp.sum(chosen, axis=1, keepdims=True)
    cnt_out_ref[0] = cnt_ref[...]


def _router(x, w_router, bias, tile_tokens, tr=512):
    t, d = x.shape
    tr = min(tr, tile_tokens)
    steps_per_tile = tile_tokens // tr
    n_tiles = t // tile_tokens
    kt = pl.BlockSpec((TOP_K, tr), lambda i: (0, i))
    idx, w, rank, counts = pl.pallas_call(
        functools.partial(_router_kernel, steps_per_tile=steps_per_tile),
        grid=(t // tr,),
        in_specs=[pl.BlockSpec((tr, d), lambda i: (i, 0)),
                  pl.BlockSpec((N_EXPERTS, d), lambda i: (0, 0)),
                  pl.BlockSpec((N_EXPERTS, 1), lambda i: (0, 0))],
        out_specs=[kt, kt, kt,
                   pl.BlockSpec((1, N_EXPERTS, 1), lambda i: (i // steps_per_tile, 0, 0))],
        out_shape=[jax.ShapeDtypeStruct((TOP_K, t), jnp.int32),
                   jax.ShapeDtypeStruct((TOP_K, t), F32),
                   jax.ShapeDtypeStruct((TOP_K, t), jnp.int32),
                   jax.ShapeDtypeStruct((n_tiles, N_EXPERTS, 1), F32)],
        scratch_shapes=[pltpu.VMEM((N_EXPERTS, 1), F32)],
        compiler_params=_cparams("arbitrary"),
        name="moe_router",
    )(x, w_router.T, bias.reshape(N_EXPERTS, 1))
    return idx, w, rank, counts.reshape(n_tiles, N_EXPERTS).astype(jnp.int32)


def _steps_per_tile(tile_tokens):
    return tile_tokens * TOP_K // STEP_ROWS + N_EXPERTS


def _plan_kernel(fill_lo_ref, fill_hi_ref, pos_ref, src_ref, *, tile_tokens):
    i = pl.program_id(0)
    n_assign = tile_tokens * TOP_K
    empty = n_assign
    group = 16

    def fill_range(g, carry):
        lo = fill_lo_ref[i * N_EXPERTS + g]
        hi = fill_hi_ref[i * N_EXPERTS + g]

        def fill(b, carry):
            for d in range(group):
                src_ref[0, 0, jnp.maximum(hi - 1 - b * group - d, 0)] = empty
            return carry
        return lax.fori_loop(0, (hi - lo + group - 1) // group, fill, carry)
    lax.fori_loop(0, N_EXPERTS, fill_range, 0)

    def place(b, carry):
        base = b * group
        slots = [pos_ref[0, 0, base + d] for d in range(group)]
        for d in range(group):
            src_ref[0, 0, slots[d]] = base + d
        return carry
    lax.fori_loop(0, n_assign // group, place, 0)


def _dispatch_plan(idx_t, w_t, rank_t, counts, tile_tokens):
    t = idx_t.shape[1]
    n_tiles = t // tile_tokens
    steps = _steps_per_tile(tile_tokens)
    padded = (counts + STEP_ROWS - 1) // STEP_ROWS * STEP_ROWS
    pad_end = jnp.cumsum(padded, axis=1)
    pad_start = (pad_end - padded).astype(jnp.int32)
    n_used = (pad_end[:, -1] // STEP_ROWS).astype(jnp.int32)
    starts = jnp.arange(steps, dtype=jnp.int32) * STEP_ROWS
    step_e = jnp.sum(starts[None, :, None] >= pad_end[:, None, :], axis=-1).astype(jnp.int32)
    step_e = jnp.minimum(step_e, N_EXPERTS - 1)
    last = jnp.take_along_axis(step_e, jnp.maximum(n_used - 1, 0)[:, None], axis=1)
    step_e = jnp.where(starts[None, :] // STEP_ROWS < n_used[:, None], step_e, last)

    start_of = jnp.repeat(pad_start.T, tile_tokens, axis=1)
    experts = jnp.arange(N_EXPERTS, dtype=jnp.int32)[None, :, None]
    pos_t = rank_t + jnp.sum(jnp.where(idx_t[:, None, :] == experts, start_of[None], 0), axis=1)
    fill_lo = (pad_start + counts).astype(jnp.int32)
    fill_hi = pad_end.astype(jnp.int32)

    n_assign = tile_tokens * TOP_K
    n_slots = steps * STEP_ROWS
    single = pl.Buffered(1)
    src = pl.pallas_call(
        functools.partial(_plan_kernel, tile_tokens=tile_tokens),
        grid_spec=pltpu.PrefetchScalarGridSpec(
            num_scalar_prefetch=2,
            grid=(n_tiles,),
            in_specs=[pl.BlockSpec((1, 1, n_assign), lambda i, lo, hi: (i, 0, 0),
                                   memory_space=pltpu.SMEM, pipeline_mode=single)],
            out_specs=pl.BlockSpec((1, 1, n_slots), lambda i, lo, hi: (i, 0, 0),
                                   memory_space=pltpu.SMEM, pipeline_mode=single),
        ),
        out_shape=jax.ShapeDtypeStruct((n_tiles, 1, n_slots), jnp.int32),
        compiler_params=_cparams("parallel"),
        name="moe_plan",
    )(fill_lo.reshape(-1), fill_hi.reshape(-1), pos_t.T.reshape(n_tiles, 1, n_assign))
    src = jnp.where(jnp.arange(n_slots, dtype=jnp.int32)[None, None, :] < pad_end[:, -1][:, None, None],
                    src, tile_tokens * TOP_K)
    rows = src & -SUBLANES
    w_flat = jnp.pad(w_t.T.reshape(n_tiles, tile_tokens * TOP_K), ((0, 0), (0, SUBLANES)))
    slot_w = jnp.take_along_axis(w_flat, src[:, 0, :], axis=1)
    held = jnp.clip(jnp.take_along_axis(fill_lo, step_e, axis=1) - starts[None, :], 0, STEP_ROWS)
    live = jnp.where(starts[None, :] // STEP_ROWS < n_used[:, None], (held + EXPERT_ROWS - 1) // EXPERT_ROWS, 0)
    return rows, slot_w, step_e.reshape(-1), n_used, live.reshape(-1).astype(jnp.int32), steps


def _experts_kernel(step_e_ref, n_used_ref, live_ref, row_ref, sw_ref, x_ref, wg_ref, wu_ref, wd_ref, acc_ref,
                    *stage_refs):
    n_sub = BLOCKS_PER_STEP
    xs_refs = stage_refs[:n_sub]
    ys_refs = stage_refs[n_sub:]
    i = pl.program_id(0)
    j = pl.program_id(1)
    rows = EXPERT_ROWS
    pitch = STAGE_PITCH

    @pl.when(j == 0)
    def _():
        acc_ref[...] = jnp.zeros_like(acc_ref)

    last_row = x_ref.shape[1] - SUBLANES

    def token_rows(slot, limit=None):
        row = row_ref[0, 0, slot]
        if limit is not None:
            row = jnp.minimum(row, limit)
        return pl.ds(pl.multiple_of(row, SUBLANES), SUBLANES)

    def run(subs):
        for s in subs:
            for r in range(rows):
                xs_refs[s][pl.ds(r, ROW_CHUNKS, stride=pitch), :] = x_ref[0, token_rows(s * rows + r, last_row), :]
        eye = (lax.broadcasted_iota(jnp.int32, (rows, rows), 0)
               == lax.broadcasted_iota(jnp.int32, (rows, rows), 1))
        ys = {}
        for s in subs:
            x = jnp.concatenate([xs_refs[s][pl.ds(c * pitch, rows), :] for c in range(ROW_CHUNKS)], axis=1)
            hid = _silu(_bdot(x, wg_ref[0, 0])) * _bdot(x, wu_ref[0, 0])
            w_row = sw_ref[0, :, s * rows:(s + 1) * rows]
            w_col = jnp.sum(jnp.where(eye, w_row, 0.0), axis=1, keepdims=True)
            ys[s] = _bdot(hid, wd_ref[0, 0]) * w_col
        for s in subs:
            for c in range(ROW_CHUNKS):
                ys_refs[s][pl.ds(c * pitch, rows), :] = ys[s][:, c * LANES:(c + 1) * LANES]
        for s in subs:
            for r0 in range(0, rows, SCATTER_BATCH):
                batch = range(r0, r0 + SCATTER_BATCH)
                ats = [token_rows(s * rows + r) for r in batch]
                new = [acc_ref[0, at, :] + ys_refs[s][pl.ds(r, ROW_CHUNKS, stride=pitch), :]
                       for at, r in zip(ats, batch)]
                for at, val in zip(ats, new):
                    acc_ref[0, at, :] = val

    live = live_ref[i * pl.num_programs(1) + j]
    for n_live in range(1, n_sub + 1):
        pl.when(live == n_live)(functools.partial(run, range(n_live)))


def _routed_experts(x_slabs, idx_t, w_t, rank_t, counts, w_gate_up, w_down, layer, tile_tokens):
    d = D_MODEL
    t = x_slabs.shape[0] // ROW_CHUNKS
    n_tiles = t // tile_tokens
    rows, slot_w, step_e, n_used, live, steps = _dispatch_plan(idx_t, w_t, rank_t, counts, tile_tokens)
    rows = rows.reshape(n_tiles * steps, 1, STEP_ROWS)
    slot_w = slot_w.reshape(n_tiles * steps, 1, STEP_ROWS)
    x_slabs = x_slabs.reshape(n_tiles, tile_tokens * ROW_CHUNKS, LANES)
    slab_rows = (tile_tokens + 1) * ROW_CHUNKS

    def used_step(i, j, se, nu, lv):
        return (i * steps + jnp.minimum(j, jnp.maximum(nu[i] - 1, 0)), 0, 0)

    pick = lambda i, j, se, nu, lv: (layer, se[i * steps + j], 0, 0)
    tile = lambda i, j, se, nu, lv: (i, 0, 0)
    resident = pl.Buffered(1)
    stage = pltpu.VMEM((ROW_CHUNKS * STAGE_PITCH, LANES), F32)
    grid_spec = pltpu.PrefetchScalarGridSpec(
        num_scalar_prefetch=3,
        grid=(n_tiles, steps),
        in_specs=[
            pl.BlockSpec((1, 1, STEP_ROWS), used_step, memory_space=pltpu.SMEM),
            pl.BlockSpec((1, 1, STEP_ROWS), used_step),
            pl.BlockSpec((1, tile_tokens * ROW_CHUNKS, LANES), tile, pipeline_mode=resident),
            pl.BlockSpec((1, 1, d, D_EXPERT), pick),
            pl.BlockSpec((1, 1, d, D_EXPERT), pick),
            pl.BlockSpec((1, 1, D_EXPERT, d), pick),
        ],
        out_specs=pl.BlockSpec((1, slab_rows, LANES), tile, pipeline_mode=resident),
        scratch_shapes=[stage] * (2 * BLOCKS_PER_STEP),
    )
    return pl.pallas_call(
        _experts_kernel,
        grid_spec=grid_spec,
        out_shape=jax.ShapeDtypeStruct((n_tiles, slab_rows, LANES), F32),
        compiler_params=_cparams("parallel", "arbitrary"),
        name="moe_experts",
    )(step_e, n_used, live, rows, slot_w, x_slabs, *w_gate_up, w_down)


def _moe_out_kernel(x_ref, r_ref, wg_ref, wu_ref, wd_ref, g_ref, b_ref, o_ref):
    x = x_ref[...]
    tm = x.shape[0]
    hid = _silu(_bdot(x, wg_ref[...])) * _bdot(x, wu_ref[...])
    routed = jnp.concatenate([r_ref[0, pl.ds(c, tm, stride=ROW_CHUNKS), :] for c in range(ROW_CHUNKS)], axis=1)
    y = routed + _bdot(hid, wd_ref[...])
    o_ref[...] = _layer_norm(DN_ALPHA * x + y, g_ref[...], b_ref[...])


def _moe_out(x, routed, ws_gate, ws_up, ws_down, g, b, tile_tokens, tm=512):
    t, d = x.shape
    tm = min(tm, tile_tokens)
    per_tile = tile_tokens // tm
    return pl.pallas_call(
        _moe_out_kernel,
        grid=(t // tm,),
        in_specs=[pl.BlockSpec((tm, d), lambda i: (i, 0)),
                  pl.BlockSpec((1, tm * ROW_CHUNKS, LANES), lambda i: (i // per_tile, i % per_tile, 0)),
                  pl.BlockSpec((d, D_SHARED), lambda i: (0, 0)),
                  pl.BlockSpec((d, D_SHARED), lambda i: (0, 0)),
                  pl.BlockSpec((D_SHARED, d), lambda i: (0, 0)),
                  pl.BlockSpec((1, d), lambda i: (0, 0)),
                  pl.BlockSpec((1, d), lambda i: (0, 0))],
        out_specs=pl.BlockSpec((tm, d), lambda i: (i, 0)),
        out_shape=jax.ShapeDtypeStruct((t, d), F32),
        compiler_params=_cparams("parallel"),
        name="moe_out",
    )(x, routed, ws_gate.astype(BF16), ws_up.astype(BF16), ws_down.astype(BF16),
      g.reshape(1, d), b.reshape(1, d))


def _moe_ln(hid, hid_slabs, w_router, router_bias, w_gate_up, w_down, layer, ws_gate, ws_up, ws_down, ln_g, ln_b):
    b, l, d = hid.shape
    x = hid.reshape(b * l, d)
    tile_tokens = min(MOE_TILE_TOKENS, b * l)
    idx_t, w_t, rank_t, counts = _router(x, w_router, router_bias, tile_tokens)
    routed = _routed_experts(hid_slabs, idx_t, w_t, rank_t, counts, w_gate_up, w_down, layer, tile_tokens)
    return _moe_out(x, routed, ws_gate, ws_up, ws_down, ln_g, ln_b, tile_tokens).reshape(b, l, d)


def kernel(x, a_w_in, a_conv_w, a_A_log, a_dt_bias, a_onorm_w, a_w_out, b_w_dq, b_qnorm_w, b_w_uq, b_w_o,
           kv_w_dkv, kv_norm_w, kv_w_ukv, ln1_g, ln1_b, ln2_g, ln2_b, moe_w_router, moe_router_bias,
           moe_w_gate, moe_w_up, moe_w_down, moe_ws_gate, moe_ws_up, moe_ws_down):
    l = x.shape[1]
    cos_t, sin_t = _rope_lane_tables(l)
    w_gate_up = (moe_w_gate.astype(BF16), moe_w_up.astype(BF16))
    w_down = moe_w_down.astype(BF16)
    h = x
    k = v = None
    for layer in range(DEPTH):
        if layer < N_A_LAYERS:
            i = layer
            h, slabs = _gated_deltanet_ln(h, a_w_in[i], a_conv_w[i], a_A_log[i], a_dt_bias[i], a_onorm_w[i],
                                          a_w_out[i], ln1_g[layer], ln1_b[layer])
        else:
            i = layer - N_A_LAYERS
            h, slabs = _mla_ln(h, b_w_dq[i], b_qnorm_w[i], b_w_uq[i], b_w_o[i], k, v, cos_t, sin_t,
                               ln1_g[layer], ln1_b[layer])
        h = _moe_ln(h, slabs, moe_w_router[layer], moe_router_bias[layer], w_gate_up, w_down, layer,
                    moe_ws_gate[layer], moe_ws_up[layer], moe_ws_down[layer], ln2_g[layer], ln2_b[layer])
        if layer == N_A_LAYERS - 1:
            k, v = _mla_kv(h, kv_w_dkv, kv_norm_w, kv_w_ukv, cos_t, sin_t)
    return h
```

```python
import functools

import jax
import jax.numpy as jnp
import numpy as np
from jax import lax
from jax.experimental import pallas as pl
from jax.experimental.pallas import tpu as pltpu

F32 = jnp.float32
BF16 = jnp.bfloat16
HIGHEST = lax.Precision.HIGHEST

D_MODEL = 1024
DEPTH = 4
N_A_LAYERS = DEPTH // 2
GDN_HEADS = 8
GDN_DK = 128
GDN_DV = 128
D_QK = GDN_HEADS * GDN_DK
D_VA = GDN_HEADS * GDN_DV
GDN_CONV_CH = 2 * D_QK + D_VA
CONV_WIDTH = 4
CHUNK = 64
MLA_HEADS = 8
Q_LORA = 512
KV_LORA = 256
D_NOPE = 128
D_ROPE = 64
D_V = 128
ROPE_THETA = 10000.0
N_EXPERTS = 64
TOP_K = 8
N_GROUPS = 8
GROUP_SIZE = N_EXPERTS // N_GROUPS
TOPK_GROUPS = 4
D_EXPERT = 256
D_SHARED = 256
ROUTED_SCALE = 2.5
DN_ALPHA = (2 * DEPTH) ** 0.25
EPS = 1e-6
LN_EPS = 1e-5
LOG2_E = 1.4426950408889634

LANES = 128
SUBLANES = 8
VMEM_LIMIT_BYTES = 56 * 1024 * 1024

EXPERT_ROWS = 128
ROW_CHUNKS = D_MODEL // LANES
STAGE_PITCH = EXPERT_ROWS + 1
SCATTER_BATCH = 8
BLOCKS_PER_STEP = 2
STEP_ROWS = BLOCKS_PER_STEP * EXPERT_ROWS
MOE_TILE_TOKENS = 4096
ATTN_ROW_GROUPS = 4
assert TOP_K == SUBLANES


def _cparams(*sem):
    return pltpu.CompilerParams(dimension_semantics=sem, vmem_limit_bytes=VMEM_LIMIT_BYTES)


def _silu(x):
    return x * jax.nn.sigmoid(x)


def _layer_norm(x, g, b):
    mu = jnp.mean(x, -1, keepdims=True)
    xc = x - mu
    var = jnp.mean(xc * xc, -1, keepdims=True)
    return xc * lax.rsqrt(var + LN_EPS) * g + b


def _rms_norm(x, w):
    return x * lax.rsqrt(jnp.mean(x * x, -1, keepdims=True) + EPS) * w


def _bdot(a, b):
    return jnp.dot(a.astype(BF16), b.astype(BF16), preferred_element_type=F32)


def _bdot_nt(a, b):
    return lax.dot_general(a.astype(BF16), b.astype(BF16), (((1,), (1,)), ((), ())),
                           preferred_element_type=F32)


def _hdot(a, b):
    return jnp.dot(a, b, precision=HIGHEST, preferred_element_type=F32)


def _matmul_kernel(x_ref, w_ref, o_ref):
    o_ref[...] = _bdot(x_ref[...], w_ref[...]).astype(o_ref.dtype)


def _matmul(x, w, out_dtype, tm, tn):
    m, k = x.shape
    n = w.shape[1]
    tm = min(tm, m)
    tn = min(tn, n)
    return pl.pallas_call(
        _matmul_kernel,
        grid=(m // tm, n // tn),
        in_specs=[pl.BlockSpec((tm, k), lambda i, j: (i, 0)),
                  pl.BlockSpec((k, tn), lambda i, j: (0, j))],
        out_specs=pl.BlockSpec((tm, tn), lambda i, j: (i, j)),
        out_shape=jax.ShapeDtypeStruct((m, n), out_dtype),
        compiler_params=_cparams("parallel", "parallel"),
        name="matmul",
    )(x, w)


def _matmul_ln_kernel(x_ref, w_ref, r_ref, g_ref, b_ref, o_ref, slab_ref):
    mix = _bdot(x_ref[...], w_ref[...])
    out = _layer_norm(DN_ALPHA * r_ref[...] + mix, g_ref[...], b_ref[...])
    o_ref[...] = out
    tm = out.shape[0]
    for c in range(ROW_CHUNKS):
        slab_ref[pl.ds(c, tm, stride=ROW_CHUNKS), :] = out[:, c * LANES:(c + 1) * LANES]


def _matmul_ln(x, w, res, g, b, tm=1024):
    m, k = x.shape
    n = w.shape[1]
    tm = min(tm, m)
    chunks = n // LANES
    return pl.pallas_call(
        _matmul_ln_kernel,
        grid=(m // tm,),
        in_specs=[pl.BlockSpec((tm, k), lambda i: (i, 0)),
                  pl.BlockSpec((k, n), lambda i: (0, 0)),
                  pl.BlockSpec((tm, n), lambda i: (i, 0)),
                  pl.BlockSpec((1, n), lambda i: (0, 0)),
                  pl.BlockSpec((1, n), lambda i: (0, 0))],
        out_specs=[pl.BlockSpec((tm, n), lambda i: (i, 0)),
                   pl.BlockSpec((tm * chunks, LANES), lambda i: (i, 0))],
        out_shape=[jax.ShapeDtypeStruct((m, n), F32),
                   jax.ShapeDtypeStruct((m * chunks, LANES), F32)],
        compiler_params=_cparams("parallel"),
        name="matmul_ln",
    )(x, w, res, g.reshape(1, n), b.reshape(1, n))


def _softplus(x):
    return jnp.maximum(x, 0.0) + jnp.log1p(jnp.exp(-jnp.abs(x)))


def _gdn_gates_kernel(x_ref, wab_ref, wabt_ref, alog_ref, dtb_ref, alogt_ref, dtbt_ref,
                      gc_ref, beta_ref, gct_ref):
    x = x_ref[...]
    tl = x.shape[0]
    h = GDN_HEADS
    ab = _bdot(x, wab_ref[...])
    abt = _bdot_nt(wabt_ref[...], x)
    g = -jnp.exp(alog_ref[...]) * _softplus(ab[:, :h] + dtb_ref[...])
    gt = -jnp.exp(alogt_ref[...]) * _softplus(abt[:h, :] + dtbt_ref[...])
    beta_ref[...] = jax.nn.sigmoid(ab[:, h:2 * h])
    row = lax.broadcasted_iota(jnp.int32, (tl, tl), 0)
    col = lax.broadcasted_iota(jnp.int32, (tl, tl), 1)
    shift = CHUNK.bit_length() - 1
    same = (row >> shift) == (col >> shift)
    lower = jnp.where(same & (col <= row), 1.0, 0.0).astype(F32)
    upper = jnp.where(same & (row <= col), 1.0, 0.0).astype(F32)
    gc_ref[...] = _hdot(lower, g)
    gct_ref[...] = _hdot(gt, upper)


def _gdn_gates(x, w_ab, a_log, dt_bias, tl=256):
    t = x.shape[0]
    tl = min(tl, t)
    h = GDN_HEADS
    wab = jnp.zeros((D_MODEL, LANES), F32).at[:, :2 * h].set(w_ab)
    wabt = w_ab.T
    return pl.pallas_call(
        _gdn_gates_kernel,
        grid=(t // tl,),
        in_specs=[pl.BlockSpec((tl, D_MODEL), lambda i: (i, 0)),
                  pl.BlockSpec((D_MODEL, LANES), lambda i: (0, 0)),
                  pl.BlockSpec((2 * h, D_MODEL), lambda i: (0, 0)),
                  pl.BlockSpec((1, h), lambda i: (0, 0)),
                  pl.BlockSpec((1, h), lambda i: (0, 0)),
                  pl.BlockSpec((h, 1), lambda i: (0, 0)),
                  pl.BlockSpec((h, 1), lambda i: (0, 0))],
        out_specs=[pl.BlockSpec((tl, h), lambda i: (i, 0)),
                   pl.BlockSpec((tl, h), lambda i: (i, 0)),
                   pl.BlockSpec((h, tl), lambda i: (0, i))],
        out_shape=[jax.ShapeDtypeStruct((t, h), F32),
                   jax.ShapeDtypeStruct((t, h), F32),
                   jax.ShapeDtypeStruct((h, t), F32)],
        compiler_params=_cparams("parallel"),
        name="gdn_gates",
    )(x, wab, wabt, a_log.reshape(1, h), dt_bias.reshape(1, h),
      a_log.reshape(h, 1), dt_bias.reshape(h, 1))


def _gdn_conv_kernel(cur_ref, prev_ref, w_ref, q_ref, k_ref, v_ref):
    i = pl.program_id(1)
    x = cur_ref[0]
    tl = x.shape[0]
    prev = jnp.where(i > 0, prev_ref[0], 0.0)
    xx = jnp.concatenate([prev, x], axis=0)
    w = w_ref[...]
    y = None
    for j in range(CONV_WIDTH):
        off = SUBLANES - (CONV_WIDTH - 1) + j
        term = xx[off:off + tl, :] * w[j:j + 1, :]
        y = term if y is None else y + term
    y = _silu(y)
    for h in range(GDN_HEADS):
        sl = slice(h * GDN_DK, (h + 1) * GDN_DK)
        qh = y[:, sl]
        q_ref[0, :, sl] = qh * lax.rsqrt(jnp.sum(qh * qh, -1, keepdims=True) + EPS) * (GDN_DK ** -0.5)
        kh = y[:, D_QK + h * GDN_DK:D_QK + (h + 1) * GDN_DK]
        k_ref[0, :, sl] = kh * lax.rsqrt(jnp.sum(kh * kh, -1, keepdims=True) + EPS)
    v_ref[0] = y[:, 2 * D_QK:]


def _gdn_conv(proj, conv_w, tl=256):
    b, l, _ = proj.shape
    tl = min(tl, l)
    c3 = GDN_CONV_CH
    per = tl // SUBLANES
    out = jax.ShapeDtypeStruct((b, l, D_QK), F32)
    return pl.pallas_call(
        _gdn_conv_kernel,
        grid=(b, l // tl),
        in_specs=[pl.BlockSpec((1, tl, c3), lambda bi, i: (bi, i, 0)),
                  pl.BlockSpec((1, SUBLANES, c3), lambda bi, i: (bi, jnp.maximum(i * per - 1, 0), 0)),
                  pl.BlockSpec((CONV_WIDTH, c3), lambda bi, i: (0, 0))],
        out_specs=[pl.BlockSpec((1, tl, D_QK), lambda bi, i: (bi, i, 0))] * 3,
        out_shape=[out, out, out],
        compiler_params=_cparams("parallel", "parallel"),
        name="gdn_conv",
    )(proj, proj, conv_w)


def _split_bf16(x):
    hi = x.astype(BF16)
    lo = (x - hi.astype(F32)).astype(BF16)
    return hi, lo


def _dot_split(a, b):
    (ah, al), (bh, bl) = a, b
    dot = functools.partial(jnp.dot, preferred_element_type=F32)
    return dot(ah, bh) + (dot(ah, bl) + dot(al, bh))


def _gdn_prep_kernel(q_ref, k_ref, v_ref, gc_ref, beta_ref, gct_ref, wq_ref, u_ref, attn_ref, kgt_ref,
                     *, chunks):
    c = CHUNK
    row = lax.broadcasted_iota(jnp.int32, (c, c), 0)
    col = lax.broadcasted_iota(jnp.int32, (c, c), 1)
    incl = row >= col
    strict = row > col
    eye = jnp.where(row == col, 1.0, 0.0).astype(F32)
    items = [(ci, h) for ci in range(chunks) for h in range(GDN_HEADS)]

    st = {}
    for ci, h in items:
        rows = slice(ci * c, (ci + 1) * c)
        sl = slice(h * GDN_DK, (h + 1) * GDN_DK)
        q = q_ref[0, rows, sl]
        k = k_ref[0, rows, sl]
        v = v_ref[0, rows, sl]
        gcol = gc_ref[0, rows, h:h + 1]
        grow = gct_ref[0, ci, h:h + 1, :]
        bcol = beta_ref[0, rows, h:h + 1]
        decay = jnp.where(incl, jnp.exp(jnp.where(incl, gcol - grow, 0.0)), 0.0)
        kb = k * bcol
        eg = jnp.exp(gcol)
        g_last = gcol[c - 1:c, :]
        sc = _bdot_nt(jnp.concatenate([kb, q], axis=0), k)
        a = jnp.where(strict, sc[:c] * decay, 0.0)
        attn_ref[0, ci, h] = (sc[c:] * decay).astype(attn_ref.dtype)
        kgt_ref[0, ci, h] = (k * jnp.exp(g_last - gcol)).T.astype(kgt_ref.dtype)
        wq_ref[0, ci, h, c:, :] = (q * eg).astype(wq_ref.dtype)
        st[ci, h] = dict(m=-a, rhs=jnp.concatenate([v * bcol, kb * eg], axis=1))

    for it in items:
        st[it]["p"] = eye + st[it]["m"]
    span = 2
    while span < c:
        for it in items:
            ms = _split_bf16(st[it]["m"])
            st[it]["m"] = _dot_split(ms, ms)
        for it in items:
            st[it]["p"] = st[it]["p"] + _dot_split(_split_bf16(st[it]["p"]), _split_bf16(st[it]["m"]))
        span *= 2

    for ci, h in items:
        sol = _dot_split(_split_bf16(st[ci, h]["p"]), _split_bf16(st[ci, h]["rhs"]))
        u_ref[0, ci * c:(ci + 1) * c, h * GDN_DV:(h + 1) * GDN_DV] = sol[:, :GDN_DV]
        wq_ref[0, ci, h, :c, :] = sol[:, GDN_DV:].astype(wq_ref.dtype)


def _gdn_prep(q, k, v, gc, beta, gct, chunks=2):
    b, l, _ = q.shape
    n = l // CHUNK
    chunks = min(chunks, n)
    h = GDN_HEADS
    rows = chunks * CHUNK
    blk = pl.BlockSpec((1, rows, D_QK), lambda bi, i: (bi, i, 0))
    gate = pl.BlockSpec((1, rows, h), lambda bi, i: (bi, i, 0))
    return pl.pallas_call(
        functools.partial(_gdn_prep_kernel, chunks=chunks),
        grid=(b, n // chunks),
        in_specs=[blk, blk, blk, gate, gate,
                  pl.BlockSpec((1, chunks, h, CHUNK), lambda bi, i: (bi, i, 0, 0))],
        out_specs=[pl.BlockSpec((1, chunks, h, 2 * CHUNK, GDN_DK), lambda bi, i: (bi, i, 0, 0, 0)),
                   pl.BlockSpec((1, rows, D_VA), lambda bi, i: (bi, i, 0)),
                   pl.BlockSpec((1, chunks, h, CHUNK, CHUNK), lambda bi, i: (bi, i, 0, 0, 0)),
                   pl.BlockSpec((1, chunks, h, GDN_DK, CHUNK), lambda bi, i: (bi, i, 0, 0, 0))],
        out_shape=[jax.ShapeDtypeStruct((b, n, h, 2 * CHUNK, GDN_DK), BF16),
                   jax.ShapeDtypeStruct((b, l, D_VA), F32),
                   jax.ShapeDtypeStruct((b, n, h, CHUNK, CHUNK), BF16),
                   jax.ShapeDtypeStruct((b, n, h, GDN_DK, CHUNK), BF16)],
        compiler_params=_cparams("parallel", "parallel"),
        name="gdn_prep",
    )(q, k, v, gc, beta, gct)


def _gdn_state_kernel(wq_ref, u_ref, attn_ref, kgt_ref, gc_ref, z_ref, onw_ref, o_ref, s_ref, *, chunks):
    @pl.when(pl.program_id(1) == 0)
    def _():
        s_ref[...] = jnp.zeros_like(s_ref)

    c = CHUNK
    onw = onw_ref[...]
    dot = functools.partial(jnp.dot, preferred_element_type=F32)
    heads = range(GDN_HEADS)
    for ci in range(chunks):
        rows = slice(ci * c, (ci + 1) * c)
        s_old = [s_ref[h] for h in heads]
        r = [dot(wq_ref[0, ci, h], s_old[h].astype(BF16)) for h in heads]
        v_new = [(u_ref[0, rows, h * GDN_DV:(h + 1) * GDN_DV] - r[h][:c]).astype(BF16) for h in heads]
        for h in heads:
            decay_last = jnp.exp(gc_ref[0, (ci + 1) * c - 1:(ci + 1) * c, h:h + 1])
            s_ref[h] = s_old[h] * decay_last + dot(kgt_ref[0, ci, h], v_new[h])
        for h in heads:
            sl = slice(h * GDN_DV, (h + 1) * GDN_DV)
            o = r[h][c:] + dot(attn_ref[0, ci, h], v_new[h])
            o_ref[0, rows, sl] = (_rms_norm(o, onw) * _silu(z_ref[0, rows, sl])).astype(o_ref.dtype)


def _gdn_state(wq, u, attn, kgt, gc, proj, onorm_w, chunks=8):
    b, n, h = wq.shape[:3]
    l = n * CHUNK
    chunks = min(chunks, n)
    rows = chunks * CHUNK
    z_block = GDN_CONV_CH // D_VA
    return pl.pallas_call(
        functools.partial(_gdn_state_kernel, chunks=chunks),
        grid=(b, n // chunks),
        in_specs=[pl.BlockSpec((1, chunks, h, 2 * CHUNK, GDN_DK), lambda bi, i: (bi, i, 0, 0, 0)),
                  pl.BlockSpec((1, rows, D_VA), lambda bi, i: (bi, i, 0)),
                  pl.BlockSpec((1, chunks, h, CHUNK, CHUNK), lambda bi, i: (bi, i, 0, 0, 0)),
                  pl.BlockSpec((1, chunks, h, GDN_DK, CHUNK), lambda bi, i: (bi, i, 0, 0, 0)),
                  pl.BlockSpec((1, rows, h), lambda bi, i: (bi, i, 0)),
                  pl.BlockSpec((1, rows, D_VA), lambda bi, i: (bi, i, z_block)),
                  pl.BlockSpec((1, GDN_DV), lambda bi, i: (0, 0))],
        out_specs=pl.BlockSpec((1, rows, D_VA), lambda bi, i: (bi, i, 0)),
        out_shape=jax.ShapeDtypeStruct((b, l, D_VA), BF16),
        scratch_shapes=[pltpu.VMEM((h, GDN_DK, GDN_DV), F32)],
        compiler_params=_cparams("parallel", "arbitrary"),
        name="gdn_state",
    )(wq, u, attn, kgt, gc, proj, onorm_w.reshape(1, GDN_DV))


def _gated_deltanet_ln(hid, w_in, conv_w, a_log, dt_bias, onorm_w, w_out, ln_g, ln_b):
    b, l, d = hid.shape
    h = GDN_HEADS
    n = l // CHUNK
    x = hid.reshape(b * l, d)
    n_main = GDN_CONV_CH + D_VA
    proj = _matmul(x, w_in[:, :n_main].astype(BF16), F32, 1024, 1024).reshape(b, l, n_main)
    gc, beta, gct = _gdn_gates(x, w_in[:, n_main:], a_log, dt_bias)
    gc = gc.reshape(b, l, h)
    gct = gct.reshape(h, b, n, CHUNK).transpose(1, 2, 0, 3)
    q, k, v = _gdn_conv(proj, conv_w)
    wq, u, attn, kgt = _gdn_prep(q, k, v, gc, beta.reshape(b, l, h), gct)
    o = _gdn_state(wq, u, attn, kgt, gc, proj, onorm_w)
    out, slabs = _matmul_ln(o.reshape(b * l, D_VA), w_out.astype(BF16), x, ln_g, ln_b)
    return out.reshape(b, l, d), slabs


def _rope_lane_tables(l):
    inv = 1.0 / (ROPE_THETA ** (jnp.arange(0, D_ROPE, 2, dtype=F32) / D_ROPE))
    ang = jnp.arange(l, dtype=F32)[:, None] * inv[None, :]
    cos, sin = jnp.cos(ang), jnp.sin(ang)
    zero = jnp.zeros((l, LANES - D_ROPE), F32)
    return (jnp.concatenate([cos, cos, zero], -1),
            jnp.concatenate([-sin, sin, zero], -1))


def _rope_weight_groups(w_rope):
    kdim = w_rope.shape[0]
    half = D_ROPE // 2
    zero = jnp.zeros((kdim, LANES - D_ROPE), w_rope.dtype)
    x1, x2 = w_rope[:, :half], w_rope[:, half:]
    return jnp.concatenate([x1, x2, zero, x2, x1, zero], axis=-1)


def _mla_kv_kernel(x_ref, wd_ref, nw_ref, wu_ref, cos_ref, sin_ref, k_ref, v_ref):
    x = x_ref[0]
    ckv = _bdot(x, wd_ref[...])
    c = _rms_norm(ckv[:, :KV_LORA], nw_ref[...])
    k_rope_t = (ckv[:, KV_LORA:KV_LORA + LANES] * cos_ref[...]
                + ckv[:, KV_LORA + LANES:] * sin_ref[...]).T.astype(k_ref.dtype)
    kv = _bdot(c, wu_ref[...])
    per = D_NOPE + D_V
    ones_col = jnp.where(lax.broadcasted_iota(jnp.int32, (x.shape[0], LANES), 1) == 0, 1.0, 0.0).astype(v_ref.dtype)
    for h in range(MLA_HEADS):
        k_ref[0, h, :D_NOPE, :] = kv[:, h * per:h * per + D_NOPE].T.astype(k_ref.dtype)
        k_ref[0, h, D_NOPE:, :] = k_rope_t
        v_ref[0, h, :, :D_V] = kv[:, h * per + D_NOPE:(h + 1) * per].astype(v_ref.dtype)
        v_ref[0, h, :, D_V:] = ones_col


def _mla_kv(hid, w_dkv, kv_norm_w, w_ukv, cos_t, sin_t, tl=512):
    b, l, d = hid.shape
    tl = min(tl, l)
    wd = jnp.concatenate([w_dkv[:, :KV_LORA], _rope_weight_groups(w_dkv[:, KV_LORA:])], -1).astype(BF16)
    nd = wd.shape[1]
    hh = MLA_HEADS
    return pl.pallas_call(
        _mla_kv_kernel,
        grid=(b, l // tl),
        in_specs=[pl.BlockSpec((1, tl, d), lambda bi, i: (bi, i, 0)),
                  pl.BlockSpec((d, nd), lambda bi, i: (0, 0)),
                  pl.BlockSpec((1, KV_LORA), lambda bi, i: (0, 0)),
                  pl.BlockSpec((KV_LORA, hh * (D_NOPE + D_V)), lambda bi, i: (0, 0)),
                  pl.BlockSpec((tl, LANES), lambda bi, i: (i, 0)),
                  pl.BlockSpec((tl, LANES), lambda bi, i: (i, 0))],
        out_specs=[pl.BlockSpec((1, hh, D_NOPE + LANES, tl), lambda bi, i: (bi, 0, 0, i)),
                   pl.BlockSpec((1, hh, tl, D_V + LANES), lambda bi, i: (bi, 0, i, 0))],
        out_shape=[jax.ShapeDtypeStruct((b, hh, D_NOPE + LANES, l), BF16),
                   jax.ShapeDtypeStruct((b, hh, l, D_V + LANES), BF16)],
        compiler_params=_cparams("parallel", "parallel"),
        name="mla_kv",
    )(hid, wd, kv_norm_w.reshape(1, KV_LORA), w_ukv.astype(BF16), cos_t, sin_t)


def _mla_cq_kernel(x_ref, w_ref, nw_ref, o_ref):
    o_ref[...] = _rms_norm(_bdot(x_ref[...], w_ref[...]), nw_ref[...]).astype(o_ref.dtype)


def _mla_cq(x, w_dq, qnorm_w, tm=512):
    m, k = x.shape
    tm = min(tm, m)
    return pl.pallas_call(
        _mla_cq_kernel,
        grid=(m // tm,),
        in_specs=[pl.BlockSpec((tm, k), lambda i: (i, 0)),
                  pl.BlockSpec((k, Q_LORA), lambda i: (0, 0)),
                  pl.BlockSpec((1, Q_LORA), lambda i: (0, 0))],
        out_specs=pl.BlockSpec((tm, Q_LORA), lambda i: (i, 0)),
        out_shape=jax.ShapeDtypeStruct((m, Q_LORA), BF16),
        compiler_params=_cparams("parallel"),
        name="mla_cq",
    )(x, w_dq.astype(BF16), qnorm_w.reshape(1, Q_LORA))


def _mla_q_kernel(c_ref, w_ref, cos_ref, sin_ref, q_ref):
    c = c_ref[0]
    scale = (D_NOPE + D_ROPE) ** -0.5 * LOG2_E
    per = D_NOPE + 2 * LANES
    for h in range(MLA_HEADS):
        qh = _bdot(c, w_ref[:, h * per:(h + 1) * per])
        rope = qh[:, D_NOPE:D_NOPE + LANES] * cos_ref[...] + qh[:, D_NOPE + LANES:] * sin_ref[...]
        q_ref[0, h, :, :D_NOPE] = (qh[:, :D_NOPE] * scale).astype(q_ref.dtype)
        q_ref[0, h, :, D_NOPE:] = (rope * scale).astype(q_ref.dtype)


def _mla_q(cq, w_uq, cos_t, sin_t, tl=512):
    b, l, _ = cq.shape
    tl = min(tl, l)
    hh = MLA_HEADS
    per_in = D_NOPE + D_ROPE
    groups = []
    for h in range(hh):
        wh = w_uq[:, h * per_in:(h + 1) * per_in]
        groups += [wh[:, :D_NOPE], _rope_weight_groups(wh[:, D_NOPE:])]
    w = jnp.concatenate(groups, -1).astype(BF16)
    return pl.pallas_call(
        _mla_q_kernel,
        grid=(b, l // tl),
        in_specs=[pl.BlockSpec((1, tl, Q_LORA), lambda bi, i: (bi, i, 0)),
                  pl.BlockSpec(w.shape, lambda bi, i: (0, 0)),
                  pl.BlockSpec((tl, LANES), lambda bi, i: (i, 0)),
                  pl.BlockSpec((tl, LANES), lambda bi, i: (i, 0))],
        out_specs=pl.BlockSpec((1, hh, tl, D_NOPE + LANES), lambda bi, i: (bi, 0, i, 0)),
        out_shape=jax.ShapeDtypeStruct((b, hh, l, D_NOPE + LANES), BF16),
        compiler_params=_cparams("parallel", "parallel"),
        name="mla_q",
    )(cq, w, cos_t, sin_t)


def _mla_attn_kernel(q_ref, k_ref, v_ref, o_ref, *, tk):
    qi = pl.program_id(2)
    q = q_ref[0, 0]
    tq = q.shape[0]
    per = tq // tk

    groups = ATTN_ROW_GROUPS
    rows = tq // groups
    qs = [q[g * rows:(g + 1) * rows] for g in range(groups)]

    def block(j, carry, mask_offset=None):
        at = pl.ds(pl.multiple_of(j * tk, tk), tk)
        kb = k_ref[0, 0, :, at]
        vb = v_ref[0, 0, at, :]
        if mask_offset is None:
            widths = [tk] * groups
            ss = [_bdot(qg, kb) for qg in qs]
        else:
            widths = [min(max((g + 1) * rows - mask_offset, LANES), tk) for g in range(groups)]
            ss = []
            for g, (qg, wd) in enumerate(zip(qs, widths)):
                qpos = lax.broadcasted_iota(jnp.int32, (rows, wd), 0)
                kpos = lax.broadcasted_iota(jnp.int32, (rows, wd), 1)
                ss.append(jnp.where(kpos + mask_offset <= qpos + g * rows, _bdot(qg, kb[:, :wd]), -jnp.inf))
        m_new = [jnp.maximum(c[0], jnp.max(s, -1, keepdims=True)) for c, s in zip(carry, ss)]
        ps = [jnp.exp2(s - m) for s, m in zip(ss, m_new)]
        return tuple((mn, jnp.exp2(m - mn) * acc + _bdot(p, vb[:wd]))
                     for (m, acc), mn, p, wd in zip(carry, m_new, ps, widths))

    first = qi * per
    init = (jnp.full((rows, 1), -jnp.inf, F32), jnp.zeros((rows, D_V + LANES), F32))
    carry = lax.fori_loop(0, first, block, (init,) * groups)
    for d in range(per):
        carry = block(first + d, carry, mask_offset=d * tk)
    for g, (_, acc) in enumerate(carry):
        o_ref[0, g * rows:(g + 1) * rows, :] = (acc[:, :D_V] / acc[:, D_V:D_V + 1]).astype(o_ref.dtype)


def _mla_attn(q, k, v, tq=1024, tk=1024):
    b, hh, l, dq = q.shape
    tq = min(tq, l)
    tk = min(tk, tq)
    return pl.pallas_call(
        functools.partial(_mla_attn_kernel, tk=tk),
        grid=(b, hh, l // tq),
        in_specs=[pl.BlockSpec((1, 1, tq, dq), lambda bi, h, i: (bi, h, i, 0)),
                  pl.BlockSpec((1, 1, dq, l), lambda bi, h, i: (bi, h, 0, 0)),
                  pl.BlockSpec((1, 1, l, D_V + LANES), lambda bi, h, i: (bi, h, 0, 0))],
        out_specs=pl.BlockSpec((1, tq, D_V), lambda bi, h, i: (bi, i, h)),
        out_shape=jax.ShapeDtypeStruct((b, l, hh * D_V), BF16),
        compiler_params=_cparams("parallel", "parallel", "parallel"),
        name="mla_attn",
    )(q, k, v)


def _mla_ln(hid, w_dq, qnorm_w, w_uq, w_o, k, v, cos_t, sin_t, ln_g, ln_b):
    b, l, d = hid.shape
    x = hid.reshape(b * l, d)
    cq = _mla_cq(x, w_dq, qnorm_w).reshape(b, l, Q_LORA)
    q = _mla_q(cq, w_uq, cos_t, sin_t)
    o = _mla_attn(q, k, v)
    out, slabs = _matmul_ln(o.reshape(b * l, MLA_HEADS * D_V), w_o.astype(BF16), x, ln_g, ln_b)
    return out.reshape(b, l, d), slabs


def _first_argmax(x, ids, n):
    m = jnp.max(x, axis=0, keepdims=True)
    first = jnp.min(jnp.where(x == m, ids, n), axis=0, keepdims=True)
    return m, first


def _router_kernel(x_ref, wt_ref, bias_ref, before_ref, idx_ref, w_ref, rank_ref, cnt_out_ref, cnt_ref,
                   *, steps_per_tile):
    @pl.when(pl.program_id(0) % steps_per_tile == 0)
    def _():
        cnt_ref[...] = jnp.zeros_like(cnt_ref)

    x = x_ref[...]
    t = x.shape[0]
    logits = lax.dot_general(wt_ref[...], x, (((1,), (1,)), ((), ())),
                             precision=HIGHEST, preferred_element_type=F32)
    scores = jax.nn.sigmoid(logits)
    biased = scores + bias_ref[...]
    neg = -jnp.inf
    sub = lax.broadcasted_iota(jnp.int32, (GROUP_SIZE, t), 0).astype(F32)
    gscores = []
    for g in range(N_GROUPS):
        xg = biased[g * GROUP_SIZE:(g + 1) * GROUP_SIZE, :]
        m1, i1 = _first_argmax(xg, sub, float(GROUP_SIZE))
        m2 = jnp.max(jnp.where(sub == i1, neg, xg), axis=0, keepdims=True)
        gscores.append(m1 + m2)
    gs = jnp.concatenate(gscores, axis=0)
    gid = lax.broadcasted_iota(jnp.int32, (N_GROUPS, t), 0).astype(F32)
    gsel = jnp.zeros((N_GROUPS, t), F32)
    for _ in range(TOPK_GROUPS):
        _, gi = _first_argmax(gs, gid, float(N_GROUPS))
        hit = gid == gi
        gsel = jnp.where(hit, 1.0, gsel)
        gs = jnp.where(hit, neg, gs)
    eid = lax.broadcasted_iota(jnp.int32, (N_EXPERTS, t), 0).astype(F32)
    allowed = jnp.concatenate(
        [jnp.broadcast_to(gsel[g:g + 1, :], (GROUP_SIZE, t)) for g in range(N_GROUPS)], axis=0)
    cand = jnp.where(allowed > 0.0, biased, neg)
    idxs, ws, hits = [], [], []
    for _ in range(TOP_K):
        _, ei = _first_argmax(cand, eid, float(N_EXPERTS))
        hit = eid == ei
        idxs.append(ei)
        hits.append(hit)
        ws.append(jnp.sum(jnp.where(hit, scores, 0.0), axis=0, keepdims=True))
        cand = jnp.where(hit, neg, cand)
    w = jnp.concatenate(ws, axis=0)
    w = w / (jnp.sum(w, axis=0, keepdims=True) + 1e-20) * ROUTED_SCALE
    idx_ref[...] = jnp.concatenate(idxs, axis=0).astype(jnp.int32)
    w_ref[...] = w
    chosen = jnp.zeros((N_EXPERTS, t), F32)
    for hit in hits:
        chosen = jnp.where(hit, 1.0, chosen)
    prior = jnp.dot(chosen.astype(BF16), before_ref[...],
                    preferred_element_type=F32) + cnt_ref[...]
    rank_ref[...] = jnp.concatenate(
        [jnp.sum(jnp.where(hit, prior, 0.0), axis=0, keepdims=True) for hit in hits], axis=0).astype(jnp.int32)
    cnt_ref[...] += jnp.sum(chosen, axis=1, keepdims=True)
    cnt_out_ref[0] = cnt_ref[...]


def _router(x, w_router, bias, tile_tokens, tr=1024):
    t, d = x.shape
    tr = min(tr, tile_tokens)
    steps_per_tile = tile_tokens // tr
    n_tiles = t // tile_tokens
    kt = pl.BlockSpec((TOP_K, tr), lambda i: (0, i))
    before = jnp.triu(jnp.ones((tr, tr), BF16), k=1)
    idx, w, rank, counts = pl.pallas_call(
        functools.partial(_router_kernel, steps_per_tile=steps_per_tile),
        grid=(t // tr,),
        in_specs=[pl.BlockSpec((tr, d), lambda i: (i, 0)),
                  pl.BlockSpec((N_EXPERTS, d), lambda i: (0, 0)),
                  pl.BlockSpec((N_EXPERTS, 1), lambda i: (0, 0)),
                  pl.BlockSpec((tr, tr), lambda i: (0, 0))],
        out_specs=[kt, kt, kt,
                   pl.BlockSpec((1, N_EXPERTS, 1), lambda i: (i // steps_per_tile, 0, 0))],
        out_shape=[jax.ShapeDtypeStruct((TOP_K, t), jnp.int32),
                   jax.ShapeDtypeStruct((TOP_K, t), F32),
                   jax.ShapeDtypeStruct((TOP_K, t), jnp.int32),
                   jax.ShapeDtypeStruct((n_tiles, N_EXPERTS, 1), F32)],
        scratch_shapes=[pltpu.VMEM((N_EXPERTS, 1), F32)],
        compiler_params=_cparams("arbitrary"),
        name="moe_router",
    )(x, w_router.T, bias.reshape(N_EXPERTS, 1), before)
    return idx, w, rank, counts.reshape(n_tiles, N_EXPERTS).astype(jnp.int32)


def _steps_per_tile(tile_tokens):
    return tile_tokens * TOP_K // STEP_ROWS + N_EXPERTS


def _plan_kernel(fill_lo_ref, fill_hi_ref, pos_ref, src_ref, *, tile_tokens):
    i = pl.program_id(0)
    n_assign = tile_tokens * TOP_K
    empty = n_assign
    group = 16

    def fill_range(g, carry):
        lo = fill_lo_ref[i * N_EXPERTS + g]
        hi = fill_hi_ref[i * N_EXPERTS + g]

        def fill(b, carry):
            for d in range(group):
                src_ref[0, 0, jnp.maximum(hi - 1 - b * group - d, 0)] = empty
            return carry
        return lax.fori_loop(0, (hi - lo + group - 1) // group, fill, carry)
    lax.fori_loop(0, N_EXPERTS, fill_range, 0)

    def place(b, carry):
        base = b * group
        slots = [pos_ref[0, 0, base + d] for d in range(group)]
        for d in range(group):
            src_ref[0, 0, slots[d]] = base + d
        return carry
    lax.fori_loop(0, n_assign // group, place, 0)


def _dispatch_plan(idx_t, w_t, rank_t, counts, tile_tokens):
    t = idx_t.shape[1]
    n_tiles = t // tile_tokens
    steps = _steps_per_tile(tile_tokens)
    padded = (counts + STEP_ROWS - 1) // STEP_ROWS * STEP_ROWS
    pad_end = jnp.cumsum(padded, axis=1)
    pad_start = (pad_end - padded).astype(jnp.int32)
    n_used = (pad_end[:, -1] // STEP_ROWS).astype(jnp.int32)
    starts = jnp.arange(steps, dtype=jnp.int32) * STEP_ROWS
    step_e = jnp.sum(starts[None, :, None] >= pad_end[:, None, :], axis=-1).astype(jnp.int32)
    step_e = jnp.minimum(step_e, N_EXPERTS - 1)
    last = jnp.take_along_axis(step_e, jnp.maximum(n_used - 1, 0)[:, None], axis=1)
    step_e = jnp.where(starts[None, :] // STEP_ROWS < n_used[:, None], step_e, last)

    start_of = jnp.repeat(pad_start.T, tile_tokens, axis=1)
    experts = jnp.arange(N_EXPERTS, dtype=jnp.int32)[None, :, None]
    pos_t = rank_t + jnp.sum(jnp.where(idx_t[:, None, :] == experts, start_of[None], 0), axis=1)
    fill_lo = (pad_start + counts).astype(jnp.int32)
    fill_hi = pad_end.astype(jnp.int32)

    n_assign = tile_tokens * TOP_K
    n_slots = steps * STEP_ROWS
    single = pl.Buffered(1)
    src = pl.pallas_call(
        functools.partial(_plan_kernel, tile_tokens=tile_tokens),
        grid_spec=pltpu.PrefetchScalarGridSpec(
            num_scalar_prefetch=2,
            grid=(n_tiles,),
            in_specs=[pl.BlockSpec((1, 1, n_assign), lambda i, lo, hi: (i, 0, 0),
                                   memory_space=pltpu.SMEM, pipeline_mode=single)],
            out_specs=pl.BlockSpec((1, 1, n_slots), lambda i, lo, hi: (i, 0, 0),
                                   memory_space=pltpu.SMEM, pipeline_mode=single),
        ),
        out_shape=jax.ShapeDtypeStruct((n_tiles, 1, n_slots), jnp.int32),
        compiler_params=_cparams("parallel"),
        name="moe_plan",
    )(fill_lo.reshape(-1), fill_hi.reshape(-1), pos_t.T.reshape(n_tiles, 1, n_assign))
    src = jnp.where(jnp.arange(n_slots, dtype=jnp.int32)[None, None, :] < pad_end[:, -1][:, None, None],
                    src, tile_tokens * TOP_K)
    rows = src & -SUBLANES
    w_flat = jnp.pad(w_t.T.reshape(n_tiles, tile_tokens * TOP_K), ((0, 0), (0, SUBLANES)))
    slot_w = jnp.take_along_axis(w_flat, src[:, 0, :], axis=1)
    held = jnp.clip(jnp.take_along_axis(fill_lo, step_e, axis=1) - starts[None, :], 0, STEP_ROWS)
    live = jnp.where(starts[None, :] // STEP_ROWS < n_used[:, None], (held + EXPERT_ROWS - 1) // EXPERT_ROWS, 0)
    return rows, slot_w, step_e.reshape(-1), n_used, live.reshape(-1).astype(jnp.int32), steps


def _experts_kernel(step_e_ref, n_used_ref, live_ref, row_ref, sw_ref, x_ref, wg_ref, wu_ref, wd_ref, acc_ref,
                    *stage_refs):
    n_sub = BLOCKS_PER_STEP
    xs_refs = stage_refs[:n_sub]
    ys_refs = stage_refs[n_sub:]
    i = pl.program_id(0)
    j = pl.program_id(1)
    rows = EXPERT_ROWS
    pitch = STAGE_PITCH

    @pl.when(j == 0)
    def _():
        acc_ref[...] = jnp.zeros_like(acc_ref)

    last_row = x_ref.shape[1] - SUBLANES

    def token_rows(slot, limit=None):
        row = row_ref[0, 0, slot]
        if limit is not None:
            row = jnp.minimum(row, limit)
        return pl.ds(pl.multiple_of(row, SUBLANES), SUBLANES)

    def run(subs):
        for s in subs:
            for r in range(rows):
                xs_refs[s][pl.ds(r, ROW_CHUNKS, stride=pitch), :] = x_ref[0, token_rows(s * rows + r, last_row), :]
        eye = (lax.broadcasted_iota(jnp.int32, (rows, rows), 0)
               == lax.broadcasted_iota(jnp.int32, (rows, rows), 1))
        ys = {}
        for s in subs:
            x = jnp.concatenate([xs_refs[s][pl.ds(c * pitch, rows), :] for c in range(ROW_CHUNKS)], axis=1)
            hid = _silu(_bdot(x, wg_ref[0, 0])) * _bdot(x, wu_ref[0, 0])
            w_row = sw_ref[0, :, s * rows:(s + 1) * rows]
            w_col = jnp.sum(jnp.where(eye, w_row, 0.0), axis=1, keepdims=True)
            ys[s] = _bdot(hid, wd_ref[0, 0]) * w_col
        for s in subs:
            for c in range(ROW_CHUNKS):
                ys_refs[s][pl.ds(c * pitch, rows), :] = ys[s][:, c * LANES:(c + 1) * LANES]
        for s in subs:
            for r0 in range(0, rows, SCATTER_BATCH):
                batch = range(r0, r0 + SCATTER_BATCH)
                ats = [token_rows(s * rows + r) for r in batch]
                new = [acc_ref[0, at, :] + ys_refs[s][pl.ds(r, ROW_CHUNKS, stride=pitch), :]
                       for at, r in zip(ats, batch)]
                for at, val in zip(ats, new):
                    acc_ref[0, at, :] = val

    live = live_ref[i * pl.num_programs(1) + j]
    for n_live in range(1, n_sub + 1):
        pl.when(live == n_live)(functools.partial(run, range(n_live)))


def _routed_experts(x_slabs, idx_t, w_t, rank_t, counts, w_gate_up, w_down, layer, tile_tokens):
    d = D_MODEL
    t = x_slabs.shape[0] // ROW_CHUNKS
    n_tiles = t // tile_tokens
    rows, slot_w, step_e, n_used, live, steps = _dispatch_plan(idx_t, w_t, rank_t, counts, tile_tokens)
    rows = rows.reshape(n_tiles * steps, 1, STEP_ROWS)
    slot_w = slot_w.reshape(n_tiles * steps, 1, STEP_ROWS)
    x_slabs = x_slabs.reshape(n_tiles, tile_tokens * ROW_CHUNKS, LANES)
    slab_rows = (tile_tokens + 1) * ROW_CHUNKS

    def used_step(i, j, se, nu, lv):
        return (i * steps + jnp.minimum(j, jnp.maximum(nu[i] - 1, 0)), 0, 0)

    pick = lambda i, j, se, nu, lv: (layer, se[i * steps + j], 0, 0)
    tile = lambda i, j, se, nu, lv: (i, 0, 0)
    resident = pl.Buffered(1)
    stage = pltpu.VMEM((ROW_CHUNKS * STAGE_PITCH, LANES), F32)
    grid_spec = pltpu.PrefetchScalarGridSpec(
        num_scalar_prefetch=3,
        grid=(n_tiles, steps),
        in_specs=[
            pl.BlockSpec((1, 1, STEP_ROWS), used_step, memory_space=pltpu.SMEM),
            pl.BlockSpec((1, 1, STEP_ROWS), used_step),
            pl.BlockSpec((1, tile_tokens * ROW_CHUNKS, LANES), tile, pipeline_mode=resident),
            pl.BlockSpec((1, 1, d, D_EXPERT), pick),
            pl.BlockSpec((1, 1, d, D_EXPERT), pick),
            pl.BlockSpec((1, 1, D_EXPERT, d), pick),
        ],
        out_specs=pl.BlockSpec((1, slab_rows, LANES), tile, pipeline_mode=resident),
        scratch_shapes=[stage] * (2 * BLOCKS_PER_STEP),
    )
    return pl.pallas_call(
        _experts_kernel,
        grid_spec=grid_spec,
        out_shape=jax.ShapeDtypeStruct((n_tiles, slab_rows, LANES), F32),
        compiler_params=_cparams("parallel", "arbitrary"),
        name="moe_experts",
    )(step_e, n_used, live, rows, slot_w, x_slabs, *w_gate_up, w_down)


def _moe_out_kernel(x_ref, r_ref, wg_ref, wu_ref, wd_ref, g_ref, b_ref, o_ref):
    x = x_ref[...]
    tm = x.shape[0]
    hid = _silu(_bdot(x, wg_ref[...])) * _bdot(x, wu_ref[...])
    routed = jnp.concatenate([r_ref[0, pl.ds(c, tm, stride=ROW_CHUNKS), :] for c in range(ROW_CHUNKS)], axis=1)
    y = routed + _bdot(hid, wd_ref[...])
    o_ref[...] = _layer_norm(DN_ALPHA * x + y, g_ref[...], b_ref[...])


def _moe_out(x, routed, ws_gate, ws_up, ws_down, g, b, tile_tokens, tm=1024):
    t, d = x.shape
    tm = min(tm, tile_tokens)
    per_tile = tile_tokens // tm
    return pl.pallas_call(
        _moe_out_kernel,
        grid=(t // tm,),
        in_specs=[pl.BlockSpec((tm, d), lambda i: (i, 0)),
                  pl.BlockSpec((1, tm * ROW_CHUNKS, LANES), lambda i: (i // per_tile, i % per_tile, 0)),
                  pl.BlockSpec((d, D_SHARED), lambda i: (0, 0)),
                  pl.BlockSpec((d, D_SHARED), lambda i: (0, 0)),
                  pl.BlockSpec((D_SHARED, d), lambda i: (0, 0)),
                  pl.BlockSpec((1, d), lambda i: (0, 0)),
                  pl.BlockSpec((1, d), lambda i: (0, 0))],
        out_specs=pl.BlockSpec((tm, d), lambda i: (i, 0)),
        out_shape=jax.ShapeDtypeStruct((t, d), F32),
        compiler_params=_cparams("parallel"),
        name="moe_out",
    )(x, routed, ws_gate.astype(BF16), ws_up.astype(BF16), ws_down.astype(BF16),
      g.reshape(1, d), b.reshape(1, d))


def _moe_ln(hid, hid_slabs, w_router, router_bias, w_gate_up, w_down, layer, ws_gate, ws_up, ws_down, ln_g, ln_b):
    b, l, d = hid.shape
    x = hid.reshape(b * l, d)
    tile_tokens = min(MOE_TILE_TOKENS, b * l)
    idx_t, w_t, rank_t, counts = _router(x, w_router, router_bias, tile_tokens)
    routed = _routed_experts(hid_slabs, idx_t, w_t, rank_t, counts, w_gate_up, w_down, layer, tile_tokens)
    return _moe_out(x, routed, ws_gate, ws_up, ws_down, ln_g, ln_b, tile_tokens).reshape(b, l, d)


def kernel(x, a_w_in, a_conv_w, a_A_log, a_dt_bias, a_onorm_w, a_w_out, b_w_dq, b_qnorm_w, b_w_uq, b_w_o,
           kv_w_dkv, kv_norm_w, kv_w_ukv, ln1_g, ln1_b, ln2_g, ln2_b, moe_w_router, moe_router_bias,
           moe_w_gate, moe_w_up, moe_w_down, moe_ws_gate, moe_ws_up, moe_ws_down):
    l = x.shape[1]
    cos_t, sin_t = _rope_lane_tables(l)
    w_gate_up = (moe_w_gate.astype(BF16), moe_w_up.astype(BF16))
    w_down = moe_w_down.astype(BF16)
    h = x
    k = v = None
    for layer in range(DEPTH):
        if layer < N_A_LAYERS:
            i = layer
            h, slabs = _gated_deltanet_ln(h, a_w_in[i], a_conv_w[i], a_A_log[i], a_dt_bias[i], a_onorm_w[i],
                                          a_w_out[i], ln1_g[layer], ln1_b[layer])
        else:
            i = layer - N_A_LAYERS
            h, slabs = _mla_ln(h, b_w_dq[i], b_qnorm_w[i], b_w_uq[i], b_w_o[i], k, v, cos_t, sin_t,
                               ln1_g[layer], ln1_b[layer])
        h = _moe_ln(h, slabs, moe_w_router[layer], moe_router_bias[layer], w_gate_up, w_down, layer,
                    moe_ws_gate[layer], moe_ws_up[layer], moe_ws_down[layer], ln2_g[layer], ln2_b[layer])
        if layer == N_A_LAYERS - 1:
            k, v = _mla_kv(h, kv_w_dkv, kv_norm_w, kv_w_ukv, cos_t, sin_t)
    return h
```

```python
import functools

import jax
import jax.numpy as jnp
import numpy as np
from jax import lax
from jax.experimental import pallas as pl
from jax.experimental.pallas import tpu as pltpu

F32 = jnp.float32
BF16 = jnp.bfloat16
HIGHEST = lax.Precision.HIGHEST

D_MODEL = 1024
DEPTH = 4
N_A_LAYERS = DEPTH // 2
GDN_HEADS = 8
GDN_DK = 128
GDN_DV = 128
D_QK = GDN_HEADS * GDN_DK
D_VA = GDN_HEADS * GDN_DV
GDN_CONV_CH = 2 * D_QK + D_VA
CONV_WIDTH = 4
CHUNK = 64
MLA_HEADS = 8
Q_LORA = 512
KV_LORA = 256
D_NOPE = 128
D_ROPE = 64
D_V = 128
ROPE_THETA = 10000.0
N_EXPERTS = 64
TOP_K = 8
N_GROUPS = 8
GROUP_SIZE = N_EXPERTS // N_GROUPS
TOPK_GROUPS = 4
D_EXPERT = 256
D_SHARED = 256
ROUTED_SCALE = 2.5
DN_ALPHA = (2 * DEPTH) ** 0.25
EPS = 1e-6
LN_EPS = 1e-5
LOG2_E = 1.4426950408889634

LANES = 128
SUBLANES = 8
VMEM_LIMIT_BYTES = 56 * 1024 * 1024

EXPERT_ROWS = 128
ROW_CHUNKS = D_MODEL // LANES
STAGE_PITCH = EXPERT_ROWS + 1
SCATTER_BATCH = 8
BLOCKS_PER_STEP = 2
STEP_ROWS = BLOCKS_PER_STEP * EXPERT_ROWS
MOE_TILE_TOKENS = 4096
ATTN_ROW_GROUPS = 4
assert TOP_K == SUBLANES


def _cparams(*sem):
    return pltpu.CompilerParams(dimension_semantics=sem, vmem_limit_bytes=VMEM_LIMIT_BYTES)


def _silu(x):
    return x * jax.nn.sigmoid(x)


def _layer_norm(x, g, b):
    mu = jnp.mean(x, -1, keepdims=True)
    xc = x - mu
    var = jnp.mean(xc * xc, -1, keepdims=True)
    return xc * lax.rsqrt(var + LN_EPS) * g + b


def _rms_norm(x, w):
    return x * lax.rsqrt(jnp.mean(x * x, -1, keepdims=True) + EPS) * w


def _bdot(a, b):
    return jnp.dot(a.astype(BF16), b.astype(BF16), preferred_element_type=F32)


def _bdot_nt(a, b):
    return lax.dot_general(a.astype(BF16), b.astype(BF16), (((1,), (1,)), ((), ())),
                           preferred_element_type=F32)


def _hdot(a, b):
    return jnp.dot(a, b, precision=HIGHEST, preferred_element_type=F32)


def _matmul_kernel(x_ref, w_ref, o_ref):
    o_ref[...] = _bdot(x_ref[...], w_ref[...]).astype(o_ref.dtype)


def _matmul(x, w, out_dtype, tm, tn):
    m, k = x.shape
    n = w.shape[1]
    tm = min(tm, m)
    tn = min(tn, n)
    return pl.pallas_call(
        _matmul_kernel,
        grid=(m // tm, n // tn),
        in_specs=[pl.BlockSpec((tm, k), lambda i, j: (i, 0)),
                  pl.BlockSpec((k, tn), lambda i, j: (0, j))],
        out_specs=pl.BlockSpec((tm, tn), lambda i, j: (i, j)),
        out_shape=jax.ShapeDtypeStruct((m, n), out_dtype),
        compiler_params=_cparams("parallel", "parallel"),
        name="matmul",
    )(x, w)


def _matmul_ln_kernel(x_ref, w_ref, r_ref, g_ref, b_ref, o_ref, slab_ref):
    mix = _bdot(x_ref[...], w_ref[...])
    out = _layer_norm(DN_ALPHA * r_ref[...] + mix, g_ref[...], b_ref[...])
    o_ref[...] = out
    tm = out.shape[0]
    for c in range(ROW_CHUNKS):
        slab_ref[pl.ds(c, tm, stride=ROW_CHUNKS), :] = out[:, c * LANES:(c + 1) * LANES]


def _matmul_ln(x, w, res, g, b, tm=1024):
    m, k = x.shape
    n = w.shape[1]
    tm = min(tm, m)
    chunks = n // LANES
    return pl.pallas_call(
        _matmul_ln_kernel,
        grid=(m // tm,),
        in_specs=[pl.BlockSpec((tm, k), lambda i: (i, 0)),
                  pl.BlockSpec((k, n), lambda i: (0, 0)),
                  pl.BlockSpec((tm, n), lambda i: (i, 0)),
                  pl.BlockSpec((1, n), lambda i: (0, 0)),
                  pl.BlockSpec((1, n), lambda i: (0, 0))],
        out_specs=[pl.BlockSpec((tm, n), lambda i: (i, 0)),
                   pl.BlockSpec((tm * chunks, LANES), lambda i: (i, 0))],
        out_shape=[jax.ShapeDtypeStruct((m, n), F32),
                   jax.ShapeDtypeStruct((m * chunks, LANES), F32)],
        compiler_params=_cparams("parallel"),
        name="matmul_ln",
    )(x, w, res, g.reshape(1, n), b.reshape(1, n))


def _softplus(x):
    return jnp.maximum(x, 0.0) + jnp.log1p(jnp.exp(-jnp.abs(x)))


def _gdn_gates_kernel(x_ref, wab_ref, wabt_ref, alog_ref, dtb_ref, alogt_ref, dtbt_ref,
                      gc_ref, beta_ref, gct_ref):
    x = x_ref[...]
    tl = x.shape[0]
    h = GDN_HEADS
    ab = _bdot(x, wab_ref[...])
    abt = _bdot_nt(wabt_ref[...], x)
    g = -jnp.exp(alog_ref[...]) * _softplus(ab[:, :h] + dtb_ref[...])
    gt = -jnp.exp(alogt_ref[...]) * _softplus(abt[:h, :] + dtbt_ref[...])
    beta_ref[...] = jax.nn.sigmoid(ab[:, h:2 * h])
    row = lax.broadcasted_iota(jnp.int32, (tl, tl), 0)
    col = lax.broadcasted_iota(jnp.int32, (tl, tl), 1)
    shift = CHUNK.bit_length() - 1
    same = (row >> shift) == (col >> shift)
    lower = jnp.where(same & (col <= row), 1.0, 0.0).astype(F32)
    upper = jnp.where(same & (row <= col), 1.0, 0.0).astype(F32)
    gc_ref[...] = _hdot(lower, g)
    gct_ref[...] = _hdot(gt, upper)


def _gdn_gates(x, w_ab, a_log, dt_bias, tl=256):
    t = x.shape[0]
    tl = min(tl, t)
    h = GDN_HEADS
    wab = jnp.zeros((D_MODEL, LANES), F32).at[:, :2 * h].set(w_ab)
    wabt = w_ab.T
    return pl.pallas_call(
        _gdn_gates_kernel,
        grid=(t // tl,),
        in_specs=[pl.BlockSpec((tl, D_MODEL), lambda i: (i, 0)),
                  pl.BlockSpec((D_MODEL, LANES), lambda i: (0, 0)),
                  pl.BlockSpec((2 * h, D_MODEL), lambda i: (0, 0)),
                  pl.BlockSpec((1, h), lambda i: (0, 0)),
                  pl.BlockSpec((1, h), lambda i: (0, 0)),
                  pl.BlockSpec((h, 1), lambda i: (0, 0)),
                  pl.BlockSpec((h, 1), lambda i: (0, 0))],
        out_specs=[pl.BlockSpec((tl, h), lambda i: (i, 0)),
                   pl.BlockSpec((tl, h), lambda i: (i, 0)),
                   pl.BlockSpec((h, tl), lambda i: (0, i))],
        out_shape=[jax.ShapeDtypeStruct((t, h), F32),
                   jax.ShapeDtypeStruct((t, h), F32),
                   jax.ShapeDtypeStruct((h, t), F32)],
        compiler_params=_cparams("parallel"),
        name="gdn_gates",
    )(x, wab, wabt, a_log.reshape(1, h), dt_bias.reshape(1, h),
      a_log.reshape(h, 1), dt_bias.reshape(h, 1))


def _gdn_conv_kernel(cur_ref, prev_ref, w_ref, q_ref, k_ref, v_ref):
    i = pl.program_id(1)
    x = cur_ref[0]
    tl = x.shape[0]
    prev = jnp.where(i > 0, prev_ref[0], 0.0)
    xx = jnp.concatenate([prev, x], axis=0)
    w = w_ref[...]
    y = None
    for j in range(CONV_WIDTH):
        off = SUBLANES - (CONV_WIDTH - 1) + j
        term = xx[off:off + tl, :] * w[j:j + 1, :]
        y = term if y is None else y + term
    y = _silu(y)
    for h in range(GDN_HEADS):
        sl = slice(h * GDN_DK, (h + 1) * GDN_DK)
        qh = y[:, sl]
        q_ref[0, :, sl] = qh * lax.rsqrt(jnp.sum(qh * qh, -1, keepdims=True) + EPS) * (GDN_DK ** -0.5)
        kh = y[:, D_QK + h * GDN_DK:D_QK + (h + 1) * GDN_DK]
        k_ref[0, :, sl] = kh * lax.rsqrt(jnp.sum(kh * kh, -1, keepdims=True) + EPS)
    v_ref[0] = y[:, 2 * D_QK:]


def _gdn_conv(proj, conv_w, tl=256):
    b, l, _ = proj.shape
    tl = min(tl, l)
    c3 = GDN_CONV_CH
    per = tl // SUBLANES
    out = jax.ShapeDtypeStruct((b, l, D_QK), F32)
    return pl.pallas_call(
        _gdn_conv_kernel,
        grid=(b, l // tl),
        in_specs=[pl.BlockSpec((1, tl, c3), lambda bi, i: (bi, i, 0)),
                  pl.BlockSpec((1, SUBLANES, c3), lambda bi, i: (bi, jnp.maximum(i * per - 1, 0), 0)),
                  pl.BlockSpec((CONV_WIDTH, c3), lambda bi, i: (0, 0))],
        out_specs=[pl.BlockSpec((1, tl, D_QK), lambda bi, i: (bi, i, 0))] * 3,
        out_shape=[out, out, out],
        compiler_params=_cparams("parallel", "parallel"),
        name="gdn_conv",
    )(proj, proj, conv_w)


def _split_bf16(x):
    hi = x.astype(BF16)
    lo = (x - hi.astype(F32)).astype(BF16)
    return hi, lo


def _dot_split(a, b):
    (ah, al), (bh, bl) = a, b
    dot = functools.partial(jnp.dot, preferred_element_type=F32)
    return dot(ah, bh) + (dot(ah, bl) + dot(al, bh))


def _gdn_prep_kernel(q_ref, k_ref, v_ref, gc_ref, beta_ref, gct_ref, wq_ref, u_ref, attn_ref, kgt_ref,
                     *, chunks):
    c = CHUNK
    row = lax.broadcasted_iota(jnp.int32, (c, c), 0)
    col = lax.broadcasted_iota(jnp.int32, (c, c), 1)
    incl = row >= col
    strict = row > col
    eye = jnp.where(row == col, 1.0, 0.0).astype(F32)
    items = [(ci, h) for ci in range(chunks) for h in range(GDN_HEADS)]

    st = {}
    for ci, h in items:
        rows = slice(ci * c, (ci + 1) * c)
        sl = slice(h * GDN_DK, (h + 1) * GDN_DK)
        q = q_ref[0, rows, sl]
        k = k_ref[0, rows, sl]
        v = v_ref[0, rows, sl]
        gcol = gc_ref[0, rows, h:h + 1]
        grow = gct_ref[0, ci, h:h + 1, :]
        bcol = beta_ref[0, rows, h:h + 1]
        decay = jnp.where(incl, jnp.exp(jnp.where(incl, gcol - grow, 0.0)), 0.0)
        kb = k * bcol
        eg = jnp.exp(gcol)
        g_last = gcol[c - 1:c, :]
        sc = _bdot_nt(jnp.concatenate([kb, q], axis=0), k)
        a = jnp.where(strict, sc[:c] * decay, 0.0)
        attn_ref[0, ci, h] = (sc[c:] * decay).astype(attn_ref.dtype)
        kgt_ref[0, ci, h] = (k * jnp.exp(g_last - gcol)).T.astype(kgt_ref.dtype)
        wq_ref[0, ci, h, c:, :] = (q * eg).astype(wq_ref.dtype)
        st[ci, h] = dict(m=-a, rhs=jnp.concatenate([v * bcol, kb * eg], axis=1))

    for it in items:
        st[it]["p"] = eye + st[it]["m"]
    span = 2
    while span < c:
        for it in items:
            ms = _split_bf16(st[it]["m"])
            st[it]["m"] = _dot_split(ms, ms)
        for it in items:
            st[it]["p"] = st[it]["p"] + _dot_split(_split_bf16(st[it]["p"]), _split_bf16(st[it]["m"]))
        span *= 2

    for ci, h in items:
        sol = _dot_split(_split_bf16(st[ci, h]["p"]), _split_bf16(st[ci, h]["rhs"]))
        u_ref[0, ci * c:(ci + 1) * c, h * GDN_DV:(h + 1) * GDN_DV] = sol[:, :GDN_DV]
        wq_ref[0, ci, h, :c, :] = sol[:, GDN_DV:].astype(wq_ref.dtype)


def _gdn_prep(q, k, v, gc, beta, gct, chunks=2):
    b, l, _ = q.shape
    n = l // CHUNK
    chunks = min(chunks, n)
    h = GDN_HEADS
    rows = chunks * CHUNK
    blk = pl.BlockSpec((1, rows, D_QK), lambda bi, i: (bi, i, 0))
    gate = pl.BlockSpec((1, rows, h), lambda bi, i: (bi, i, 0))
    return pl.pallas_call(
        functools.partial(_gdn_prep_kernel, chunks=chunks),
        grid=(b, n // chunks),
        in_specs=[blk, blk, blk, gate, gate,
                  pl.BlockSpec((1, chunks, h, CHUNK), lambda bi, i: (bi, i, 0, 0))],
        out_specs=[pl.BlockSpec((1, chunks, h, 2 * CHUNK, GDN_DK), lambda bi, i: (bi, i, 0, 0, 0)),
                   pl.BlockSpec((1, rows, D_VA), lambda bi, i: (bi, i, 0)),
                   pl.BlockSpec((1, chunks, h, CHUNK, CHUNK), lambda bi, i: (bi, i, 0, 0, 0)),
                   pl.BlockSpec((1, chunks, h, GDN_DK, CHUNK), lambda bi, i: (bi, i, 0, 0, 0))],
        out_shape=[jax.ShapeDtypeStruct((b, n, h, 2 * CHUNK, GDN_DK), BF16),
                   jax.ShapeDtypeStruct((b, l, D_VA), F32),
                   jax.ShapeDtypeStruct((b, n, h, CHUNK, CHUNK), BF16),
                   jax.ShapeDtypeStruct((b, n, h, GDN_DK, CHUNK), BF16)],
        compiler_params=_cparams("parallel", "parallel"),
        name="gdn_prep",
    )(q, k, v, gc, beta, gct)


def _gdn_state_kernel(wq_ref, u_ref, attn_ref, kgt_ref, gc_ref, z_ref, onw_ref, o_ref, s_ref, *, chunks):
    @pl.when(pl.program_id(1) == 0)
    def _():
        s_ref[...] = jnp.zeros_like(s_ref)

    c = CHUNK
    onw = onw_ref[...]
    dot = functools.partial(jnp.dot, preferred_element_type=F32)
    heads = range(GDN_HEADS)
    for ci in range(chunks):
        rows = slice(ci * c, (ci + 1) * c)
        s_old = [s_ref[h] for h in heads]
        r = [dot(wq_ref[0, ci, h], s_old[h].astype(BF16)) for h in heads]
        v_new = [(u_ref[0, rows, h * GDN_DV:(h + 1) * GDN_DV] - r[h][:c]).astype(BF16) for h in heads]
        for h in heads:
            decay_last = jnp.exp(gc_ref[0, (ci + 1) * c - 1:(ci + 1) * c, h:h + 1])
            s_ref[h] = s_old[h] * decay_last + dot(kgt_ref[0, ci, h], v_new[h])
        for h in heads:
            sl = slice(h * GDN_DV, (h + 1) * GDN_DV)
            o = r[h][c:] + dot(attn_ref[0, ci, h], v_new[h])
            o_ref[0, rows, sl] = (_rms_norm(o, onw) * _silu(z_ref[0, rows, sl])).astype(o_ref.dtype)


def _gdn_state(wq, u, attn, kgt, gc, proj, onorm_w, chunks=8):
    b, n, h = wq.shape[:3]
    l = n * CHUNK
    chunks = min(chunks, n)
    rows = chunks * CHUNK
    z_block = GDN_CONV_CH // D_VA
    return pl.pallas_call(
        functools.partial(_gdn_state_kernel, chunks=chunks),
        grid=(b, n // chunks),
        in_specs=[pl.BlockSpec((1, chunks, h, 2 * CHUNK, GDN_DK), lambda bi, i: (bi, i, 0, 0, 0)),
                  pl.BlockSpec((1, rows, D_VA), lambda bi, i: (bi, i, 0)),
                  pl.BlockSpec((1, chunks, h, CHUNK, CHUNK), lambda bi, i: (bi, i, 0, 0, 0)),
                  pl.BlockSpec((1, chunks, h, GDN_DK, CHUNK), lambda bi, i: (bi, i, 0, 0, 0)),
                  pl.BlockSpec((1, rows, h), lambda bi, i: (bi, i, 0)),
                  pl.BlockSpec((1, rows, D_VA), lambda bi, i: (bi, i, z_block)),
                  pl.BlockSpec((1, GDN_DV), lambda bi, i: (0, 0))],
        out_specs=pl.BlockSpec((1, rows, D_VA), lambda bi, i: (bi, i, 0)),
        out_shape=jax.ShapeDtypeStruct((b, l, D_VA), BF16),
        scratch_shapes=[pltpu.VMEM((h, GDN_DK, GDN_DV), F32)],
        compiler_params=_cparams("parallel", "arbitrary"),
        name="gdn_state",
    )(wq, u, attn, kgt, gc, proj, onorm_w.reshape(1, GDN_DV))


def _gated_deltanet_ln(hid, w_in, conv_w, a_log, dt_bias, onorm_w, w_out, ln_g, ln_b):
    b, l, d = hid.shape
    h = GDN_HEADS
    n = l // CHUNK
    x = hid.reshape(b * l, d)
    n_main = GDN_CONV_CH + D_VA
    proj = _matmul(x, w_in[:, :n_main].astype(BF16), F32, 1024, 1024).reshape(b, l, n_main)
    gc, beta, gct = _gdn_gates(x, w_in[:, n_main:], a_log, dt_bias)
    gc = gc.reshape(b, l, h)
    gct = gct.reshape(h, b, n, CHUNK).transpose(1, 2, 0, 3)
    q, k, v = _gdn_conv(proj, conv_w)
    wq, u, attn, kgt = _gdn_prep(q, k, v, gc, beta.reshape(b, l, h), gct)
    o = _gdn_state(wq, u, attn, kgt, gc, proj, onorm_w)
    out, slabs = _matmul_ln(o.reshape(b * l, D_VA), w_out.astype(BF16), x, ln_g, ln_b)
    return out.reshape(b, l, d), slabs


def _rope_lane_tables(l):
    inv = 1.0 / (ROPE_THETA ** (jnp.arange(0, D_ROPE, 2, dtype=F32) / D_ROPE))
    ang = jnp.arange(l, dtype=F32)[:, None] * inv[None, :]
    cos, sin = jnp.cos(ang), jnp.sin(ang)
    zero = jnp.zeros((l, LANES - D_ROPE), F32)
    return (jnp.concatenate([cos, cos, zero], -1),
            jnp.concatenate([-sin, sin, zero], -1))


def _rope_weight_groups(w_rope):
    kdim = w_rope.shape[0]
    half = D_ROPE // 2
    zero = jnp.zeros((kdim, LANES - D_ROPE), w_rope.dtype)
    x1, x2 = w_rope[:, :half], w_rope[:, half:]
    return jnp.concatenate([x1, x2, zero, x2, x1, zero], axis=-1)


def _mla_kv_kernel(x_ref, wd_ref, nw_ref, wu_ref, cos_ref, sin_ref, k_ref, v_ref):
    x = x_ref[0]
    ckv = _bdot(x, wd_ref[...])
    c = _rms_norm(ckv[:, :KV_LORA], nw_ref[...])
    k_rope_t = (ckv[:, KV_LORA:KV_LORA + LANES] * cos_ref[...]
                + ckv[:, KV_LORA + LANES:] * sin_ref[...]).T.astype(k_ref.dtype)
    kv = _bdot(c, wu_ref[...])
    per = D_NOPE + D_V
    ones_col = jnp.where(lax.broadcasted_iota(jnp.int32, (x.shape[0], LANES), 1) == 0, 1.0, 0.0).astype(v_ref.dtype)
    for h in range(MLA_HEADS):
        k_ref[0, h, :D_NOPE, :] = kv[:, h * per:h * per + D_NOPE].T.astype(k_ref.dtype)
        k_ref[0, h, D_NOPE:, :] = k_rope_t
        v_ref[0, h, :, :D_V] = kv[:, h * per + D_NOPE:(h + 1) * per].astype(v_ref.dtype)
        v_ref[0, h, :, D_V:] = ones_col


def _mla_kv(hid, w_dkv, kv_norm_w, w_ukv, cos_t, sin_t, tl=512):
    b, l, d = hid.shape
    tl = min(tl, l)
    wd = jnp.concatenate([w_dkv[:, :KV_LORA], _rope_weight_groups(w_dkv[:, KV_LORA:])], -1).astype(BF16)
    nd = wd.shape[1]
    hh = MLA_HEADS
    return pl.pallas_call(
        _mla_kv_kernel,
        grid=(b, l // tl),
        in_specs=[pl.BlockSpec((1, tl, d), lambda bi, i: (bi, i, 0)),
                  pl.BlockSpec((d, nd), lambda bi, i: (0, 0)),
                  pl.BlockSpec((1, KV_LORA), lambda bi, i: (0, 0)),
                  pl.BlockSpec((KV_LORA, hh * (D_NOPE + D_V)), lambda bi, i: (0, 0)),
                  pl.BlockSpec((tl, LANES), lambda bi, i: (i, 0)),
                  pl.BlockSpec((tl, LANES), lambda bi, i: (i, 0))],
        out_specs=[pl.BlockSpec((1, hh, D_NOPE + LANES, tl), lambda bi, i: (bi, 0, 0, i)),
                   pl.BlockSpec((1, hh, tl, D_V + LANES), lambda bi, i: (bi, 0, i, 0))],
        out_shape=[jax.ShapeDtypeStruct((b, hh, D_NOPE + LANES, l), BF16),
                   jax.ShapeDtypeStruct((b, hh, l, D_V + LANES), BF16)],
        compiler_params=_cparams("parallel", "parallel"),
        name="mla_kv",
    )(hid, wd, kv_norm_w.reshape(1, KV_LORA), w_ukv.astype(BF16), cos_t, sin_t)


def _mla_cq_kernel(x_ref, w_ref, nw_ref, o_ref):
    o_ref[...] = _rms_norm(_bdot(x_ref[...], w_ref[...]), nw_ref[...]).astype(o_ref.dtype)


def _mla_cq(x, w_dq, qnorm_w, tm=512):
    m, k = x.shape
    tm = min(tm, m)
    return pl.pallas_call(
        _mla_cq_kernel,
        grid=(m // tm,),
        in_specs=[pl.BlockSpec((tm, k), lambda i: (i, 0)),
                  pl.BlockSpec((k, Q_LORA), lambda i: (0, 0)),
                  pl.BlockSpec((1, Q_LORA), lambda i: (0, 0))],
        out_specs=pl.BlockSpec((tm, Q_LORA), lambda i: (i, 0)),
        out_shape=jax.ShapeDtypeStruct((m, Q_LORA), BF16),
        compiler_params=_cparams("parallel"),
        name="mla_cq",
    )(x, w_dq.astype(BF16), qnorm_w.reshape(1, Q_LORA))


def _mla_q_kernel(c_ref, w_ref, cos_ref, sin_ref, q_ref):
    c = c_ref[0]
    scale = (D_NOPE + D_ROPE) ** -0.5 * LOG2_E
    per = D_NOPE + 2 * LANES
    for h in range(MLA_HEADS):
        qh = _bdot(c, w_ref[:, h * per:(h + 1) * per])
        rope = qh[:, D_NOPE:D_NOPE + LANES] * cos_ref[...] + qh[:, D_NOPE + LANES:] * sin_ref[...]
        q_ref[0, h, :, :D_NOPE] = (qh[:, :D_NOPE] * scale).astype(q_ref.dtype)
        q_ref[0, h, :, D_NOPE:] = (rope * scale).astype(q_ref.dtype)


def _mla_q(cq, w_uq, cos_t, sin_t, tl=512):
    b, l, _ = cq.shape
    tl = min(tl, l)
    hh = MLA_HEADS
    per_in = D_NOPE + D_ROPE
    groups = []
    for h in range(hh):
        wh = w_uq[:, h * per_in:(h + 1) * per_in]
        groups += [wh[:, :D_NOPE], _rope_weight_groups(wh[:, D_NOPE:])]
    w = jnp.concatenate(groups, -1).astype(BF16)
    return pl.pallas_call(
        _mla_q_kernel,
        grid=(b, l // tl),
        in_specs=[pl.BlockSpec((1, tl, Q_LORA), lambda bi, i: (bi, i, 0)),
                  pl.BlockSpec(w.shape, lambda bi, i: (0, 0)),
                  pl.BlockSpec((tl, LANES), lambda bi, i: (i, 0)),
                  pl.BlockSpec((tl, LANES), lambda bi, i: (i, 0))],
        out_specs=pl.BlockSpec((1, hh, tl, D_NOPE + LANES), lambda bi, i: (bi, 0, i, 0)),
        out_shape=jax.ShapeDtypeStruct((b, hh, l, D_NOPE + LANES), BF16),
        compiler_params=_cparams("parallel", "parallel"),
        name="mla_q",
    )(cq, w, cos_t, sin_t)


def _mla_attn_kernel(q_ref, k_ref, v_ref, o_ref, *, tk):
    qi = pl.program_id(2)
    q = q_ref[0, 0]
    tq = q.shape[0]
    per = tq // tk

    groups = ATTN_ROW_GROUPS
    rows = tq // groups
    qs = [q[g * rows:(g + 1) * rows] for g in range(groups)]

    def block(j, carry, mask_offset=None):
        at = pl.ds(pl.multiple_of(j * tk, tk), tk)
        kb = k_ref[0, 0, :, at]
        vb = v_ref[0, 0, at, :]
        if mask_offset is None:
            widths = [tk] * groups
            ss = [_bdot(qg, kb) for qg in qs]
        else:
            widths = [min(max((g + 1) * rows - mask_offset, LANES), tk) for g in range(groups)]
            ss = []
            for g, (qg, wd) in enumerate(zip(qs, widths)):
                qpos = lax.broadcasted_iota(jnp.int32, (rows, wd), 0)
                kpos = lax.broadcasted_iota(jnp.int32, (rows, wd), 1)
                ss.append(jnp.where(kpos + mask_offset <= qpos + g * rows, _bdot(qg, kb[:, :wd]), -jnp.inf))
        m_new = [jnp.maximum(c[0], jnp.max(s, -1, keepdims=True)) for c, s in zip(carry, ss)]
        ps = [jnp.exp2(s - m) for s, m in zip(ss, m_new)]
        return tuple((mn, jnp.exp2(m - mn) * acc + _bdot(p, vb[:wd]))
                     for (m, acc), mn, p, wd in zip(carry, m_new, ps, widths))

    first = qi * per
    init = (jnp.full((rows, 1), -jnp.inf, F32), jnp.zeros((rows, D_V + LANES), F32))
    carry = lax.fori_loop(0, first, block, (init,) * groups)
    for d in range(per):
        carry = block(first + d, carry, mask_offset=d * tk)
    for g, (_, acc) in enumerate(carry):
        o_ref[0, g * rows:(g + 1) * rows, :] = (acc[:, :D_V] / acc[:, D_V:D_V + 1]).astype(o_ref.dtype)


def _mla_attn(q, k, v, tq=1024, tk=1024):
    b, hh, l, dq = q.shape
    tq = min(tq, l)
    tk = min(tk, tq)
    return pl.pallas_call(
        functools.partial(_mla_attn_kernel, tk=tk),
        grid=(b, hh, l // tq),
        in_specs=[pl.BlockSpec((1, 1, tq, dq), lambda bi, h, i: (bi, h, i, 0)),
                  pl.BlockSpec((1, 1, dq, l), lambda bi, h, i: (bi, h, 0, 0)),
                  pl.BlockSpec((1, 1, l, D_V + LANES), lambda bi, h, i: (bi, h, 0, 0))],
        out_specs=pl.BlockSpec((1, tq, D_V), lambda bi, h, i: (bi, i, h)),
        out_shape=jax.ShapeDtypeStruct((b, l, hh * D_V), BF16),
        compiler_params=_cparams("parallel", "parallel", "parallel"),
        name="mla_attn",
    )(q, k, v)


def _mla_ln(hid, w_dq, qnorm_w, w_uq, w_o, k, v, cos_t, sin_t, ln_g, ln_b):
    b, l, d = hid.shape
    x = hid.reshape(b * l, d)
    cq = _mla_cq(x, w_dq, qnorm_w).reshape(b, l, Q_LORA)
    q = _mla_q(cq, w_uq, cos_t, sin_t)
    o = _mla_attn(q, k, v)
    out, slabs = _matmul_ln(o.reshape(b * l, MLA_HEADS * D_V), w_o.astype(BF16), x, ln_g, ln_b)
    return out.reshape(b, l, d), slabs


def _first_argmax(x, ids, n):
    m = jnp.max(x, axis=0, keepdims=True)
    first = jnp.min(jnp.where(x == m, ids, n), axis=0, keepdims=True)
    return m, first


def _router_kernel(x_ref, wt_ref, bias_ref, before_ref, idx_ref, w_ref, rank_ref, cnt_out_ref, cnt_ref,
                   *, steps_per_tile):
    @pl.when(pl.program_id(0) % steps_per_tile == 0)
    def _():
        cnt_ref[...] = jnp.zeros_like(cnt_ref)

    x = x_ref[...]
    t = x.shape[0]
    logits = lax.dot_general(wt_ref[...], x, (((1,), (1,)), ((), ())),
                             precision=HIGHEST, preferred_element_type=F32)
    scores = jax.nn.sigmoid(logits)
    biased = scores + bias_ref[...]
    neg = -jnp.inf
    sub = lax.broadcasted_iota(jnp.int32, (GROUP_SIZE, t), 0).astype(F32)
    gscores = []
    for g in range(N_GROUPS):
        xg = biased[g * GROUP_SIZE:(g + 1) * GROUP_SIZE, :]
        m1, i1 = _first_argmax(xg, sub, float(GROUP_SIZE))
        m2 = jnp.max(jnp.where(sub == i1, neg, xg), axis=0, keepdims=True)
        gscores.append(m1 + m2)
    gs = jnp.concatenate(gscores, axis=0)
    gid = lax.broadcasted_iota(jnp.int32, (N_GROUPS, t), 0).astype(F32)
    gsel = jnp.zeros((N_GROUPS, t), F32)
    for _ in range(TOPK_GROUPS):
        _, gi = _first_argmax(gs, gid, float(N_GROUPS))
        hit = gid == gi
        gsel = jnp.where(hit, 1.0, gsel)
        gs = jnp.where(hit, neg, gs)
    eid = lax.broadcasted_iota(jnp.int32, (N_EXPERTS, t), 0).astype(F32)
    allowed = jnp.concatenate(
        [jnp.broadcast_to(gsel[g:g + 1, :], (GROUP_SIZE, t)) for g in range(N_GROUPS)], axis=0)
    cand = jnp.where(allowed > 0.0, biased, neg)
    idxs, ws, hits = [], [], []
    for _ in range(TOP_K):
        _, ei = _first_argmax(cand, eid, float(N_EXPERTS))
        hit = eid == ei
        idxs.append(ei)
        hits.append(hit)
        ws.append(jnp.sum(jnp.where(hit, scores, 0.0), axis=0, keepdims=True))
        cand = jnp.where(hit, neg, cand)
    w = jnp.concatenate(ws, axis=0)
    w = w / (jnp.sum(w, axis=0, keepdims=True) + 1e-20) * ROUTED_SCALE
    idx_ref[...] = jnp.concatenate(idxs, axis=0).astype(jnp.int32)
    w_ref[...] = w
    chosen = jnp.zeros((N_EXPERTS, t), F32)
    for hit in hits:
        chosen = jnp.where(hit, 1.0, chosen)
    prior = jnp.dot(chosen.astype(BF16), before_ref[...],
                    preferred_element_type=F32) + cnt_ref[...]
    rank_ref[...] = jnp.concatenate(
        [jnp.sum(jnp.where(hit, prior, 0.0), axis=0, keepdims=True) for hit in hits], axis=0).astype(jnp.int32)
    cnt_ref[...] += jnp.sum(chosen, axis=1, keepdims=True)
    cnt_out_ref[0] = cnt_ref[...]


def _router(x, w_router, bias, tile_tokens, tr=1024):
    t, d = x.shape
    tr = min(tr, tile_tokens)
    steps_per_tile = tile_tokens // tr
    n_tiles = t // tile_tokens
    kt = pl.BlockSpec((TOP_K, tr), lambda i: (0, i))
    before = jnp.triu(jnp.ones((tr, tr), BF16), k=1)
    idx, w, rank, counts = pl.pallas_call(
        functools.partial(_router_kernel, steps_per_tile=steps_per_tile),
        grid=(t // tr,),
        in_specs=[pl.BlockSpec((tr, d), lambda i: (i, 0)),
                  pl.BlockSpec((N_EXPERTS, d), lambda i: (0, 0)),
                  pl.BlockSpec((N_EXPERTS, 1), lambda i: (0, 0)),
                  pl.BlockSpec((tr, tr), lambda i: (0, 0))],
        out_specs=[kt, kt, kt,
                   pl.BlockSpec((1, N_EXPERTS, 1), lambda i: (i // steps_per_tile, 0, 0))],
        out_shape=[jax.ShapeDtypeStruct((TOP_K, t), jnp.int32),
                   jax.ShapeDtypeStruct((TOP_K, t), F32),
                   jax.ShapeDtypeStruct((TOP_K, t), jnp.int32),
                   jax.ShapeDtypeStruct((n_tiles, N_EXPERTS, 1), F32)],
        scratch_shapes=[pltpu.VMEM((N_EXPERTS, 1), F32)],
        compiler_params=_cparams("arbitrary"),
        name="moe_router",
    )(x, w_router.T, bias.reshape(N_EXPERTS, 1), before)
    return idx, w, rank, counts.reshape(n_tiles, N_EXPERTS).astype(jnp.int32)


def _steps_per_tile(tile_tokens):
    return tile_tokens * TOP_K // STEP_ROWS + N_EXPERTS


def _plan_kernel(fill_lo_ref, fill_hi_ref, pos_ref, src_ref, *, tile_tokens):
    i = pl.program_id(0)
    n_assign = tile_tokens * TOP_K
    empty = n_assign
    group = 16

    n_ranges = N_EXPERTS + 1

    def fill_range(g, carry):
        lo = fill_lo_ref[i * n_ranges + g]
        hi = fill_hi_ref[i * n_ranges + g]

        def fill(b, carry):
            for d in range(group):
                src_ref[0, 0, jnp.maximum(hi - 1 - b * group - d, 0)] = empty
            return carry
        return lax.fori_loop(0, (hi - lo + group - 1) // group, fill, carry)
    lax.fori_loop(0, n_ranges, fill_range, 0)

    def place(b, carry):
        base = b * group
        slots = [pos_ref[0, 0, base + d] for d in range(group)]
        for d in range(group):
            src_ref[0, 0, slots[d]] = base + d
        return carry
    lax.fori_loop(0, n_assign // group, place, 0)


def _dispatch_plan(idx_t, w_t, rank_t, counts, tile_tokens):
    t = idx_t.shape[1]
    n_tiles = t // tile_tokens
    steps = _steps_per_tile(tile_tokens)
    padded = (counts + STEP_ROWS - 1) // STEP_ROWS * STEP_ROWS
    pad_end = jnp.cumsum(padded, axis=1)
    pad_start = (pad_end - padded).astype(jnp.int32)
    n_used = (pad_end[:, -1] // STEP_ROWS).astype(jnp.int32)
    starts = jnp.arange(steps, dtype=jnp.int32) * STEP_ROWS
    step_e = jnp.sum(starts[None, :, None] >= pad_end[:, None, :], axis=-1).astype(jnp.int32)
    step_e = jnp.minimum(step_e, N_EXPERTS - 1)
    last = jnp.take_along_axis(step_e, jnp.maximum(n_used - 1, 0)[:, None], axis=1)
    step_e = jnp.where(starts[None, :] // STEP_ROWS < n_used[:, None], step_e, last)

    start_of = jnp.repeat(pad_start.T, tile_tokens, axis=1)
    experts = jnp.arange(N_EXPERTS, dtype=jnp.int32)[None, :, None]
    pos_t = rank_t + jnp.sum(jnp.where(idx_t[:, None, :] == experts, start_of[None], 0), axis=1)
    group_end = (pad_start + counts).astype(jnp.int32)
    n_slots = steps * STEP_ROWS
    fill_lo = jnp.concatenate([group_end, pad_end[:, -1:]], axis=1).astype(jnp.int32)
    fill_hi = jnp.concatenate([pad_end, jnp.full((n_tiles, 1), n_slots)], axis=1).astype(jnp.int32)

    n_assign = tile_tokens * TOP_K
    single = pl.Buffered(1)
    src = pl.pallas_call(
        functools.partial(_plan_kernel, tile_tokens=tile_tokens),
        grid_spec=pltpu.PrefetchScalarGridSpec(
            num_scalar_prefetch=2,
            grid=(n_tiles,),
            in_specs=[pl.BlockSpec((1, 1, n_assign), lambda i, lo, hi: (i, 0, 0),
                                   memory_space=pltpu.SMEM, pipeline_mode=single)],
            out_specs=pl.BlockSpec((1, 1, n_slots), lambda i, lo, hi: (i, 0, 0),
                                   memory_space=pltpu.SMEM, pipeline_mode=single),
        ),
        out_shape=jax.ShapeDtypeStruct((n_tiles, 1, n_slots), jnp.int32),
        compiler_params=_cparams("parallel"),
        name="moe_plan",
    )(fill_lo.reshape(-1), fill_hi.reshape(-1), pos_t.T.reshape(n_tiles, 1, n_assign))
    rows = src & -SUBLANES
    w_flat = jnp.pad(w_t.T.reshape(n_tiles, tile_tokens * TOP_K), ((0, 0), (0, SUBLANES)))
    slot_w = jnp.take_along_axis(w_flat, src[:, 0, :], axis=1)
    held = jnp.clip(jnp.take_along_axis(group_end, step_e, axis=1) - starts[None, :], 0, STEP_ROWS)
    live = jnp.where(starts[None, :] // STEP_ROWS < n_used[:, None], (held + EXPERT_ROWS - 1) // EXPERT_ROWS, 0)
    return rows, slot_w, step_e.reshape(-1), n_used, live.reshape(-1).astype(jnp.int32), steps


def _experts_kernel(step_e_ref, n_used_ref, live_ref, row_ref, sw_ref, x_ref, wg_ref, wu_ref, wd_ref, acc_ref,
                    *stage_refs):
    n_sub = BLOCKS_PER_STEP
    xs_refs = stage_refs[:n_sub]
    ys_refs = stage_refs[n_sub:]
    i = pl.program_id(0)
    j = pl.program_id(1)
    rows = EXPERT_ROWS
    pitch = STAGE_PITCH

    @pl.when(j == 0)
    def _():
        acc_ref[...] = jnp.zeros_like(acc_ref)

    last_row = x_ref.shape[1] - SUBLANES

    def token_rows(slot, limit=None):
        row = row_ref[0, 0, slot]
        if limit is not None:
            row = jnp.minimum(row, limit)
        return pl.ds(pl.multiple_of(row, SUBLANES), SUBLANES)

    def run(subs):
        for s in subs:
            for r in range(rows):
                xs_refs[s][pl.ds(r, ROW_CHUNKS, stride=pitch), :] = x_ref[0, token_rows(s * rows + r, last_row), :]
        eye = (lax.broadcasted_iota(jnp.int32, (rows, rows), 0)
               == lax.broadcasted_iota(jnp.int32, (rows, rows), 1))
        ys = {}
        for s in subs:
            x = jnp.concatenate([xs_refs[s][pl.ds(c * pitch, rows), :] for c in range(ROW_CHUNKS)], axis=1)
            hid = _silu(_bdot(x, wg_ref[0, 0])) * _bdot(x, wu_ref[0, 0])
            w_row = sw_ref[0, :, s * rows:(s + 1) * rows]
            w_col = jnp.sum(jnp.where(eye, w_row, 0.0), axis=1, keepdims=True)
            ys[s] = _bdot(hid, wd_ref[0, 0]) * w_col
        for s in subs:
            for c in range(ROW_CHUNKS):
                ys_refs[s][pl.ds(c * pitch, rows), :] = ys[s][:, c * LANES:(c + 1) * LANES]
        for s in subs:
            for r0 in range(0, rows, SCATTER_BATCH):
                batch = range(r0, r0 + SCATTER_BATCH)
                ats = [token_rows(s * rows + r) for r in batch]
                new = [acc_ref[0, at, :] + ys_refs[s][pl.ds(r, ROW_CHUNKS, stride=pitch), :]
                       for at, r in zip(ats, batch)]
                for at, val in zip(ats, new):
                    acc_ref[0, at, :] = val

    live = live_ref[i * pl.num_programs(1) + j]
    for n_live in range(1, n_sub + 1):
        pl.when(live == n_live)(functools.partial(run, range(n_live)))


def _routed_experts(x_slabs, idx_t, w_t, rank_t, counts, w_gate_up, w_down, layer, tile_tokens):
    d = D_MODEL
    t = x_slabs.shape[0] // ROW_CHUNKS
    n_tiles = t // tile_tokens
    rows, slot_w, step_e, n_used, live, steps = _dispatch_plan(idx_t, w_t, rank_t, counts, tile_tokens)
    rows = rows.reshape(n_tiles * steps, 1, STEP_ROWS)
    slot_w = slot_w.reshape(n_tiles * steps, 1, STEP_ROWS)
    x_slabs = x_slabs.reshape(n_tiles, tile_tokens * ROW_CHUNKS, LANES)
    slab_rows = (tile_tokens + 1) * ROW_CHUNKS

    def used_step(i, j, se, nu, lv):
        return (i * steps + jnp.minimum(j, jnp.maximum(nu[i] - 1, 0)), 0, 0)

    pick = lambda i, j, se, nu, lv: (layer, se[i * steps + j], 0, 0)
    tile = lambda i, j, se, nu, lv: (i, 0, 0)
    resident = pl.Buffered(1)
    stage = pltpu.VMEM((ROW_CHUNKS * STAGE_PITCH, LANES), F32)
    grid_spec = pltpu.PrefetchScalarGridSpec(
        num_scalar_prefetch=3,
        grid=(n_tiles, steps),
        in_specs=[
            pl.BlockSpec((1, 1, STEP_ROWS), used_step, memory_space=pltpu.SMEM),
            pl.BlockSpec((1, 1, STEP_ROWS), used_step),
            pl.BlockSpec((1, tile_tokens * ROW_CHUNKS, LANES), tile, pipeline_mode=resident),
            pl.BlockSpec((1, 1, d, D_EXPERT), pick),
            pl.BlockSpec((1, 1, d, D_EXPERT), pick),
            pl.BlockSpec((1, 1, D_EXPERT, d), pick),
        ],
        out_specs=pl.BlockSpec((1, slab_rows, LANES), tile, pipeline_mode=resident),
        scratch_shapes=[stage] * (2 * BLOCKS_PER_STEP),
    )
    return pl.pallas_call(
        _experts_kernel,
        grid_spec=grid_spec,
        out_shape=jax.ShapeDtypeStruct((n_tiles, slab_rows, LANES), F32),
        compiler_params=_cparams("parallel", "arbitrary"),
        name="moe_experts",
    )(step_e, n_used, live, rows, slot_w, x_slabs, *w_gate_up, w_down)


def _moe_out_kernel(x_ref, r_ref, wg_ref, wu_ref, wd_ref, g_ref, b_ref, o_ref):
    x = x_ref[...]
    tm = x.shape[0]
    hid = _silu(_bdot(x, wg_ref[...])) * _bdot(x, wu_ref[...])
    routed = jnp.concatenate([r_ref[0, pl.ds(c, tm, stride=ROW_CHUNKS), :] for c in range(ROW_CHUNKS)], axis=1)
    y = routed + _bdot(hid, wd_ref[...])
    o_ref[...] = _layer_norm(DN_ALPHA * x + y, g_ref[...], b_ref[...])


def _moe_out(x, routed, ws_gate, ws_up, ws_down, g, b, tile_tokens, tm=1024):
    t, d = x.shape
    tm = min(tm, tile_tokens)
    per_tile = tile_tokens // tm
    return pl.pallas_call(
        _moe_out_kernel,
        grid=(t // tm,),
        in_specs=[pl.BlockSpec((tm, d), lambda i: (i, 0)),
                  pl.BlockSpec((1, tm * ROW_CHUNKS, LANES), lambda i: (i // per_tile, i % per_tile, 0)),
                  pl.BlockSpec((d, D_SHARED), lambda i: (0, 0)),
                  pl.BlockSpec((d, D_SHARED), lambda i: (0, 0)),
                  pl.BlockSpec((D_SHARED, d), lambda i: (0, 0)),
                  pl.BlockSpec((1, d), lambda i: (0, 0)),
                  pl.BlockSpec((1, d), lambda i: (0, 0))],
        out_specs=pl.BlockSpec((tm, d), lambda i: (i, 0)),
        out_shape=jax.ShapeDtypeStruct((t, d), F32),
        compiler_params=_cparams("parallel"),
        name="moe_out",
    )(x, routed, ws_gate.astype(BF16), ws_up.astype(BF16), ws_down.astype(BF16),
      g.reshape(1, d), b.reshape(1, d))


def _moe_ln(hid, hid_slabs, w_router, router_bias, w_gate_up, w_down, layer, ws_gate, ws_up, ws_down, ln_g, ln_b):
    b, l, d = hid.shape
    x = hid.reshape(b * l, d)
    tile_tokens = min(MOE_TILE_TOKENS, b * l)
    idx_t, w_t, rank_t, counts = _router(x, w_router, router_bias, tile_tokens)
    routed = _routed_experts(hid_slabs, idx_t, w_t, rank_t, counts, w_gate_up, w_down, layer, tile_tokens)
    return _moe_out(x, routed, ws_gate, ws_up, ws_down, ln_g, ln_b, tile_tokens).reshape(b, l, d)


def kernel(x, a_w_in, a_conv_w, a_A_log, a_dt_bias, a_onorm_w, a_w_out, b_w_dq, b_qnorm_w, b_w_uq, b_w_o,
           kv_w_dkv, kv_norm_w, kv_w_ukv, ln1_g, ln1_b, ln2_g, ln2_b, moe_w_router, moe_router_bias,
           moe_w_gate, moe_w_up, moe_w_down, moe_ws_gate, moe_ws_up, moe_ws_down):
    l = x.shape[1]
    cos_t, sin_t = _rope_lane_tables(l)
    w_gate_up = (moe_w_gate.astype(BF16), moe_w_up.astype(BF16))
    w_down = moe_w_down.astype(BF16)
    h = x
    k = v = None
    for layer in range(DEPTH):
        if layer < N_A_LAYERS:
            i = layer
            h, slabs = _gated_deltanet_ln(h, a_w_in[i], a_conv_w[i], a_A_log[i], a_dt_bias[i], a_onorm_w[i],
                                          a_w_out[i], ln1_g[layer], ln1_b[layer])
        else:
            i = layer - N_A_LAYERS
            h, slabs = _mla_ln(h, b_w_dq[i], b_qnorm_w[i], b_w_uq[i], b_w_o[i], k, v, cos_t, sin_t,
                               ln1_g[layer], ln1_b[layer])
        h = _moe_ln(h, slabs, moe_w_router[layer], moe_router_bias[layer], w_gate_up, w_down, layer,
                    moe_ws_gate[layer], moe_ws_up[layer], moe_ws_down[layer], ln2_g[layer], ln2_b[layer])
        if layer == N_A_LAYERS - 1:
            k, v = _mla_kv(h, kv_w_dkv, kv_norm_w, kv_w_ukv, cos_t, sin_t)
    return h
```

```python
import functools

import jax
import jax.numpy as jnp
from jax import lax
from jax.experimental import pallas as pl
from jax.experimental.pallas import tpu as pltpu

F32 = jnp.float32
BF16 = jnp.bfloat16
HIGHEST = lax.Precision.HIGHEST

D_MODEL = 1024
DEPTH = 4
N_A_LAYERS = DEPTH // 2
GDN_HEADS = 8
GDN_DK = 128
GDN_DV = 128
D_QK = GDN_HEADS * GDN_DK
D_VA = GDN_HEADS * GDN_DV
GDN_CONV_CH = 2 * D_QK + D_VA
CONV_WIDTH = 4
CHUNK = 64
MLA_HEADS = 8
Q_LORA = 512
KV_LORA = 256
D_NOPE = 128
D_ROPE = 64
D_V = 128
ROPE_THETA = 10000.0
N_EXPERTS = 64
TOP_K = 8
N_GROUPS = 8
GROUP_SIZE = N_EXPERTS // N_GROUPS
TOPK_GROUPS = 4
D_EXPERT = 256
D_SHARED = 256
ROUTED_SCALE = 2.5
DN_ALPHA = (2 * DEPTH) ** 0.25
EPS = 1e-6
LN_EPS = 1e-5
LOG2_E = 1.4426950408889634

LANES = 128
SUBLANES = 8
VMEM_LIMIT_BYTES = 56 * 1024 * 1024

EXPERT_ROWS = 128
ROW_CHUNKS = D_MODEL // LANES
STAGE_PITCH = EXPERT_ROWS + 1
SCATTER_BATCH = 8
BLOCKS_PER_STEP = 2
STEP_ROWS = BLOCKS_PER_STEP * EXPERT_ROWS
MOE_TILE_TOKENS = 4096
ATTN_ROW_GROUPS = 4
assert TOP_K == SUBLANES


def _cparams(*sem):
    return pltpu.CompilerParams(dimension_semantics=sem, vmem_limit_bytes=VMEM_LIMIT_BYTES)


def _silu(x):
    return x * jax.nn.sigmoid(x)


def _layer_norm(x, g, b):
    mu = jnp.mean(x, -1, keepdims=True)
    xc = x - mu
    var = jnp.mean(xc * xc, -1, keepdims=True)
    return xc * lax.rsqrt(var + LN_EPS) * g + b


def _rms_norm(x, w):
    return x * lax.rsqrt(jnp.mean(x * x, -1, keepdims=True) + EPS) * w


def _bdot(a, b):
    return jnp.dot(a.astype(BF16), b.astype(BF16), preferred_element_type=F32)


def _bdot_nt(a, b):
    return lax.dot_general(a.astype(BF16), b.astype(BF16), (((1,), (1,)), ((), ())),
                           preferred_element_type=F32)


def _hdot(a, b):
    return jnp.dot(a, b, precision=HIGHEST, preferred_element_type=F32)


def _matmul_kernel(x_ref, w_ref, o_ref):
    o_ref[...] = _bdot(x_ref[...], w_ref[...]).astype(o_ref.dtype)


def _matmul(x, w, out_dtype, tm, tn):
    m, k = x.shape
    n = w.shape[1]
    tm = min(tm, m)
    tn = min(tn, n)
    return pl.pallas_call(
        _matmul_kernel,
        grid=(m // tm, n // tn),
        in_specs=[pl.BlockSpec((tm, k), lambda i, j: (i, 0)),
                  pl.BlockSpec((k, tn), lambda i, j: (0, j))],
        out_specs=pl.BlockSpec((tm, tn), lambda i, j: (i, j)),
        out_shape=jax.ShapeDtypeStruct((m, n), out_dtype),
        compiler_params=_cparams("parallel", "parallel"),
        name="matmul",
    )(x, w)


def _matmul_ln_kernel(x_ref, w_ref, r_ref, g_ref, b_ref, o_ref, slab_ref):
    mix = _bdot(x_ref[...], w_ref[...])
    out = _layer_norm(DN_ALPHA * r_ref[...] + mix, g_ref[...], b_ref[...])
    o_ref[...] = out
    tm = out.shape[0]
    for c in range(ROW_CHUNKS):
        slab_ref[pl.ds(c, tm, stride=ROW_CHUNKS), :] = out[:, c * LANES:(c + 1) * LANES]


def _matmul_ln(x, w, res, g, b, tm=1024):
    m, k = x.shape
    n = w.shape[1]
    tm = min(tm, m)
    chunks = n // LANES
    return pl.pallas_call(
        _matmul_ln_kernel,
        grid=(m // tm,),
        in_specs=[pl.BlockSpec((tm, k), lambda i: (i, 0)),
                  pl.BlockSpec((k, n), lambda i: (0, 0)),
                  pl.BlockSpec((tm, n), lambda i: (i, 0)),
                  pl.BlockSpec((1, n), lambda i: (0, 0)),
                  pl.BlockSpec((1, n), lambda i: (0, 0))],
        out_specs=[pl.BlockSpec((tm, n), lambda i: (i, 0)),
                   pl.BlockSpec((tm * chunks, LANES), lambda i: (i, 0))],
        out_shape=[jax.ShapeDtypeStruct((m, n), F32),
                   jax.ShapeDtypeStruct((m * chunks, LANES), F32)],
        compiler_params=_cparams("parallel"),
        name="matmul_ln",
    )(x, w, res, g.reshape(1, n), b.reshape(1, n))


def _softplus(x):
    return jnp.maximum(x, 0.0) + jnp.log1p(jnp.exp(-jnp.abs(x)))


def _gdn_gates_kernel(x_ref, wab_ref, wabt_ref, alog_ref, dtb_ref, alogt_ref, dtbt_ref,
                      gc_ref, beta_ref, gct_ref):
    x = x_ref[...]
    tl = x.shape[0]
    h = GDN_HEADS
    ab = _bdot(x, wab_ref[...])
    abt = _bdot_nt(wabt_ref[...], x)
    g = -jnp.exp(alog_ref[...]) * _softplus(ab[:, :h] + dtb_ref[...])
    gt = -jnp.exp(alogt_ref[...]) * _softplus(abt[:h, :] + dtbt_ref[...])
    beta_ref[...] = jax.nn.sigmoid(ab[:, h:2 * h])
    row = lax.broadcasted_iota(jnp.int32, (tl, tl), 0)
    col = lax.broadcasted_iota(jnp.int32, (tl, tl), 1)
    shift = CHUNK.bit_length() - 1
    same = (row >> shift) == (col >> shift)
    lower = jnp.where(same & (col <= row), 1.0, 0.0).astype(F32)
    upper = jnp.where(same & (row <= col), 1.0, 0.0).astype(F32)
    gc_ref[...] = _hdot(lower, g)
    gct_ref[...] = _hdot(gt, upper)


def _gdn_gates(x, w_ab, a_log, dt_bias, tl=256):
    t = x.shape[0]
    tl = min(tl, t)
    h = GDN_HEADS
    wab = jnp.zeros((D_MODEL, LANES), F32).at[:, :2 * h].set(w_ab)
    wabt = w_ab.T
    return pl.pallas_call(
        _gdn_gates_kernel,
        grid=(t // tl,),
        in_specs=[pl.BlockSpec((tl, D_MODEL), lambda i: (i, 0)),
                  pl.BlockSpec((D_MODEL, LANES), lambda i: (0, 0)),
                  pl.BlockSpec((2 * h, D_MODEL), lambda i: (0, 0)),
                  pl.BlockSpec((1, h), lambda i: (0, 0)),
                  pl.BlockSpec((1, h), lambda i: (0, 0)),
                  pl.BlockSpec((h, 1), lambda i: (0, 0)),
                  pl.BlockSpec((h, 1), lambda i: (0, 0))],
        out_specs=[pl.BlockSpec((tl, h), lambda i: (i, 0)),
                   pl.BlockSpec((tl, h), lambda i: (i, 0)),
                   pl.BlockSpec((h, tl), lambda i: (0, i))],
        out_shape=[jax.ShapeDtypeStruct((t, h), F32),
                   jax.ShapeDtypeStruct((t, h), F32),
                   jax.ShapeDtypeStruct((h, t), F32)],
        compiler_params=_cparams("parallel"),
        name="gdn_gates",
    )(x, wab, wabt, a_log.reshape(1, h), dt_bias.reshape(1, h),
      a_log.reshape(h, 1), dt_bias.reshape(h, 1))


def _gdn_conv_kernel(cur_ref, prev_ref, w_ref, q_ref, k_ref, v_ref):
    i = pl.program_id(1)
    x = cur_ref[0]
    tl = x.shape[0]
    prev = jnp.where(i > 0, prev_ref[0], 0.0)
    xx = jnp.concatenate([prev, x], axis=0)
    w = w_ref[...]
    y = None
    for j in range(CONV_WIDTH):
        off = SUBLANES - (CONV_WIDTH - 1) + j
        term = xx[off:off + tl, :] * w[j:j + 1, :]
        y = term if y is None else y + term
    y = _silu(y)
    for h in range(GDN_HEADS):
        sl = slice(h * GDN_DK, (h + 1) * GDN_DK)
        qh = y[:, sl]
        q_ref[0, :, sl] = qh * lax.rsqrt(jnp.sum(qh * qh, -1, keepdims=True) + EPS) * (GDN_DK ** -0.5)
        kh = y[:, D_QK + h * GDN_DK:D_QK + (h + 1) * GDN_DK]
        k_ref[0, :, sl] = kh * lax.rsqrt(jnp.sum(kh * kh, -1, keepdims=True) + EPS)
    v_ref[0] = y[:, 2 * D_QK:]


def _gdn_conv(proj, conv_w, tl=512):
    b, l, _ = proj.shape
    tl = min(tl, l)
    c3 = GDN_CONV_CH
    per = tl // SUBLANES
    out = jax.ShapeDtypeStruct((b, l, D_QK), F32)
    return pl.pallas_call(
        _gdn_conv_kernel,
        grid=(b, l // tl),
        in_specs=[pl.BlockSpec((1, tl, c3), lambda bi, i: (bi, i, 0)),
                  pl.BlockSpec((1, SUBLANES, c3), lambda bi, i: (bi, jnp.maximum(i * per - 1, 0), 0)),
                  pl.BlockSpec((CONV_WIDTH, c3), lambda bi, i: (0, 0))],
        out_specs=[pl.BlockSpec((1, tl, D_QK), lambda bi, i: (bi, i, 0))] * 3,
        out_shape=[out, out, out],
        compiler_params=_cparams("parallel", "parallel"),
        name="gdn_conv",
    )(proj, proj, conv_w)


def _split_bf16(x):
    hi = x.astype(BF16)
    lo = (x - hi.astype(F32)).astype(BF16)
    return hi, lo


def _dot_split(a, b):
    (ah, al), (bh, bl) = a, b
    dot = functools.partial(jnp.dot, preferred_element_type=F32)
    return dot(ah, bh) + (dot(ah, bl) + dot(al, bh))


def _gdn_prep_kernel(q_ref, k_ref, v_ref, gc_ref, beta_ref, gct_ref, wq_ref, u_ref, attn_ref, kgt_ref,
                     *, chunks):
    c = CHUNK
    row = lax.broadcasted_iota(jnp.int32, (c, c), 0)
    col = lax.broadcasted_iota(jnp.int32, (c, c), 1)
    incl = row >= col
    strict = row > col
    eye = jnp.where(row == col, 1.0, 0.0).astype(F32)
    items = [(ci, h) for ci in range(chunks) for h in range(GDN_HEADS)]

    st = {}
    for ci, h in items:
        rows = slice(ci * c, (ci + 1) * c)
        sl = slice(h * GDN_DK, (h + 1) * GDN_DK)
        q = q_ref[0, rows, sl]
        k = k_ref[0, rows, sl]
        v = v_ref[0, rows, sl]
        gcol = gc_ref[0, rows, h:h + 1]
        grow = gct_ref[0, ci, h:h + 1, :]
        bcol = beta_ref[0, rows, h:h + 1]
        decay = jnp.where(incl, jnp.exp(jnp.where(incl, gcol - grow, 0.0)), 0.0)
        kb = k * bcol
        eg = jnp.exp(gcol)
        g_last = gcol[c - 1:c, :]
        sc = _bdot_nt(jnp.concatenate([kb, q], axis=0), k)
        a = jnp.where(strict, sc[:c] * decay, 0.0)
        attn_ref[0, ci, h] = (sc[c:] * decay).astype(attn_ref.dtype)
        kgt_ref[0, ci, h] = (k * jnp.exp(g_last - gcol)).T.astype(kgt_ref.dtype)
        wq_ref[0, ci, h, c:, :] = (q * eg).astype(wq_ref.dtype)
        st[ci, h] = dict(m=-a, rhs=jnp.concatenate([v * bcol, kb * eg], axis=1))

    for it in items:
        st[it]["p"] = eye + st[it]["m"]
    span = 2
    while span < c:
        for it in items:
            ms = _split_bf16(st[it]["m"])
            st[it]["m"] = _dot_split(ms, ms)
        for it in items:
            st[it]["p"] = st[it]["p"] + _dot_split(_split_bf16(st[it]["p"]), _split_bf16(st[it]["m"]))
        span *= 2

    for ci, h in items:
        sol = _dot_split(_split_bf16(st[ci, h]["p"]), _split_bf16(st[ci, h]["rhs"]))
        u_ref[0, ci * c:(ci + 1) * c, h * GDN_DV:(h + 1) * GDN_DV] = sol[:, :GDN_DV]
        wq_ref[0, ci, h, :c, :] = sol[:, GDN_DV:].astype(wq_ref.dtype)


def _gdn_prep(q, k, v, gc, beta, gct, chunks=2):
    b, l, _ = q.shape
    n = l // CHUNK
    chunks = min(chunks, n)
    h = GDN_HEADS
    rows = chunks * CHUNK
    blk = pl.BlockSpec((1, rows, D_QK), lambda bi, i: (bi, i, 0))
    gate = pl.BlockSpec((1, rows, h), lambda bi, i: (bi, i, 0))
    return pl.pallas_call(
        functools.partial(_gdn_prep_kernel, chunks=chunks),
        grid=(b, n // chunks),
        in_specs=[blk, blk, blk, gate, gate,
                  pl.BlockSpec((1, chunks, h, CHUNK), lambda bi, i: (bi, i, 0, 0))],
        out_specs=[pl.BlockSpec((1, chunks, h, 2 * CHUNK, GDN_DK), lambda bi, i: (bi, i, 0, 0, 0)),
                   pl.BlockSpec((1, rows, D_VA), lambda bi, i: (bi, i, 0)),
                   pl.BlockSpec((1, chunks, h, CHUNK, CHUNK), lambda bi, i: (bi, i, 0, 0, 0)),
                   pl.BlockSpec((1, chunks, h, GDN_DK, CHUNK), lambda bi, i: (bi, i, 0, 0, 0))],
        out_shape=[jax.ShapeDtypeStruct((b, n, h, 2 * CHUNK, GDN_DK), BF16),
                   jax.ShapeDtypeStruct((b, l, D_VA), F32),
                   jax.ShapeDtypeStruct((b, n, h, CHUNK, CHUNK), BF16),
                   jax.ShapeDtypeStruct((b, n, h, GDN_DK, CHUNK), BF16)],
        compiler_params=_cparams("parallel", "parallel"),
        name="gdn_prep",
    )(q, k, v, gc, beta, gct)


def _gdn_state_kernel(wq_ref, u_ref, attn_ref, kgt_ref, gc_ref, z_ref, onw_ref, o_ref, s_ref, *, chunks):
    @pl.when(pl.program_id(1) == 0)
    def _():
        s_ref[...] = jnp.zeros_like(s_ref)

    c = CHUNK
    onw = onw_ref[...]
    dot = functools.partial(jnp.dot, preferred_element_type=F32)
    heads = range(GDN_HEADS)
    for ci in range(chunks):
        rows = slice(ci * c, (ci + 1) * c)
        s_old = [s_ref[h] for h in heads]
        r = [dot(wq_ref[0, ci, h], s_old[h].astype(BF16)) for h in heads]
        v_new = [(u_ref[0, rows, h * GDN_DV:(h + 1) * GDN_DV] - r[h][:c]).astype(BF16) for h in heads]
        for h in heads:
            decay_last = jnp.exp(gc_ref[0, (ci + 1) * c - 1:(ci + 1) * c, h:h + 1])
            s_ref[h] = s_old[h] * decay_last + dot(kgt_ref[0, ci, h], v_new[h])
        for h in heads:
            sl = slice(h * GDN_DV, (h + 1) * GDN_DV)
            o = r[h][c:] + dot(attn_ref[0, ci, h], v_new[h])
            o_ref[0, rows, sl] = (_rms_norm(o, onw) * _silu(z_ref[0, rows, sl])).astype(o_ref.dtype)


def _gdn_state(wq, u, attn, kgt, gc, proj, onorm_w, chunks=8):
    b, n, h = wq.shape[:3]
    l = n * CHUNK
    chunks = min(chunks, n)
    rows = chunks * CHUNK
    z_block = GDN_CONV_CH // D_VA
    return pl.pallas_call(
        functools.partial(_gdn_state_kernel, chunks=chunks),
        grid=(b, n // chunks),
        in_specs=[pl.BlockSpec((1, chunks, h, 2 * CHUNK, GDN_DK), lambda bi, i: (bi, i, 0, 0, 0)),
                  pl.BlockSpec((1, rows, D_VA), lambda bi, i: (bi, i, 0)),
                  pl.BlockSpec((1, chunks, h, CHUNK, CHUNK), lambda bi, i: (bi, i, 0, 0, 0)),
                  pl.BlockSpec((1, chunks, h, GDN_DK, CHUNK), lambda bi, i: (bi, i, 0, 0, 0)),
                  pl.BlockSpec((1, rows, h), lambda bi, i: (bi, i, 0)),
                  pl.BlockSpec((1, rows, D_VA), lambda bi, i: (bi, i, z_block)),
                  pl.BlockSpec((1, GDN_DV), lambda bi, i: (0, 0))],
        out_specs=pl.BlockSpec((1, rows, D_VA), lambda bi, i: (bi, i, 0)),
        out_shape=jax.ShapeDtypeStruct((b, l, D_VA), BF16),
        scratch_shapes=[pltpu.VMEM((h, GDN_DK, GDN_DV), F32)],
        compiler_params=_cparams("parallel", "arbitrary"),
        name="gdn_state",
    )(wq, u, attn, kgt, gc, proj, onorm_w.reshape(1, GDN_DV))


def _gated_deltanet_ln(hid, w_in, conv_w, a_log, dt_bias, onorm_w, w_out, ln_g, ln_b):
    b, l, d = hid.shape
    h = GDN_HEADS
    n = l // CHUNK
    x = hid.reshape(b * l, d)
    n_main = GDN_CONV_CH + D_VA
    proj = _matmul(x, w_in[:, :n_main].astype(BF16), F32, 1024, 2048).reshape(b, l, n_main)
    gc, beta, gct = _gdn_gates(x, w_in[:, n_main:], a_log, dt_bias)
    gc = gc.reshape(b, l, h)
    gct = gct.reshape(h, b, n, CHUNK).transpose(1, 2, 0, 3)
    q, k, v = _gdn_conv(proj, conv_w)
    wq, u, attn, kgt = _gdn_prep(q, k, v, gc, beta.reshape(b, l, h), gct)
    o = _gdn_state(wq, u, attn, kgt, gc, proj, onorm_w)
    out, slabs = _matmul_ln(o.reshape(b * l, D_VA), w_out.astype(BF16), x, ln_g, ln_b)
    return out.reshape(b, l, d), slabs


def _rope_lane_tables(l):
    inv = 1.0 / (ROPE_THETA ** (jnp.arange(0, D_ROPE, 2, dtype=F32) / D_ROPE))
    ang = jnp.arange(l, dtype=F32)[:, None] * inv[None, :]
    cos, sin = jnp.cos(ang), jnp.sin(ang)
    zero = jnp.zeros((l, LANES - D_ROPE), F32)
    return (jnp.concatenate([cos, cos, zero], -1),
            jnp.concatenate([-sin, sin, zero], -1))


def _rope_weight_groups(w_rope):
    kdim = w_rope.shape[0]
    half = D_ROPE // 2
    zero = jnp.zeros((kdim, LANES - D_ROPE), w_rope.dtype)
    x1, x2 = w_rope[:, :half], w_rope[:, half:]
    return jnp.concatenate([x1, x2, zero, x2, x1, zero], axis=-1)


def _mla_kv_kernel(x_ref, wd_ref, nw_ref, wu_ref, cos_ref, sin_ref, k_ref, v_ref):
    x = x_ref[0]
    ckv = _bdot(x, wd_ref[...])
    c = _rms_norm(ckv[:, :KV_LORA], nw_ref[...])
    k_rope_t = (ckv[:, KV_LORA:KV_LORA + LANES] * cos_ref[...]
                + ckv[:, KV_LORA + LANES:] * sin_ref[...]).T.astype(k_ref.dtype)
    kv = _bdot(c, wu_ref[...])
    per = D_NOPE + D_V
    ones_col = jnp.where(lax.broadcasted_iota(jnp.int32, (x.shape[0], LANES), 1) == 0, 1.0, 0.0).astype(v_ref.dtype)
    for h in range(MLA_HEADS):
        k_ref[0, h, :D_NOPE, :] = kv[:, h * per:h * per + D_NOPE].T.astype(k_ref.dtype)
        k_ref[0, h, D_NOPE:, :] = k_rope_t
        v_ref[0, h, :, :D_V] = kv[:, h * per + D_NOPE:(h + 1) * per].astype(v_ref.dtype)
        v_ref[0, h, :, D_V:] = ones_col


def _mla_kv(hid, w_dkv, kv_norm_w, w_ukv, cos_t, sin_t, tl=512):
    b, l, d = hid.shape
    tl = min(tl, l)
    wd = jnp.concatenate([w_dkv[:, :KV_LORA], _rope_weight_groups(w_dkv[:, KV_LORA:])], -1).astype(BF16)
    nd = wd.shape[1]
    hh = MLA_HEADS
    return pl.pallas_call(
        _mla_kv_kernel,
        grid=(b, l // tl),
        in_specs=[pl.BlockSpec((1, tl, d), lambda bi, i: (bi, i, 0)),
                  pl.BlockSpec((d, nd), lambda bi, i: (0, 0)),
                  pl.BlockSpec((1, KV_LORA), lambda bi, i: (0, 0)),
                  pl.BlockSpec((KV_LORA, hh * (D_NOPE + D_V)), lambda bi, i: (0, 0)),
                  pl.BlockSpec((tl, LANES), lambda bi, i: (i, 0)),
                  pl.BlockSpec((tl, LANES), lambda bi, i: (i, 0))],
        out_specs=[pl.BlockSpec((1, hh, D_NOPE + LANES, tl), lambda bi, i: (bi, 0, 0, i)),
                   pl.BlockSpec((1, hh, tl, D_V + LANES), lambda bi, i: (bi, 0, i, 0))],
        out_shape=[jax.ShapeDtypeStruct((b, hh, D_NOPE + LANES, l), BF16),
                   jax.ShapeDtypeStruct((b, hh, l, D_V + LANES), BF16)],
        compiler_params=_cparams("parallel", "parallel"),
        name="mla_kv",
    )(hid, wd, kv_norm_w.reshape(1, KV_LORA), w_ukv.astype(BF16), cos_t, sin_t)


def _mla_cq_kernel(x_ref, w_ref, nw_ref, o_ref):
    o_ref[...] = _rms_norm(_bdot(x_ref[...], w_ref[...]), nw_ref[...]).astype(o_ref.dtype)


def _mla_cq(x, w_dq, qnorm_w, tm=1024):
    m, k = x.shape
    tm = min(tm, m)
    return pl.pallas_call(
        _mla_cq_kernel,
        grid=(m // tm,),
        in_specs=[pl.BlockSpec((tm, k), lambda i: (i, 0)),
                  pl.BlockSpec((k, Q_LORA), lambda i: (0, 0)),
                  pl.BlockSpec((1, Q_LORA), lambda i: (0, 0))],
        out_specs=pl.BlockSpec((tm, Q_LORA), lambda i: (i, 0)),
        out_shape=jax.ShapeDtypeStruct((m, Q_LORA), BF16),
        compiler_params=_cparams("parallel"),
        name="mla_cq",
    )(x, w_dq.astype(BF16), qnorm_w.reshape(1, Q_LORA))


def _mla_q_kernel(c_ref, w_ref, cos_ref, sin_ref, q_ref):
    c = c_ref[0]
    scale = (D_NOPE + D_ROPE) ** -0.5 * LOG2_E
    per = D_NOPE + 2 * LANES
    for h in range(MLA_HEADS):
        qh = _bdot(c, w_ref[:, h * per:(h + 1) * per])
        rope = qh[:, D_NOPE:D_NOPE + LANES] * cos_ref[...] + qh[:, D_NOPE + LANES:] * sin_ref[...]
        q_ref[0, h, :, :D_NOPE] = (qh[:, :D_NOPE] * scale).astype(q_ref.dtype)
        q_ref[0, h, :, D_NOPE:] = (rope * scale).astype(q_ref.dtype)


def _mla_q(cq, w_uq, cos_t, sin_t, tl=1024):
    b, l, _ = cq.shape
    tl = min(tl, l)
    hh = MLA_HEADS
    per_in = D_NOPE + D_ROPE
    groups = []
    for h in range(hh):
        wh = w_uq[:, h * per_in:(h + 1) * per_in]
        groups += [wh[:, :D_NOPE], _rope_weight_groups(wh[:, D_NOPE:])]
    w = jnp.concatenate(groups, -1).astype(BF16)
    return pl.pallas_call(
        _mla_q_kernel,
        grid=(b, l // tl),
        in_specs=[pl.BlockSpec((1, tl, Q_LORA), lambda bi, i: (bi, i, 0)),
                  pl.BlockSpec(w.shape, lambda bi, i: (0, 0)),
                  pl.BlockSpec((tl, LANES), lambda bi, i: (i, 0)),
                  pl.BlockSpec((tl, LANES), lambda bi, i: (i, 0))],
        out_specs=pl.BlockSpec((1, hh, tl, D_NOPE + LANES), lambda bi, i: (bi, 0, i, 0)),
        out_shape=jax.ShapeDtypeStruct((b, hh, l, D_NOPE + LANES), BF16),
        compiler_params=_cparams("parallel", "parallel"),
        name="mla_q",
    )(cq, w, cos_t, sin_t)


def _mla_attn_kernel(q_ref, k_ref, v_ref, o_ref, *, tk):
    qi = pl.program_id(2)
    q = q_ref[0, 0]
    tq = q.shape[0]
    per = tq // tk

    groups = ATTN_ROW_GROUPS
    rows = tq // groups
    qs = [q[g * rows:(g + 1) * rows] for g in range(groups)]

    def block(j, carry, mask_offset=None):
        at = pl.ds(pl.multiple_of(j * tk, tk), tk)
        kb = k_ref[0, 0, :, at]
        vb = v_ref[0, 0, at, :]
        if mask_offset is None:
            widths = [tk] * groups
            ss = [_bdot(qg, kb) for qg in qs]
        else:
            widths = [min(max((g + 1) * rows - mask_offset, LANES), tk) for g in range(groups)]
            ss = []
            for g, (qg, wd) in enumerate(zip(qs, widths)):
                qpos = lax.broadcasted_iota(jnp.int32, (rows, wd), 0)
                kpos = lax.broadcasted_iota(jnp.int32, (rows, wd), 1)
                ss.append(jnp.where(kpos + mask_offset <= qpos + g * rows, _bdot(qg, kb[:, :wd]), -jnp.inf))
        m_new = [jnp.maximum(c[0], jnp.max(s, -1, keepdims=True)) for c, s in zip(carry, ss)]
        ps = [jnp.exp2(s - m) for s, m in zip(ss, m_new)]
        return tuple((mn, jnp.exp2(m - mn) * acc + _bdot(p, vb[:wd]))
                     for (m, acc), mn, p, wd in zip(carry, m_new, ps, widths))

    first = qi * per
    init = (jnp.full((rows, 1), -jnp.inf, F32), jnp.zeros((rows, D_V + LANES), F32))
    carry = lax.fori_loop(0, first, block, (init,) * groups)
    for d in range(per):
        carry = block(first + d, carry, mask_offset=d * tk)
    for g, (_, acc) in enumerate(carry):
        o_ref[0, g * rows:(g + 1) * rows, :] = (acc[:, :D_V] / acc[:, D_V:D_V + 1]).astype(o_ref.dtype)


def _mla_attn(q, k, v, tq=1024, tk=1024):
    b, hh, l, dq = q.shape
    tq = min(tq, l)
    tk = min(tk, tq)
    return pl.pallas_call(
        functools.partial(_mla_attn_kernel, tk=tk),
        grid=(b, hh, l // tq),
        in_specs=[pl.BlockSpec((1, 1, tq, dq), lambda bi, h, i: (bi, h, i, 0)),
                  pl.BlockSpec((1, 1, dq, l), lambda bi, h, i: (bi, h, 0, 0)),
                  pl.BlockSpec((1, 1, l, D_V + LANES), lambda bi, h, i: (bi, h, 0, 0))],
        out_specs=pl.BlockSpec((1, tq, D_V), lambda bi, h, i: (bi, i, h)),
        out_shape=jax.ShapeDtypeStruct((b, l, hh * D_V), BF16),
        compiler_params=_cparams("parallel", "parallel", "parallel"),
        name="mla_attn",
    )(q, k, v)


def _mla_ln(hid, w_dq, qnorm_w, w_uq, w_o, k, v, cos_t, sin_t, ln_g, ln_b):
    b, l, d = hid.shape
    x = hid.reshape(b * l, d)
    cq = _mla_cq(x, w_dq, qnorm_w).reshape(b, l, Q_LORA)
    q = _mla_q(cq, w_uq, cos_t, sin_t)
    o = _mla_attn(q, k, v)
    out, slabs = _matmul_ln(o.reshape(b * l, MLA_HEADS * D_V), w_o.astype(BF16), x, ln_g, ln_b)
    return out.reshape(b, l, d), slabs


def _first_argmax(x, ids, n):
    m = jnp.max(x, axis=0, keepdims=True)
    first = jnp.min(jnp.where(x == m, ids, n), axis=0, keepdims=True)
    return m, first


def _router_kernel(x_ref, wt_ref, bias_ref, before_ref, idx_ref, w_ref, rank_ref, cnt_out_ref, cnt_ref,
                   *, steps_per_tile):
    @pl.when(pl.program_id(0) % steps_per_tile == 0)
    def _():
        cnt_ref[...] = jnp.zeros_like(cnt_ref)

    x = x_ref[...]
    t = x.shape[0]
    logits = lax.dot_general(wt_ref[...], x, (((1,), (1,)), ((), ())),
                             precision=HIGHEST, preferred_element_type=F32)
    scores = jax.nn.sigmoid(logits)
    biased = scores + bias_ref[...]
    neg = -jnp.inf
    sub = lax.broadcasted_iota(jnp.int32, (GROUP_SIZE, t), 0).astype(F32)
    gscores = []
    for g in range(N_GROUPS):
        xg = biased[g * GROUP_SIZE:(g + 1) * GROUP_SIZE, :]
        m1, i1 = _first_argmax(xg, sub, float(GROUP_SIZE))
        m2 = jnp.max(jnp.where(sub == i1, neg, xg), axis=0, keepdims=True)
        gscores.append(m1 + m2)
    gs = jnp.concatenate(gscores, axis=0)
    gid = lax.broadcasted_iota(jnp.int32, (N_GROUPS, t), 0).astype(F32)
    gsel = jnp.zeros((N_GROUPS, t), F32)
    for _ in range(TOPK_GROUPS):
        _, gi = _first_argmax(gs, gid, float(N_GROUPS))
        hit = gid == gi
        gsel = jnp.where(hit, 1.0, gsel)
        gs = jnp.where(hit, neg, gs)
    eid = lax.broadcasted_iota(jnp.int32, (N_EXPERTS, t), 0).astype(F32)
    allowed = jnp.concatenate(
        [jnp.broadcast_to(gsel[g:g + 1, :], (GROUP_SIZE, t)) for g in range(N_GROUPS)], axis=0)
    cand = jnp.where(allowed > 0.0, biased, neg)
    idxs, ws, hits = [], [], []
    for _ in range(TOP_K):
        _, ei = _first_argmax(cand, eid, float(N_EXPERTS))
        hit = eid == ei
        idxs.append(ei)
        hits.append(hit)
        ws.append(jnp.sum(jnp.where(hit, scores, 0.0), axis=0, keepdims=True))
        cand = jnp.where(hit, neg, cand)
    w = jnp.concatenate(ws, axis=0)
    w = w / (jnp.sum(w, axis=0, keepdims=True) + 1e-20) * ROUTED_SCALE
    idx_ref[...] = jnp.concatenate(idxs, axis=0).astype(jnp.int32)
    w_ref[...] = w
    chosen = jnp.zeros((N_EXPERTS, t), F32)
    for hit in hits:
        chosen = jnp.where(hit, 1.0, chosen)
    prior = jnp.dot(chosen.astype(BF16), before_ref[...],
                    preferred_element_type=F32) + cnt_ref[...]
    rank_ref[...] = jnp.concatenate(
        [jnp.sum(jnp.where(hit, prior, 0.0), axis=0, keepdims=True) for hit in hits], axis=0).astype(jnp.int32)
    cnt_ref[...] += jnp.sum(chosen, axis=1, keepdims=True)
    cnt_out_ref[0] = cnt_ref[...]


def _router(x, w_router, bias, tile_tokens, tr=1024):
    t, d = x.shape
    tr = min(tr, tile_tokens)
    steps_per_tile = tile_tokens // tr
    n_tiles = t // tile_tokens
    kt = pl.BlockSpec((TOP_K, tr), lambda i: (0, i))
    before = jnp.triu(jnp.ones((tr, tr), BF16), k=1)
    idx, w, rank, counts = pl.pallas_call(
        functools.partial(_router_kernel, steps_per_tile=steps_per_tile),
        grid=(t // tr,),
        in_specs=[pl.BlockSpec((tr, d), lambda i: (i, 0)),
                  pl.BlockSpec((N_EXPERTS, d), lambda i: (0, 0)),
                  pl.BlockSpec((N_EXPERTS, 1), lambda i: (0, 0)),
                  pl.BlockSpec((tr, tr), lambda i: (0, 0))],
        out_specs=[kt, kt, kt,
                   pl.BlockSpec((1, N_EXPERTS, 1), lambda i: (i // steps_per_tile, 0, 0))],
        out_shape=[jax.ShapeDtypeStruct((TOP_K, t), jnp.int32),
                   jax.ShapeDtypeStruct((TOP_K, t), F32),
                   jax.ShapeDtypeStruct((TOP_K, t), jnp.int32),
                   jax.ShapeDtypeStruct((n_tiles, N_EXPERTS, 1), F32)],
        scratch_shapes=[pltpu.VMEM((N_EXPERTS, 1), F32)],
        compiler_params=_cparams("arbitrary"),
        name="moe_router",
    )(x, w_router.T, bias.reshape(N_EXPERTS, 1), before)
    return idx, w, rank, counts.reshape(n_tiles, N_EXPERTS).astype(jnp.int32)


def _steps_per_tile(tile_tokens):
    return tile_tokens * TOP_K // STEP_ROWS + N_EXPERTS


def _plan_kernel(fill_lo_ref, fill_hi_ref, pos_ref, src_ref, *, tile_tokens):
    i = pl.program_id(0)
    n_assign = tile_tokens * TOP_K
    empty = n_assign
    group = 16

    n_ranges = N_EXPERTS + 1

    def fill_range(g, carry):
        lo = fill_lo_ref[i * n_ranges + g] // group * group
        hi = fill_hi_ref[i * n_ranges + g]

        def fill(b, carry):
            for d in range(group):
                src_ref[0, 0, lo + b * group + d] = empty
            return carry
        return lax.fori_loop(0, (hi - lo) // group, fill, carry)
    lax.fori_loop(0, n_ranges, fill_range, 0)

    def place(b, carry):
        base = b * group
        slots = [pos_ref[0, 0, base + d] for d in range(group)]
        for d in range(group):
            src_ref[0, 0, slots[d]] = base + d
        return carry
    lax.fori_loop(0, n_assign // group, place, 0)


def _dispatch_plan(idx_t, w_t, rank_t, counts, tile_tokens):
    t = idx_t.shape[1]
    n_tiles = t // tile_tokens
    steps = _steps_per_tile(tile_tokens)
    padded = (counts + STEP_ROWS - 1) // STEP_ROWS * STEP_ROWS
    pad_end = jnp.cumsum(padded, axis=1)
    pad_start = (pad_end - padded).astype(jnp.int32)
    n_used = (pad_end[:, -1] // STEP_ROWS).astype(jnp.int32)
    starts = jnp.arange(steps, dtype=jnp.int32) * STEP_ROWS
    step_e = jnp.sum(starts[None, :, None] >= pad_end[:, None, :], axis=-1).astype(jnp.int32)
    step_e = jnp.minimum(step_e, N_EXPERTS - 1)
    last = jnp.take_along_axis(step_e, jnp.maximum(n_used - 1, 0)[:, None], axis=1)
    step_e = jnp.where(starts[None, :] // STEP_ROWS < n_used[:, None], step_e, last)

    start_of = jnp.repeat(pad_start.T, tile_tokens, axis=1)
    experts = jnp.arange(N_EXPERTS, dtype=jnp.int32)[None, :, None]
    pos_t = rank_t + jnp.sum(jnp.where(idx_t[:, None, :] == experts, start_of[None], 0), axis=1)
    group_end = (pad_start + counts).astype(jnp.int32)
    n_slots = steps * STEP_ROWS
    fill_lo = jnp.concatenate([group_end, pad_end[:, -1:]], axis=1).astype(jnp.int32)
    fill_hi = jnp.concatenate([pad_end, jnp.full((n_tiles, 1), n_slots)], axis=1).astype(jnp.int32)

    n_assign = tile_tokens * TOP_K
    single = pl.Buffered(1)
    src = pl.pallas_call(
        functools.partial(_plan_kernel, tile_tokens=tile_tokens),
        grid_spec=pltpu.PrefetchScalarGridSpec(
            num_scalar_prefetch=2,
            grid=(n_tiles,),
            in_specs=[pl.BlockSpec((1, 1, n_assign), lambda i, lo, hi: (i, 0, 0),
                                   memory_space=pltpu.SMEM, pipeline_mode=single)],
            out_specs=pl.BlockSpec((1, 1, n_slots), lambda i, lo, hi: (i, 0, 0),
                                   memory_space=pltpu.SMEM, pipeline_mode=single),
        ),
        out_shape=jax.ShapeDtypeStruct((n_tiles, 1, n_slots), jnp.int32),
        compiler_params=_cparams("parallel"),
        name="moe_plan",
    )(fill_lo.reshape(-1), fill_hi.reshape(-1), pos_t.T.reshape(n_tiles, 1, n_assign))
    rows = src & -SUBLANES
    w_flat = jnp.pad(w_t.T.reshape(n_tiles, tile_tokens * TOP_K), ((0, 0), (0, SUBLANES)))
    slot_w = jnp.take_along_axis(w_flat, src[:, 0, :], axis=1)
    held = jnp.clip(jnp.take_along_axis(group_end, step_e, axis=1) - starts[None, :], 0, STEP_ROWS)
    live = jnp.where(starts[None, :] // STEP_ROWS < n_used[:, None], (held + EXPERT_ROWS - 1) // EXPERT_ROWS, 0)
    return rows, slot_w, step_e.reshape(-1), n_used, live.reshape(-1).astype(jnp.int32), steps


def _experts_kernel(step_e_ref, n_used_ref, live_ref, row_ref, sw_ref, x_ref, wg_ref, wu_ref, wd_ref, acc_ref,
                    *stage_refs):
    n_sub = BLOCKS_PER_STEP
    xs_refs = stage_refs[:n_sub]
    ys_refs = stage_refs[n_sub:]
    i = pl.program_id(0)
    j = pl.program_id(1)
    rows = EXPERT_ROWS
    pitch = STAGE_PITCH

    @pl.when(j == 0)
    def _():
        acc_ref[...] = jnp.zeros_like(acc_ref)

    last_row = x_ref.shape[1] - SUBLANES

    def token_rows(slot, limit=None):
        row = row_ref[0, 0, slot]
        if limit is not None:
            row = jnp.minimum(row, limit)
        return pl.ds(pl.multiple_of(row, SUBLANES), SUBLANES)

    def run(subs):
        for s in subs:
            for r in range(rows):
                xs_refs[s][pl.ds(r, ROW_CHUNKS, stride=pitch), :] = x_ref[0, token_rows(s * rows + r, last_row), :]
        eye = (lax.broadcasted_iota(jnp.int32, (rows, rows), 0)
               == lax.broadcasted_iota(jnp.int32, (rows, rows), 1))
        ys = {}
        for s in subs:
            x = jnp.concatenate([xs_refs[s][pl.ds(c * pitch, rows), :] for c in range(ROW_CHUNKS)], axis=1)
            hid = _silu(_bdot(x, wg_ref[0, 0])) * _bdot(x, wu_ref[0, 0])
            w_row = sw_ref[0, :, s * rows:(s + 1) * rows]
            w_col = jnp.sum(jnp.where(eye, w_row, 0.0), axis=1, keepdims=True)
            ys[s] = _bdot(hid, wd_ref[0, 0]) * w_col
        for s in subs:
            for c in range(ROW_CHUNKS):
                ys_refs[s][pl.ds(c * pitch, rows), :] = ys[s][:, c * LANES:(c + 1) * LANES]
        for s in subs:
            for r0 in range(0, rows, SCATTER_BATCH):
                batch = range(r0, r0 + SCATTER_BATCH)
                ats = [token_rows(s * rows + r) for r in batch]
                new = [acc_ref[0, at, :] + ys_refs[s][pl.ds(r, ROW_CHUNKS, stride=pitch), :]
                       for at, r in zip(ats, batch)]
                for at, val in zip(ats, new):
                    acc_ref[0, at, :] = val

    live = live_ref[i * pl.num_programs(1) + j]
    for n_live in range(1, n_sub + 1):
        pl.when(live == n_live)(functools.partial(run, range(n_live)))


def _routed_experts(x_slabs, idx_t, w_t, rank_t, counts, w_gate_up, w_down, layer, tile_tokens):
    d = D_MODEL
    t = x_slabs.shape[0] // ROW_CHUNKS
    n_tiles = t // tile_tokens
    rows, slot_w, step_e, n_used, live, steps = _dispatch_plan(idx_t, w_t, rank_t, counts, tile_tokens)
    rows = rows.reshape(n_tiles * steps, 1, STEP_ROWS)
    slot_w = slot_w.reshape(n_tiles * steps, 1, STEP_ROWS)
    x_slabs = x_slabs.reshape(n_tiles, tile_tokens * ROW_CHUNKS, LANES)
    slab_rows = (tile_tokens + 1) * ROW_CHUNKS

    def used_step(i, j, se, nu, lv):
        return (i * steps + jnp.minimum(j, jnp.maximum(nu[i] - 1, 0)), 0, 0)

    pick = lambda i, j, se, nu, lv: (layer, se[i * steps + j], 0, 0)
    tile = lambda i, j, se, nu, lv: (i, 0, 0)
    resident = pl.Buffered(1)
    stage = pltpu.VMEM((ROW_CHUNKS * STAGE_PITCH, LANES), F32)
    grid_spec = pltpu.PrefetchScalarGridSpec(
        num_scalar_prefetch=3,
        grid=(n_tiles, steps),
        in_specs=[
            pl.BlockSpec((1, 1, STEP_ROWS), used_step, memory_space=pltpu.SMEM),
            pl.BlockSpec((1, 1, STEP_ROWS), used_step),
            pl.BlockSpec((1, tile_tokens * ROW_CHUNKS, LANES), tile, pipeline_mode=resident),
            pl.BlockSpec((1, 1, d, D_EXPERT), pick),
            pl.BlockSpec((1, 1, d, D_EXPERT), pick),
            pl.BlockSpec((1, 1, D_EXPERT, d), pick),
        ],
        out_specs=pl.BlockSpec((1, slab_rows, LANES), tile, pipeline_mode=resident),
        scratch_shapes=[stage] * (2 * BLOCKS_PER_STEP),
    )
    return pl.pallas_call(
        _experts_kernel,
        grid_spec=grid_spec,
        out_shape=jax.ShapeDtypeStruct((n_tiles, slab_rows, LANES), F32),
        compiler_params=_cparams("parallel", "arbitrary"),
        name="moe_experts",
    )(step_e, n_used, live, rows, slot_w, x_slabs, *w_gate_up, w_down)


def _moe_out_kernel(x_ref, r_ref, wg_ref, wu_ref, wd_ref, g_ref, b_ref, o_ref):
    x = x_ref[...]
    tm = x.shape[0]
    hid = _silu(_bdot(x, wg_ref[...])) * _bdot(x, wu_ref[...])
    routed = jnp.concatenate([r_ref[0, pl.ds(c, tm, stride=ROW_CHUNKS), :] for c in range(ROW_CHUNKS)], axis=1)
    y = routed + _bdot(hid, wd_ref[...])
    o_ref[...] = _layer_norm(DN_ALPHA * x + y, g_ref[...], b_ref[...])


def _moe_out(x, routed, ws_gate, ws_up, ws_down, g, b, tile_tokens, tm=1024):
    t, d = x.shape
    tm = min(tm, tile_tokens)
    per_tile = tile_tokens // tm
    return pl.pallas_call(
        _moe_out_kernel,
        grid=(t // tm,),
        in_specs=[pl.BlockSpec((tm, d), lambda i: (i, 0)),
                  pl.BlockSpec((1, tm * ROW_CHUNKS, LANES), lambda i: (i // per_tile, i % per_tile, 0)),
                  pl.BlockSpec((d, D_SHARED), lambda i: (0, 0)),
                  pl.BlockSpec((d, D_SHARED), lambda i: (0, 0)),
                  pl.BlockSpec((D_SHARED, d), lambda i: (0, 0)),
                  pl.BlockSpec((1, d), lambda i: (0, 0)),
                  pl.BlockSpec((1, d), lambda i: (0, 0))],
        out_specs=pl.BlockSpec((tm, d), lambda i: (i, 0)),
        out_shape=jax.ShapeDtypeStruct((t, d), F32),
        compiler_params=_cparams("parallel"),
        name="moe_out",
    )(x, routed, ws_gate.astype(BF16), ws_up.astype(BF16), ws_down.astype(BF16),
      g.reshape(1, d), b.reshape(1, d))


def _moe_ln(hid, hid_slabs, w_router, router_bias, w_gate_up, w_down, layer, ws_gate, ws_up, ws_down, ln_g, ln_b):
    b, l, d = hid.shape
    x = hid.reshape(b * l, d)
    tile_tokens = min(MOE_TILE_TOKENS, b * l)
    idx_t, w_t, rank_t, counts = _router(x, w_router, router_bias, tile_tokens)
    routed = _routed_experts(hid_slabs, idx_t, w_t, rank_t, counts, w_gate_up, w_down, layer, tile_tokens)
    return _moe_out(x, routed, ws_gate, ws_up, ws_down, ln_g, ln_b, tile_tokens).reshape(b, l, d)


def kernel(x, a_w_in, a_conv_w, a_A_log, a_dt_bias, a_onorm_w, a_w_out, b_w_dq, b_qnorm_w, b_w_uq, b_w_o,
           kv_w_dkv, kv_norm_w, kv_w_ukv, ln1_g, ln1_b, ln2_g, ln2_b, moe_w_router, moe_router_bias,
           moe_w_gate, moe_w_up, moe_w_down, moe_ws_gate, moe_ws_up, moe_ws_down):
    l = x.shape[1]
    cos_t, sin_t = _rope_lane_tables(l)
    w_gate_up = (moe_w_gate.astype(BF16), moe_w_up.astype(BF16))
    w_down = moe_w_down.astype(BF16)
    h = x
    k = v = None
    for layer in range(DEPTH):
        if layer < N_A_LAYERS:
            i = layer
            h, slabs = _gated_deltanet_ln(h, a_w_in[i], a_conv_w[i], a_A_log[i], a_dt_bias[i], a_onorm_w[i],
                                          a_w_out[i], ln1_g[layer], ln1_b[layer])
        else:
            i = layer - N_A_LAYERS
            h, slabs = _mla_ln(h, b_w_dq[i], b_qnorm_w[i], b_w_uq[i], b_w_o[i], k, v, cos_t, sin_t,
                               ln1_g[layer], ln1_b[layer])
        h = _moe_ln(h, slabs, moe_w_router[layer], moe_router_bias[layer], w_gate_up, w_down, layer,
                    moe_ws_gate[layer], moe_ws_up[layer], moe_ws_down[layer], ln2_g[layer], ln2_b[layer])
        if layer == N_A_LAYERS - 1:
            k, v = _mla_kv(h, kv_w_dkv, kv_norm_w, kv_w_ukv, cos_t, sin_t)
    return h
```

```python
import functools

import jax
import jax.numpy as jnp
from jax import lax
from jax.experimental import pallas as pl
from jax.experimental.pallas import tpu as pltpu

F32 = jnp.float32
BF16 = jnp.bfloat16
HIGHEST = lax.Precision.HIGHEST

D_MODEL = 1024
DEPTH = 4
N_A_LAYERS = DEPTH // 2
GDN_HEADS = 8
GDN_DK = 128
GDN_DV = 128
D_QK = GDN_HEADS * GDN_DK
D_VA = GDN_HEADS * GDN_DV
GDN_CONV_CH = 2 * D_QK + D_VA
CONV_WIDTH = 4
CHUNK = 64
MLA_HEADS = 8
Q_LORA = 512
KV_LORA = 256
D_NOPE = 128
D_ROPE = 64
D_V = 128
ROPE_THETA = 10000.0
N_EXPERTS = 64
TOP_K = 8
N_GROUPS = 8
GROUP_SIZE = N_EXPERTS // N_GROUPS
TOPK_GROUPS = 4
D_EXPERT = 256
D_SHARED = 256
ROUTED_SCALE = 2.5
DN_ALPHA = (2 * DEPTH) ** 0.25
EPS = 1e-6
LN_EPS = 1e-5
LOG2_E = 1.4426950408889634

LANES = 128
SUBLANES = 8
VMEM_LIMIT_BYTES = 56 * 1024 * 1024

EXPERT_ROWS = 128
ROW_CHUNKS = D_MODEL // LANES
STAGE_PITCH = EXPERT_ROWS + 1
SCATTER_BATCH = 8
BLOCKS_PER_STEP = 2
STEP_ROWS = BLOCKS_PER_STEP * EXPERT_ROWS
MOE_TILE_TOKENS = 4096
ATTN_ROW_GROUPS = 4
assert TOP_K == SUBLANES


def _cparams(*sem):
    return pltpu.CompilerParams(dimension_semantics=sem, vmem_limit_bytes=VMEM_LIMIT_BYTES)


def _silu(x):
    return x * jax.nn.sigmoid(x)


def _layer_norm(x, g, b):
    mu = jnp.mean(x, -1, keepdims=True)
    xc = x - mu
    var = jnp.mean(xc * xc, -1, keepdims=True)
    return xc * lax.rsqrt(var + LN_EPS) * g + b


def _rms_norm(x, w):
    return x * lax.rsqrt(jnp.mean(x * x, -1, keepdims=True) + EPS) * w


def _bdot(a, b):
    return jnp.dot(a.astype(BF16), b.astype(BF16), preferred_element_type=F32)


def _bdot_nt(a, b):
    return lax.dot_general(a.astype(BF16), b.astype(BF16), (((1,), (1,)), ((), ())),
                           preferred_element_type=F32)


def _hdot(a, b):
    return jnp.dot(a, b, precision=HIGHEST, preferred_element_type=F32)


def _matmul_kernel(x_ref, w_ref, o_ref):
    o_ref[...] = _bdot(x_ref[...], w_ref[...]).astype(o_ref.dtype)


def _matmul(x, w, out_dtype, tm, tn):
    m, k = x.shape
    n = w.shape[1]
    tm = min(tm, m)
    tn = min(tn, n)
    return pl.pallas_call(
        _matmul_kernel,
        grid=(m // tm, n // tn),
        in_specs=[pl.BlockSpec((tm, k), lambda i, j: (i, 0)),
                  pl.BlockSpec((k, tn), lambda i, j: (0, j))],
        out_specs=pl.BlockSpec((tm, tn), lambda i, j: (i, j)),
        out_shape=jax.ShapeDtypeStruct((m, n), out_dtype),
        compiler_params=_cparams("parallel", "parallel"),
        name="matmul",
    )(x, w)


def _matmul_ln_kernel(x_ref, w_ref, r_ref, g_ref, b_ref, o_ref, slab_ref):
    mix = _bdot(x_ref[...], w_ref[...])
    out = _layer_norm(DN_ALPHA * r_ref[...] + mix, g_ref[...], b_ref[...])
    o_ref[...] = out
    tm = out.shape[0]
    for c in range(ROW_CHUNKS):
        slab_ref[pl.ds(c, tm, stride=ROW_CHUNKS), :] = out[:, c * LANES:(c + 1) * LANES]


def _matmul_ln(x, w, res, g, b, tm=1024):
    m, k = x.shape
    n = w.shape[1]
    tm = min(tm, m)
    chunks = n // LANES
    return pl.pallas_call(
        _matmul_ln_kernel,
        grid=(m // tm,),
        in_specs=[pl.BlockSpec((tm, k), lambda i: (i, 0)),
                  pl.BlockSpec((k, n), lambda i: (0, 0)),
                  pl.BlockSpec((tm, n), lambda i: (i, 0)),
                  pl.BlockSpec((1, n), lambda i: (0, 0)),
                  pl.BlockSpec((1, n), lambda i: (0, 0))],
        out_specs=[pl.BlockSpec((tm, n), lambda i: (i, 0)),
                   pl.BlockSpec((tm * chunks, LANES), lambda i: (i, 0))],
        out_shape=[jax.ShapeDtypeStruct((m, n), F32),
                   jax.ShapeDtypeStruct((m * chunks, LANES), F32)],
        compiler_params=_cparams("parallel"),
        name="matmul_ln",
    )(x, w, res, g.reshape(1, n), b.reshape(1, n))


def _softplus(x):
    return jnp.maximum(x, 0.0) + jnp.log1p(jnp.exp(-jnp.abs(x)))


def _gdn_gates_kernel(x_ref, wab_ref, wabt_ref, alog_ref, dtb_ref, alogt_ref, dtbt_ref,
                      gc_ref, beta_ref, gct_ref):
    x = x_ref[...]
    tl = x.shape[0]
    h = GDN_HEADS
    ab = _bdot(x, wab_ref[...])
    abt = _bdot_nt(wabt_ref[...], x)
    g = -jnp.exp(alog_ref[...]) * _softplus(ab[:, :h] + dtb_ref[...])
    gt = -jnp.exp(alogt_ref[...]) * _softplus(abt[:h, :] + dtbt_ref[...])
    beta_ref[...] = jax.nn.sigmoid(ab[:, h:2 * h])
    row = lax.broadcasted_iota(jnp.int32, (tl, tl), 0)
    col = lax.broadcasted_iota(jnp.int32, (tl, tl), 1)
    shift = CHUNK.bit_length() - 1
    same = (row >> shift) == (col >> shift)
    lower = jnp.where(same & (col <= row), 1.0, 0.0).astype(F32)
    upper = jnp.where(same & (row <= col), 1.0, 0.0).astype(F32)
    gc_ref[...] = _hdot(lower, g)
    gct_ref[...] = _hdot(gt, upper)


def _gdn_gates(x, w_ab, a_log, dt_bias, tl=256):
    t = x.shape[0]
    tl = min(tl, t)
    h = GDN_HEADS
    wab = jnp.zeros((D_MODEL, LANES), F32).at[:, :2 * h].set(w_ab)
    wabt = w_ab.T
    return pl.pallas_call(
        _gdn_gates_kernel,
        grid=(t // tl,),
        in_specs=[pl.BlockSpec((tl, D_MODEL), lambda i: (i, 0)),
                  pl.BlockSpec((D_MODEL, LANES), lambda i: (0, 0)),
                  pl.BlockSpec((2 * h, D_MODEL), lambda i: (0, 0)),
                  pl.BlockSpec((1, h), lambda i: (0, 0)),
                  pl.BlockSpec((1, h), lambda i: (0, 0)),
                  pl.BlockSpec((h, 1), lambda i: (0, 0)),
                  pl.BlockSpec((h, 1), lambda i: (0, 0))],
        out_specs=[pl.BlockSpec((tl, h), lambda i: (i, 0)),
                   pl.BlockSpec((tl, h), lambda i: (i, 0)),
                   pl.BlockSpec((h, tl), lambda i: (0, i))],
        out_shape=[jax.ShapeDtypeStruct((t, h), F32),
                   jax.ShapeDtypeStruct((t, h), F32),
                   jax.ShapeDtypeStruct((h, t), F32)],
        compiler_params=_cparams("parallel"),
        name="gdn_gates",
    )(x, wab, wabt, a_log.reshape(1, h), dt_bias.reshape(1, h),
      a_log.reshape(h, 1), dt_bias.reshape(h, 1))


def _gdn_conv_kernel(cur_ref, prev_ref, w_ref, q_ref, k_ref, v_ref):
    i = pl.program_id(1)
    x = cur_ref[0]
    tl = x.shape[0]
    prev = jnp.where(i > 0, prev_ref[0], 0.0)
    xx = jnp.concatenate([prev, x], axis=0)
    w = w_ref[...]
    y = None
    for j in range(CONV_WIDTH):
        off = SUBLANES - (CONV_WIDTH - 1) + j
        term = xx[off:off + tl, :] * w[j:j + 1, :]
        y = term if y is None else y + term
    y = _silu(y)
    for h in range(GDN_HEADS):
        sl = slice(h * GDN_DK, (h + 1) * GDN_DK)
        qh = y[:, sl]
        q_ref[0, :, sl] = qh * lax.rsqrt(jnp.sum(qh * qh, -1, keepdims=True) + EPS) * (GDN_DK ** -0.5)
        kh = y[:, D_QK + h * GDN_DK:D_QK + (h + 1) * GDN_DK]
        k_ref[0, :, sl] = kh * lax.rsqrt(jnp.sum(kh * kh, -1, keepdims=True) + EPS)
    v_ref[0] = y[:, 2 * D_QK:]


def _gdn_conv(proj, conv_w, tl=512):
    b, l, _ = proj.shape
    tl = min(tl, l)
    c3 = GDN_CONV_CH
    per = tl // SUBLANES
    out = jax.ShapeDtypeStruct((b, l, D_QK), F32)
    return pl.pallas_call(
        _gdn_conv_kernel,
        grid=(b, l // tl),
        in_specs=[pl.BlockSpec((1, tl, c3), lambda bi, i: (bi, i, 0)),
                  pl.BlockSpec((1, SUBLANES, c3), lambda bi, i: (bi, jnp.maximum(i * per - 1, 0), 0)),
                  pl.BlockSpec((CONV_WIDTH, c3), lambda bi, i: (0, 0))],
        out_specs=[pl.BlockSpec((1, tl, D_QK), lambda bi, i: (bi, i, 0))] * 3,
        out_shape=[out, out, out],
        compiler_params=_cparams("parallel", "parallel"),
        name="gdn_conv",
    )(proj, proj, conv_w)


def _split_bf16(x):
    hi = x.astype(BF16)
    lo = (x - hi.astype(F32)).astype(BF16)
    return hi, lo


def _dot_split(a, b):
    (ah, al), (bh, bl) = a, b
    dot = functools.partial(jnp.dot, preferred_element_type=F32)
    return dot(ah, bh) + (dot(ah, bl) + dot(al, bh))


def _gdn_prep_kernel(q_ref, k_ref, v_ref, gc_ref, beta_ref, gct_ref, wq_ref, u_ref, attn_ref, kgt_ref,
                     *, chunks):
    c = CHUNK
    row = lax.broadcasted_iota(jnp.int32, (c, c), 0)
    col = lax.broadcasted_iota(jnp.int32, (c, c), 1)
    incl = row >= col
    strict = row > col
    eye = jnp.where(row == col, 1.0, 0.0).astype(F32)
    items = [(ci, h) for ci in range(chunks) for h in range(GDN_HEADS)]

    st = {}
    for ci, h in items:
        rows = slice(ci * c, (ci + 1) * c)
        sl = slice(h * GDN_DK, (h + 1) * GDN_DK)
        q = q_ref[0, rows, sl]
        k = k_ref[0, rows, sl]
        v = v_ref[0, rows, sl]
        gcol = gc_ref[0, rows, h:h + 1]
        grow = gct_ref[0, ci, h:h + 1, :]
        bcol = beta_ref[0, rows, h:h + 1]
        decay = jnp.where(incl, jnp.exp(jnp.where(incl, gcol - grow, 0.0)), 0.0)
        kb = k * bcol
        eg = jnp.exp(gcol)
        g_last = gcol[c - 1:c, :]
        sc = _bdot_nt(jnp.concatenate([kb, q], axis=0), k)
        a = jnp.where(strict, sc[:c] * decay, 0.0)
        attn_ref[0, ci, h] = (sc[c:] * decay).astype(attn_ref.dtype)
        kgt_ref[0, ci, h] = (k * jnp.exp(g_last - gcol)).T.astype(kgt_ref.dtype)
        wq_ref[0, ci, h, c:, :] = (q * eg).astype(wq_ref.dtype)
        st[ci, h] = dict(m=-a, rhs=jnp.concatenate([v * bcol, kb * eg], axis=1))

    for it in items:
        st[it]["p"] = eye + st[it]["m"]
    span = 2
    while span < c:
        for it in items:
            ms = _split_bf16(st[it]["m"])
            st[it]["m"] = _dot_split(ms, ms)
        for it in items:
            st[it]["p"] = st[it]["p"] + _dot_split(_split_bf16(st[it]["p"]), _split_bf16(st[it]["m"]))
        span *= 2

    for ci, h in items:
        sol = _dot_split(_split_bf16(st[ci, h]["p"]), _split_bf16(st[ci, h]["rhs"]))
        u_ref[0, ci * c:(ci + 1) * c, h * GDN_DV:(h + 1) * GDN_DV] = sol[:, :GDN_DV]
        wq_ref[0, ci, h, :c, :] = sol[:, GDN_DV:].astype(wq_ref.dtype)


def _gdn_prep(q, k, v, gc, beta, gct, chunks=2):
    b, l, _ = q.shape
    n = l // CHUNK
    chunks = min(chunks, n)
    h = GDN_HEADS
    rows = chunks * CHUNK
    blk = pl.BlockSpec((1, rows, D_QK), lambda bi, i: (bi, i, 0))
    gate = pl.BlockSpec((1, rows, h), lambda bi, i: (bi, i, 0))
    return pl.pallas_call(
        functools.partial(_gdn_prep_kernel, chunks=chunks),
        grid=(b, n // chunks),
        in_specs=[blk, blk, blk, gate, gate,
                  pl.BlockSpec((1, chunks, h, CHUNK), lambda bi, i: (bi, i, 0, 0))],
        out_specs=[pl.BlockSpec((1, chunks, h, 2 * CHUNK, GDN_DK), lambda bi, i: (bi, i, 0, 0, 0)),
                   pl.BlockSpec((1, rows, D_VA), lambda bi, i: (bi, i, 0)),
                   pl.BlockSpec((1, chunks, h, CHUNK, CHUNK), lambda bi, i: (bi, i, 0, 0, 0)),
                   pl.BlockSpec((1, chunks, h, GDN_DK, CHUNK), lambda bi, i: (bi, i, 0, 0, 0))],
        out_shape=[jax.ShapeDtypeStruct((b, n, h, 2 * CHUNK, GDN_DK), BF16),
                   jax.ShapeDtypeStruct((b, l, D_VA), F32),
                   jax.ShapeDtypeStruct((b, n, h, CHUNK, CHUNK), BF16),
                   jax.ShapeDtypeStruct((b, n, h, GDN_DK, CHUNK), BF16)],
        compiler_params=_cparams("parallel", "parallel"),
        name="gdn_prep",
    )(q, k, v, gc, beta, gct)


def _gdn_state_kernel(wq_ref, u_ref, attn_ref, kgt_ref, gc_ref, z_ref, onw_ref, o_ref, s_ref, *, chunks):
    @pl.when(pl.program_id(1) == 0)
    def _():
        s_ref[...] = jnp.zeros_like(s_ref)

    c = CHUNK
    onw = onw_ref[...]
    dot = functools.partial(jnp.dot, preferred_element_type=F32)
    heads = range(GDN_HEADS)
    for ci in range(chunks):
        rows = slice(ci * c, (ci + 1) * c)
        s_old = [s_ref[h] for h in heads]
        r = [dot(wq_ref[0, ci, h], s_old[h].astype(BF16)) for h in heads]
        v_new = [(u_ref[0, rows, h * GDN_DV:(h + 1) * GDN_DV] - r[h][:c]).astype(BF16) for h in heads]
        for h in heads:
            decay_last = jnp.exp(gc_ref[0, (ci + 1) * c - 1:(ci + 1) * c, h:h + 1])
            s_ref[h] = s_old[h] * decay_last + dot(kgt_ref[0, ci, h], v_new[h])
        for h in heads:
            sl = slice(h * GDN_DV, (h + 1) * GDN_DV)
            o = r[h][c:] + dot(attn_ref[0, ci, h], v_new[h])
            o_ref[0, rows, sl] = (_rms_norm(o, onw) * _silu(z_ref[0, rows, sl])).astype(o_ref.dtype)


def _gdn_state(wq, u, attn, kgt, gc, proj, onorm_w, chunks=16):
    b, n, h = wq.shape[:3]
    l = n * CHUNK
    chunks = min(chunks, n)
    rows = chunks * CHUNK
    z_block = GDN_CONV_CH // D_VA
    return pl.pallas_call(
        functools.partial(_gdn_state_kernel, chunks=chunks),
        grid=(b, n // chunks),
        in_specs=[pl.BlockSpec((1, chunks, h, 2 * CHUNK, GDN_DK), lambda bi, i: (bi, i, 0, 0, 0)),
                  pl.BlockSpec((1, rows, D_VA), lambda bi, i: (bi, i, 0)),
                  pl.BlockSpec((1, chunks, h, CHUNK, CHUNK), lambda bi, i: (bi, i, 0, 0, 0)),
                  pl.BlockSpec((1, chunks, h, GDN_DK, CHUNK), lambda bi, i: (bi, i, 0, 0, 0)),
                  pl.BlockSpec((1, rows, h), lambda bi, i: (bi, i, 0)),
                  pl.BlockSpec((1, rows, D_VA), lambda bi, i: (bi, i, z_block)),
                  pl.BlockSpec((1, GDN_DV), lambda bi, i: (0, 0))],
        out_specs=pl.BlockSpec((1, rows, D_VA), lambda bi, i: (bi, i, 0)),
        out_shape=jax.ShapeDtypeStruct((b, l, D_VA), BF16),
        scratch_shapes=[pltpu.VMEM((h, GDN_DK, GDN_DV), F32)],
        compiler_params=_cparams("parallel", "arbitrary"),
        name="gdn_state",
    )(wq, u, attn, kgt, gc, proj, onorm_w.reshape(1, GDN_DV))


def _gated_deltanet_ln(hid, w_in, conv_w, a_log, dt_bias, onorm_w, w_out, ln_g, ln_b):
    b, l, d = hid.shape
    h = GDN_HEADS
    n = l // CHUNK
    x = hid.reshape(b * l, d)
    n_main = GDN_CONV_CH + D_VA
    proj = _matmul(x, w_in[:, :n_main].astype(BF16), F32, 1024, 2048).reshape(b, l, n_main)
    gc, beta, gct = _gdn_gates(x, w_in[:, n_main:], a_log, dt_bias)
    gc = gc.reshape(b, l, h)
    gct = gct.reshape(h, b, n, CHUNK).transpose(1, 2, 0, 3)
    q, k, v = _gdn_conv(proj, conv_w)
    wq, u, attn, kgt = _gdn_prep(q, k, v, gc, beta.reshape(b, l, h), gct)
    o = _gdn_state(wq, u, attn, kgt, gc, proj, onorm_w)
    out, slabs = _matmul_ln(o.reshape(b * l, D_VA), w_out.astype(BF16), x, ln_g, ln_b)
    return out.reshape(b, l, d), slabs


def _rope_lane_tables(l):
    inv = 1.0 / (ROPE_THETA ** (jnp.arange(0, D_ROPE, 2, dtype=F32) / D_ROPE))
    ang = jnp.arange(l, dtype=F32)[:, None] * inv[None, :]
    cos, sin = jnp.cos(ang), jnp.sin(ang)
    zero = jnp.zeros((l, LANES - D_ROPE), F32)
    return (jnp.concatenate([cos, cos, zero], -1),
            jnp.concatenate([-sin, sin, zero], -1))


def _rope_weight_groups(w_rope):
    kdim = w_rope.shape[0]
    half = D_ROPE // 2
    zero = jnp.zeros((kdim, LANES - D_ROPE), w_rope.dtype)
    x1, x2 = w_rope[:, :half], w_rope[:, half:]
    return jnp.concatenate([x1, x2, zero, x2, x1, zero], axis=-1)


def _mla_kv_kernel(x_ref, wd_ref, nw_ref, wu_ref, cos_ref, sin_ref, k_ref, v_ref):
    x = x_ref[0]
    ckv = _bdot(x, wd_ref[...])
    c = _rms_norm(ckv[:, :KV_LORA], nw_ref[...])
    k_rope_t = (ckv[:, KV_LORA:KV_LORA + LANES] * cos_ref[...]
                + ckv[:, KV_LORA + LANES:] * sin_ref[...]).T.astype(k_ref.dtype)
    kv = _bdot(c, wu_ref[...])
    per = D_NOPE + D_V
    ones_col = jnp.where(lax.broadcasted_iota(jnp.int32, (x.shape[0], LANES), 1) == 0, 1.0, 0.0).astype(v_ref.dtype)
    for h in range(MLA_HEADS):
        k_ref[0, h, :D_NOPE, :] = kv[:, h * per:h * per + D_NOPE].T.astype(k_ref.dtype)
        k_ref[0, h, D_NOPE:, :] = k_rope_t
        v_ref[0, h, :, :D_V] = kv[:, h * per + D_NOPE:(h + 1) * per].astype(v_ref.dtype)
        v_ref[0, h, :, D_V:] = ones_col


def _mla_kv(hid, w_dkv, kv_norm_w, w_ukv, cos_t, sin_t, tl=1024):
    b, l, d = hid.shape
    tl = min(tl, l)
    wd = jnp.concatenate([w_dkv[:, :KV_LORA], _rope_weight_groups(w_dkv[:, KV_LORA:])], -1).astype(BF16)
    nd = wd.shape[1]
    hh = MLA_HEADS
    return pl.pallas_call(
        _mla_kv_kernel,
        grid=(b, l // tl),
        in_specs=[pl.BlockSpec((1, tl, d), lambda bi, i: (bi, i, 0)),
                  pl.BlockSpec((d, nd), lambda bi, i: (0, 0)),
                  pl.BlockSpec((1, KV_LORA), lambda bi, i: (0, 0)),
                  pl.BlockSpec((KV_LORA, hh * (D_NOPE + D_V)), lambda bi, i: (0, 0)),
                  pl.BlockSpec((tl, LANES), lambda bi, i: (i, 0)),
                  pl.BlockSpec((tl, LANES), lambda bi, i: (i, 0))],
        out_specs=[pl.BlockSpec((1, hh, D_NOPE + LANES, tl), lambda bi, i: (bi, 0, 0, i)),
                   pl.BlockSpec((1, hh, tl, D_V + LANES), lambda bi, i: (bi, 0, i, 0))],
        out_shape=[jax.ShapeDtypeStruct((b, hh, D_NOPE + LANES, l), BF16),
                   jax.ShapeDtypeStruct((b, hh, l, D_V + LANES), BF16)],
        compiler_params=_cparams("parallel", "parallel"),
        name="mla_kv",
    )(hid, wd, kv_norm_w.reshape(1, KV_LORA), w_ukv.astype(BF16), cos_t, sin_t)


def _mla_cq_kernel(x_ref, w_ref, nw_ref, o_ref):
    o_ref[...] = _rms_norm(_bdot(x_ref[...], w_ref[...]), nw_ref[...]).astype(o_ref.dtype)


def _mla_cq(x, w_dq, qnorm_w, tm=1024):
    m, k = x.shape
    tm = min(tm, m)
    return pl.pallas_call(
        _mla_cq_kernel,
        grid=(m // tm,),
        in_specs=[pl.BlockSpec((tm, k), lambda i: (i, 0)),
                  pl.BlockSpec((k, Q_LORA), lambda i: (0, 0)),
                  pl.BlockSpec((1, Q_LORA), lambda i: (0, 0))],
        out_specs=pl.BlockSpec((tm, Q_LORA), lambda i: (i, 0)),
        out_shape=jax.ShapeDtypeStruct((m, Q_LORA), BF16),
        compiler_params=_cparams("parallel"),
        name="mla_cq",
    )(x, w_dq.astype(BF16), qnorm_w.reshape(1, Q_LORA))


def _mla_q_kernel(c_ref, w_ref, cos_ref, sin_ref, q_ref):
    c = c_ref[0]
    scale = (D_NOPE + D_ROPE) ** -0.5 * LOG2_E
    per = D_NOPE + 2 * LANES
    for h in range(MLA_HEADS):
        qh = _bdot(c, w_ref[:, h * per:(h + 1) * per])
        rope = qh[:, D_NOPE:D_NOPE + LANES] * cos_ref[...] + qh[:, D_NOPE + LANES:] * sin_ref[...]
        q_ref[0, h, :, :D_NOPE] = (qh[:, :D_NOPE] * scale).astype(q_ref.dtype)
        q_ref[0, h, :, D_NOPE:] = (rope * scale).astype(q_ref.dtype)


def _mla_q(cq, w_uq, cos_t, sin_t, tl=1024):
    b, l, _ = cq.shape
    tl = min(tl, l)
    hh = MLA_HEADS
    per_in = D_NOPE + D_ROPE
    groups = []
    for h in range(hh):
        wh = w_uq[:, h * per_in:(h + 1) * per_in]
        groups += [wh[:, :D_NOPE], _rope_weight_groups(wh[:, D_NOPE:])]
    w = jnp.concatenate(groups, -1).astype(BF16)
    return pl.pallas_call(
        _mla_q_kernel,
        grid=(b, l // tl),
        in_specs=[pl.BlockSpec((1, tl, Q_LORA), lambda bi, i: (bi, i, 0)),
                  pl.BlockSpec(w.shape, lambda bi, i: (0, 0)),
                  pl.BlockSpec((tl, LANES), lambda bi, i: (i, 0)),
                  pl.BlockSpec((tl, LANES), lambda bi, i: (i, 0))],
        out_specs=pl.BlockSpec((1, hh, tl, D_NOPE + LANES), lambda bi, i: (bi, 0, i, 0)),
        out_shape=jax.ShapeDtypeStruct((b, hh, l, D_NOPE + LANES), BF16),
        compiler_params=_cparams("parallel", "parallel"),
        name="mla_q",
    )(cq, w, cos_t, sin_t)


def _mla_attn_kernel(q_ref, k_ref, v_ref, o_ref, *, tk):
    qi = pl.program_id(2)
    q = q_ref[0, 0]
    tq = q.shape[0]
    per = tq // tk

    groups = ATTN_ROW_GROUPS
    rows = tq // groups
    qs = [q[g * rows:(g + 1) * rows] for g in range(groups)]

    def block(j, carry, mask_offset=None):
        at = pl.ds(pl.multiple_of(j * tk, tk), tk)
        kb = k_ref[0, 0, :, at]
        vb = v_ref[0, 0, at, :]
        if mask_offset is None:
            widths = [tk] * groups
            ss = [_bdot(qg, kb) for qg in qs]
        else:
            widths = [min(max((g + 1) * rows - mask_offset, LANES), tk) for g in range(groups)]
            ss = []
            for g, (qg, wd) in enumerate(zip(qs, widths)):
                qpos = lax.broadcasted_iota(jnp.int32, (rows, wd), 0)
                kpos = lax.broadcasted_iota(jnp.int32, (rows, wd), 1)
                ss.append(jnp.where(kpos + mask_offset <= qpos + g * rows, _bdot(qg, kb[:, :wd]), -jnp.inf))
        m_new = [jnp.maximum(c[0], jnp.max(s, -1, keepdims=True)) for c, s in zip(carry, ss)]
        ps = [jnp.exp2(s - m) for s, m in zip(ss, m_new)]
        return tuple((mn, jnp.exp2(m - mn) * acc + _bdot(p, vb[:wd]))
                     for (m, acc), mn, p, wd in zip(carry, m_new, ps, widths))

    first = qi * per
    init = (jnp.full((rows, 1), -jnp.inf, F32), jnp.zeros((rows, D_V + LANES), F32))
    carry = lax.fori_loop(0, first, block, (init,) * groups)
    for d in range(per):
        carry = block(first + d, carry, mask_offset=d * tk)
    for g, (_, acc) in enumerate(carry):
        o_ref[0, g * rows:(g + 1) * rows, :] = (acc[:, :D_V] / acc[:, D_V:D_V + 1]).astype(o_ref.dtype)


def _mla_attn(q, k, v, tq=1024, tk=1024):
    b, hh, l, dq = q.shape
    tq = min(tq, l)
    tk = min(tk, tq)
    return pl.pallas_call(
        functools.partial(_mla_attn_kernel, tk=tk),
        grid=(b, hh, l // tq),
        in_specs=[pl.BlockSpec((1, 1, tq, dq), lambda bi, h, i: (bi, h, i, 0)),
                  pl.BlockSpec((1, 1, dq, l), lambda bi, h, i: (bi, h, 0, 0)),
                  pl.BlockSpec((1, 1, l, D_V + LANES), lambda bi, h, i: (bi, h, 0, 0))],
        out_specs=pl.BlockSpec((1, tq, D_V), lambda bi, h, i: (bi, i, h)),
        out_shape=jax.ShapeDtypeStruct((b, l, hh * D_V), BF16),
        compiler_params=_cparams("parallel", "parallel", "parallel"),
        name="mla_attn",
    )(q, k, v)


def _mla_ln(hid, w_dq, qnorm_w, w_uq, w_o, k, v, cos_t, sin_t, ln_g, ln_b):
    b, l, d = hid.shape
    x = hid.reshape(b * l, d)
    cq = _mla_cq(x, w_dq, qnorm_w).reshape(b, l, Q_LORA)
    q = _mla_q(cq, w_uq, cos_t, sin_t)
    o = _mla_attn(q, k, v)
    out, slabs = _matmul_ln(o.reshape(b * l, MLA_HEADS * D_V), w_o.astype(BF16), x, ln_g, ln_b)
    return out.reshape(b, l, d), slabs


def _first_argmax(x, ids, n):
    m = jnp.max(x, axis=0, keepdims=True)
    first = jnp.min(jnp.where(x == m, ids, n), axis=0, keepdims=True)
    return m, first


def _router_kernel(x_ref, wt_ref, bias_ref, before_ref, idx_ref, w_ref, rank_ref, cnt_out_ref, cnt_ref,
                   *, steps_per_tile):
    @pl.when(pl.program_id(0) % steps_per_tile == 0)
    def _():
        cnt_ref[...] = jnp.zeros_like(cnt_ref)

    x = x_ref[...]
    t = x.shape[0]
    logits = lax.dot_general(wt_ref[...], x, (((1,), (1,)), ((), ())),
                             precision=HIGHEST, preferred_element_type=F32)
    scores = jax.nn.sigmoid(logits)
    biased = scores + bias_ref[...]
    neg = -jnp.inf
    sub = lax.broadcasted_iota(jnp.int32, (GROUP_SIZE, t), 0).astype(F32)
    gscores = []
    for g in range(N_GROUPS):
        xg = biased[g * GROUP_SIZE:(g + 1) * GROUP_SIZE, :]
        m1, i1 = _first_argmax(xg, sub, float(GROUP_SIZE))
        m2 = jnp.max(jnp.where(sub == i1, neg, xg), axis=0, keepdims=True)
        gscores.append(m1 + m2)
    gs = jnp.concatenate(gscores, axis=0)
    gid = lax.broadcasted_iota(jnp.int32, (N_GROUPS, t), 0).astype(F32)
    gsel = jnp.zeros((N_GROUPS, t), F32)
    for _ in range(TOPK_GROUPS):
        _, gi = _first_argmax(gs, gid, float(N_GROUPS))
        hit = gid == gi
        gsel = jnp.where(hit, 1.0, gsel)
        gs = jnp.where(hit, neg, gs)
    eid = lax.broadcasted_iota(jnp.int32, (N_EXPERTS, t), 0).astype(F32)
    allowed = jnp.concatenate(
        [jnp.broadcast_to(gsel[g:g + 1, :], (GROUP_SIZE, t)) for g in range(N_GROUPS)], axis=0)
    cand = jnp.where(allowed > 0.0, biased, neg)
    idxs, ws, hits = [], [], []
    for _ in range(TOP_K):
        _, ei = _first_argmax(cand, eid, float(N_EXPERTS))
        hit = eid == ei
        idxs.append(ei)
        hits.append(hit)
        ws.append(jnp.sum(jnp.where(hit, scores, 0.0), axis=0, keepdims=True))
        cand = jnp.where(hit, neg, cand)
    w = jnp.concatenate(ws, axis=0)
    w = w / (jnp.sum(w, axis=0, keepdims=True) + 1e-20) * ROUTED_SCALE
    idx_ref[...] = jnp.concatenate(idxs, axis=0).astype(jnp.int32)
    w_ref[...] = w
    chosen = jnp.zeros((N_EXPERTS, t), F32)
    for hit in hits:
        chosen = jnp.where(hit, 1.0, chosen)
    prior = jnp.dot(chosen.astype(BF16), before_ref[...],
                    preferred_element_type=F32) + cnt_ref[...]
    rank_ref[...] = jnp.concatenate(
        [jnp.sum(jnp.where(hit, prior, 0.0), axis=0, keepdims=True) for hit in hits], axis=0).astype(jnp.int32)
    cnt_ref[...] += jnp.sum(chosen, axis=1, keepdims=True)
    cnt_out_ref[0] = cnt_ref[...]


def _router(x, w_router, bias, tile_tokens, tr=1024):
    t, d = x.shape
    tr = min(tr, tile_tokens)
    steps_per_tile = tile_tokens // tr
    n_tiles = t // tile_tokens
    kt = pl.BlockSpec((TOP_K, tr), lambda i: (0, i))
    before = jnp.triu(jnp.ones((tr, tr), BF16), k=1)
    idx, w, rank, counts = pl.pallas_call(
        functools.partial(_router_kernel, steps_per_tile=steps_per_tile),
        grid=(t // tr,),
        in_specs=[pl.BlockSpec((tr, d), lambda i: (i, 0)),
                  pl.BlockSpec((N_EXPERTS, d), lambda i: (0, 0)),
                  pl.BlockSpec((N_EXPERTS, 1), lambda i: (0, 0)),
                  pl.BlockSpec((tr, tr), lambda i: (0, 0))],
        out_specs=[kt, kt, kt,
                   pl.BlockSpec((1, N_EXPERTS, 1), lambda i: (i // steps_per_tile, 0, 0))],
        out_shape=[jax.ShapeDtypeStruct((TOP_K, t), jnp.int32),
                   jax.ShapeDtypeStruct((TOP_K, t), F32),
                   jax.ShapeDtypeStruct((TOP_K, t), jnp.int32),
                   jax.ShapeDtypeStruct((n_tiles, N_EXPERTS, 1), F32)],
        scratch_shapes=[pltpu.VMEM((N_EXPERTS, 1), F32)],
        compiler_params=_cparams("arbitrary"),
        name="moe_router",
    )(x, w_router.T, bias.reshape(N_EXPERTS, 1), before)
    return idx, w, rank, counts.reshape(n_tiles, N_EXPERTS).astype(jnp.int32)


def _steps_per_tile(tile_tokens):
    return tile_tokens * TOP_K // STEP_ROWS + N_EXPERTS


def _plan_kernel(fill_lo_ref, fill_hi_ref, pos_ref, src_ref, *, tile_tokens):
    i = pl.program_id(0)
    n_assign = tile_tokens * TOP_K
    empty = n_assign
    group = 16

    n_ranges = N_EXPERTS + 1

    def fill_range(g, carry):
        lo = fill_lo_ref[i * n_ranges + g] // group * group
        hi = fill_hi_ref[i * n_ranges + g]

        def fill(b, carry):
            for d in range(group):
                src_ref[0, 0, lo + b * group + d] = empty
            return carry
        return lax.fori_loop(0, (hi - lo) // group, fill, carry)
    lax.fori_loop(0, n_ranges, fill_range, 0)

    def place(b, carry):
        base = b * group
        slots = [pos_ref[0, 0, base + d] for d in range(group)]
        for d in range(group):
            src_ref[0, 0, slots[d]] = base + d
        return carry
    lax.fori_loop(0, n_assign // group, place, 0)


def _dispatch_plan(idx_t, w_t, rank_t, counts, tile_tokens):
    t = idx_t.shape[1]
    n_tiles = t // tile_tokens
    steps = _steps_per_tile(tile_tokens)
    padded = (counts + STEP_ROWS - 1) // STEP_ROWS * STEP_ROWS
    pad_end = jnp.cumsum(padded, axis=1)
    pad_start = (pad_end - padded).astype(jnp.int32)
    n_used = (pad_end[:, -1] // STEP_ROWS).astype(jnp.int32)
    starts = jnp.arange(steps, dtype=jnp.int32) * STEP_ROWS
    step_e = jnp.sum(starts[None, :, None] >= pad_end[:, None, :], axis=-1).astype(jnp.int32)
    step_e = jnp.minimum(step_e, N_EXPERTS - 1)
    last = jnp.take_along_axis(step_e, jnp.maximum(n_used - 1, 0)[:, None], axis=1)
    step_e = jnp.where(starts[None, :] // STEP_ROWS < n_used[:, None], step_e, last)

    start_of = jnp.repeat(pad_start.T, tile_tokens, axis=1)
    experts = jnp.arange(N_EXPERTS, dtype=jnp.int32)[None, :, None]
    pos_t = rank_t + jnp.sum(jnp.where(idx_t[:, None, :] == experts, start_of[None], 0), axis=1)
    group_end = (pad_start + counts).astype(jnp.int32)
    n_slots = steps * STEP_ROWS
    fill_lo = jnp.concatenate([group_end, pad_end[:, -1:]], axis=1).astype(jnp.int32)
    fill_hi = jnp.concatenate([pad_end, jnp.full((n_tiles, 1), n_slots)], axis=1).astype(jnp.int32)

    n_assign = tile_tokens * TOP_K
    single = pl.Buffered(1)
    src = pl.pallas_call(
        functools.partial(_plan_kernel, tile_tokens=tile_tokens),
        grid_spec=pltpu.PrefetchScalarGridSpec(
            num_scalar_prefetch=2,
            grid=(n_tiles,),
            in_specs=[pl.BlockSpec((1, 1, n_assign), lambda i, lo, hi: (i, 0, 0),
                                   memory_space=pltpu.SMEM, pipeline_mode=single)],
            out_specs=pl.BlockSpec((1, 1, n_slots), lambda i, lo, hi: (i, 0, 0),
                                   memory_space=pltpu.SMEM, pipeline_mode=single),
        ),
        out_shape=jax.ShapeDtypeStruct((n_tiles, 1, n_slots), jnp.int32),
        compiler_params=_cparams("parallel"),
        name="moe_plan",
    )(fill_lo.reshape(-1), fill_hi.reshape(-1), pos_t.T.reshape(n_tiles, 1, n_assign))
    rows = src & -SUBLANES
    w_flat = jnp.pad(w_t.T.reshape(n_tiles, tile_tokens * TOP_K), ((0, 0), (0, SUBLANES)))
    slot_w = jnp.take_along_axis(w_flat, src[:, 0, :], axis=1)
    held = jnp.clip(jnp.take_along_axis(group_end, step_e, axis=1) - starts[None, :], 0, STEP_ROWS)
    live = jnp.where(starts[None, :] // STEP_ROWS < n_used[:, None], (held + EXPERT_ROWS - 1) // EXPERT_ROWS, 0)
    return rows, slot_w, step_e.reshape(-1), n_used, live.reshape(-1).astype(jnp.int32), steps


def _experts_kernel(step_e_ref, n_used_ref, live_ref, row_ref, sw_ref, x_ref, wg_ref, wu_ref, wd_ref, acc_ref,
                    *stage_refs):
    n_sub = BLOCKS_PER_STEP
    xs_refs = stage_refs[:n_sub]
    ys_refs = stage_refs[n_sub:]
    i = pl.program_id(0)
    j = pl.program_id(1)
    rows = EXPERT_ROWS
    pitch = STAGE_PITCH

    @pl.when(j == 0)
    def _():
        acc_ref[...] = jnp.zeros_like(acc_ref)

    last_row = x_ref.shape[1] - SUBLANES

    def token_rows(slot, limit=None):
        row = row_ref[0, 0, slot]
        if limit is not None:
            row = jnp.minimum(row, limit)
        return pl.ds(pl.multiple_of(row, SUBLANES), SUBLANES)

    def run(subs):
        for s in subs:
            for r in range(rows):
                xs_refs[s][pl.ds(r, ROW_CHUNKS, stride=pitch), :] = x_ref[0, token_rows(s * rows + r, last_row), :]
        eye = (lax.broadcasted_iota(jnp.int32, (rows, rows), 0)
               == lax.broadcasted_iota(jnp.int32, (rows, rows), 1))
        ys = {}
        for s in subs:
            x = jnp.concatenate([xs_refs[s][pl.ds(c * pitch, rows), :] for c in range(ROW_CHUNKS)], axis=1)
            hid = _silu(_bdot(x, wg_ref[0, 0])) * _bdot(x, wu_ref[0, 0])
            w_row = sw_ref[0, :, s * rows:(s + 1) * rows]
            w_col = jnp.sum(jnp.where(eye, w_row, 0.0), axis=1, keepdims=True)
            ys[s] = _bdot(hid, wd_ref[0, 0]) * w_col
        for s in subs:
            for c in range(ROW_CHUNKS):
                ys_refs[s][pl.ds(c * pitch, rows), :] = ys[s][:, c * LANES:(c + 1) * LANES]
        for s in subs:
            for r0 in range(0, rows, SCATTER_BATCH):
                batch = range(r0, r0 + SCATTER_BATCH)
                ats = [token_rows(s * rows + r) for r in batch]
                new = [acc_ref[0, at, :] + ys_refs[s][pl.ds(r, ROW_CHUNKS, stride=pitch), :]
                       for at, r in zip(ats, batch)]
                for at, val in zip(ats, new):
                    acc_ref[0, at, :] = val

    live = live_ref[i * pl.num_programs(1) + j]
    for n_live in range(1, n_sub + 1):
        pl.when(live == n_live)(functools.partial(run, range(n_live)))


def _routed_experts(x_slabs, idx_t, w_t, rank_t, counts, w_gate_up, w_down, layer, tile_tokens):
    d = D_MODEL
    t = x_slabs.shape[0] // ROW_CHUNKS
    n_tiles = t // tile_tokens
    rows, slot_w, step_e, n_used, live, steps = _dispatch_plan(idx_t, w_t, rank_t, counts, tile_tokens)
    rows = rows.reshape(n_tiles * steps, 1, STEP_ROWS)
    slot_w = slot_w.reshape(n_tiles * steps, 1, STEP_ROWS)
    x_slabs = x_slabs.reshape(n_tiles, tile_tokens * ROW_CHUNKS, LANES)
    slab_rows = (tile_tokens + 1) * ROW_CHUNKS

    def used_step(i, j, se, nu, lv):
        return (i * steps + jnp.minimum(j, jnp.maximum(nu[i] - 1, 0)), 0, 0)

    pick = lambda i, j, se, nu, lv: (layer, se[i * steps + j], 0, 0)
    tile = lambda i, j, se, nu, lv: (i, 0, 0)
    resident = pl.Buffered(1)
    stage = pltpu.VMEM((ROW_CHUNKS * STAGE_PITCH, LANES), F32)
    grid_spec = pltpu.PrefetchScalarGridSpec(
        num_scalar_prefetch=3,
        grid=(n_tiles, steps),
        in_specs=[
            pl.BlockSpec((1, 1, STEP_ROWS), used_step, memory_space=pltpu.SMEM),
            pl.BlockSpec((1, 1, STEP_ROWS), used_step),
            pl.BlockSpec((1, tile_tokens * ROW_CHUNKS, LANES), tile, pipeline_mode=resident),
            pl.BlockSpec((1, 1, d, D_EXPERT), pick),
            pl.BlockSpec((1, 1, d, D_EXPERT), pick),
            pl.BlockSpec((1, 1, D_EXPERT, d), pick),
        ],
        out_specs=pl.BlockSpec((1, slab_rows, LANES), tile, pipeline_mode=resident),
        scratch_shapes=[stage] * (2 * BLOCKS_PER_STEP),
    )
    return pl.pallas_call(
        _experts_kernel,
        grid_spec=grid_spec,
        out_shape=jax.ShapeDtypeStruct((n_tiles, slab_rows, LANES), F32),
        compiler_params=_cparams("parallel", "arbitrary"),
        name="moe_experts",
    )(step_e, n_used, live, rows, slot_w, x_slabs, *w_gate_up, w_down)


def _moe_out_kernel(x_ref, r_ref, wg_ref, wu_ref, wd_ref, g_ref, b_ref, o_ref):
    x = x_ref[...]
    tm = x.shape[0]
    hid = _silu(_bdot(x, wg_ref[...])) * _bdot(x, wu_ref[...])
    routed = jnp.concatenate([r_ref[0, pl.ds(c, tm, stride=ROW_CHUNKS), :] for c in range(ROW_CHUNKS)], axis=1)
    y = routed + _bdot(hid, wd_ref[...])
    o_ref[...] = _layer_norm(DN_ALPHA * x + y, g_ref[...], b_ref[...])


def _moe_out(x, routed, ws_gate, ws_up, ws_down, g, b, tile_tokens, tm=1024):
    t, d = x.shape
    tm = min(tm, tile_tokens)
    per_tile = tile_tokens // tm
    return pl.pallas_call(
        _moe_out_kernel,
        grid=(t // tm,),
        in_specs=[pl.BlockSpec((tm, d), lambda i: (i, 0)),
                  pl.BlockSpec((1, tm * ROW_CHUNKS, LANES), lambda i: (i // per_tile, i % per_tile, 0)),
                  pl.BlockSpec((d, D_SHARED), lambda i: (0, 0)),
                  pl.BlockSpec((d, D_SHARED), lambda i: (0, 0)),
                  pl.BlockSpec((D_SHARED, d), lambda i: (0, 0)),
                  pl.BlockSpec((1, d), lambda i: (0, 0)),
                  pl.BlockSpec((1, d), lambda i: (0, 0))],
        out_specs=pl.BlockSpec((tm, d), lambda i: (i, 0)),
        out_shape=jax.ShapeDtypeStruct((t, d), F32),
        compiler_params=_cparams("parallel"),
        name="moe_out",
    )(x, routed, ws_gate.astype(BF16), ws_up.astype(BF16), ws_down.astype(BF16),
      g.reshape(1, d), b.reshape(1, d))


def _moe_ln(hid, hid_slabs, w_router, router_bias, w_gate_up, w_down, layer, ws_gate, ws_up, ws_down, ln_g, ln_b):
    b, l, d = hid.shape
    x = hid.reshape(b * l, d)
    tile_tokens = min(MOE_TILE_TOKENS, b * l)
    idx_t, w_t, rank_t, counts = _router(x, w_router, router_bias, tile_tokens)
    routed = _routed_experts(hid_slabs, idx_t, w_t, rank_t, counts, w_gate_up, w_down, layer, tile_tokens)
    return _moe_out(x, routed, ws_gate, ws_up, ws_down, ln_g, ln_b, tile_tokens).reshape(b, l, d)


def kernel(x, a_w_in, a_conv_w, a_A_log, a_dt_bias, a_onorm_w, a_w_out, b_w_dq, b_qnorm_w, b_w_uq, b_w_o,
           kv_w_dkv, kv_norm_w, kv_w_ukv, ln1_g, ln1_b, ln2_g, ln2_b, moe_w_router, moe_router_bias,
           moe_w_gate, moe_w_up, moe_w_down, moe_ws_gate, moe_ws_up, moe_ws_down):
    l = x.shape[1]
    cos_t, sin_t = _rope_lane_tables(l)
    w_gate_up = (moe_w_gate.astype(BF16), moe_w_up.astype(BF16))
    w_down = moe_w_down.astype(BF16)
    h = x
    k = v = None
    for layer in range(DEPTH):
        if layer < N_A_LAYERS:
            i = layer
            h, slabs = _gated_deltanet_ln(h, a_w_in[i], a_conv_w[i], a_A_log[i], a_dt_bias[i], a_onorm_w[i],
                                          a_w_out[i], ln1_g[layer], ln1_b[layer])
        else:
            i = layer - N_A_LAYERS
            h, slabs = _mla_ln(h, b_w_dq[i], b_qnorm_w[i], b_w_uq[i], b_w_o[i], k, v, cos_t, sin_t,
                               ln1_g[layer], ln1_b[layer])
        h = _moe_ln(h, slabs, moe_w_router[layer], moe_router_bias[layer], w_gate_up, w_down, layer,
                    moe_ws_gate[layer], moe_ws_up[layer], moe_ws_down[layer], ln2_g[layer], ln2_b[layer])
        if layer == N_A_LAYERS - 1:
            k, v = _mla_kv(h, kv_w_dkv, kv_norm_w, kv_w_ukv, cos_t, sin_t)
    return h
```
